```python
import jax, jax.numpy as jnp
from jax import lax
import numpy as np

D_MODEL = 1024
BATCH = 8
SEQ = 2048
DEPTH = 1
DEC_BATCH = 128
DEC_SEQ = 4
PAST_LEN = 16384
PAGE_SIZE = 128

RET_WIDTH = D_MODEL // 2
N_RET_HEADS = 4
RET_HEAD_DIM = RET_WIDTH // N_RET_HEADS
RET_CHUNK = 128
ROPE_THETA = 10000.0
CONV_DIM = D_MODEL - RET_WIDTH
CONV_K = 3
IN_COLS = 4 * RET_WIDTH + 3 * CONV_DIM
N_MEM = 256
N_MEM_HEADS = 4
MEM_HEAD_DIM = D_MODEL // N_MEM_HEADS
N_EXPERTS = 32
TOP_K = 4
D_FF_EXPERT = D_MODEL
SWIGLU_ALPHA = 1.702
SWIGLU_LIMIT = 7.0
MOE_BLOCK = 256
EPS = 1e-6

kernel_name = 'hybrid_retention_shortconv_moe_decode_step'


def rmsnorm(x, g):
    xf = x.astype(jnp.float32)
    xf = xf * lax.rsqrt(jnp.mean(xf * xf, axis=-1, keepdims=True) + EPS)
    return xf.astype(x.dtype) * g


def rope(x, pos):
    half = x.shape[-1] // 2
    inv = ROPE_THETA ** (-jnp.arange(half, dtype=jnp.float32) / half)
    ang = pos[:, None] * inv[None, :]
    cos = jnp.cos(ang)[None, :, None, :]
    sin = jnp.sin(ang)[None, :, None, :]
    x1, x2 = x[..., :half], x[..., half:]
    return jnp.concatenate([x1 * cos - x2 * sin, x1 * sin + x2 * cos], axis=-1)


def retention(q, k, v, s0, log_gamma):
    B, T, H, Dk = q.shape
    Dv = v.shape[-1]
    C = RET_CHUNK if T % RET_CHUNK == 0 else T
    n = T // C
    qc = q.reshape(B, n, C, H, Dk)
    kc = k.reshape(B, n, C, H, Dk)
    vc = v.reshape(B, n, C, H, Dv)
    i = jnp.arange(C, dtype=jnp.float32)
    diff = i[:, None] - i[None, :]
    decay = jnp.where(diff >= 0, jnp.exp(jnp.maximum(diff, 0.0)[None] * log_gamma[:, None, None]), 0.0)
    xi = jnp.exp((i + 1.0)[None, :] * log_gamma[:, None])
    zeta = jnp.exp((C - 1.0 - i)[None, :] * log_gamma[:, None])
    chunk_decay = jnp.exp(C * log_gamma)[None, :, None, None]
    scores = jnp.einsum('bnqhd,bnkhd->bnhqk', qc, kc) * decay
    inner = jnp.einsum('bnhqk,bnkhe->bnqhe', scores, vc)
    kv = jnp.einsum('bnkhd,hk,bnkhe->bnhde', kc, zeta, vc)

    def step(s, kv_i):
        return chunk_decay * s + kv_i, s

    s_final, s_prev = lax.scan(step, s0, jnp.moveaxis(kv, 1, 0))
    s_prev = jnp.moveaxis(s_prev, 0, 1)
    cross = jnp.einsum('bnqhd,hq,bnhde->bnqhe', qc, xi, s_prev)
    return (inner + cross).reshape(B, T, H, Dv), s_final


def token_mixer(xn, s_ret, conv_prev, pos0, w_in, conv_w, ret_gn, w_out):
    B, T, _ = xn.shape
    proj = xn @ w_in
    R, Cd = RET_WIDTH, CONV_DIM
    q, k, v, g, hc, cg, bg = jnp.split(proj, [R, 2 * R, 3 * R, 4 * R, 4 * R + Cd, 4 * R + 2 * Cd], axis=-1)
    pos = pos0 + jnp.arange(T, dtype=jnp.float32)
    H, Dh = N_RET_HEADS, RET_HEAD_DIM
    qf = rope(q.reshape(B, T, H, Dh).astype(jnp.float32), pos)
    kf = rope(k.reshape(B, T, H, Dh).astype(jnp.float32), pos) * (Dh ** -0.5)
    vf = v.reshape(B, T, H, Dh).astype(jnp.float32)
    log_gamma = jnp.log1p(-jnp.exp2(-5.0 - jnp.arange(H, dtype=jnp.float32)))
    o, s_new = retention(qf, kf, vf, s_ret.astype(jnp.float32), log_gamma)
    mu = jnp.mean(o, axis=-1, keepdims=True)
    var = jnp.mean(jnp.square(o - mu), axis=-1, keepdims=True)
    o = ((o - mu) * lax.rsqrt(var + EPS)).reshape(B, T, R).astype(xn.dtype) * ret_gn
    o = jax.nn.silu(g) * o
    u = cg * hc
    full = jnp.concatenate([conv_prev.astype(u.dtype), u], axis=1)
    conv = sum(full[:, j:j + T] * conv_w[j] for j in range(CONV_K))
    c = bg * conv
    y = jnp.concatenate([o, c], axis=-1) @ w_out
    return y, s_new.astype(s_ret.dtype), full[:, T:]


def mem_kv(mem, norm_mem, w_mk, w_mv):
    B, M, _ = mem.shape
    mn = rmsnorm(mem, norm_mem)
    mk = (mn @ w_mk).reshape(B, M, N_MEM_HEADS, MEM_HEAD_DIM)
    mv = (mn @ w_mv).reshape(B, M, N_MEM_HEADS, MEM_HEAD_DIM)
    return mk, mv


def cross_attn(xn, mk, mv, w_mq, w_mo):
    B, T, D = xn.shape
    q = (xn @ w_mq).reshape(B, T, N_MEM_HEADS, MEM_HEAD_DIM)
    s = jnp.einsum('bthd,bmhd->bhtm', q, mk.astype(q.dtype)).astype(jnp.float32) * (MEM_HEAD_DIM ** -0.5)
    p = jax.nn.softmax(s, axis=-1).astype(q.dtype)
    o = jnp.einsum('bhtm,bmhd->bthd', p, mv.astype(q.dtype)).reshape(B, T, D)
    return o @ w_mo


def moe(x2d, w_router, b_router, w_up, b_up, w_down, b_down):
    N, D = x2d.shape
    logits = (x2d @ w_router + b_router).astype(jnp.float32)
    top_v, top_e = lax.top_k(logits, TOP_K)
    gates = jax.nn.softmax(top_v, axis=-1)
    A = N * TOP_K
    e_flat = top_e.reshape(A)
    g_flat = gates.reshape(A)
    t_flat = jnp.arange(A, dtype=jnp.int32) // TOP_K
    order = jnp.argsort(e_flat)
    e_sorted = e_flat[order]
    counts = jnp.bincount(e_flat, length=N_EXPERTS)
    padded = (counts + MOE_BLOCK - 1) // MOE_BLOCK * MOE_BLOCK
    pad_end = jnp.cumsum(padded)
    pad_start = pad_end - padded
    start = jnp.cumsum(counts) - counts
    dest = pad_start[e_sorted] + jnp.arange(A, dtype=jnp.int32) - start[e_sorted]
    n_blocks = -(-(A + N_EXPERTS * (MOE_BLOCK - 1)) // MOE_BLOCK)
    P = n_blocks * MOE_BLOCK
    row_tok = jnp.full((P,), N, jnp.int32).at[dest].set(t_flat[order])
    row_gate = jnp.zeros((P,), jnp.float32).at[dest].set(g_flat[order])
    blk_start = jnp.arange(n_blocks, dtype=jnp.int32) * MOE_BLOCK
    blk_e = jnp.minimum(jnp.sum(pad_end[None, :] <= blk_start[:, None], axis=1), N_EXPERTS - 1)
    x_pad = jnp.concatenate([x2d, jnp.zeros((1, D), x2d.dtype)], axis=0)

    def expert_block(args):
        tok, gate, e = args
        h = x_pad[tok] @ w_up[e] + b_up[e]
        glu = jnp.minimum(h[:, 0::2], SWIGLU_LIMIT)
        lin = jnp.clip(h[:, 1::2], -SWIGLU_LIMIT, SWIGLU_LIMIT)
        a = glu * jax.nn.sigmoid(SWIGLU_ALPHA * glu) * (lin + 1.0)
        out = a @ w_down[e] + b_down[e]
        return out * gate[:, None].astype(out.dtype)

    yb = lax.map(expert_block, (row_tok.reshape(n_blocks, MOE_BLOCK), row_gate.reshape(n_blocks, MOE_BLOCK), blk_e))
    return jax.ops.segment_sum(yb.reshape(P, D), row_tok, num_segments=N + 1)[:N]


def setup_inputs(seed: int = 0) -> dict:
    key = jax.random.key(seed)
    ks = jax.random.split(key, 32)
    D, F, E = D_MODEL, D_FF_EXPERT, N_EXPERTS

    def nrm(k, shape, scale):
        return jax.random.normal(k, shape, jnp.float32) * scale

    def gain(k, shape):
        return 1.0 + 0.1 * jax.random.normal(k, shape, jnp.float32)

    return {
        'x_prompt': nrm(ks[0], (BATCH, SEQ, D), 1.0),
        'x_sample': nrm(ks[1], (DEC_BATCH, DEC_SEQ, D), 1.0),
        'mem_prompt': nrm(ks[2], (BATCH, N_MEM, D), 1.0),
        'state_ret': nrm(ks[3], (DEPTH, DEC_BATCH, N_RET_HEADS, RET_HEAD_DIM, RET_HEAD_DIM), 1.0),
        'state_conv': nrm(ks[4], (DEPTH, DEC_BATCH, CONV_K - 1, CONV_DIM), 0.5),
        'cache_mem_k': nrm(ks[5], (DEPTH, DEC_BATCH, N_MEM, N_MEM_HEADS, MEM_HEAD_DIM), 1.0),
        'cache_mem_v': nrm(ks[6], (DEPTH, DEC_BATCH, N_MEM, N_MEM_HEADS, MEM_HEAD_DIM), 1.0),
        'norm_mix': gain(ks[7], (DEPTH, D)),
        'w_in': nrm(ks[8], (DEPTH, D, IN_COLS), D ** -0.5),
        'conv_w': nrm(ks[9], (DEPTH, CONV_K, CONV_DIM), CONV_K ** -0.5),
        'ret_gn': gain(ks[10], (DEPTH, RET_WIDTH)),
        'w_out': nrm(ks[11], (DEPTH, D, D), D ** -0.5),
        'norm_cross': gain(ks[12], (DEPTH, D)),
        'norm_mem': gain(ks[13], (DEPTH, D)),
        'w_mq': nrm(ks[14], (DEPTH, D, D), D ** -0.5),
        'w_mk': nrm(ks[15], (DEPTH, D, D), D ** -0.5),
        'w_mv': nrm(ks[16], (DEPTH, D, D), D ** -0.5),
        'w_mo': nrm(ks[17], (DEPTH, D, D), D ** -0.5),
        'norm_ffn': gain(ks[18], (DEPTH, D)),
        'w_router': nrm(ks[19], (DEPTH, D, E), D ** -0.5),
        'b_router': nrm(ks[20], (DEPTH, E), 0.01),
        'w_up': nrm(ks[21], (DEPTH, E, D, 2 * F), D ** -0.5),
        'b_up': nrm(ks[22], (DEPTH, E, 2 * F), 0.02),
        'w_down': nrm(ks[23], (DEPTH, E, F, D), F ** -0.5),
        'b_down': nrm(ks[24], (DEPTH, E, D), 0.02),
        'norm_final': gain(ks[25], (D,)),
    }


def reference(x_prompt, x_sample, mem_prompt, state_ret, state_conv, cache_mem_k, cache_mem_v,
              norm_mix, w_in, conv_w, ret_gn, w_out, norm_cross, norm_mem, w_mq, w_mk, w_mv, w_mo,
              norm_ffn, w_router, b_router, w_up, b_up, w_down, b_down, norm_final):
    hp, hs = x_prompt, x_sample
    B, T, D = hp.shape
    Bs, Ts, _ = hs.shape
    s0_prompt = jnp.zeros((B, N_RET_HEADS, RET_HEAD_DIM, RET_HEAD_DIM), hp.dtype)
    c0_prompt = jnp.zeros((B, CONV_K - 1, CONV_DIM), hp.dtype)
    ret_p, conv_p, mk_p, mv_p, ret_s, conv_s = [], [], [], [], [], []
    for l in range(DEPTH):
        yp, sp, cp = token_mixer(rmsnorm(hp, norm_mix[l]), s0_prompt, c0_prompt, 0.0,
                                 w_in[l], conv_w[l], ret_gn[l], w_out[l])
        ys, ss, cs = token_mixer(rmsnorm(hs, norm_mix[l]), state_ret[l], state_conv[l], float(PAST_LEN),
                                 w_in[l], conv_w[l], ret_gn[l], w_out[l])
        hp = hp + yp
        hs = hs + ys
        mk, mv = mem_kv(mem_prompt, norm_mem[l], w_mk[l], w_mv[l])
        hp = hp + cross_attn(rmsnorm(hp, norm_cross[l]), mk, mv, w_mq[l], w_mo[l])
        hs = hs + cross_attn(rmsnorm(hs, norm_cross[l]), cache_mem_k[l], cache_mem_v[l], w_mq[l], w_mo[l])
        tok = jnp.concatenate([rmsnorm(hp, norm_ffn[l]).reshape(B * T, D),
                               rmsnorm(hs, norm_ffn[l]).reshape(Bs * Ts, D)], axis=0)
        f = moe(tok, w_router[l], b_router[l], w_up[l], b_up[l], w_down[l], b_down[l])
        hp = hp + f[:B * T].reshape(B, T, D)
        hs = hs + f[B * T:].reshape(Bs, Ts, D)
        ret_p.append(sp)
        conv_p.append(cp)
        mk_p.append(mk)
        mv_p.append(mv)
        ret_s.append(ss)
        conv_s.append(cs)
    y_prompt = rmsnorm(hp, norm_final)
    y_sample = rmsnorm(hs, norm_final)
    return (y_prompt, y_sample, jnp.stack(ret_p), jnp.stack(conv_p), jnp.stack(mk_p), jnp.stack(mv_p),
            jnp.stack(ret_s), jnp.stack(conv_s))
```

```python
import functools

import jax
import jax.numpy as jnp
from jax import lax
from jax.experimental import pallas as pl
from jax.experimental.pallas import tpu as pltpu

f32 = jnp.float32
bf16 = jnp.bfloat16
i32 = jnp.int32

EPS = 1e-6
ROPE_THETA = 10000.0
RET_CHUNK = 128
PAST_LEN = 16384
N_RET_HEADS = 4
N_MEM_HEADS = 4
TOP_K = 4
SWIGLU_ALPHA = 1.702
SWIGLU_LIMIT = 7.0

LANES = 128
SUBLANES = 8
MIB = 1024 * 1024

ROW_TILE = 512
SAMPLE_BLOCK = 8
MOE_CHUNKS = 4
MOE_SUB = 128
SCATTER_UNROLL = 8


def _cparams(sem, vmem_mib):
    return pltpu.CompilerParams(dimension_semantics=sem, vmem_limit_bytes=vmem_mib * MIB)


def _rms(x, g):
    ms = jnp.mean(x * x, axis=-1, keepdims=True)
    return (x * lax.rsqrt(ms + EPS)) * g


def _norm_matmul_kernel(x_ref, g_ref, w_ref, o_ref):
    xn = _rms(x_ref[...], g_ref[...])
    o_ref[...] = jnp.dot(xn.astype(bf16), w_ref[...], preferred_element_type=f32)


def norm_matmul(x, g, w, name):
    m, d = x.shape
    f = w.shape[1]
    tm = min(ROW_TILE, m)
    assert m % tm == 0
    return pl.pallas_call(
        _norm_matmul_kernel,
        out_shape=jax.ShapeDtypeStruct((m, f), f32),
        grid=(m // tm,),
        in_specs=[
            pl.BlockSpec((tm, d), lambda i: (i, 0)),
            pl.BlockSpec((1, d), lambda i: (0, 0)),
            pl.BlockSpec((d, f), lambda i: (0, 0)),
        ],
        out_specs=pl.BlockSpec((tm, f), lambda i: (i, 0)),
        compiler_params=_cparams(("arbitrary",), 48),
        name=name,
    )(x, g.reshape(1, d), w)


def _matmul_res_kernel(ap_ref, as_ref, w_ref, r_ref, o_ref, *, prompt_tiles):
    a = jnp.where(pl.program_id(0) < prompt_tiles, ap_ref[...], as_ref[...])
    o_ref[...] = r_ref[...] + jnp.dot(a, w_ref[...], preferred_element_type=f32)


def matmul_res(a_prompt, a_sample, w, res, name):
    m_p, d = a_prompt.shape
    m_s = a_sample.shape[0]
    f = w.shape[1]
    tm = ROW_TILE
    assert m_p % tm == 0 and m_s == tm
    prompt_tiles = m_p // tm
    return pl.pallas_call(
        functools.partial(_matmul_res_kernel, prompt_tiles=prompt_tiles),
        out_shape=jax.ShapeDtypeStruct((m_p + m_s, f), f32),
        grid=(prompt_tiles + 1,),
        in_specs=[
            pl.BlockSpec((tm, d), lambda i: (jnp.minimum(i, prompt_tiles - 1), 0)),
            pl.BlockSpec((tm, d), lambda i: (0, 0)),
            pl.BlockSpec((d, f), lambda i: (0, 0)),
            pl.BlockSpec((tm, f), lambda i: (i, 0)),
        ],
        out_specs=pl.BlockSpec((tm, f), lambda i: (i, 0)),
        compiler_params=_cparams(("arbitrary",), 32),
        name=name,
    )(a_prompt, a_sample, w, res)


def _rope(x, cos, sin_signed, half):
    return x * cos + pltpu.roll(x, half, 1) * sin_signed


def _group_norm_gate(o, gate, gn):
    mu = jnp.mean(o, axis=-1, keepdims=True)
    var = jnp.mean(jnp.square(o - mu), axis=-1, keepdims=True)
    on = ((o - mu) * lax.rsqrt(var + EPS)) * gn
    return (gate * jax.nn.sigmoid(gate)) * on


def _prompt_mixer_kernel(cd_ref, proj_ref, cos_ref, sin_ref, decay_ref, xi_ref, zeta_ref, convw_ref, gn_ref,
                         mix_ref, sfin_ref, cfin_ref, s_scr, u_scr, *, tt, r, cd_width, dh):
    j = pl.program_id(1)
    heads = r // dh
    pad = SUBLANES

    @pl.when(j == 0)
    def _():
        s_scr[...] = jnp.zeros_like(s_scr)
        u_scr[0:pad, :] = jnp.zeros((pad, cd_width), f32)

    k_scale = dh ** -0.5
    for c in range(tt // RET_CHUNK):
        rows = pl.ds(c * RET_CHUNK, RET_CHUNK)
        cosb = cos_ref[rows, :]
        sinb = sin_ref[rows, :]
        for h in range(heads):
            lo, hi = h * dh, (h + 1) * dh
            qh = _rope(proj_ref[rows, lo:hi], cosb, sinb, dh // 2)
            kh = _rope(proj_ref[rows, r + lo:r + hi], cosb, sinb, dh // 2) * k_scale
            vb = proj_ref[rows, 2 * r + lo:2 * r + hi].astype(bf16)
            gate = proj_ref[rows, 3 * r + lo:3 * r + hi]
            qb = qh.astype(bf16)
            kb = kh.astype(bf16)
            scores = lax.dot_general(qb, kb, (((1,), (1,)), ((), ())), preferred_element_type=f32) * decay_ref[h]
            inner = jnp.dot(scores.astype(bf16), vb, preferred_element_type=f32)
            s_prev = s_scr[h]
            cross = jnp.dot((qh * xi_ref[h]).astype(bf16), s_prev.astype(bf16), preferred_element_type=f32)
            kv = lax.dot_general((kh * zeta_ref[h]).astype(bf16), vb, (((0,), (0,)), ((), ())),
                                 preferred_element_type=f32)
            s_scr[h] = cd_ref[h] * s_prev + kv
            mix_ref[rows, lo:hi] = _group_norm_gate(inner + cross, gate, gn_ref[:, lo:hi]).astype(bf16)

    base = 4 * r
    u = proj_ref[:, base + cd_width:base + 2 * cd_width] * proj_ref[:, base:base + cd_width]
    u_scr[pad:pad + tt, :] = u
    conv = (u_scr[pad - 2:pad - 2 + tt, :] * convw_ref[0:1, :] + u_scr[pad - 1:pad - 1 + tt, :] * convw_ref[1:2, :]
            + u * convw_ref[2:3, :])
    mix_ref[:, r:r + cd_width] = (proj_ref[:, base + 2 * cd_width:base + 3 * cd_width] * conv).astype(bf16)
    u_scr[0:pad, :] = u_scr[tt:tt + pad, :]
    cfin_ref[0] = u_scr[pad - 2:pad, :]

    @pl.when(j == pl.num_programs(1) - 1)
    def _():
        sfin_ref[0] = s_scr[...]


def prompt_mixer(proj, tables, conv_w, ret_gn, batch, seq, r, cd_width):
    tt = ROW_TILE
    dh = r // N_RET_HEADS
    steps = seq // tt
    in_cols = proj.shape[1]
    kern = functools.partial(_prompt_mixer_kernel, tt=tt, r=r, cd_width=cd_width, dh=dh)
    tab = lambda shape: pl.BlockSpec(shape, lambda b, j, cd: (0,) * len(shape))
    return pl.pallas_call(
        kern,
        out_shape=(
            jax.ShapeDtypeStruct((batch * seq, r + cd_width), bf16),
            jax.ShapeDtypeStruct((batch, N_RET_HEADS, dh, dh), f32),
            jax.ShapeDtypeStruct((batch, 2, cd_width), f32),
        ),
        grid_spec=pltpu.PrefetchScalarGridSpec(
            num_scalar_prefetch=1,
            grid=(batch, steps),
            in_specs=[
                pl.BlockSpec((tt, in_cols), lambda b, j, cd: (b * steps + j, 0)),
                pl.BlockSpec((tt, dh), lambda b, j, cd: (j, 0)),
                pl.BlockSpec((tt, dh), lambda b, j, cd: (j, 0)),
                tab((N_RET_HEADS, RET_CHUNK, RET_CHUNK)),
                tab((N_RET_HEADS, RET_CHUNK, dh)),
                tab((N_RET_HEADS, RET_CHUNK, dh)),
                tab((3, cd_width)),
                tab((1, r)),
            ],
            out_specs=(
                pl.BlockSpec((tt, r + cd_width), lambda b, j, cd: (b * steps + j, 0)),
                pl.BlockSpec((1, N_RET_HEADS, dh, dh), lambda b, j, cd: (b, 0, 0, 0)),
                pl.BlockSpec((1, 2, cd_width), lambda b, j, cd: (b, 0, 0)),
            ),
            scratch_shapes=[
                pltpu.VMEM((N_RET_HEADS, dh, dh), f32),
                pltpu.VMEM((tt + 2 * SUBLANES, cd_width), f32),
            ],
        ),
        compiler_params=_cparams(("arbitrary", "arbitrary"), 48),
        name="prompt_mixer",
    )(tables["chunk_decay"], proj, tables["cos"], tables["sin"], tables["decay"], tables["xi"], tables["zeta"],
      conv_w, ret_gn.reshape(1, r))


def _sample_mixer_kernel(dec_ref, proj_ref, cos_ref, sin_ref, xi_ref, zeta_ref, convw_ref, gn_ref, s0_ref, c0_ref,
                         mix_ref, snew_ref, cnew_ref, *, ts, r, cd_width, dh):
    heads = r // dh
    nb = SAMPLE_BLOCK
    rows = ts * nb
    k_scale = dh ** -0.5
    cosb = cos_ref[...]
    sinb = sin_ref[...]
    seq_of_row = lax.broadcasted_iota(i32, (rows, dh), 0) % nb
    for h in range(heads):
        lo, hi = h * dh, (h + 1) * dh
        qh = _rope(proj_ref[:, lo:hi], cosb, sinb, dh // 2)
        kh = _rope(proj_ref[:, r + lo:r + hi], cosb, sinb, dh // 2) * k_scale
        vh = proj_ref[:, 2 * r + lo:2 * r + hi]
        gate = proj_ref[:, 3 * r + lo:3 * r + hi]
        inner = []
        for t in range(ts):
            qt = qh[t * nb:(t + 1) * nb]
            acc = jnp.zeros((nb, dh), f32)
            for s in range(t + 1):
                sc = jnp.sum(qt * kh[s * nb:(s + 1) * nb], axis=-1, keepdims=True) * dec_ref[h * (ts + 1) + t - s]
                acc = acc + sc * vh[s * nb:(s + 1) * nb]
            inner.append(acc)
        inner = jnp.concatenate(inner, axis=0)
        qx = (qh * xi_ref[h]).astype(bf16)
        kz = kh * zeta_ref[h]
        vb = vh.astype(bf16)
        cross = jnp.zeros((rows, dh), f32)
        for b in range(nb):
            mine = seq_of_row == b
            s_prev = s0_ref[b, h]
            res = jnp.dot(qx, s_prev.astype(bf16), preferred_element_type=f32)
            cross = cross + jnp.where(mine, res, 0.0)
            kv = lax.dot_general(jnp.where(mine, kz, 0.0).astype(bf16), vb, (((0,), (0,)), ((), ())),
                                 preferred_element_type=f32)
            snew_ref[b, h] = dec_ref[h * (ts + 1) + ts] * s_prev + kv
        mix_ref[:, lo:hi] = _group_norm_gate(inner + cross, gate, gn_ref[:, lo:hi]).astype(bf16)

    base = 4 * r
    u = proj_ref[:, base + cd_width:base + 2 * cd_width] * proj_ref[:, base:base + cd_width]
    full = [c0_ref[0], c0_ref[1]] + [u[t * nb:(t + 1) * nb] for t in range(ts)]
    conv = jnp.concatenate(
        [full[t] * convw_ref[0:1, :] + full[t + 1] * convw_ref[1:2, :] + full[t + 2] * convw_ref[2:3, :]
         for t in range(ts)], axis=0)
    mix_ref[:, r:r + cd_width] = (proj_ref[:, base + 2 * cd_width:base + 3 * cd_width] * conv).astype(bf16)
    cnew_ref[0] = full[ts]
    cnew_ref[1] = full[ts + 1]


def sample_mixer(proj, tables, conv_w, ret_gn, state_ret, state_conv_t, row0, ts, r, cd_width):
    n_seq = state_ret.shape[0]
    dh = r // N_RET_HEADS
    rows = ts * SAMPLE_BLOCK
    blk0 = row0 // rows
    assert row0 % rows == 0 and n_seq % SAMPLE_BLOCK == 0
    in_cols = proj.shape[1]
    kern = functools.partial(_sample_mixer_kernel, ts=ts, r=r, cd_width=cd_width, dh=dh)
    tab = lambda shape: pl.BlockSpec(shape, lambda i, d: (0,) * len(shape))
    return pl.pallas_call(
        kern,
        out_shape=(
            jax.ShapeDtypeStruct((n_seq * ts, r + cd_width), bf16),
            jax.ShapeDtypeStruct(state_ret.shape, f32),
            jax.ShapeDtypeStruct(state_conv_t.shape, f32),
        ),
        grid_spec=pltpu.PrefetchScalarGridSpec(
            num_scalar_prefetch=1,
            grid=(n_seq // SAMPLE_BLOCK,),
            in_specs=[
                pl.BlockSpec((rows, in_cols), lambda i, d: (blk0 + i, 0)),
                tab((rows, dh)),
                tab((rows, dh)),
                tab((N_RET_HEADS, rows, dh)),
                tab((N_RET_HEADS, rows, dh)),
                tab((3, cd_width)),
                tab((1, r)),
                pl.BlockSpec((SAMPLE_BLOCK, N_RET_HEADS, dh, dh), lambda i, d: (i, 0, 0, 0)),
                pl.BlockSpec((2, SAMPLE_BLOCK, cd_width), lambda i, d: (0, i, 0)),
            ],
            out_specs=(
                pl.BlockSpec((rows, r + cd_width), lambda i, d: (i, 0)),
                pl.BlockSpec((SAMPLE_BLOCK, N_RET_HEADS, dh, dh), lambda i, d: (i, 0, 0, 0)),
                pl.BlockSpec((2, SAMPLE_BLOCK, cd_width), lambda i, d: (0, i, 0)),
            ),
        ),
        compiler_params=_cparams(("arbitrary",), 32),
        name="sample_mixer",
    )(tables["dec"], proj, tables["cos"], tables["sin"], tables["xi"], tables["zeta"], conv_w, ret_gn.reshape(1, r),
      state_ret, state_conv_t)


def _log_gamma():
    return jnp.log1p(-jnp.exp2(-5.0 - jnp.arange(N_RET_HEADS, dtype=f32)))


def _rope_tables(pos, dh):
    half = dh // 2
    inv = ROPE_THETA ** (-jnp.arange(half, dtype=f32) / half)
    ang = pos[:, None] * inv[None, :]
    cos, sin = jnp.cos(ang), jnp.sin(ang)
    return jnp.concatenate([cos, cos], axis=-1), jnp.concatenate([-sin, sin], axis=-1)


def prompt_tables(seq, dh):
    c = RET_CHUNK
    lg = _log_gamma()
    i = jnp.arange(c, dtype=f32)
    diff = i[:, None] - i[None, :]
    decay = jnp.where(diff >= 0, jnp.exp(jnp.maximum(diff, 0.0)[None] * lg[:, None, None]), 0.0)
    xi = jnp.exp((i + 1.0)[None, :] * lg[:, None])
    zeta = jnp.exp((c - 1.0 - i)[None, :] * lg[:, None])
    cos, sin = _rope_tables(jnp.arange(seq, dtype=f32), dh)
    bc = lambda t: jnp.broadcast_to(t[:, :, None], (N_RET_HEADS, c, dh))
    return dict(cos=cos, sin=sin, decay=decay, xi=bc(xi), zeta=bc(zeta), chunk_decay=jnp.exp(c * lg))


def sample_tables(ts, pos0, dh):
    lg = _log_gamma()
    i = jnp.arange(ts, dtype=f32)
    dec = jnp.exp(jnp.arange(ts + 1, dtype=f32)[None, :] * lg[:, None])
    xi = jnp.exp((i + 1.0)[None, :] * lg[:, None])
    zeta = jnp.exp((ts - 1.0 - i)[None, :] * lg[:, None])
    cos, sin = _rope_tables(pos0 + i, dh)
    rep = lambda t: jnp.repeat(t, SAMPLE_BLOCK, axis=0)
    bc = lambda t: jnp.broadcast_to(jnp.repeat(t, SAMPLE_BLOCK, axis=1)[:, :, None],
                                    (N_RET_HEADS, ts * SAMPLE_BLOCK, dh))
    return dict(cos=rep(cos), sin=rep(sin), xi=bc(xi), zeta=bc(zeta), dec=dec.reshape(-1))


def _softmax_rows(s):
    m = jnp.max(s, axis=-1, keepdims=True)
    p = jnp.exp(s - m)
    return p / jnp.sum(p, axis=-1, keepdims=True)


def _cross_prompt_kernel(q_ref, k_ref, v_ref, o_ref, *, dh):
    scale = dh ** -0.5
    for h in range(N_MEM_HEADS):
        cols = slice(h * dh, (h + 1) * dh)
        s = lax.dot_general(q_ref[:, cols].astype(bf16), k_ref[:, cols].astype(bf16), (((1,), (1,)), ((), ())),
                            preferred_element_type=f32) * scale
        p = _softmax_rows(s)
        o_ref[:, cols] = jnp.dot(p.astype(bf16), v_ref[:, cols].astype(bf16), preferred_element_type=f32).astype(bf16)


def cross_prompt(q, mkv, batch, seq, n_mem, d):
    tq = ROW_TILE
    steps = seq // tq
    kern = functools.partial(_cross_prompt_kernel, dh=d // N_MEM_HEADS)
    return pl.pallas_call(
        kern,
        out_shape=jax.ShapeDtypeStruct((batch * seq, d), bf16),
        grid=(batch, steps),
        in_specs=[
            pl.BlockSpec((tq, d), lambda b, j: (b * steps + j, 0)),
            pl.BlockSpec((n_mem, d), lambda b, j: (b, 0)),
            pl.BlockSpec((n_mem, d), lambda b, j: (b, 1)),
        ],
        out_specs=pl.BlockSpec((tq, d), lambda b, j: (b * steps + j, 0)),
        compiler_params=_cparams(("arbitrary", "arbitrary"), 32),
        name="cross_prompt",
    )(q, mkv, mkv)


def _cross_sample_kernel(q_ref, k_ref, v_ref, o_ref, acc_ref, *, dh, seqs):
    part = pl.program_id(1)
    rows = q_ref.shape[0]
    scale = dh ** -0.5

    @pl.when(part == 0)
    def _():
        acc_ref[...] = jnp.zeros_like(acc_ref)

    seq_of_row = lax.broadcasted_iota(i32, (rows, dh), 0) % SAMPLE_BLOCK
    for h in range(N_MEM_HEADS):
        cols = slice(h * dh, (h + 1) * dh)
        qb = q_ref[:, cols].astype(bf16)
        out = jnp.zeros((rows, dh), f32)
        for b in range(seqs):
            s = lax.dot_general(qb, k_ref[b, :, cols].astype(bf16), (((1,), (1,)), ((), ())),
                                preferred_element_type=f32) * scale
            p = _softmax_rows(s)
            o = jnp.dot(p.astype(bf16), v_ref[b, :, cols].astype(bf16), preferred_element_type=f32)
            out = out + jnp.where(seq_of_row == part * seqs + b, o, 0.0)
        acc_ref[:, cols] += out

    @pl.when(part == pl.num_programs(1) - 1)
    def _():
        o_ref[...] = acc_ref[...].astype(bf16)


def cross_sample(q, cache_k, cache_v, row0, ts):
    n_seq, n_mem, d = cache_k.shape
    rows = ts * SAMPLE_BLOCK
    blk0 = row0 // rows
    parts = 2
    seqs = SAMPLE_BLOCK // parts
    kern = functools.partial(_cross_sample_kernel, dh=d // N_MEM_HEADS, seqs=seqs)
    return pl.pallas_call(
        kern,
        out_shape=jax.ShapeDtypeStruct((n_seq * ts, d), bf16),
        grid=(n_seq // SAMPLE_BLOCK, parts),
        in_specs=[
            pl.BlockSpec((rows, d), lambda i, p: (blk0 + i, 0)),
            pl.BlockSpec((seqs, n_mem, d), lambda i, p: (i * parts + p, 0, 0)),
            pl.BlockSpec((seqs, n_mem, d), lambda i, p: (i * parts + p, 0, 0)),
        ],
        out_specs=pl.BlockSpec((rows, d), lambda i, p: (i, 0)),
        scratch_shapes=[pltpu.VMEM((rows, d), f32)],
        compiler_params=_cparams(("arbitrary", "arbitrary"), 40),
        name="cross_sample",
    )(q, cache_k, cache_v)


def _router_kernel(h_ref, g_ref, wr_ref, br_ref, xp_ref, e_ref, gate_ref, rank_ref, cnt_ref, carry_ref,
                   *, tiles_per_chunk, n_exp):
    i = pl.program_id(0)

    @pl.when(i % tiles_per_chunk == 0)
    def _():
        carry_ref[...] = jnp.zeros_like(carry_ref)

    xn = _rms(h_ref[...], g_ref[...])
    tm, d = xn.shape
    xp_ref[...] = xn

    logits = jnp.dot(xn, wr_ref[...], precision=lax.Precision.HIGHEST, preferred_element_type=f32) + br_ref[...]
    lane = lax.broadcasted_iota(i32, (tm, n_exp), 1)
    chosen, vals, hots = [], [], []
    work = logits
    for _ in range(TOP_K):
        m = jnp.max(work, axis=-1, keepdims=True)
        idx = jnp.min(jnp.where(work == m, lane, n_exp), axis=-1, keepdims=True)
        hot = lane == idx
        chosen.append(idx)
        vals.append(m)
        hots.append(hot)
        work = jnp.where(hot, -jnp.inf, work)
    ex = [jnp.exp(v - vals[0]) for v in vals]
    denom = ex[0] + ex[1] + ex[2] + ex[3]
    e_ref[...] = jnp.concatenate(chosen, axis=-1)
    gate_ref[...] = jnp.concatenate([x / denom for x in ex], axis=-1)

    member = jnp.zeros((tm, n_exp), f32)
    for hot in hots:
        member = member + hot.astype(f32)
    earlier = (lax.broadcasted_iota(i32, (tm, tm), 0) > lax.broadcasted_iota(i32, (tm, tm), 1)).astype(bf16)
    before = jnp.dot(earlier, member.astype(bf16), preferred_element_type=f32) + carry_ref[...]
    rank_ref[...] = jnp.concatenate(
        [jnp.sum(jnp.where(hot, before, 0.0), axis=-1, keepdims=True) for hot in hots], axis=-1).astype(i32)
    carry_ref[...] += jnp.sum(member, axis=0, keepdims=True)
    cnt_ref[0] = carry_ref[...].astype(i32)


def router(h, g, w_router, b_router, tc):
    n, d = h.shape
    n_exp = w_router.shape[1]
    tm = ROW_TILE
    while tc % tm:
        tm -= LANES
    tiles_per_chunk = tc // tm
    kern = functools.partial(_router_kernel, tiles_per_chunk=tiles_per_chunk, n_exp=n_exp)
    return pl.pallas_call(
        kern,
        out_shape=(
            jax.ShapeDtypeStruct((n, d), f32),
            jax.ShapeDtypeStruct((n, TOP_K), i32),
            jax.ShapeDtypeStruct((n, TOP_K), f32),
            jax.ShapeDtypeStruct((n, TOP_K), i32),
            jax.ShapeDtypeStruct((n // tc, 1, n_exp), i32),
        ),
        grid=(n // tm,),
        in_specs=[
            pl.BlockSpec((tm, d), lambda i: (i, 0)),
            pl.BlockSpec((1, d), lambda i: (0, 0)),
            pl.BlockSpec((d, n_exp), lambda i: (0, 0)),
            pl.BlockSpec((1, n_exp), lambda i: (0, 0)),
        ],
        out_specs=(
            pl.BlockSpec((tm, d), lambda i: (i, 0)),
            pl.BlockSpec((tm, TOP_K), lambda i: (i, 0)),
            pl.BlockSpec((tm, TOP_K), lambda i: (i, 0)),
            pl.BlockSpec((tm, TOP_K), lambda i: (i, 0)),
            pl.BlockSpec((1, 1, n_exp), lambda i: (i // tiles_per_chunk, 0, 0)),
        ),
        scratch_shapes=[pltpu.VMEM((1, n_exp), f32)],
        compiler_params=_cparams(("arbitrary",), 32),
        name="router",
    )(h, g.reshape(1, d), w_router, b_router.reshape(1, n_exp))


def _dense_row_index(r, tiles):
    return (r // SUBLANES) * tiles * SUBLANES + r % SUBLANES


def _experts_kernel(tok_ref, start_ref, nsub_ref, xp_hbm, gate_ref, wup_ref, bup_ref, wdn_ref, bdn_ref, f_hbm,
                    xs_ref, acc_ref, xt_ref, y_ref, sem_ref, *, tc, n_exp, d):
    c = pl.program_id(0)
    e = pl.program_id(1)
    ms = MOE_SUB
    tiles = d // LANES

    @pl.when(e == 0)
    def _():
        cp = pltpu.make_async_copy(xp_hbm.at[pl.ds(c * tc, tc)], xs_ref.at[pl.ds(0, tc)], sem_ref.at[0])
        cp.start()
        xs_ref[pl.ds(tc, SUBLANES)] = jnp.zeros((SUBLANES, tiles, LANES), f32)
        acc_ref[...] = jnp.zeros_like(acc_ref)
        cp.wait()

    g = c * n_exp + e
    sb0 = start_ref[g]

    def sub_block(i, carry):
        sb = sb0 + i
        row0 = sb * ms
        for r in range(ms):
            xt_ref[pl.ds(_dense_row_index(r, tiles), tiles, stride=SUBLANES), :] = xs_ref[tok_ref[row0 + r]]
        x = jnp.concatenate(
            [jnp.concatenate([xt_ref[pl.ds((rg * tiles + jt) * SUBLANES, SUBLANES), :] for jt in range(tiles)], axis=1)
             for rg in range(ms // SUBLANES)], axis=0).astype(bf16)
        hmid = jnp.dot(x, wup_ref[0], preferred_element_type=f32) + bup_ref[0]
        glu = jnp.minimum(hmid[:, :d], SWIGLU_LIMIT)
        lin = jnp.clip(hmid[:, d:], -SWIGLU_LIMIT, SWIGLU_LIMIT)
        act = glu * jax.nn.sigmoid(SWIGLU_ALPHA * glu) * (lin + 1.0)
        y = jnp.dot(act.astype(bf16), wdn_ref[0], preferred_element_type=f32) + bdn_ref[0]
        gcol = jnp.broadcast_to(gate_ref[pl.ds(sb, 1), :], (ms, ms)).T[:, 0:1]
        y = y * gcol
        for rg in range(ms // SUBLANES):
            for jt in range(tiles):
                y_ref[pl.ds((rg * tiles + jt) * SUBLANES, SUBLANES), :] = (
                    y[rg * SUBLANES:(rg + 1) * SUBLANES, jt * LANES:(jt + 1) * LANES])
        for r0 in range(0, ms, SCATTER_UNROLL):
            toks, sums = [], []
            for r in range(r0, r0 + SCATTER_UNROLL):
                t = tok_ref[row0 + r]
                yrow = y_ref[pl.ds(_dense_row_index(r, tiles), tiles, stride=SUBLANES), :]
                toks.append(t)
                sums.append(acc_ref[t] + yrow)
            for t, s in zip(toks, sums):
                acc_ref[t] = s
        return carry

    lax.fori_loop(0, nsub_ref[g], sub_block, 0)

    @pl.when(e == n_exp - 1)
    def _():
        cp = pltpu.make_async_copy(acc_ref.at[pl.ds(0, tc)], f_hbm.at[pl.ds(c * tc, tc)], sem_ref.at[1])
        cp.start()
        cp.wait()


def experts(row_tok, sb_start, n_sub, xp, row_gate, w_up, b_up, w_down, b_down, tc):
    n, tiles, _ = xp.shape
    d = tiles * LANES
    n_exp = w_up.shape[0]
    kern = functools.partial(_experts_kernel, tc=tc, n_exp=n_exp, d=d)
    return pl.pallas_call(
        kern,
        out_shape=jax.ShapeDtypeStruct((n, tiles, LANES), f32),
        grid_spec=pltpu.PrefetchScalarGridSpec(
            num_scalar_prefetch=3,
            grid=(n // tc, n_exp),
            in_specs=[
                pl.BlockSpec(memory_space=pl.ANY),
                pl.BlockSpec(memory_space=pltpu.VMEM),
                pl.BlockSpec((1, d, 2 * d), lambda c, e, *_: (e, 0, 0)),
                pl.BlockSpec((1, 1, 2 * d), lambda c, e, *_: (e, 0, 0)),
                pl.BlockSpec((1, d, d), lambda c, e, *_: (e, 0, 0)),
                pl.BlockSpec((1, 1, d), lambda c, e, *_: (e, 0, 0)),
            ],
            out_specs=pl.BlockSpec(memory_space=pl.ANY),
            scratch_shapes=[
                pltpu.VMEM((tc + SUBLANES, tiles, LANES), f32),
                pltpu.VMEM((tc + SUBLANES, tiles, LANES), f32),
                pltpu.VMEM((MOE_SUB * tiles, LANES), f32),
                pltpu.VMEM((MOE_SUB * tiles, LANES), f32),
                pltpu.SemaphoreType.DMA((2,)),
            ],
        ),
        compiler_params=_cparams(("arbitrary", "arbitrary"), 58),
        name="experts",
    )(row_tok, sb_start, n_sub, xp, row_gate, w_up, b_up.reshape(n_exp, 1, 2 * d), w_down,
      b_down.reshape(n_exp, 1, d))


def _final_kernel(h_ref, f_ref, g_ref, o_ref):
    tm, d = h_ref.shape
    tiles = d // LANES
    parts = [h_ref[:, jt * LANES:(jt + 1) * LANES] + f_ref[pl.ds(jt, tm, stride=tiles), :] for jt in range(tiles)]
    o_ref[...] = _rms(jnp.concatenate(parts, axis=1), g_ref[...])


def final_norm(h, f2, g, row0, rows):
    d = h.shape[1]
    tiles = d // LANES
    tm = min(ROW_TILE, rows)
    blk0 = row0 // tm
    assert rows % tm == 0 and row0 % tm == 0
    return pl.pallas_call(
        _final_kernel,
        out_shape=jax.ShapeDtypeStruct((rows, d), f32),
        grid=(rows // tm,),
        in_specs=[
            pl.BlockSpec((tm, d), lambda i: (blk0 + i, 0)),
            pl.BlockSpec((tm * tiles, LANES), lambda i: (blk0 + i, 0)),
            pl.BlockSpec((1, d), lambda i: (0, 0)),
        ],
        out_specs=pl.BlockSpec((tm, d), lambda i: (i, 0)),
        compiler_params=_cparams(("arbitrary",), 32),
        name="final_norm",
    )(h, f2, g.reshape(1, d))


def moe(h, g, w_router, b_router, w_up, b_up, w_down, b_down):
    n, d = h.shape
    n_exp = w_router.shape[1]
    tc = n // MOE_CHUNKS
    xp, top_e, gates, rank, counts = router(h, g, w_router, b_router, tc)
    counts = counts.reshape(-1)
    n_sub = (counts + MOE_SUB - 1) // MOE_SUB
    sb_start = jnp.cumsum(n_sub) - n_sub
    total_sub = (n * TOP_K + MOE_CHUNKS * n_exp * (MOE_SUB - 1)) // MOE_SUB
    group = (jnp.arange(n, dtype=i32) // tc)[:, None] * n_exp + top_e
    dest = (sb_start[group] * MOE_SUB + rank).reshape(-1)
    t_local = jnp.broadcast_to((jnp.arange(n, dtype=i32) % tc)[:, None], (n, TOP_K)).reshape(-1)
    row_tok = jnp.full((total_sub * MOE_SUB,), tc, i32).at[dest].set(t_local, unique_indices=True)
    row_gate = jnp.zeros((total_sub * MOE_SUB,), f32).at[dest].set(gates.reshape(-1), unique_indices=True)
    w_up_p = jnp.concatenate([w_up[:, :, 0::2], w_up[:, :, 1::2]], axis=-1).astype(bf16)
    b_up_p = jnp.concatenate([b_up[:, 0::2], b_up[:, 1::2]], axis=-1)
    f = experts(row_tok, sb_start.astype(i32), n_sub.astype(i32), xp.reshape(n, d // LANES, LANES),
                row_gate.reshape(total_sub, MOE_SUB), w_up_p, b_up_p, w_down.astype(bf16), b_down, tc)
    return f.reshape(n * (d // LANES), LANES)


def kernel(x_prompt, x_sample, mem_prompt, state_ret, state_conv, cache_mem_k, cache_mem_v, norm_mix, w_in, conv_w,
           ret_gn, w_out, norm_cross, norm_mem, w_mq, w_mk, w_mv, w_mo, norm_ffn, w_router, b_router, w_up, b_up,
           w_down, b_down, norm_final):
    batch, seq, d = x_prompt.shape
    n_seq, ts, _ = x_sample.shape
    depth = w_in.shape[0]
    n_mem = mem_prompt.shape[1]
    r = ret_gn.shape[1]
    cd_width = conv_w.shape[2]
    dh = r // N_RET_HEADS
    n_p = batch * seq
    n_s = n_seq * ts
    n = n_p + n_s
    nblk = n_seq // SAMPLE_BLOCK
    assert seq % ROW_TILE == 0 and n % ROW_TILE == 0 and n % MOE_CHUNKS == 0

    xs = x_sample.reshape(nblk, SAMPLE_BLOCK, ts, d).transpose(0, 2, 1, 3).reshape(n_s, d)
    h = jnp.concatenate([x_prompt.reshape(n_p, d), xs], axis=0)
    mem2d = mem_prompt.reshape(batch * n_mem, d)
    tab_p = prompt_tables(seq, dh)
    tab_s = sample_tables(ts, float(PAST_LEN), dh)

    ret_p, conv_p, mk_p, mv_p, ret_s, conv_s = [], [], [], [], [], []
    for l in range(depth):
        proj = norm_matmul(h, norm_mix[l], w_in[l].astype(bf16), "in_proj")
        mix_p, s_p, c_p = prompt_mixer(proj, tab_p, conv_w[l], ret_gn[l], batch, seq, r, cd_width)
        mix_s, s_s, c_s = sample_mixer(proj, tab_s, conv_w[l], ret_gn[l], state_ret[l],
                                       state_conv[l].transpose(1, 0, 2), n_p, ts, r, cd_width)
        h = matmul_res(mix_p, mix_s, w_out[l].astype(bf16), h, "out_proj")

        mkv = norm_matmul(mem2d, norm_mem[l], jnp.concatenate([w_mk[l], w_mv[l]], axis=1).astype(bf16), "mem_kv")
        q = norm_matmul(h, norm_cross[l], w_mq[l].astype(bf16), "q_proj")
        attn_p = cross_prompt(q, mkv, batch, seq, n_mem, d)
        attn_s = cross_sample(q, cache_mem_k[l].reshape(n_seq, n_mem, d), cache_mem_v[l].reshape(n_seq, n_mem, d),
                              n_p, ts)
        h = matmul_res(attn_p, attn_s, w_mo[l].astype(bf16), h, "o_proj")

        f2 = moe(h, norm_ffn[l], w_router[l], b_router[l], w_up[l], b_up[l], w_down[l], b_down[l])
        if l + 1 < depth:
            tiles = d // LANES
            h = h + f2.reshape(n, tiles, LANES).reshape(n, d)

        ret_p.append(s_p)
        conv_p.append(c_p)
        mk_p.append(mkv[:, :d].reshape(batch, n_mem, N_MEM_HEADS, d // N_MEM_HEADS))
        mv_p.append(mkv[:, d:].reshape(batch, n_mem, N_MEM_HEADS, d // N_MEM_HEADS))
        ret_s.append(s_s)
        conv_s.append(c_s.transpose(1, 0, 2))

    y_p = final_norm(h, f2, norm_final, 0, n_p).reshape(batch, seq, d)
    y_s = final_norm(h, f2, norm_final, n_p, n_s)
    y_s = y_s.reshape(nblk, ts, SAMPLE_BLOCK, d).transpose(0, 2, 1, 3).reshape(n_seq, ts, d)
    return (y_p, y_s, jnp.stack(ret_p), jnp.stack(conv_p), jnp.stack(mk_p), jnp.stack(mv_p), jnp.stack(ret_s),
            jnp.stack(conv_s))
```

```python
import functools

import jax
import jax.numpy as jnp
from jax import lax
from jax.experimental import pallas as pl
from jax.experimental.pallas import tpu as pltpu

f32 = jnp.float32
bf16 = jnp.bfloat16
i32 = jnp.int32

EPS = 1e-6
ROPE_THETA = 10000.0
RET_CHUNK = 128
PAST_LEN = 16384
N_RET_HEADS = 4
N_MEM_HEADS = 4
TOP_K = 4
SWIGLU_ALPHA = 1.702
SWIGLU_LIMIT = 7.0

LANES = 128
SUBLANES = 8
MIB = 1024 * 1024

ROW_TILE = 512
SAMPLE_BLOCK = 8
MOE_CHUNKS = 4
MOE_SUB = 128
SCATTER_UNROLL = 8


def _cparams(sem, vmem_mib):
    return pltpu.CompilerParams(dimension_semantics=sem, vmem_limit_bytes=vmem_mib * MIB)


def _rms(x, g):
    ms = jnp.mean(x * x, axis=-1, keepdims=True)
    return (x * lax.rsqrt(ms + EPS)) * g


def _norm_matmul_kernel(x_ref, g_ref, w_ref, o_ref):
    xn = _rms(x_ref[...], g_ref[...])
    o_ref[...] = jnp.dot(xn.astype(bf16), w_ref[...], preferred_element_type=f32)


def norm_matmul(x, g, w, name):
    m, d = x.shape
    f = w.shape[1]
    tm = min(ROW_TILE, m)
    assert m % tm == 0
    return pl.pallas_call(
        _norm_matmul_kernel,
        out_shape=jax.ShapeDtypeStruct((m, f), f32),
        grid=(m // tm,),
        in_specs=[
            pl.BlockSpec((tm, d), lambda i: (i, 0)),
            pl.BlockSpec((1, d), lambda i: (0, 0)),
            pl.BlockSpec((d, f), lambda i: (0, 0)),
        ],
        out_specs=pl.BlockSpec((tm, f), lambda i: (i, 0)),
        compiler_params=_cparams(("arbitrary",), 48),
        name=name,
    )(x, g.reshape(1, d), w)


def _matmul_res_kernel(ap_ref, as_ref, w_ref, r_ref, o_ref, *, prompt_tiles):
    a = jnp.where(pl.program_id(0) < prompt_tiles, ap_ref[...], as_ref[...])
    o_ref[...] = r_ref[...] + jnp.dot(a, w_ref[...], preferred_element_type=f32)


def matmul_res(a_prompt, a_sample, w, res, name):
    m_p, d = a_prompt.shape
    m_s = a_sample.shape[0]
    f = w.shape[1]
    tm = ROW_TILE
    assert m_p % tm == 0 and m_s == tm
    prompt_tiles = m_p // tm
    return pl.pallas_call(
        functools.partial(_matmul_res_kernel, prompt_tiles=prompt_tiles),
        out_shape=jax.ShapeDtypeStruct((m_p + m_s, f), f32),
        grid=(prompt_tiles + 1,),
        in_specs=[
            pl.BlockSpec((tm, d), lambda i: (jnp.minimum(i, prompt_tiles - 1), 0)),
            pl.BlockSpec((tm, d), lambda i: (0, 0)),
            pl.BlockSpec((d, f), lambda i: (0, 0)),
            pl.BlockSpec((tm, f), lambda i: (i, 0)),
        ],
        out_specs=pl.BlockSpec((tm, f), lambda i: (i, 0)),
        compiler_params=_cparams(("arbitrary",), 32),
        name=name,
    )(a_prompt, a_sample, w, res)


def _rope(x, cos, sin_signed, half):
    return x * cos + pltpu.roll(x, half, 1) * sin_signed


def _group_norm_gate(o, gate, gn):
    mu = jnp.mean(o, axis=-1, keepdims=True)
    var = jnp.mean(jnp.square(o - mu), axis=-1, keepdims=True)
    on = ((o - mu) * lax.rsqrt(var + EPS)) * gn
    return (gate * jax.nn.sigmoid(gate)) * on


def _prompt_mixer_kernel(cd_ref, proj_ref, cos_ref, sin_ref, decay_ref, xi_ref, zeta_ref, convw_ref, gn_ref,
                         mix_ref, sfin_ref, cfin_ref, s_scr, u_scr, *, tt, r, cd_width, dh):
    j = pl.program_id(1)
    heads = r // dh
    pad = SUBLANES

    @pl.when(j == 0)
    def _():
        s_scr[...] = jnp.zeros_like(s_scr)
        u_scr[0:pad, :] = jnp.zeros((pad, cd_width), f32)

    k_scale = dh ** -0.5
    for c in range(tt // RET_CHUNK):
        rows = pl.ds(c * RET_CHUNK, RET_CHUNK)
        cosb = cos_ref[rows, :]
        sinb = sin_ref[rows, :]
        for h in range(heads):
            lo, hi = h * dh, (h + 1) * dh
            qh = _rope(proj_ref[rows, lo:hi], cosb, sinb, dh // 2)
            kh = _rope(proj_ref[rows, r + lo:r + hi], cosb, sinb, dh // 2) * k_scale
            vb = proj_ref[rows, 2 * r + lo:2 * r + hi].astype(bf16)
            gate = proj_ref[rows, 3 * r + lo:3 * r + hi]
            qb = qh.astype(bf16)
            kb = kh.astype(bf16)
            scores = lax.dot_general(qb, kb, (((1,), (1,)), ((), ())), preferred_element_type=f32) * decay_ref[h]
            inner = jnp.dot(scores.astype(bf16), vb, preferred_element_type=f32)
            s_prev = s_scr[h]
            cross = jnp.dot((qh * xi_ref[h]).astype(bf16), s_prev.astype(bf16), preferred_element_type=f32)
            kv = lax.dot_general((kh * zeta_ref[h]).astype(bf16), vb, (((0,), (0,)), ((), ())),
                                 preferred_element_type=f32)
            s_scr[h] = cd_ref[h] * s_prev + kv
            mix_ref[rows, lo:hi] = _group_norm_gate(inner + cross, gate, gn_ref[:, lo:hi]).astype(bf16)

    base = 4 * r
    u = proj_ref[:, base + cd_width:base + 2 * cd_width] * proj_ref[:, base:base + cd_width]
    u_scr[pad:pad + tt, :] = u
    conv = (u_scr[pad - 2:pad - 2 + tt, :] * convw_ref[0:1, :] + u_scr[pad - 1:pad - 1 + tt, :] * convw_ref[1:2, :]
            + u * convw_ref[2:3, :])
    mix_ref[:, r:r + cd_width] = (proj_ref[:, base + 2 * cd_width:base + 3 * cd_width] * conv).astype(bf16)
    u_scr[0:pad, :] = u_scr[tt:tt + pad, :]
    cfin_ref[0] = u_scr[pad - 2:pad, :]

    @pl.when(j == pl.num_programs(1) - 1)
    def _():
        sfin_ref[0] = s_scr[...]


def prompt_mixer(proj, tables, conv_w, ret_gn, batch, seq, r, cd_width):
    tt = ROW_TILE
    dh = r // N_RET_HEADS
    steps = seq // tt
    in_cols = proj.shape[1]
    kern = functools.partial(_prompt_mixer_kernel, tt=tt, r=r, cd_width=cd_width, dh=dh)
    tab = lambda shape: pl.BlockSpec(shape, lambda b, j, cd: (0,) * len(shape))
    return pl.pallas_call(
        kern,
        out_shape=(
            jax.ShapeDtypeStruct((batch * seq, r + cd_width), bf16),
            jax.ShapeDtypeStruct((batch, N_RET_HEADS, dh, dh), f32),
            jax.ShapeDtypeStruct((batch, 2, cd_width), f32),
        ),
        grid_spec=pltpu.PrefetchScalarGridSpec(
            num_scalar_prefetch=1,
            grid=(batch, steps),
            in_specs=[
                pl.BlockSpec((tt, in_cols), lambda b, j, cd: (b * steps + j, 0)),
                pl.BlockSpec((tt, dh), lambda b, j, cd: (j, 0)),
                pl.BlockSpec((tt, dh), lambda b, j, cd: (j, 0)),
                tab((N_RET_HEADS, RET_CHUNK, RET_CHUNK)),
                tab((N_RET_HEADS, RET_CHUNK, dh)),
                tab((N_RET_HEADS, RET_CHUNK, dh)),
                tab((3, cd_width)),
                tab((1, r)),
            ],
            out_specs=(
                pl.BlockSpec((tt, r + cd_width), lambda b, j, cd: (b * steps + j, 0)),
                pl.BlockSpec((1, N_RET_HEADS, dh, dh), lambda b, j, cd: (b, 0, 0, 0)),
                pl.BlockSpec((1, 2, cd_width), lambda b, j, cd: (b, 0, 0)),
            ),
            scratch_shapes=[
                pltpu.VMEM((N_RET_HEADS, dh, dh), f32),
                pltpu.VMEM((tt + 2 * SUBLANES, cd_width), f32),
            ],
        ),
        compiler_params=_cparams(("arbitrary", "arbitrary"), 48),
        name="prompt_mixer",
    )(tables["chunk_decay"], proj, tables["cos"], tables["sin"], tables["decay"], tables["xi"], tables["zeta"],
      conv_w, ret_gn.reshape(1, r))


def _sample_mixer_kernel(dec_ref, proj_ref, cos_ref, sin_ref, xi_ref, zeta_ref, convw_ref, gn_ref, s0_ref, c0_ref,
                         mix_ref, snew_ref, cnew_ref, *, ts, r, cd_width, dh):
    heads = r // dh
    nb = SAMPLE_BLOCK
    rows = ts * nb
    k_scale = dh ** -0.5
    cosb = cos_ref[...]
    sinb = sin_ref[...]
    seq_of_row = lax.broadcasted_iota(i32, (rows, dh), 0) % nb
    for h in range(heads):
        lo, hi = h * dh, (h + 1) * dh
        qh = _rope(proj_ref[:, lo:hi], cosb, sinb, dh // 2)
        kh = _rope(proj_ref[:, r + lo:r + hi], cosb, sinb, dh // 2) * k_scale
        vh = proj_ref[:, 2 * r + lo:2 * r + hi]
        gate = proj_ref[:, 3 * r + lo:3 * r + hi]
        inner = []
        for t in range(ts):
            qt = qh[t * nb:(t + 1) * nb]
            acc = jnp.zeros((nb, dh), f32)
            for s in range(t + 1):
                sc = jnp.sum(qt * kh[s * nb:(s + 1) * nb], axis=-1, keepdims=True) * dec_ref[h * (ts + 1) + t - s]
                acc = acc + sc * vh[s * nb:(s + 1) * nb]
            inner.append(acc)
        inner = jnp.concatenate(inner, axis=0)
        qx = (qh * xi_ref[h]).astype(bf16)
        kz = kh * zeta_ref[h]
        vb = vh.astype(bf16)
        cross = jnp.zeros((rows, dh), f32)
        for b in range(nb):
            mine = seq_of_row == b
            s_prev = s0_ref[b, h]
            res = jnp.dot(qx, s_prev.astype(bf16), preferred_element_type=f32)
            cross = cross + jnp.where(mine, res, 0.0)
            kv = lax.dot_general(jnp.where(mine, kz, 0.0).astype(bf16), vb, (((0,), (0,)), ((), ())),
                                 preferred_element_type=f32)
            snew_ref[b, h] = dec_ref[h * (ts + 1) + ts] * s_prev + kv
        mix_ref[:, lo:hi] = _group_norm_gate(inner + cross, gate, gn_ref[:, lo:hi]).astype(bf16)

    base = 4 * r
    u = proj_ref[:, base + cd_width:base + 2 * cd_width] * proj_ref[:, base:base + cd_width]
    full = [c0_ref[0], c0_ref[1]] + [u[t * nb:(t + 1) * nb] for t in range(ts)]
    conv = jnp.concatenate(
        [full[t] * convw_ref[0:1, :] + full[t + 1] * convw_ref[1:2, :] + full[t + 2] * convw_ref[2:3, :]
         for t in range(ts)], axis=0)
    mix_ref[:, r:r + cd_width] = (proj_ref[:, base + 2 * cd_width:base + 3 * cd_width] * conv).astype(bf16)
    cnew_ref[0] = full[ts]
    cnew_ref[1] = full[ts + 1]


def sample_mixer(proj, tables, conv_w, ret_gn, state_ret, state_conv_t, row0, ts, r, cd_width):
    n_seq = state_ret.shape[0]
    dh = r // N_RET_HEADS
    rows = ts * SAMPLE_BLOCK
    blk0 = row0 // rows
    assert row0 % rows == 0 and n_seq % SAMPLE_BLOCK == 0
    in_cols = proj.shape[1]
    kern = functools.partial(_sample_mixer_kernel, ts=ts, r=r, cd_width=cd_width, dh=dh)
    tab = lambda shape: pl.BlockSpec(shape, lambda i, d: (0,) * len(shape))
    return pl.pallas_call(
        kern,
        out_shape=(
            jax.ShapeDtypeStruct((n_seq * ts, r + cd_width), bf16),
            jax.ShapeDtypeStruct(state_ret.shape, f32),
            jax.ShapeDtypeStruct(state_conv_t.shape, f32),
        ),
        grid_spec=pltpu.PrefetchScalarGridSpec(
            num_scalar_prefetch=1,
            grid=(n_seq // SAMPLE_BLOCK,),
            in_specs=[
                pl.BlockSpec((rows, in_cols), lambda i, d: (blk0 + i, 0)),
                tab((rows, dh)),
                tab((rows, dh)),
                tab((N_RET_HEADS, rows, dh)),
                tab((N_RET_HEADS, rows, dh)),
                tab((3, cd_width)),
                tab((1, r)),
                pl.BlockSpec((SAMPLE_BLOCK, N_RET_HEADS, dh, dh), lambda i, d: (i, 0, 0, 0)),
                pl.BlockSpec((2, SAMPLE_BLOCK, cd_width), lambda i, d: (0, i, 0)),
            ],
            out_specs=(
                pl.BlockSpec((rows, r + cd_width), lambda i, d: (i, 0)),
                pl.BlockSpec((SAMPLE_BLOCK, N_RET_HEADS, dh, dh), lambda i, d: (i, 0, 0, 0)),
                pl.BlockSpec((2, SAMPLE_BLOCK, cd_width), lambda i, d: (0, i, 0)),
            ),
        ),
        compiler_params=_cparams(("arbitrary",), 32),
        name="sample_mixer",
    )(tables["dec"], proj, tables["cos"], tables["sin"], tables["xi"], tables["zeta"], conv_w, ret_gn.reshape(1, r),
      state_ret, state_conv_t)


def _log_gamma():
    return jnp.log1p(-jnp.exp2(-5.0 - jnp.arange(N_RET_HEADS, dtype=f32)))


def _rope_tables(pos, dh):
    half = dh // 2
    inv = ROPE_THETA ** (-jnp.arange(half, dtype=f32) / half)
    ang = pos[:, None] * inv[None, :]
    cos, sin = jnp.cos(ang), jnp.sin(ang)
    return jnp.concatenate([cos, cos], axis=-1), jnp.concatenate([-sin, sin], axis=-1)


def prompt_tables(seq, dh):
    c = RET_CHUNK
    lg = _log_gamma()
    i = jnp.arange(c, dtype=f32)
    diff = i[:, None] - i[None, :]
    decay = jnp.where(diff >= 0, jnp.exp(jnp.maximum(diff, 0.0)[None] * lg[:, None, None]), 0.0)
    xi = jnp.exp((i + 1.0)[None, :] * lg[:, None])
    zeta = jnp.exp((c - 1.0 - i)[None, :] * lg[:, None])
    cos, sin = _rope_tables(jnp.arange(seq, dtype=f32), dh)
    bc = lambda t: jnp.broadcast_to(t[:, :, None], (N_RET_HEADS, c, dh))
    return dict(cos=cos, sin=sin, decay=decay, xi=bc(xi), zeta=bc(zeta), chunk_decay=jnp.exp(c * lg))


def sample_tables(ts, pos0, dh):
    lg = _log_gamma()
    i = jnp.arange(ts, dtype=f32)
    dec = jnp.exp(jnp.arange(ts + 1, dtype=f32)[None, :] * lg[:, None])
    xi = jnp.exp((i + 1.0)[None, :] * lg[:, None])
    zeta = jnp.exp((ts - 1.0 - i)[None, :] * lg[:, None])
    cos, sin = _rope_tables(pos0 + i, dh)
    rep = lambda t: jnp.repeat(t, SAMPLE_BLOCK, axis=0)
    bc = lambda t: jnp.broadcast_to(jnp.repeat(t, SAMPLE_BLOCK, axis=1)[:, :, None],
                                    (N_RET_HEADS, ts * SAMPLE_BLOCK, dh))
    return dict(cos=rep(cos), sin=rep(sin), xi=bc(xi), zeta=bc(zeta), dec=dec.reshape(-1))


def _softmax_rows(s):
    m = jnp.max(s, axis=-1, keepdims=True)
    p = jnp.exp(s - m)
    return p / jnp.sum(p, axis=-1, keepdims=True)


def _cross_prompt_kernel(q_ref, k_ref, v_ref, o_ref, *, dh):
    scale = dh ** -0.5
    for h in range(N_MEM_HEADS):
        cols = slice(h * dh, (h + 1) * dh)
        s = lax.dot_general(q_ref[:, cols].astype(bf16), k_ref[:, cols].astype(bf16), (((1,), (1,)), ((), ())),
                            preferred_element_type=f32) * scale
        p = _softmax_rows(s)
        o_ref[:, cols] = jnp.dot(p.astype(bf16), v_ref[:, cols].astype(bf16), preferred_element_type=f32).astype(bf16)


def cross_prompt(q, mkv, batch, seq, n_mem, d):
    tq = ROW_TILE
    steps = seq // tq
    kern = functools.partial(_cross_prompt_kernel, dh=d // N_MEM_HEADS)
    return pl.pallas_call(
        kern,
        out_shape=jax.ShapeDtypeStruct((batch * seq, d), bf16),
        grid=(batch, steps),
        in_specs=[
            pl.BlockSpec((tq, d), lambda b, j: (b * steps + j, 0)),
            pl.BlockSpec((n_mem, d), lambda b, j: (b, 0)),
            pl.BlockSpec((n_mem, d), lambda b, j: (b, 1)),
        ],
        out_specs=pl.BlockSpec((tq, d), lambda b, j: (b * steps + j, 0)),
        compiler_params=_cparams(("arbitrary", "arbitrary"), 32),
        name="cross_prompt",
    )(q, mkv, mkv)


def _cross_sample_kernel(q_ref, k_ref, v_ref, o_ref, acc_ref, *, dh, seqs):
    part = pl.program_id(1)
    rows = q_ref.shape[0]
    scale = dh ** -0.5

    @pl.when(part == 0)
    def _():
        acc_ref[...] = jnp.zeros_like(acc_ref)

    seq_of_row = lax.broadcasted_iota(i32, (rows, dh), 0) % SAMPLE_BLOCK
    for h in range(N_MEM_HEADS):
        cols = slice(h * dh, (h + 1) * dh)
        qb = q_ref[:, cols].astype(bf16)
        out = jnp.zeros((rows, dh), f32)
        for b in range(seqs):
            s = lax.dot_general(qb, k_ref[b, :, h, :].astype(bf16), (((1,), (1,)), ((), ())),
                                preferred_element_type=f32) * scale
            p = _softmax_rows(s)
            o = jnp.dot(p.astype(bf16), v_ref[b, :, h, :].astype(bf16), preferred_element_type=f32)
            out = out + jnp.where(seq_of_row == part * seqs + b, o, 0.0)
        acc_ref[:, cols] += out

    @pl.when(part == pl.num_programs(1) - 1)
    def _():
        o_ref[...] = acc_ref[...].astype(bf16)


def cross_sample(q, cache_k, cache_v, row0, ts):
    n_seq, n_mem, heads, dh = cache_k.shape
    d = heads * dh
    rows = ts * SAMPLE_BLOCK
    blk0 = row0 // rows
    parts = 4
    seqs = SAMPLE_BLOCK // parts
    kern = functools.partial(_cross_sample_kernel, dh=dh, seqs=seqs)
    return pl.pallas_call(
        kern,
        out_shape=jax.ShapeDtypeStruct((n_seq * ts, d), bf16),
        grid=(n_seq // SAMPLE_BLOCK, parts),
        in_specs=[
            pl.BlockSpec((rows, d), lambda i, p: (blk0 + i, 0)),
            pl.BlockSpec((seqs, n_mem, heads, dh), lambda i, p: (i * parts + p, 0, 0, 0)),
            pl.BlockSpec((seqs, n_mem, heads, dh), lambda i, p: (i * parts + p, 0, 0, 0)),
        ],
        out_specs=pl.BlockSpec((rows, d), lambda i, p: (i, 0)),
        scratch_shapes=[pltpu.VMEM((rows, d), f32)],
        compiler_params=_cparams(("arbitrary", "arbitrary"), 40),
        name="cross_sample",
    )(q, cache_k, cache_v)


def _router_kernel(h_ref, g_ref, wr_ref, br_ref, xp_ref, e_ref, gate_ref, rank_ref, cnt_ref, carry_ref,
                   *, tiles_per_chunk, n_exp):
    i = pl.program_id(0)

    @pl.when(i % tiles_per_chunk == 0)
    def _():
        carry_ref[...] = jnp.zeros_like(carry_ref)

    xn = _rms(h_ref[...], g_ref[...])
    tm, d = xn.shape
    xp_ref[...] = xn

    logits = jnp.dot(xn, wr_ref[...], precision=lax.Precision.HIGHEST, preferred_element_type=f32) + br_ref[...]
    lane = lax.broadcasted_iota(i32, (tm, n_exp), 1)
    chosen, vals, hots = [], [], []
    work = logits
    for _ in range(TOP_K):
        m = jnp.max(work, axis=-1, keepdims=True)
        idx = jnp.min(jnp.where(work == m, lane, n_exp), axis=-1, keepdims=True)
        hot = lane == idx
        chosen.append(idx)
        vals.append(m)
        hots.append(hot)
        work = jnp.where(hot, -jnp.inf, work)
    ex = [jnp.exp(v - vals[0]) for v in vals]
    denom = ex[0] + ex[1] + ex[2] + ex[3]
    e_ref[...] = jnp.concatenate(chosen, axis=-1)
    gate_ref[...] = jnp.concatenate([x / denom for x in ex], axis=-1)

    member = jnp.zeros((tm, n_exp), f32)
    for hot in hots:
        member = member + hot.astype(f32)
    earlier = (lax.broadcasted_iota(i32, (tm, tm), 0) > lax.broadcasted_iota(i32, (tm, tm), 1)).astype(bf16)
    before = jnp.dot(earlier, member.astype(bf16), preferred_element_type=f32) + carry_ref[...]
    rank_ref[...] = jnp.concatenate(
        [jnp.sum(jnp.where(hot, before, 0.0), axis=-1, keepdims=True) for hot in hots], axis=-1).astype(i32)
    carry_ref[...] += jnp.sum(member, axis=0, keepdims=True)
    cnt_ref[0] = carry_ref[...].astype(i32)


def router(h, g, w_router, b_router, tc):
    n, d = h.shape
    n_exp = w_router.shape[1]
    tm = ROW_TILE
    while tc % tm:
        tm -= LANES
    tiles_per_chunk = tc // tm
    kern = functools.partial(_router_kernel, tiles_per_chunk=tiles_per_chunk, n_exp=n_exp)
    return pl.pallas_call(
        kern,
        out_shape=(
            jax.ShapeDtypeStruct((n, d), f32),
            jax.ShapeDtypeStruct((n, TOP_K), i32),
            jax.ShapeDtypeStruct((n, TOP_K), f32),
            jax.ShapeDtypeStruct((n, TOP_K), i32),
            jax.ShapeDtypeStruct((n // tc, 1, n_exp), i32),
        ),
        grid=(n // tm,),
        in_specs=[
            pl.BlockSpec((tm, d), lambda i: (i, 0)),
            pl.BlockSpec((1, d), lambda i: (0, 0)),
            pl.BlockSpec((d, n_exp), lambda i: (0, 0)),
            pl.BlockSpec((1, n_exp), lambda i: (0, 0)),
        ],
        out_specs=(
            pl.BlockSpec((tm, d), lambda i: (i, 0)),
            pl.BlockSpec((tm, TOP_K), lambda i: (i, 0)),
            pl.BlockSpec((tm, TOP_K), lambda i: (i, 0)),
            pl.BlockSpec((tm, TOP_K), lambda i: (i, 0)),
            pl.BlockSpec((1, 1, n_exp), lambda i: (i // tiles_per_chunk, 0, 0)),
        ),
        scratch_shapes=[pltpu.VMEM((1, n_exp), f32)],
        compiler_params=_cparams(("arbitrary",), 32),
        name="router",
    )(h, g.reshape(1, d), w_router, b_router.reshape(1, n_exp))


def _split_pairs_kernel(w_ref, p_ref, o_ref):
    width = p_ref.shape[0]
    for b in range(w_ref.shape[2] // width):
        cols = slice(b * width, (b + 1) * width)
        o_ref[0, :, cols] = jnp.dot(w_ref[0, :, cols].astype(bf16), p_ref[...],
                                    preferred_element_type=f32).astype(bf16)


def split_pairs(w):
    n_exp, d, f2 = w.shape
    width = 2 * LANES
    j = jnp.arange(width)
    src = jnp.where(j < LANES, 2 * j, 2 * (j - LANES) + 1)
    perm = (jnp.arange(width)[:, None] == src[None, :]).astype(bf16)
    return pl.pallas_call(
        _split_pairs_kernel,
        out_shape=jax.ShapeDtypeStruct((n_exp, d, f2), bf16),
        grid=(n_exp,),
        in_specs=[
            pl.BlockSpec((1, d, f2), lambda e: (e, 0, 0)),
            pl.BlockSpec((width, width), lambda e: (0, 0)),
        ],
        out_specs=pl.BlockSpec((1, d, f2), lambda e: (e, 0, 0)),
        compiler_params=_cparams(("arbitrary",), 40),
        name="split_pairs",
    )(w, perm)


def _dense_row_index(r, tiles):
    return (r // SUBLANES) * tiles * SUBLANES + r % SUBLANES


def _experts_kernel(tok_ref, start_ref, nsub_ref, xp_hbm, gate_ref, wup_ref, bup_ref, wdn_ref, bdn_ref, f_hbm,
                    xs_ref, acc_ref, xt_ref, y_ref, sem_ref, *, tc, n_exp, d):
    c = pl.program_id(0)
    e = pl.program_id(1)
    ms = MOE_SUB
    tiles = d // LANES

    @pl.when(e == 0)
    def _():
        cp = pltpu.make_async_copy(xp_hbm.at[pl.ds(c * tc, tc)], xs_ref.at[pl.ds(0, tc)], sem_ref.at[0])
        cp.start()
        xs_ref[pl.ds(tc, SUBLANES)] = jnp.zeros((SUBLANES, tiles, LANES), f32)
        acc_ref[...] = jnp.zeros_like(acc_ref)
        cp.wait()

    g = c * n_exp + e
    sb0 = start_ref[g]

    def sub_block(i, carry):
        sb = sb0 + i
        row0 = sb * ms
        for r in range(ms):
            xt_ref[pl.ds(_dense_row_index(r, tiles), tiles, stride=SUBLANES), :] = xs_ref[tok_ref[row0 + r]]
        x = jnp.concatenate(
            [jnp.concatenate([xt_ref[pl.ds((rg * tiles + jt) * SUBLANES, SUBLANES), :] for jt in range(tiles)], axis=1)
             for rg in range(ms // SUBLANES)], axis=0).astype(bf16)
        hmid = jnp.dot(x, wup_ref[0], preferred_element_type=f32) + bup_ref[0]
        glu = jnp.concatenate([hmid[:, 2 * jt * LANES:(2 * jt + 1) * LANES] for jt in range(tiles)], axis=1)
        lin = jnp.concatenate([hmid[:, (2 * jt + 1) * LANES:(2 * jt + 2) * LANES] for jt in range(tiles)], axis=1)
        glu = jnp.minimum(glu, SWIGLU_LIMIT)
        lin = jnp.clip(lin, -SWIGLU_LIMIT, SWIGLU_LIMIT)
        act = glu * jax.nn.sigmoid(SWIGLU_ALPHA * glu) * (lin + 1.0)
        y = jnp.dot(act.astype(bf16), wdn_ref[0], preferred_element_type=f32) + bdn_ref[0]
        gcol = jnp.broadcast_to(gate_ref[pl.ds(sb, 1), :], (ms, ms)).T[:, 0:1]
        y = y * gcol
        for rg in range(ms // SUBLANES):
            for jt in range(tiles):
                y_ref[pl.ds((rg * tiles + jt) * SUBLANES, SUBLANES), :] = (
                    y[rg * SUBLANES:(rg + 1) * SUBLANES, jt * LANES:(jt + 1) * LANES])
        for r0 in range(0, ms, SCATTER_UNROLL):
            toks, sums = [], []
            for r in range(r0, r0 + SCATTER_UNROLL):
                t = tok_ref[row0 + r]
                yrow = y_ref[pl.ds(_dense_row_index(r, tiles), tiles, stride=SUBLANES), :]
                toks.append(t)
                sums.append(acc_ref[t] + yrow)
            for t, s in zip(toks, sums):
                acc_ref[t] = s
        return carry

    lax.fori_loop(0, nsub_ref[g], sub_block, 0)

    @pl.when(e == n_exp - 1)
    def _():
        cp = pltpu.make_async_copy(acc_ref.at[pl.ds(0, tc)], f_hbm.at[pl.ds(c * tc, tc)], sem_ref.at[1])
        cp.start()
        cp.wait()


def experts(row_tok, sb_start, n_sub, xp, row_gate, w_up, b_up, w_down, b_down, tc):
    n, tiles, _ = xp.shape
    d = tiles * LANES
    n_exp = w_up.shape[0]
    kern = functools.partial(_experts_kernel, tc=tc, n_exp=n_exp, d=d)
    return pl.pallas_call(
        kern,
        out_shape=jax.ShapeDtypeStruct((n, tiles, LANES), f32),
        grid_spec=pltpu.PrefetchScalarGridSpec(
            num_scalar_prefetch=3,
            grid=(n // tc, n_exp),
            in_specs=[
                pl.BlockSpec(memory_space=pl.ANY),
                pl.BlockSpec(memory_space=pltpu.VMEM),
                pl.BlockSpec((1, d, 2 * d), lambda c, e, *_: (e, 0, 0)),
                pl.BlockSpec((1, 1, 2 * d), lambda c, e, *_: (e, 0, 0)),
                pl.BlockSpec((1, d, d), lambda c, e, *_: (e, 0, 0)),
                pl.BlockSpec((1, 1, d), lambda c, e, *_: (e, 0, 0)),
            ],
            out_specs=pl.BlockSpec(memory_space=pl.ANY),
            scratch_shapes=[
                pltpu.VMEM((tc + SUBLANES, tiles, LANES), f32),
                pltpu.VMEM((tc + SUBLANES, tiles, LANES), f32),
                pltpu.VMEM((MOE_SUB * tiles, LANES), f32),
                pltpu.VMEM((MOE_SUB * tiles, LANES), f32),
                pltpu.SemaphoreType.DMA((2,)),
            ],
        ),
        compiler_params=_cparams(("arbitrary", "arbitrary"), 58),
        name="experts",
    )(row_tok, sb_start, n_sub, xp, row_gate, w_up, b_up.reshape(n_exp, 1, 2 * d), w_down,
      b_down.reshape(n_exp, 1, d))


def _final_kernel(h_ref, f_ref, g_ref, o_ref):
    tm, d = h_ref.shape
    tiles = d // LANES
    parts = [h_ref[:, jt * LANES:(jt + 1) * LANES] + f_ref[pl.ds(jt, tm, stride=tiles), :] for jt in range(tiles)]
    o_ref[...] = _rms(jnp.concatenate(parts, axis=1), g_ref[...])


def final_norm(h, f2, g, row0, rows):
    d = h.shape[1]
    tiles = d // LANES
    tm = min(ROW_TILE, rows)
    blk0 = row0 // tm
    assert rows % tm == 0 and row0 % tm == 0
    return pl.pallas_call(
        _final_kernel,
        out_shape=jax.ShapeDtypeStruct((rows, d), f32),
        grid=(rows // tm,),
        in_specs=[
            pl.BlockSpec((tm, d), lambda i: (blk0 + i, 0)),
            pl.BlockSpec((tm * tiles, LANES), lambda i: (blk0 + i, 0)),
            pl.BlockSpec((1, d), lambda i: (0, 0)),
        ],
        out_specs=pl.BlockSpec((tm, d), lambda i: (i, 0)),
        compiler_params=_cparams(("arbitrary",), 32),
        name="final_norm",
    )(h, f2, g.reshape(1, d))


def moe(h, g, w_router, b_router, w_up, b_up, w_down, b_down):
    n, d = h.shape
    n_exp = w_router.shape[1]
    tc = n // MOE_CHUNKS
    xp, top_e, gates, rank, counts = router(h, g, w_router, b_router, tc)
    counts = counts.reshape(-1)
    n_sub = (counts + MOE_SUB - 1) // MOE_SUB
    sb_start = jnp.cumsum(n_sub) - n_sub
    total_sub = (n * TOP_K + MOE_CHUNKS * n_exp * (MOE_SUB - 1)) // MOE_SUB
    hot = top_e.reshape(MOE_CHUNKS, tc, TOP_K, 1) == jnp.arange(n_exp, dtype=i32)
    row_base = (sb_start * MOE_SUB).reshape(MOE_CHUNKS, 1, 1, n_exp)
    dest = (jnp.sum(jnp.where(hot, row_base, 0), axis=-1).reshape(n, TOP_K) + rank).reshape(-1)
    t_local = jnp.broadcast_to((jnp.arange(n, dtype=i32) % tc)[:, None], (n, TOP_K)).astype(f32)
    vals = jnp.stack([t_local.reshape(-1), gates.reshape(-1)], axis=-1)
    init = jnp.broadcast_to(jnp.array([tc, 0.0], f32), (total_sub * MOE_SUB, 2))
    rows = init.at[dest].set(vals, unique_indices=True)
    row_tok = rows[:, 0].astype(i32)
    row_gate = rows[:, 1]
    tiles = d // LANES
    b_up_p = b_up.reshape(n_exp, tiles, LANES, 2).transpose(0, 1, 3, 2).reshape(n_exp, 2 * d)
    f = experts(row_tok, sb_start.astype(i32), n_sub.astype(i32), xp.reshape(n, tiles, LANES),
                row_gate.reshape(total_sub, MOE_SUB), split_pairs(w_up), b_up_p, w_down.astype(bf16), b_down, tc)
    return f.reshape(n * (d // LANES), LANES)


def kernel(x_prompt, x_sample, mem_prompt, state_ret, state_conv, cache_mem_k, cache_mem_v, norm_mix, w_in, conv_w,
           ret_gn, w_out, norm_cross, norm_mem, w_mq, w_mk, w_mv, w_mo, norm_ffn, w_router, b_router, w_up, b_up,
           w_down, b_down, norm_final):
    batch, seq, d = x_prompt.shape
    n_seq, ts, _ = x_sample.shape
    depth = w_in.shape[0]
    n_mem = mem_prompt.shape[1]
    r = ret_gn.shape[1]
    cd_width = conv_w.shape[2]
    dh = r // N_RET_HEADS
    n_p = batch * seq
    n_s = n_seq * ts
    n = n_p + n_s
    nblk = n_seq // SAMPLE_BLOCK
    assert seq % ROW_TILE == 0 and n % ROW_TILE == 0 and n % MOE_CHUNKS == 0

    xs = x_sample.reshape(nblk, SAMPLE_BLOCK, ts, d).transpose(0, 2, 1, 3).reshape(n_s, d)
    h = jnp.concatenate([x_prompt.reshape(n_p, d), xs], axis=0)
    mem2d = mem_prompt.reshape(batch * n_mem, d)
    tab_p = prompt_tables(seq, dh)
    tab_s = sample_tables(ts, float(PAST_LEN), dh)

    ret_p, conv_p, mk_p, mv_p, ret_s, conv_s = [], [], [], [], [], []
    for l in range(depth):
        proj = norm_matmul(h, norm_mix[l], w_in[l].astype(bf16), "in_proj")
        mix_p, s_p, c_p = prompt_mixer(proj, tab_p, conv_w[l], ret_gn[l], batch, seq, r, cd_width)
        mix_s, s_s, c_s = sample_mixer(proj, tab_s, conv_w[l], ret_gn[l], state_ret[l],
                                       state_conv[l].transpose(1, 0, 2), n_p, ts, r, cd_width)
        h = matmul_res(mix_p, mix_s, w_out[l].astype(bf16), h, "out_proj")

        mkv = norm_matmul(mem2d, norm_mem[l], jnp.concatenate([w_mk[l], w_mv[l]], axis=1).astype(bf16), "mem_kv")
        q = norm_matmul(h, norm_cross[l], w_mq[l].astype(bf16), "q_proj")
        attn_p = cross_prompt(q, mkv, batch, seq, n_mem, d)
        attn_s = cross_sample(q, cache_mem_k[l], cache_mem_v[l], n_p, ts)
        h = matmul_res(attn_p, attn_s, w_mo[l].astype(bf16), h, "o_proj")

        f2 = moe(h, norm_ffn[l], w_router[l], b_router[l], w_up[l], b_up[l], w_down[l], b_down[l])
        if l + 1 < depth:
            tiles = d // LANES
            h = h + f2.reshape(n, tiles, LANES).reshape(n, d)

        ret_p.append(s_p)
        conv_p.append(c_p)
        mk_p.append(mkv[:, :d].reshape(batch, n_mem, N_MEM_HEADS, d // N_MEM_HEADS))
        mv_p.append(mkv[:, d:].reshape(batch, n_mem, N_MEM_HEADS, d // N_MEM_HEADS))
        ret_s.append(s_s)
        conv_s.append(c_s.transpose(1, 0, 2))

    y_p = final_norm(h, f2, norm_final, 0, n_p).reshape(batch, seq, d)
    y_s = final_norm(h, f2, norm_final, n_p, n_s)
    y_s = y_s.reshape(nblk, ts, SAMPLE_BLOCK, d).transpose(0, 2, 1, 3).reshape(n_seq, ts, d)
    return (y_p, y_s, jnp.stack(ret_p), jnp.stack(conv_p), jnp.stack(mk_p), jnp.stack(mv_p), jnp.stack(ret_s),
            jnp.stack(conv_s))
```

```python
import functools

import jax
import jax.numpy as jnp
from jax import lax
from jax.experimental import pallas as pl
from jax.experimental.pallas import tpu as pltpu

f32 = jnp.float32
bf16 = jnp.bfloat16
i32 = jnp.int32

EPS = 1e-6
ROPE_THETA = 10000.0
RET_CHUNK = 128
PAST_LEN = 16384
N_RET_HEADS = 4
N_MEM_HEADS = 4
TOP_K = 4
SWIGLU_ALPHA = 1.702
SWIGLU_LIMIT = 7.0

LANES = 128
SUBLANES = 8
MIB = 1024 * 1024

ROW_TILE = 512
SAMPLE_BLOCK = 8
MOE_CHUNKS = 4
MOE_SUB = 128
SCATTER_UNROLL = 8


def _cparams(sem, vmem_mib):
    return pltpu.CompilerParams(dimension_semantics=sem, vmem_limit_bytes=vmem_mib * MIB)


def _rms(x, g):
    ms = jnp.mean(x * x, axis=-1, keepdims=True)
    return (x * lax.rsqrt(ms + EPS)) * g


def _norm_matmul_kernel(x_ref, g_ref, w_ref, o_ref):
    xn = _rms(x_ref[...], g_ref[...])
    o_ref[...] = jnp.dot(xn.astype(bf16), w_ref[...], preferred_element_type=f32)


def norm_matmul(x, g, w, name):
    m, d = x.shape
    f = w.shape[1]
    tm = min(ROW_TILE, m)
    assert m % tm == 0
    return pl.pallas_call(
        _norm_matmul_kernel,
        out_shape=jax.ShapeDtypeStruct((m, f), f32),
        grid=(m // tm,),
        in_specs=[
            pl.BlockSpec((tm, d), lambda i: (i, 0)),
            pl.BlockSpec((1, d), lambda i: (0, 0)),
            pl.BlockSpec((d, f), lambda i: (0, 0)),
        ],
        out_specs=pl.BlockSpec((tm, f), lambda i: (i, 0)),
        compiler_params=_cparams(("arbitrary",), 48),
        name=name,
    )(x, g.reshape(1, d), w)


def _matmul_res_kernel(ap_ref, as_ref, w_ref, r_ref, o_ref, *, prompt_tiles):
    a = jnp.where(pl.program_id(0) < prompt_tiles, ap_ref[...], as_ref[...])
    o_ref[...] = r_ref[...] + jnp.dot(a, w_ref[...], preferred_element_type=f32)


def matmul_res(a_prompt, a_sample, w, res, name):
    m_p, d = a_prompt.shape
    m_s = a_sample.shape[0]
    f = w.shape[1]
    tm = ROW_TILE
    assert m_p % tm == 0 and m_s == tm
    prompt_tiles = m_p // tm
    return pl.pallas_call(
        functools.partial(_matmul_res_kernel, prompt_tiles=prompt_tiles),
        out_shape=jax.ShapeDtypeStruct((m_p + m_s, f), f32),
        grid=(prompt_tiles + 1,),
        in_specs=[
            pl.BlockSpec((tm, d), lambda i: (jnp.minimum(i, prompt_tiles - 1), 0)),
            pl.BlockSpec((tm, d), lambda i: (0, 0)),
            pl.BlockSpec((d, f), lambda i: (0, 0)),
            pl.BlockSpec((tm, f), lambda i: (i, 0)),
        ],
        out_specs=pl.BlockSpec((tm, f), lambda i: (i, 0)),
        compiler_params=_cparams(("arbitrary",), 32),
        name=name,
    )(a_prompt, a_sample, w, res)


def _rope(x, cos, sin_signed, half):
    return x * cos + pltpu.roll(x, half, 1) * sin_signed


def _group_norm_gate(o, gate, gn):
    mu = jnp.mean(o, axis=-1, keepdims=True)
    var = jnp.mean(jnp.square(o - mu), axis=-1, keepdims=True)
    on = ((o - mu) * lax.rsqrt(var + EPS)) * gn
    return (gate * jax.nn.sigmoid(gate)) * on


def _prompt_mixer_kernel(cd_ref, proj_ref, cos_ref, sin_ref, decay_ref, xi_ref, zeta_ref, convw_ref, gn_ref,
                         mix_ref, sfin_ref, cfin_ref, s_scr, u_scr, *, tt, r, cd_width, dh):
    j = pl.program_id(1)
    heads = r // dh
    pad = SUBLANES

    @pl.when(j == 0)
    def _():
        s_scr[...] = jnp.zeros_like(s_scr)
        u_scr[0:pad, :] = jnp.zeros((pad, cd_width), f32)

    k_scale = dh ** -0.5
    for c in range(tt // RET_CHUNK):
        rows = pl.ds(c * RET_CHUNK, RET_CHUNK)
        cosb = cos_ref[rows, :]
        sinb = sin_ref[rows, :]
        for h in range(heads):
            lo, hi = h * dh, (h + 1) * dh
            qh = _rope(proj_ref[rows, lo:hi], cosb, sinb, dh // 2)
            kh = _rope(proj_ref[rows, r + lo:r + hi], cosb, sinb, dh // 2) * k_scale
            vb = proj_ref[rows, 2 * r + lo:2 * r + hi].astype(bf16)
            gate = proj_ref[rows, 3 * r + lo:3 * r + hi]
            qb = qh.astype(bf16)
            kb = kh.astype(bf16)
            scores = lax.dot_general(qb, kb, (((1,), (1,)), ((), ())), preferred_element_type=f32) * decay_ref[h]
            inner = jnp.dot(scores.astype(bf16), vb, preferred_element_type=f32)
            s_prev = s_scr[h]
            cross = jnp.dot((qh * xi_ref[h]).astype(bf16), s_prev.astype(bf16), preferred_element_type=f32)
            kv = lax.dot_general((kh * zeta_ref[h]).astype(bf16), vb, (((0,), (0,)), ((), ())),
                                 preferred_element_type=f32)
            s_scr[h] = cd_ref[h] * s_prev + kv
            mix_ref[rows, lo:hi] = _group_norm_gate(inner + cross, gate, gn_ref[:, lo:hi]).astype(bf16)

    base = 4 * r
    u = proj_ref[:, base + cd_width:base + 2 * cd_width] * proj_ref[:, base:base + cd_width]
    u_scr[pad:pad + tt, :] = u
    conv = (u_scr[pad - 2:pad - 2 + tt, :] * convw_ref[0:1, :] + u_scr[pad - 1:pad - 1 + tt, :] * convw_ref[1:2, :]
            + u * convw_ref[2:3, :])
    mix_ref[:, r:r + cd_width] = (proj_ref[:, base + 2 * cd_width:base + 3 * cd_width] * conv).astype(bf16)
    u_scr[0:pad, :] = u_scr[tt:tt + pad, :]
    cfin_ref[0] = u_scr[pad - 2:pad, :]

    @pl.when(j == pl.num_programs(1) - 1)
    def _():
        sfin_ref[0] = s_scr[...]


def prompt_mixer(proj, tables, conv_w, ret_gn, batch, seq, r, cd_width):
    tt = ROW_TILE
    dh = r // N_RET_HEADS
    steps = seq // tt
    in_cols = proj.shape[1]
    kern = functools.partial(_prompt_mixer_kernel, tt=tt, r=r, cd_width=cd_width, dh=dh)
    tab = lambda shape: pl.BlockSpec(shape, lambda b, j, cd: (0,) * len(shape))
    return pl.pallas_call(
        kern,
        out_shape=(
            jax.ShapeDtypeStruct((batch * seq, r + cd_width), bf16),
            jax.ShapeDtypeStruct((batch, N_RET_HEADS, dh, dh), f32),
            jax.ShapeDtypeStruct((batch, 2, cd_width), f32),
        ),
        grid_spec=pltpu.PrefetchScalarGridSpec(
            num_scalar_prefetch=1,
            grid=(batch, steps),
            in_specs=[
                pl.BlockSpec((tt, in_cols), lambda b, j, cd: (b * steps + j, 0)),
                pl.BlockSpec((tt, dh), lambda b, j, cd: (j, 0)),
                pl.BlockSpec((tt, dh), lambda b, j, cd: (j, 0)),
                tab((N_RET_HEADS, RET_CHUNK, RET_CHUNK)),
                tab((N_RET_HEADS, RET_CHUNK, dh)),
                tab((N_RET_HEADS, RET_CHUNK, dh)),
                tab((3, cd_width)),
                tab((1, r)),
            ],
            out_specs=(
                pl.BlockSpec((tt, r + cd_width), lambda b, j, cd: (b * steps + j, 0)),
                pl.BlockSpec((1, N_RET_HEADS, dh, dh), lambda b, j, cd: (b, 0, 0, 0)),
                pl.BlockSpec((1, 2, cd_width), lambda b, j, cd: (b, 0, 0)),
            ),
            scratch_shapes=[
                pltpu.VMEM((N_RET_HEADS, dh, dh), f32),
                pltpu.VMEM((tt + 2 * SUBLANES, cd_width), f32),
            ],
        ),
        compiler_params=_cparams(("arbitrary", "arbitrary"), 48),
        name="prompt_mixer",
    )(tables["chunk_decay"], proj, tables["cos"], tables["sin"], tables["decay"], tables["xi"], tables["zeta"],
      conv_w, ret_gn.reshape(1, r))


def _sample_mixer_kernel(dec_ref, proj_ref, cos_ref, sin_ref, xi_ref, zeta_ref, convw_ref, gn_ref, s0_ref, c0_ref,
                         mix_ref, snew_ref, cnew_ref, *, ts, r, cd_width, dh):
    heads = r // dh
    nb = SAMPLE_BLOCK
    rows = ts * nb
    k_scale = dh ** -0.5
    cosb = cos_ref[...]
    sinb = sin_ref[...]
    seq_of_row = lax.broadcasted_iota(i32, (rows, dh), 0) % nb
    for h in range(heads):
        lo, hi = h * dh, (h + 1) * dh
        qh = _rope(proj_ref[:, lo:hi], cosb, sinb, dh // 2)
        kh = _rope(proj_ref[:, r + lo:r + hi], cosb, sinb, dh // 2) * k_scale
        vh = proj_ref[:, 2 * r + lo:2 * r + hi]
        gate = proj_ref[:, 3 * r + lo:3 * r + hi]
        inner = []
        for t in range(ts):
            qt = qh[t * nb:(t + 1) * nb]
            acc = jnp.zeros((nb, dh), f32)
            for s in range(t + 1):
                sc = jnp.sum(qt * kh[s * nb:(s + 1) * nb], axis=-1, keepdims=True) * dec_ref[h * (ts + 1) + t - s]
                acc = acc + sc * vh[s * nb:(s + 1) * nb]
            inner.append(acc)
        inner = jnp.concatenate(inner, axis=0)
        qx = (qh * xi_ref[h]).astype(bf16)
        kz = kh * zeta_ref[h]
        vb = vh.astype(bf16)
        cross = jnp.zeros((rows, dh), f32)
        for b in range(nb):
            mine = seq_of_row == b
            s_prev = s0_ref[b, h]
            res = jnp.dot(qx, s_prev.astype(bf16), preferred_element_type=f32)
            cross = cross + jnp.where(mine, res, 0.0)
            kv = lax.dot_general(jnp.where(mine, kz, 0.0).astype(bf16), vb, (((0,), (0,)), ((), ())),
                                 preferred_element_type=f32)
            snew_ref[b, h] = dec_ref[h * (ts + 1) + ts] * s_prev + kv
        mix_ref[:, lo:hi] = _group_norm_gate(inner + cross, gate, gn_ref[:, lo:hi]).astype(bf16)

    base = 4 * r
    u = proj_ref[:, base + cd_width:base + 2 * cd_width] * proj_ref[:, base:base + cd_width]
    full = [c0_ref[0], c0_ref[1]] + [u[t * nb:(t + 1) * nb] for t in range(ts)]
    conv = jnp.concatenate(
        [full[t] * convw_ref[0:1, :] + full[t + 1] * convw_ref[1:2, :] + full[t + 2] * convw_ref[2:3, :]
         for t in range(ts)], axis=0)
    mix_ref[:, r:r + cd_width] = (proj_ref[:, base + 2 * cd_width:base + 3 * cd_width] * conv).astype(bf16)
    cnew_ref[0] = full[ts]
    cnew_ref[1] = full[ts + 1]


def sample_mixer(proj, tables, conv_w, ret_gn, state_ret, state_conv_t, row0, ts, r, cd_width):
    n_seq = state_ret.shape[0]
    dh = r // N_RET_HEADS
    rows = ts * SAMPLE_BLOCK
    blk0 = row0 // rows
    assert row0 % rows == 0 and n_seq % SAMPLE_BLOCK == 0
    in_cols = proj.shape[1]
    kern = functools.partial(_sample_mixer_kernel, ts=ts, r=r, cd_width=cd_width, dh=dh)
    tab = lambda shape: pl.BlockSpec(shape, lambda i, d: (0,) * len(shape))
    return pl.pallas_call(
        kern,
        out_shape=(
            jax.ShapeDtypeStruct((n_seq * ts, r + cd_width), bf16),
            jax.ShapeDtypeStruct(state_ret.shape, f32),
            jax.ShapeDtypeStruct(state_conv_t.shape, f32),
        ),
        grid_spec=pltpu.PrefetchScalarGridSpec(
            num_scalar_prefetch=1,
            grid=(n_seq // SAMPLE_BLOCK,),
            in_specs=[
                pl.BlockSpec((rows, in_cols), lambda i, d: (blk0 + i, 0)),
                tab((rows, dh)),
                tab((rows, dh)),
                tab((N_RET_HEADS, rows, dh)),
                tab((N_RET_HEADS, rows, dh)),
                tab((3, cd_width)),
                tab((1, r)),
                pl.BlockSpec((SAMPLE_BLOCK, N_RET_HEADS, dh, dh), lambda i, d: (i, 0, 0, 0)),
                pl.BlockSpec((2, SAMPLE_BLOCK, cd_width), lambda i, d: (0, i, 0)),
            ],
            out_specs=(
                pl.BlockSpec((rows, r + cd_width), lambda i, d: (i, 0)),
                pl.BlockSpec((SAMPLE_BLOCK, N_RET_HEADS, dh, dh), lambda i, d: (i, 0, 0, 0)),
                pl.BlockSpec((2, SAMPLE_BLOCK, cd_width), lambda i, d: (0, i, 0)),
            ),
        ),
        compiler_params=_cparams(("arbitrary",), 32),
        name="sample_mixer",
    )(tables["dec"], proj, tables["cos"], tables["sin"], tables["xi"], tables["zeta"], conv_w, ret_gn.reshape(1, r),
      state_ret, state_conv_t)


def _log_gamma():
    return jnp.log1p(-jnp.exp2(-5.0 - jnp.arange(N_RET_HEADS, dtype=f32)))


def _rope_tables(pos, dh):
    half = dh // 2
    inv = ROPE_THETA ** (-jnp.arange(half, dtype=f32) / half)
    ang = pos[:, None] * inv[None, :]
    cos, sin = jnp.cos(ang), jnp.sin(ang)
    return jnp.concatenate([cos, cos], axis=-1), jnp.concatenate([-sin, sin], axis=-1)


def prompt_tables(seq, dh):
    c = RET_CHUNK
    lg = _log_gamma()
    i = jnp.arange(c, dtype=f32)
    diff = i[:, None] - i[None, :]
    decay = jnp.where(diff >= 0, jnp.exp(jnp.maximum(diff, 0.0)[None] * lg[:, None, None]), 0.0)
    xi = jnp.exp((i + 1.0)[None, :] * lg[:, None])
    zeta = jnp.exp((c - 1.0 - i)[None, :] * lg[:, None])
    cos, sin = _rope_tables(jnp.arange(seq, dtype=f32), dh)
    bc = lambda t: jnp.broadcast_to(t[:, :, None], (N_RET_HEADS, c, dh))
    return dict(cos=cos, sin=sin, decay=decay, xi=bc(xi), zeta=bc(zeta), chunk_decay=jnp.exp(c * lg))


def sample_tables(ts, pos0, dh):
    lg = _log_gamma()
    i = jnp.arange(ts, dtype=f32)
    dec = jnp.exp(jnp.arange(ts + 1, dtype=f32)[None, :] * lg[:, None])
    xi = jnp.exp((i + 1.0)[None, :] * lg[:, None])
    zeta = jnp.exp((ts - 1.0 - i)[None, :] * lg[:, None])
    cos, sin = _rope_tables(pos0 + i, dh)
    rep = lambda t: jnp.repeat(t, SAMPLE_BLOCK, axis=0)
    bc = lambda t: jnp.broadcast_to(jnp.repeat(t, SAMPLE_BLOCK, axis=1)[:, :, None],
                                    (N_RET_HEADS, ts * SAMPLE_BLOCK, dh))
    return dict(cos=rep(cos), sin=rep(sin), xi=bc(xi), zeta=bc(zeta), dec=dec.reshape(-1))


def _softmax_rows(s):
    m = jnp.max(s, axis=-1, keepdims=True)
    p = jnp.exp(s - m)
    return p / jnp.sum(p, axis=-1, keepdims=True)


def _cross_prompt_kernel(q_ref, k_ref, v_ref, o_ref, *, dh):
    scale = dh ** -0.5
    for h in range(N_MEM_HEADS):
        cols = slice(h * dh, (h + 1) * dh)
        s = lax.dot_general(q_ref[:, cols].astype(bf16), k_ref[:, cols].astype(bf16), (((1,), (1,)), ((), ())),
                            preferred_element_type=f32) * scale
        p = _softmax_rows(s)
        o_ref[:, cols] = jnp.dot(p.astype(bf16), v_ref[:, cols].astype(bf16), preferred_element_type=f32).astype(bf16)


def cross_prompt(q, mkv, batch, seq, n_mem, d):
    tq = ROW_TILE
    steps = seq // tq
    kern = functools.partial(_cross_prompt_kernel, dh=d // N_MEM_HEADS)
    return pl.pallas_call(
        kern,
        out_shape=jax.ShapeDtypeStruct((batch * seq, d), bf16),
        grid=(batch, steps),
        in_specs=[
            pl.BlockSpec((tq, d), lambda b, j: (b * steps + j, 0)),
            pl.BlockSpec((n_mem, d), lambda b, j: (b, 0)),
            pl.BlockSpec((n_mem, d), lambda b, j: (b, 1)),
        ],
        out_specs=pl.BlockSpec((tq, d), lambda b, j: (b * steps + j, 0)),
        compiler_params=_cparams(("arbitrary", "arbitrary"), 32),
        name="cross_prompt",
    )(q, mkv, mkv)


def _cross_sample_kernel(q_ref, k_ref, v_ref, o_ref, acc_ref, *, dh, seqs):
    part = pl.program_id(1)
    rows = q_ref.shape[0]
    scale = dh ** -0.5

    @pl.when(part == 0)
    def _():
        acc_ref[...] = jnp.zeros_like(acc_ref)

    heads = N_MEM_HEADS
    n_mem = k_ref.shape[1]
    qx = jnp.concatenate([q_ref[:, h * dh:(h + 1) * dh] for h in range(heads)], axis=0).astype(bf16)
    row = lax.broadcasted_iota(i32, (heads * rows, n_mem * heads), 0)
    col = lax.broadcasted_iota(i32, (heads * rows, n_mem * heads), 1)
    same_head = col % heads == row // rows
    seq_of_row = lax.broadcasted_iota(i32, (heads * rows, dh), 0) % SAMPLE_BLOCK
    out = jnp.zeros((heads * rows, dh), f32)
    for b in range(seqs):
        k2 = k_ref[b].reshape(n_mem * heads, dh).astype(bf16)
        v2 = v_ref[b].reshape(n_mem * heads, dh).astype(bf16)
        s = lax.dot_general(qx, k2, (((1,), (1,)), ((), ())), preferred_element_type=f32) * scale
        p = _softmax_rows(jnp.where(same_head, s, -jnp.inf))
        o = jnp.dot(p.astype(bf16), v2, preferred_element_type=f32)
        out = out + jnp.where(seq_of_row == part * seqs + b, o, 0.0)
    acc_ref[...] += out

    @pl.when(part == pl.num_programs(1) - 1)
    def _():
        for h in range(heads):
            o_ref[:, h * dh:(h + 1) * dh] = acc_ref[h * rows:(h + 1) * rows, :].astype(bf16)


def cross_sample(q, cache_k, cache_v, row0, ts):
    n_seq, n_mem, heads, dh = cache_k.shape
    d = heads * dh
    rows = ts * SAMPLE_BLOCK
    blk0 = row0 // rows
    parts = 4
    seqs = SAMPLE_BLOCK // parts
    kern = functools.partial(_cross_sample_kernel, dh=dh, seqs=seqs)
    return pl.pallas_call(
        kern,
        out_shape=jax.ShapeDtypeStruct((n_seq * ts, d), bf16),
        grid=(n_seq // SAMPLE_BLOCK, parts),
        in_specs=[
            pl.BlockSpec((rows, d), lambda i, p: (blk0 + i, 0)),
            pl.BlockSpec((seqs, n_mem, heads, dh), lambda i, p: (i * parts + p, 0, 0, 0)),
            pl.BlockSpec((seqs, n_mem, heads, dh), lambda i, p: (i * parts + p, 0, 0, 0)),
        ],
        out_specs=pl.BlockSpec((rows, d), lambda i, p: (i, 0)),
        scratch_shapes=[pltpu.VMEM((heads * rows, dh), f32)],
        compiler_params=_cparams(("arbitrary", "arbitrary"), 40),
        name="cross_sample",
    )(q, cache_k, cache_v)


def _router_kernel(h_ref, g_ref, wr_ref, br_ref, xp_ref, e_ref, gate_ref, rank_ref, cnt_ref, carry_ref,
                   *, tiles_per_chunk, n_exp):
    i = pl.program_id(0)

    @pl.when(i % tiles_per_chunk == 0)
    def _():
        carry_ref[...] = jnp.zeros_like(carry_ref)

    xn = _rms(h_ref[...], g_ref[...])
    tm, d = xn.shape
    xp_ref[...] = xn

    w = wr_ref[...]
    w_hi = w.astype(bf16)
    w_lo = (w - w_hi.astype(f32)).astype(bf16)
    x_hi = xn.astype(bf16)
    x_lo = (xn - x_hi.astype(f32)).astype(bf16)
    nt = (((1,), (1,)), ((), ()))
    both = lax.dot_general(jnp.concatenate([w_hi, w_lo], axis=0), x_hi, nt, preferred_element_type=f32)
    logits = (both[:n_exp] + both[n_exp:] + lax.dot_general(w_hi, x_lo, nt, preferred_element_type=f32)
              + br_ref[...])
    sub = lax.broadcasted_iota(i32, (n_exp, tm), 0).astype(f32)
    chosen, vals, hots = [], [], []
    work = logits
    for _ in range(TOP_K):
        m = jnp.max(work, axis=0, keepdims=True)
        idx = jnp.min(jnp.where(work == m, sub, float(n_exp)), axis=0, keepdims=True)
        hot = sub == idx
        chosen.append(idx)
        vals.append(m)
        hots.append(hot)
        work = jnp.where(hot, -jnp.inf, work)
    ex = [jnp.exp(v - vals[0]) for v in vals]
    denom = ex[0] + ex[1] + ex[2] + ex[3]
    e_ref[...] = jnp.concatenate(chosen, axis=0).astype(i32)
    gate_ref[...] = jnp.concatenate([x / denom for x in ex], axis=0)

    member = jnp.zeros((n_exp, tm), f32)
    for hot in hots:
        member = member + hot.astype(f32)
    earlier = (lax.broadcasted_iota(i32, (tm, tm), 0) < lax.broadcasted_iota(i32, (tm, tm), 1)).astype(bf16)
    before = jnp.dot(member.astype(bf16), earlier, preferred_element_type=f32) + carry_ref[...]
    rank_ref[...] = jnp.concatenate(
        [jnp.sum(jnp.where(hot, before, 0.0), axis=0, keepdims=True) for hot in hots], axis=0).astype(i32)
    carry_ref[...] += jnp.sum(member, axis=1, keepdims=True)
    cnt_ref[0] = carry_ref[...].astype(i32)


def router(h, g, w_router, b_router, tc):
    n, d = h.shape
    n_exp = w_router.shape[1]
    tm = ROW_TILE
    while tc % tm:
        tm -= LANES
    tiles_per_chunk = tc // tm
    kern = functools.partial(_router_kernel, tiles_per_chunk=tiles_per_chunk, n_exp=n_exp)
    return pl.pallas_call(
        kern,
        out_shape=(
            jax.ShapeDtypeStruct((n, d), f32),
            jax.ShapeDtypeStruct((TOP_K, n), i32),
            jax.ShapeDtypeStruct((TOP_K, n), f32),
            jax.ShapeDtypeStruct((TOP_K, n), i32),
            jax.ShapeDtypeStruct((n // tc, n_exp, 1), i32),
        ),
        grid=(n // tm,),
        in_specs=[
            pl.BlockSpec((tm, d), lambda i: (i, 0)),
            pl.BlockSpec((1, d), lambda i: (0, 0)),
            pl.BlockSpec((n_exp, d), lambda i: (0, 0)),
            pl.BlockSpec((n_exp, 1), lambda i: (0, 0)),
        ],
        out_specs=(
            pl.BlockSpec((tm, d), lambda i: (i, 0)),
            pl.BlockSpec((TOP_K, tm), lambda i: (0, i)),
            pl.BlockSpec((TOP_K, tm), lambda i: (0, i)),
            pl.BlockSpec((TOP_K, tm), lambda i: (0, i)),
            pl.BlockSpec((1, n_exp, 1), lambda i: (i // tiles_per_chunk, 0, 0)),
        ),
        scratch_shapes=[pltpu.VMEM((n_exp, 1), f32)],
        compiler_params=_cparams(("arbitrary",), 32),
        name="router",
    )(h, g.reshape(1, d), w_router.T, b_router.reshape(n_exp, 1))


def _split_pairs_kernel(w_ref, p_ref, o_ref):
    width = p_ref.shape[0]
    for b in range(w_ref.shape[2] // width):
        cols = slice(b * width, (b + 1) * width)
        o_ref[0, :, cols] = jnp.dot(w_ref[0, :, cols].astype(bf16), p_ref[...],
                                    preferred_element_type=f32).astype(bf16)


def split_pairs(w):
    n_exp, d, f2 = w.shape
    width = 2 * LANES
    j = jnp.arange(width)
    src = jnp.where(j < LANES, 2 * j, 2 * (j - LANES) + 1)
    perm = (jnp.arange(width)[:, None] == src[None, :]).astype(bf16)
    return pl.pallas_call(
        _split_pairs_kernel,
        out_shape=jax.ShapeDtypeStruct((n_exp, d, f2), bf16),
        grid=(n_exp,),
        in_specs=[
            pl.BlockSpec((1, d, f2), lambda e: (e, 0, 0)),
            pl.BlockSpec((width, width), lambda e: (0, 0)),
        ],
        out_specs=pl.BlockSpec((1, d, f2), lambda e: (e, 0, 0)),
        compiler_params=_cparams(("arbitrary",), 40),
        name="split_pairs",
    )(w, perm)


def _dense_row_index(r, tiles):
    return (r // SUBLANES) * tiles * SUBLANES + r % SUBLANES


def _experts_kernel(tok_ref, start_ref, nsub_ref, xp_hbm, gate_ref, wup_ref, bup_ref, wdn_ref, bdn_ref, f_hbm,
                    xs_ref, acc_ref, xt_ref, y_ref, sem_ref, *, tc, n_exp, d):
    c = pl.program_id(0)
    e = pl.program_id(1)
    ms = MOE_SUB
    tiles = d // LANES

    @pl.when(e == 0)
    def _():
        cp = pltpu.make_async_copy(xp_hbm.at[pl.ds(c * tc, tc)], xs_ref.at[pl.ds(0, tc)], sem_ref.at[0])
        cp.start()
        xs_ref[pl.ds(tc, SUBLANES)] = jnp.zeros((SUBLANES, tiles, LANES), f32)
        acc_ref[...] = jnp.zeros_like(acc_ref)
        cp.wait()

    g = c * n_exp + e
    sb0 = start_ref[g]

    def sub_block(i, carry):
        sb = sb0 + i
        row0 = sb * ms
        for r in range(ms):
            xt_ref[pl.ds(_dense_row_index(r, tiles), tiles, stride=SUBLANES), :] = xs_ref[tok_ref[row0 + r]]
        x = jnp.concatenate(
            [jnp.concatenate([xt_ref[pl.ds((rg * tiles + jt) * SUBLANES, SUBLANES), :] for jt in range(tiles)], axis=1)
             for rg in range(ms // SUBLANES)], axis=0).astype(bf16)
        hmid = jnp.dot(x, wup_ref[0], preferred_element_type=f32) + bup_ref[0]
        glu = jnp.concatenate([hmid[:, 2 * jt * LANES:(2 * jt + 1) * LANES] for jt in range(tiles)], axis=1)
        lin = jnp.concatenate([hmid[:, (2 * jt + 1) * LANES:(2 * jt + 2) * LANES] for jt in range(tiles)], axis=1)
        glu = jnp.minimum(glu, SWIGLU_LIMIT)
        lin = jnp.clip(lin, -SWIGLU_LIMIT, SWIGLU_LIMIT)
        act = glu * jax.nn.sigmoid(SWIGLU_ALPHA * glu) * (lin + 1.0)
        y = jnp.dot(act.astype(bf16), wdn_ref[0], preferred_element_type=f32) + bdn_ref[0]
        gcol = jnp.broadcast_to(gate_ref[pl.ds(sb, 1), :], (ms, ms)).T[:, 0:1]
        y = y * gcol
        for rg in range(ms // SUBLANES):
            for jt in range(tiles):
                y_ref[pl.ds((rg * tiles + jt) * SUBLANES, SUBLANES), :] = (
                    y[rg * SUBLANES:(rg + 1) * SUBLANES, jt * LANES:(jt + 1) * LANES])
        for r0 in range(0, ms, SCATTER_UNROLL):
            toks, sums = [], []
            for r in range(r0, r0 + SCATTER_UNROLL):
                t = tok_ref[row0 + r]
                yrow = y_ref[pl.ds(_dense_row_index(r, tiles), tiles, stride=SUBLANES), :]
                toks.append(t)
                sums.append(acc_ref[t] + yrow)
            for t, s in zip(toks, sums):
                acc_ref[t] = s
        return carry

    lax.fori_loop(0, nsub_ref[g], sub_block, 0)

    @pl.when(e == n_exp - 1)
    def _():
        cp = pltpu.make_async_copy(acc_ref.at[pl.ds(0, tc)], f_hbm.at[pl.ds(c * tc, tc)], sem_ref.at[1])
        cp.start()
        cp.wait()


def experts(row_tok, sb_start, n_sub, xp, row_gate, w_up, b_up, w_down, b_down, tc):
    n, tiles, _ = xp.shape
    d = tiles * LANES
    n_exp = w_up.shape[0]
    kern = functools.partial(_experts_kernel, tc=tc, n_exp=n_exp, d=d)
    return pl.pallas_call(
        kern,
        out_shape=jax.ShapeDtypeStruct((n, tiles, LANES), f32),
        grid_spec=pltpu.PrefetchScalarGridSpec(
            num_scalar_prefetch=3,
            grid=(n // tc, n_exp),
            in_specs=[
                pl.BlockSpec(memory_space=pl.ANY),
                pl.BlockSpec(memory_space=pltpu.VMEM),
                pl.BlockSpec((1, d, 2 * d), lambda c, e, *_: (e, 0, 0)),
                pl.BlockSpec((1, 1, 2 * d), lambda c, e, *_: (e, 0, 0)),
                pl.BlockSpec((1, d, d), lambda c, e, *_: (e, 0, 0)),
                pl.BlockSpec((1, 1, d), lambda c, e, *_: (e, 0, 0)),
            ],
            out_specs=pl.BlockSpec(memory_space=pl.ANY),
            scratch_shapes=[
                pltpu.VMEM((tc + SUBLANES, tiles, LANES), f32),
                pltpu.VMEM((tc + SUBLANES, tiles, LANES), f32),
                pltpu.VMEM((MOE_SUB * tiles, LANES), f32),
                pltpu.VMEM((MOE_SUB * tiles, LANES), f32),
                pltpu.SemaphoreType.DMA((2,)),
            ],
        ),
        compiler_params=_cparams(("arbitrary", "arbitrary"), 58),
        name="experts",
    )(row_tok, sb_start, n_sub, xp, row_gate, w_up, b_up.reshape(n_exp, 1, 2 * d), w_down,
      b_down.reshape(n_exp, 1, d))


def _final_kernel(h_ref, f_ref, g_ref, o_ref):
    tm, d = h_ref.shape
    tiles = d // LANES
    parts = [h_ref[:, jt * LANES:(jt + 1) * LANES] + f_ref[pl.ds(jt, tm, stride=tiles), :] for jt in range(tiles)]
    o_ref[...] = _rms(jnp.concatenate(parts, axis=1), g_ref[...])


def final_norm(h, f2, g, row0, rows):
    d = h.shape[1]
    tiles = d // LANES
    tm = min(ROW_TILE, rows)
    blk0 = row0 // tm
    assert rows % tm == 0 and row0 % tm == 0
    return pl.pallas_call(
        _final_kernel,
        out_shape=jax.ShapeDtypeStruct((rows, d), f32),
        grid=(rows // tm,),
        in_specs=[
            pl.BlockSpec((tm, d), lambda i: (blk0 + i, 0)),
            pl.BlockSpec((tm * tiles, LANES), lambda i: (blk0 + i, 0)),
            pl.BlockSpec((1, d), lambda i: (0, 0)),
        ],
        out_specs=pl.BlockSpec((tm, d), lambda i: (i, 0)),
        compiler_params=_cparams(("arbitrary",), 32),
        name="final_norm",
    )(h, f2, g.reshape(1, d))


def moe(h, g, w_router, b_router, w_up, b_up, w_down, b_down):
    n, d = h.shape
    n_exp = w_router.shape[1]
    tc = n // MOE_CHUNKS
    xp, top_e, gates, rank, counts = router(h, g, w_router, b_router, tc)
    counts = counts.reshape(-1)
    n_sub = (counts + MOE_SUB - 1) // MOE_SUB
    sb_start = jnp.cumsum(n_sub) - n_sub
    total_sub = (n * TOP_K + MOE_CHUNKS * n_exp * (MOE_SUB - 1)) // MOE_SUB
    hot = top_e.reshape(TOP_K, MOE_CHUNKS, tc, 1) == jnp.arange(n_exp, dtype=i32)
    row_base = (sb_start * MOE_SUB).reshape(1, MOE_CHUNKS, 1, n_exp)
    dest = (jnp.sum(jnp.where(hot, row_base, 0), axis=-1).reshape(TOP_K, n) + rank).reshape(-1)
    t_local = jnp.broadcast_to((jnp.arange(n, dtype=i32) % tc)[None, :], (TOP_K, n)).astype(f32)
    vals = jnp.stack([t_local.reshape(-1), gates.reshape(-1)], axis=-1)
    init = jnp.broadcast_to(jnp.array([tc, 0.0], f32), (total_sub * MOE_SUB, 2))
    rows = init.at[dest].set(vals, unique_indices=True)
    row_tok = rows[:, 0].astype(i32)
    row_gate = rows[:, 1]
    tiles = d // LANES
    b_up_p = b_up.reshape(n_exp, tiles, LANES, 2).transpose(0, 1, 3, 2).reshape(n_exp, 2 * d)
    f = experts(row_tok, sb_start.astype(i32), n_sub.astype(i32), xp.reshape(n, tiles, LANES),
                row_gate.reshape(total_sub, MOE_SUB), split_pairs(w_up), b_up_p, w_down.astype(bf16), b_down, tc)
    return f.reshape(n * (d // LANES), LANES)


def kernel(x_prompt, x_sample, mem_prompt, state_ret, state_conv, cache_mem_k, cache_mem_v, norm_mix, w_in, conv_w,
           ret_gn, w_out, norm_cross, norm_mem, w_mq, w_mk, w_mv, w_mo, norm_ffn, w_router, b_router, w_up, b_up,
           w_down, b_down, norm_final):
    batch, seq, d = x_prompt.shape
    n_seq, ts, _ = x_sample.shape
    depth = w_in.shape[0]
    n_mem = mem_prompt.shape[1]
    r = ret_gn.shape[1]
    cd_width = conv_w.shape[2]
    dh = r // N_RET_HEADS
    n_p = batch * seq
    n_s = n_seq * ts
    n = n_p + n_s
    nblk = n_seq // SAMPLE_BLOCK
    assert seq % ROW_TILE == 0 and n % ROW_TILE == 0 and n % MOE_CHUNKS == 0

    xs = x_sample.reshape(nblk, SAMPLE_BLOCK, ts, d).transpose(0, 2, 1, 3).reshape(n_s, d)
    h = jnp.concatenate([x_prompt.reshape(n_p, d), xs], axis=0)
    mem2d = mem_prompt.reshape(batch * n_mem, d)
    tab_p = prompt_tables(seq, dh)
    tab_s = sample_tables(ts, float(PAST_LEN), dh)

    ret_p, conv_p, mk_p, mv_p, ret_s, conv_s = [], [], [], [], [], []
    for l in range(depth):
        proj = norm_matmul(h, norm_mix[l], w_in[l].astype(bf16), "in_proj")
        mix_p, s_p, c_p = prompt_mixer(proj, tab_p, conv_w[l], ret_gn[l], batch, seq, r, cd_width)
        mix_s, s_s, c_s = sample_mixer(proj, tab_s, conv_w[l], ret_gn[l], state_ret[l],
                                       state_conv[l].transpose(1, 0, 2), n_p, ts, r, cd_width)
        h = matmul_res(mix_p, mix_s, w_out[l].astype(bf16), h, "out_proj")

        mkv = norm_matmul(mem2d, norm_mem[l], jnp.concatenate([w_mk[l], w_mv[l]], axis=1).astype(bf16), "mem_kv")
        q = norm_matmul(h, norm_cross[l], w_mq[l].astype(bf16), "q_proj")
        attn_p = cross_prompt(q, mkv, batch, seq, n_mem, d)
        attn_s = cross_sample(q, cache_mem_k[l], cache_mem_v[l], n_p, ts)
        h = matmul_res(attn_p, attn_s, w_mo[l].astype(bf16), h, "o_proj")

        f2 = moe(h, norm_ffn[l], w_router[l], b_router[l], w_up[l], b_up[l], w_down[l], b_down[l])
        if l + 1 < depth:
            tiles = d // LANES
            h = h + f2.reshape(n, tiles, LANES).reshape(n, d)

        ret_p.append(s_p)
        conv_p.append(c_p)
        mk_p.append(mkv[:, :d].reshape(batch, n_mem, N_MEM_HEADS, d // N_MEM_HEADS))
        mv_p.append(mkv[:, d:].reshape(batch, n_mem, N_MEM_HEADS, d // N_MEM_HEADS))
        ret_s.append(s_s)
        conv_s.append(c_s.transpose(1, 0, 2))

    y_p = final_norm(h, f2, norm_final, 0, n_p).reshape(batch, seq, d)
    y_s = final_norm(h, f2, norm_final, n_p, n_s)
    y_s = y_s.reshape(nblk, ts, SAMPLE_BLOCK, d).transpose(0, 2, 1, 3).reshape(n_seq, ts, d)
    return (y_p, y_s, jnp.stack(ret_p), jnp.stack(conv_p), jnp.stack(mk_p), jnp.stack(mv_p), jnp.stack(ret_s),
            jnp.stack(conv_s))
```

```python
import functools

import jax
import jax.numpy as jnp
from jax import lax
from jax.experimental import pallas as pl
from jax.experimental.pallas import tpu as pltpu

f32 = jnp.float32
bf16 = jnp.bfloat16
i32 = jnp.int32

EPS = 1e-6
ROPE_THETA = 10000.0
RET_CHUNK = 128
PAST_LEN = 16384
N_RET_HEADS = 4
N_MEM_HEADS = 4
TOP_K = 4
SWIGLU_ALPHA = 1.702
SWIGLU_LIMIT = 7.0

LANES = 128
SUBLANES = 8
MIB = 1024 * 1024

ROW_TILE = 512
SAMPLE_BLOCK = 8
MOE_CHUNKS = 4
MOE_SUB = 128
SCATTER_UNROLL = 8


def _cparams(sem, vmem_mib):
    return pltpu.CompilerParams(dimension_semantics=sem, vmem_limit_bytes=vmem_mib * MIB)


def _rms(x, g):
    ms = jnp.mean(x * x, axis=-1, keepdims=True)
    return (x * lax.rsqrt(ms + EPS)) * g


def _norm_matmul_kernel(x_ref, g_ref, w_ref, o_ref):
    xn = _rms(x_ref[...], g_ref[...])
    o_ref[...] = jnp.dot(xn.astype(bf16), w_ref[...], preferred_element_type=f32)


def norm_matmul(x, g, w, name):
    m, d = x.shape
    f = w.shape[1]
    tm = min(ROW_TILE, m)
    assert m % tm == 0
    return pl.pallas_call(
        _norm_matmul_kernel,
        out_shape=jax.ShapeDtypeStruct((m, f), f32),
        grid=(m // tm,),
        in_specs=[
            pl.BlockSpec((tm, d), lambda i: (i, 0)),
            pl.BlockSpec((1, d), lambda i: (0, 0)),
            pl.BlockSpec((d, f), lambda i: (0, 0)),
        ],
        out_specs=pl.BlockSpec((tm, f), lambda i: (i, 0)),
        compiler_params=_cparams(("arbitrary",), 48),
        name=name,
    )(x, g.reshape(1, d), w)


def _matmul_res_kernel(ap_ref, as_ref, w_ref, r_ref, o_ref, *, prompt_tiles):
    a = jnp.where(pl.program_id(0) < prompt_tiles, ap_ref[...], as_ref[...])
    o_ref[...] = r_ref[...] + jnp.dot(a, w_ref[...], preferred_element_type=f32)


def matmul_res(a_prompt, a_sample, w, res, name):
    m_p, d = a_prompt.shape
    m_s = a_sample.shape[0]
    f = w.shape[1]
    tm = ROW_TILE
    assert m_p % tm == 0 and m_s == tm
    prompt_tiles = m_p // tm
    return pl.pallas_call(
        functools.partial(_matmul_res_kernel, prompt_tiles=prompt_tiles),
        out_shape=jax.ShapeDtypeStruct((m_p + m_s, f), f32),
        grid=(prompt_tiles + 1,),
        in_specs=[
            pl.BlockSpec((tm, d), lambda i: (jnp.minimum(i, prompt_tiles - 1), 0)),
            pl.BlockSpec((tm, d), lambda i: (0, 0)),
            pl.BlockSpec((d, f), lambda i: (0, 0)),
            pl.BlockSpec((tm, f), lambda i: (i, 0)),
        ],
        out_specs=pl.BlockSpec((tm, f), lambda i: (i, 0)),
        compiler_params=_cparams(("arbitrary",), 32),
        name=name,
    )(a_prompt, a_sample, w, res)


def _rope(x, cos, sin_signed, half):
    return x * cos + pltpu.roll(x, half, 1) * sin_signed


def _group_norm_gate(o, gate, gn):
    mu = jnp.mean(o, axis=-1, keepdims=True)
    var = jnp.mean(jnp.square(o - mu), axis=-1, keepdims=True)
    on = ((o - mu) * lax.rsqrt(var + EPS)) * gn
    return (gate * jax.nn.sigmoid(gate)) * on


def _prompt_mixer_kernel(cd_ref, proj_ref, cos_ref, sin_ref, decay_ref, xi_ref, zeta_ref, convw_ref, gn_ref,
                         mix_ref, sfin_ref, cfin_ref, s_scr, u_scr, *, tt, r, cd_width, dh):
    j = pl.program_id(1)
    heads = r // dh
    pad = SUBLANES

    @pl.when(j == 0)
    def _():
        s_scr[...] = jnp.zeros_like(s_scr)
        u_scr[0:pad, :] = jnp.zeros((pad, cd_width), f32)

    k_scale = dh ** -0.5
    for c in range(tt // RET_CHUNK):
        rows = pl.ds(c * RET_CHUNK, RET_CHUNK)
        cosb = cos_ref[rows, :]
        sinb = sin_ref[rows, :]
        for h in range(heads):
            lo, hi = h * dh, (h + 1) * dh
            qh = _rope(proj_ref[rows, lo:hi], cosb, sinb, dh // 2)
            kh = _rope(proj_ref[rows, r + lo:r + hi], cosb, sinb, dh // 2) * k_scale
            vb = proj_ref[rows, 2 * r + lo:2 * r + hi].astype(bf16)
            gate = proj_ref[rows, 3 * r + lo:3 * r + hi]
            qb = qh.astype(bf16)
            kb = kh.astype(bf16)
            scores = lax.dot_general(qb, kb, (((1,), (1,)), ((), ())), preferred_element_type=f32) * decay_ref[h]
            inner = jnp.dot(scores.astype(bf16), vb, preferred_element_type=f32)
            s_prev = s_scr[h]
            cross = jnp.dot((qh * xi_ref[h]).astype(bf16), s_prev.astype(bf16), preferred_element_type=f32)
            kv = lax.dot_general((kh * zeta_ref[h]).astype(bf16), vb, (((0,), (0,)), ((), ())),
                                 preferred_element_type=f32)
            s_scr[h] = cd_ref[h] * s_prev + kv
            mix_ref[rows, lo:hi] = _group_norm_gate(inner + cross, gate, gn_ref[:, lo:hi]).astype(bf16)

    base = 4 * r
    u = proj_ref[:, base + cd_width:base + 2 * cd_width] * proj_ref[:, base:base + cd_width]
    u_scr[pad:pad + tt, :] = u
    conv = (u_scr[pad - 2:pad - 2 + tt, :] * convw_ref[0:1, :] + u_scr[pad - 1:pad - 1 + tt, :] * convw_ref[1:2, :]
            + u * convw_ref[2:3, :])
    mix_ref[:, r:r + cd_width] = (proj_ref[:, base + 2 * cd_width:base + 3 * cd_width] * conv).astype(bf16)
    u_scr[0:pad, :] = u_scr[tt:tt + pad, :]
    cfin_ref[0] = u_scr[pad - 2:pad, :]

    @pl.when(j == pl.num_programs(1) - 1)
    def _():
        sfin_ref[0] = s_scr[...]


def prompt_mixer(proj, tables, conv_w, ret_gn, batch, seq, r, cd_width):
    tt = ROW_TILE
    dh = r // N_RET_HEADS
    steps = seq // tt
    in_cols = proj.shape[1]
    kern = functools.partial(_prompt_mixer_kernel, tt=tt, r=r, cd_width=cd_width, dh=dh)
    tab = lambda shape: pl.BlockSpec(shape, lambda b, j, cd: (0,) * len(shape))
    return pl.pallas_call(
        kern,
        out_shape=(
            jax.ShapeDtypeStruct((batch * seq, r + cd_width), bf16),
            jax.ShapeDtypeStruct((batch, N_RET_HEADS, dh, dh), f32),
            jax.ShapeDtypeStruct((batch, 2, cd_width), f32),
        ),
        grid_spec=pltpu.PrefetchScalarGridSpec(
            num_scalar_prefetch=1,
            grid=(batch, steps),
            in_specs=[
                pl.BlockSpec((tt, in_cols), lambda b, j, cd: (b * steps + j, 0)),
                pl.BlockSpec((tt, dh), lambda b, j, cd: (j, 0)),
                pl.BlockSpec((tt, dh), lambda b, j, cd: (j, 0)),
                tab((N_RET_HEADS, RET_CHUNK, RET_CHUNK)),
                tab((N_RET_HEADS, RET_CHUNK, dh)),
                tab((N_RET_HEADS, RET_CHUNK, dh)),
                tab((3, cd_width)),
                tab((1, r)),
            ],
            out_specs=(
                pl.BlockSpec((tt, r + cd_width), lambda b, j, cd: (b * steps + j, 0)),
                pl.BlockSpec((1, N_RET_HEADS, dh, dh), lambda b, j, cd: (b, 0, 0, 0)),
                pl.BlockSpec((1, 2, cd_width), lambda b, j, cd: (b, 0, 0)),
            ),
            scratch_shapes=[
                pltpu.VMEM((N_RET_HEADS, dh, dh), f32),
                pltpu.VMEM((tt + 2 * SUBLANES, cd_width), f32),
            ],
        ),
        compiler_params=_cparams(("arbitrary", "arbitrary"), 48),
        name="prompt_mixer",
    )(tables["chunk_decay"], proj, tables["cos"], tables["sin"], tables["decay"], tables["xi"], tables["zeta"],
      conv_w, ret_gn.reshape(1, r))


def _sample_mixer_kernel(dec_ref, proj_ref, cos_ref, sin_ref, xi_ref, zeta_ref, convw_ref, gn_ref, s0_ref, c0_ref,
                         mix_ref, snew_ref, cnew_ref, *, ts, r, cd_width, dh):
    heads = r // dh
    nb = SAMPLE_BLOCK
    rows = ts * nb
    k_scale = dh ** -0.5
    cosb = cos_ref[...]
    sinb = sin_ref[...]
    seq_of_row = lax.broadcasted_iota(i32, (rows, dh), 0) % nb
    for h in range(heads):
        lo, hi = h * dh, (h + 1) * dh
        qh = _rope(proj_ref[:, lo:hi], cosb, sinb, dh // 2)
        kh = _rope(proj_ref[:, r + lo:r + hi], cosb, sinb, dh // 2) * k_scale
        vh = proj_ref[:, 2 * r + lo:2 * r + hi]
        gate = proj_ref[:, 3 * r + lo:3 * r + hi]
        inner = []
        for t in range(ts):
            qt = qh[t * nb:(t + 1) * nb]
            acc = jnp.zeros((nb, dh), f32)
            for s in range(t + 1):
                sc = jnp.sum(qt * kh[s * nb:(s + 1) * nb], axis=-1, keepdims=True) * dec_ref[h * (ts + 1) + t - s]
                acc = acc + sc * vh[s * nb:(s + 1) * nb]
            inner.append(acc)
        inner = jnp.concatenate(inner, axis=0)
        qx = (qh * xi_ref[h]).astype(bf16)
        kz = kh * zeta_ref[h]
        vb = vh.astype(bf16)
        cross = jnp.zeros((rows, dh), f32)
        for b in range(nb):
            mine = seq_of_row == b
            s_prev = s0_ref[b, h]
            res = jnp.dot(qx, s_prev.astype(bf16), preferred_element_type=f32)
            cross = cross + jnp.where(mine, res, 0.0)
            kv = lax.dot_general(jnp.where(mine, kz, 0.0).astype(bf16), vb, (((0,), (0,)), ((), ())),
                                 preferred_element_type=f32)
            snew_ref[b, h] = dec_ref[h * (ts + 1) + ts] * s_prev + kv
        mix_ref[:, lo:hi] = _group_norm_gate(inner + cross, gate, gn_ref[:, lo:hi]).astype(bf16)

    base = 4 * r
    u = proj_ref[:, base + cd_width:base + 2 * cd_width] * proj_ref[:, base:base + cd_width]
    full = [c0_ref[0], c0_ref[1]] + [u[t * nb:(t + 1) * nb] for t in range(ts)]
    conv = jnp.concatenate(
        [full[t] * convw_ref[0:1, :] + full[t + 1] * convw_ref[1:2, :] + full[t + 2] * convw_ref[2:3, :]
         for t in range(ts)], axis=0)
    mix_ref[:, r:r + cd_width] = (proj_ref[:, base + 2 * cd_width:base + 3 * cd_width] * conv).astype(bf16)
    cnew_ref[0] = full[ts]
    cnew_ref[1] = full[ts + 1]


def sample_mixer(proj, tables, conv_w, ret_gn, state_ret, state_conv_t, row0, ts, r, cd_width):
    n_seq = state_ret.shape[0]
    dh = r // N_RET_HEADS
    rows = ts * SAMPLE_BLOCK
    blk0 = row0 // rows
    assert row0 % rows == 0 and n_seq % SAMPLE_BLOCK == 0
    in_cols = proj.shape[1]
    kern = functools.partial(_sample_mixer_kernel, ts=ts, r=r, cd_width=cd_width, dh=dh)
    tab = lambda shape: pl.BlockSpec(shape, lambda i, d: (0,) * len(shape))
    return pl.pallas_call(
        kern,
        out_shape=(
            jax.ShapeDtypeStruct((n_seq * ts, r + cd_width), bf16),
            jax.ShapeDtypeStruct(state_ret.shape, f32),
            jax.ShapeDtypeStruct(state_conv_t.shape, f32),
        ),
        grid_spec=pltpu.PrefetchScalarGridSpec(
            num_scalar_prefetch=1,
            grid=(n_seq // SAMPLE_BLOCK,),
            in_specs=[
                pl.BlockSpec((rows, in_cols), lambda i, d: (blk0 + i, 0)),
                tab((rows, dh)),
                tab((rows, dh)),
                tab((N_RET_HEADS, rows, dh)),
                tab((N_RET_HEADS, rows, dh)),
                tab((3, cd_width)),
                tab((1, r)),
                pl.BlockSpec((SAMPLE_BLOCK, N_RET_HEADS, dh, dh), lambda i, d: (i, 0, 0, 0)),
                pl.BlockSpec((2, SAMPLE_BLOCK, cd_width), lambda i, d: (0, i, 0)),
            ],
            out_specs=(
                pl.BlockSpec((rows, r + cd_width), lambda i, d: (i, 0)),
                pl.BlockSpec((SAMPLE_BLOCK, N_RET_HEADS, dh, dh), lambda i, d: (i, 0, 0, 0)),
                pl.BlockSpec((2, SAMPLE_BLOCK, cd_width), lambda i, d: (0, i, 0)),
            ),
        ),
        compiler_params=_cparams(("arbitrary",), 32),
        name="sample_mixer",
    )(tables["dec"], proj, tables["cos"], tables["sin"], tables["xi"], tables["zeta"], conv_w, ret_gn.reshape(1, r),
      state_ret, state_conv_t)


def _log_gamma():
    return jnp.log1p(-jnp.exp2(-5.0 - jnp.arange(N_RET_HEADS, dtype=f32)))


def _rope_tables(pos, dh):
    half = dh // 2
    inv = ROPE_THETA ** (-jnp.arange(half, dtype=f32) / half)
    ang = pos[:, None] * inv[None, :]
    cos, sin = jnp.cos(ang), jnp.sin(ang)
    return jnp.concatenate([cos, cos], axis=-1), jnp.concatenate([-sin, sin], axis=-1)


def prompt_tables(seq, dh):
    c = RET_CHUNK
    lg = _log_gamma()
    i = jnp.arange(c, dtype=f32)
    diff = i[:, None] - i[None, :]
    decay = jnp.where(diff >= 0, jnp.exp(jnp.maximum(diff, 0.0)[None] * lg[:, None, None]), 0.0)
    xi = jnp.exp((i + 1.0)[None, :] * lg[:, None])
    zeta = jnp.exp((c - 1.0 - i)[None, :] * lg[:, None])
    cos, sin = _rope_tables(jnp.arange(seq, dtype=f32), dh)
    bc = lambda t: jnp.broadcast_to(t[:, :, None], (N_RET_HEADS, c, dh))
    return dict(cos=cos, sin=sin, decay=decay, xi=bc(xi), zeta=bc(zeta), chunk_decay=jnp.exp(c * lg))


def sample_tables(ts, pos0, dh):
    lg = _log_gamma()
    i = jnp.arange(ts, dtype=f32)
    dec = jnp.exp(jnp.arange(ts + 1, dtype=f32)[None, :] * lg[:, None])
    xi = jnp.exp((i + 1.0)[None, :] * lg[:, None])
    zeta = jnp.exp((ts - 1.0 - i)[None, :] * lg[:, None])
    cos, sin = _rope_tables(pos0 + i, dh)
    rep = lambda t: jnp.repeat(t, SAMPLE_BLOCK, axis=0)
    bc = lambda t: jnp.broadcast_to(jnp.repeat(t, SAMPLE_BLOCK, axis=1)[:, :, None],
                                    (N_RET_HEADS, ts * SAMPLE_BLOCK, dh))
    return dict(cos=rep(cos), sin=rep(sin), xi=bc(xi), zeta=bc(zeta), dec=dec.reshape(-1))


def _softmax_rows(s):
    m = jnp.max(s, axis=-1, keepdims=True)
    p = jnp.exp(s - m)
    return p / jnp.sum(p, axis=-1, keepdims=True)


def _cross_prompt_kernel(q_ref, k_ref, v_ref, o_ref, *, dh):
    scale = dh ** -0.5
    for h in range(N_MEM_HEADS):
        cols = slice(h * dh, (h + 1) * dh)
        s = lax.dot_general(q_ref[:, cols].astype(bf16), k_ref[:, cols].astype(bf16), (((1,), (1,)), ((), ())),
                            preferred_element_type=f32) * scale
        p = _softmax_rows(s)
        o_ref[:, cols] = jnp.dot(p.astype(bf16), v_ref[:, cols].astype(bf16), preferred_element_type=f32).astype(bf16)


def cross_prompt(q, mkv, batch, seq, n_mem, d):
    tq = ROW_TILE
    steps = seq // tq
    kern = functools.partial(_cross_prompt_kernel, dh=d // N_MEM_HEADS)
    return pl.pallas_call(
        kern,
        out_shape=jax.ShapeDtypeStruct((batch * seq, d), bf16),
        grid=(batch, steps),
        in_specs=[
            pl.BlockSpec((tq, d), lambda b, j: (b * steps + j, 0)),
            pl.BlockSpec((n_mem, d), lambda b, j: (b, 0)),
            pl.BlockSpec((n_mem, d), lambda b, j: (b, 1)),
        ],
        out_specs=pl.BlockSpec((tq, d), lambda b, j: (b * steps + j, 0)),
        compiler_params=_cparams(("arbitrary", "arbitrary"), 32),
        name="cross_prompt",
    )(q, mkv, mkv)


def _cross_sample_kernel(q_ref, k_ref, v_ref, o_ref, acc_ref, *, dh, seqs):
    part = pl.program_id(1)
    rows = q_ref.shape[0]
    scale = dh ** -0.5

    @pl.when(part == 0)
    def _():
        acc_ref[...] = jnp.zeros_like(acc_ref)

    heads = N_MEM_HEADS
    n_mem = k_ref.shape[1]
    qx = jnp.concatenate([q_ref[:, h * dh:(h + 1) * dh] for h in range(heads)], axis=0).astype(bf16)
    row = lax.broadcasted_iota(i32, (heads * rows, n_mem * heads), 0)
    col = lax.broadcasted_iota(i32, (heads * rows, n_mem * heads), 1)
    same_head = col % heads == row // rows
    seq_of_row = lax.broadcasted_iota(i32, (heads * rows, dh), 0) % SAMPLE_BLOCK
    out = jnp.zeros((heads * rows, dh), f32)
    for b in range(seqs):
        k2 = k_ref[b].reshape(n_mem * heads, dh).astype(bf16)
        v2 = v_ref[b].reshape(n_mem * heads, dh).astype(bf16)
        s = lax.dot_general(qx, k2, (((1,), (1,)), ((), ())), preferred_element_type=f32) * scale
        p = _softmax_rows(jnp.where(same_head, s, -jnp.inf))
        o = jnp.dot(p.astype(bf16), v2, preferred_element_type=f32)
        out = out + jnp.where(seq_of_row == part * seqs + b, o, 0.0)
    acc_ref[...] += out

    @pl.when(part == pl.num_programs(1) - 1)
    def _():
        for h in range(heads):
            o_ref[:, h * dh:(h + 1) * dh] = acc_ref[h * rows:(h + 1) * rows, :].astype(bf16)


def cross_sample(q, cache_k, cache_v, row0, ts):
    n_seq, n_mem, heads, dh = cache_k.shape
    d = heads * dh
    rows = ts * SAMPLE_BLOCK
    blk0 = row0 // rows
    parts = 4
    seqs = SAMPLE_BLOCK // parts
    kern = functools.partial(_cross_sample_kernel, dh=dh, seqs=seqs)
    return pl.pallas_call(
        kern,
        out_shape=jax.ShapeDtypeStruct((n_seq * ts, d), bf16),
        grid=(n_seq // SAMPLE_BLOCK, parts),
        in_specs=[
            pl.BlockSpec((rows, d), lambda i, p: (blk0 + i, 0)),
            pl.BlockSpec((seqs, n_mem, heads, dh), lambda i, p: (i * parts + p, 0, 0, 0)),
            pl.BlockSpec((seqs, n_mem, heads, dh), lambda i, p: (i * parts + p, 0, 0, 0)),
        ],
        out_specs=pl.BlockSpec((rows, d), lambda i, p: (i, 0)),
        scratch_shapes=[pltpu.VMEM((heads * rows, dh), f32)],
        compiler_params=_cparams(("arbitrary", "arbitrary"), 40),
        name="cross_sample",
    )(q, cache_k, cache_v)


def _router_kernel(h_ref, g_ref, wr_ref, br_ref, xp_ref, e_ref, gate_ref, rank_ref, cnt_ref, carry_ref,
                   *, tiles_per_chunk, n_exp):
    i = pl.program_id(0)

    @pl.when(i % tiles_per_chunk == 0)
    def _():
        carry_ref[...] = jnp.zeros_like(carry_ref)

    xn = _rms(h_ref[...], g_ref[...])
    tm, d = xn.shape
    xp_ref[...] = xn

    w = wr_ref[...]
    w_hi = w.astype(bf16)
    w_lo = (w - w_hi.astype(f32)).astype(bf16)
    x_hi = xn.astype(bf16)
    x_lo = (xn - x_hi.astype(f32)).astype(bf16)
    nt = (((1,), (1,)), ((), ()))
    both = lax.dot_general(jnp.concatenate([w_hi, w_lo], axis=0), x_hi, nt, preferred_element_type=f32)
    logits = (both[:n_exp] + both[n_exp:] + lax.dot_general(w_hi, x_lo, nt, preferred_element_type=f32)
              + br_ref[...])
    sub = lax.broadcasted_iota(i32, (n_exp, tm), 0).astype(f32)
    chosen, vals, hots = [], [], []
    work = logits
    for _ in range(TOP_K):
        m = jnp.max(work, axis=0, keepdims=True)
        idx = jnp.min(jnp.where(work == m, sub, float(n_exp)), axis=0, keepdims=True)
        hot = sub == idx
        chosen.append(idx)
        vals.append(m)
        hots.append(hot)
        work = jnp.where(hot, -jnp.inf, work)
    ex = [jnp.exp(v - vals[0]) for v in vals]
    denom = ex[0] + ex[1] + ex[2] + ex[3]
    e_ref[...] = jnp.concatenate(chosen, axis=0).astype(i32)
    gate_ref[...] = jnp.concatenate([x / denom for x in ex], axis=0)

    member = jnp.zeros((n_exp, tm), f32)
    for hot in hots:
        member = member + hot.astype(f32)
    earlier = (lax.broadcasted_iota(i32, (tm, tm), 0) < lax.broadcasted_iota(i32, (tm, tm), 1)).astype(bf16)
    before = jnp.dot(member.astype(bf16), earlier, preferred_element_type=f32) + carry_ref[...]
    rank_ref[...] = jnp.concatenate(
        [jnp.sum(jnp.where(hot, before, 0.0), axis=0, keepdims=True) for hot in hots], axis=0).astype(i32)
    carry_ref[...] += jnp.sum(member, axis=1, keepdims=True)
    cnt_ref[0] = carry_ref[...].astype(i32)


def router(h, g, w_router, b_router, tc):
    n, d = h.shape
    n_exp = w_router.shape[1]
    tm = ROW_TILE
    while tc % tm:
        tm -= LANES
    tiles_per_chunk = tc // tm
    kern = functools.partial(_router_kernel, tiles_per_chunk=tiles_per_chunk, n_exp=n_exp)
    return pl.pallas_call(
        kern,
        out_shape=(
            jax.ShapeDtypeStruct((n, d), f32),
            jax.ShapeDtypeStruct((TOP_K, n), i32),
            jax.ShapeDtypeStruct((TOP_K, n), f32),
            jax.ShapeDtypeStruct((TOP_K, n), i32),
            jax.ShapeDtypeStruct((n // tc, n_exp, 1), i32),
        ),
        grid=(n // tm,),
        in_specs=[
            pl.BlockSpec((tm, d), lambda i: (i, 0)),
            pl.BlockSpec((1, d), lambda i: (0, 0)),
            pl.BlockSpec((n_exp, d), lambda i: (0, 0)),
            pl.BlockSpec((n_exp, 1), lambda i: (0, 0)),
        ],
        out_specs=(
            pl.BlockSpec((tm, d), lambda i: (i, 0)),
            pl.BlockSpec((TOP_K, tm), lambda i: (0, i)),
            pl.BlockSpec((TOP_K, tm), lambda i: (0, i)),
            pl.BlockSpec((TOP_K, tm), lambda i: (0, i)),
            pl.BlockSpec((1, n_exp, 1), lambda i: (i // tiles_per_chunk, 0, 0)),
        ),
        scratch_shapes=[pltpu.VMEM((n_exp, 1), f32)],
        compiler_params=_cparams(("arbitrary",), 32),
        name="router",
    )(h, g.reshape(1, d), w_router.T, b_router.reshape(n_exp, 1))


def _split_pairs_kernel(w_ref, p_ref, o_ref):
    width = p_ref.shape[0]
    for b in range(w_ref.shape[2] // width):
        cols = slice(b * width, (b + 1) * width)
        o_ref[0, :, cols] = jnp.dot(w_ref[0, :, cols].astype(bf16), p_ref[...],
                                    preferred_element_type=f32).astype(bf16)


def split_pairs(w):
    n_exp, d, f2 = w.shape
    width = 2 * LANES
    j = jnp.arange(width)
    src = jnp.where(j < LANES, 2 * j, 2 * (j - LANES) + 1)
    perm = (jnp.arange(width)[:, None] == src[None, :]).astype(bf16)
    return pl.pallas_call(
        _split_pairs_kernel,
        out_shape=jax.ShapeDtypeStruct((n_exp, d, f2), bf16),
        grid=(n_exp,),
        in_specs=[
            pl.BlockSpec((1, d, f2), lambda e: (e, 0, 0)),
            pl.BlockSpec((width, width), lambda e: (0, 0)),
        ],
        out_specs=pl.BlockSpec((1, d, f2), lambda e: (e, 0, 0)),
        compiler_params=_cparams(("arbitrary",), 40),
        name="split_pairs",
    )(w, perm)


def _dense_row_index(r, tiles):
    return (r // SUBLANES) * tiles * SUBLANES + r % SUBLANES


def _experts_kernel(start_ref, nsub_ref, xp_hbm, dest_hbm, gates_hbm, fill_hbm, wup_ref, bup_ref, wdn_ref, bdn_ref,
                    f_hbm, xs_ref, acc_ref, xt_a, xt_b, y_a, y_b, dest_s, gate_s, rmap_s, sem_ref,
                    *, tc, n_exp, d, null_row0):
    c = pl.program_id(0)
    e = pl.program_id(1)
    ms = MOE_SUB
    tiles = d // LANES

    def gather(row0, xt_ref):
        for r in range(ms):
            t = rmap_s[row0 + r] >> 2
            xt_ref[pl.ds(_dense_row_index(r, tiles), tiles, stride=SUBLANES), :] = xs_ref[t]

    def scatter(row0, y_ref):
        for r0 in range(0, ms, SCATTER_UNROLL):
            toks, sums = [], []
            for r in range(r0, r0 + SCATTER_UNROLL):
                a = rmap_s[row0 + r]
                t = a >> 2
                yrow = y_ref[pl.ds(_dense_row_index(r, tiles), tiles, stride=SUBLANES), :]
                toks.append(t)
                sums.append(acc_ref[t] + gate_s[a] * yrow)
            for t, s in zip(toks, sums):
                acc_ref[t] = s

    @pl.when(e == 0)
    def _():
        copies = [
            pltpu.make_async_copy(xp_hbm.at[pl.ds(c * tc, tc)], xs_ref.at[pl.ds(0, tc)], sem_ref.at[0]),
            pltpu.make_async_copy(dest_hbm.at[c], dest_s, sem_ref.at[1]),
            pltpu.make_async_copy(gates_hbm.at[c], gate_s, sem_ref.at[2]),
            pltpu.make_async_copy(fill_hbm, rmap_s, sem_ref.at[3]),
        ]
        for cp in copies:
            cp.start()
        xs_ref[pl.ds(tc, SUBLANES)] = jnp.zeros((SUBLANES, tiles, LANES), f32)
        acc_ref[...] = jnp.zeros_like(acc_ref)

        @pl.when(c == 0)
        def _():
            y_a[...] = jnp.zeros_like(y_a)
            y_b[...] = jnp.zeros_like(y_b)

        for cp in copies:
            cp.wait()

        def place(i, carry):
            for u in range(SUBLANES):
                a = i * SUBLANES + u
                rmap_s[dest_s[a]] = a
            return carry

        lax.fori_loop(0, TOP_K * tc // SUBLANES, place, 0)
        gather(0, xt_a)

    g = c * n_exp + e
    j0 = start_ref[g]

    def step(j, xt_cur, xt_nxt, y_cur, y_prv):
        gather((j + 1) * ms, xt_nxt)
        scatter(jnp.where(j == 0, null_row0, (j - 1) * ms), y_prv)
        x = jnp.concatenate(
            [jnp.concatenate([xt_cur[pl.ds((rg * tiles + jt) * SUBLANES, SUBLANES), :] for jt in range(tiles)], axis=1)
             for rg in range(ms // SUBLANES)], axis=0).astype(bf16)
        hmid = jnp.dot(x, wup_ref[0], preferred_element_type=f32) + bup_ref[0]
        glu = jnp.concatenate([hmid[:, 2 * jt * LANES:(2 * jt + 1) * LANES] for jt in range(tiles)], axis=1)
        lin = jnp.concatenate([hmid[:, (2 * jt + 1) * LANES:(2 * jt + 2) * LANES] for jt in range(tiles)], axis=1)
        glu = jnp.minimum(glu, SWIGLU_LIMIT)
        lin = jnp.clip(lin, -SWIGLU_LIMIT, SWIGLU_LIMIT)
        act = glu * jax.nn.sigmoid(SWIGLU_ALPHA * glu) * (lin + 1.0)
        y = jnp.dot(act.astype(bf16), wdn_ref[0], preferred_element_type=f32) + bdn_ref[0]
        for rg in range(ms // SUBLANES):
            for jt in range(tiles):
                y_cur[pl.ds((rg * tiles + jt) * SUBLANES, SUBLANES), :] = (
                    y[rg * SUBLANES:(rg + 1) * SUBLANES, jt * LANES:(jt + 1) * LANES])

    def sub_block(i, carry):
        j = j0 + i

        @pl.when(j % 2 == 0)
        def _():
            step(j, xt_a, xt_b, y_a, y_b)

        @pl.when(j % 2 == 1)
        def _():
            step(j, xt_b, xt_a, y_b, y_a)

        return carry

    lax.fori_loop(0, nsub_ref[g], sub_block, 0)

    @pl.when(e == n_exp - 1)
    def _():
        last = j0 + nsub_ref[g] - 1

        @pl.when(last % 2 == 0)
        def _():
            scatter(last * ms, y_a)

        @pl.when(last % 2 == 1)
        def _():
            scatter(last * ms, y_b)

        cp = pltpu.make_async_copy(acc_ref.at[pl.ds(0, tc)], f_hbm.at[pl.ds(c * tc, tc)], sem_ref.at[0])
        cp.start()
        cp.wait()


def experts(sb_start, n_sub, xp, dest, gates, w_up, b_up, w_down, b_down, tc):
    n, tiles, _ = xp.shape
    d = tiles * LANES
    n_exp = w_up.shape[0]
    chunks, padded_len = dest.shape
    max_sub = (TOP_K * tc + n_exp * (MOE_SUB - 1)) // MOE_SUB
    null_row0 = (max_sub + 1) * MOE_SUB
    map_len = null_row0 + MOE_SUB
    fill = jnp.full((map_len,), TOP_K * tc, i32)
    kern = functools.partial(_experts_kernel, tc=tc, n_exp=n_exp, d=d, null_row0=null_row0)
    block = pltpu.VMEM((MOE_SUB * tiles, LANES), f32)
    return pl.pallas_call(
        kern,
        out_shape=jax.ShapeDtypeStruct((n, tiles, LANES), f32),
        grid_spec=pltpu.PrefetchScalarGridSpec(
            num_scalar_prefetch=2,
            grid=(chunks, n_exp),
            in_specs=[
                pl.BlockSpec(memory_space=pl.ANY),
                pl.BlockSpec(memory_space=pl.ANY),
                pl.BlockSpec(memory_space=pl.ANY),
                pl.BlockSpec(memory_space=pl.ANY),
                pl.BlockSpec((1, d, 2 * d), lambda c, e, *_: (e, 0, 0)),
                pl.BlockSpec((1, 1, 2 * d), lambda c, e, *_: (e, 0, 0)),
                pl.BlockSpec((1, d, d), lambda c, e, *_: (e, 0, 0)),
                pl.BlockSpec((1, 1, d), lambda c, e, *_: (e, 0, 0)),
            ],
            out_specs=pl.BlockSpec(memory_space=pl.ANY),
            scratch_shapes=[
                pltpu.VMEM((tc + SUBLANES, tiles, LANES), f32),
                pltpu.VMEM((tc + SUBLANES, tiles, LANES), f32),
                block, block, block, block,
                pltpu.SMEM((padded_len,), i32),
                pltpu.SMEM((padded_len,), f32),
                pltpu.SMEM((map_len,), i32),
                pltpu.SemaphoreType.DMA((4,)),
            ],
        ),
        compiler_params=_cparams(("arbitrary", "arbitrary"), 58),
        name="experts",
    )(sb_start, n_sub, xp, dest, gates, fill, w_up, b_up.reshape(n_exp, 1, 2 * d), w_down,
      b_down.reshape(n_exp, 1, d))


def _final_kernel(h_ref, f_ref, g_ref, o_ref):
    tm, d = h_ref.shape
    tiles = d // LANES
    parts = [h_ref[:, jt * LANES:(jt + 1) * LANES] + f_ref[pl.ds(jt, tm, stride=tiles), :] for jt in range(tiles)]
    o_ref[...] = _rms(jnp.concatenate(parts, axis=1), g_ref[...])


def final_norm(h, f2, g, row0, rows):
    d = h.shape[1]
    tiles = d // LANES
    tm = min(ROW_TILE, rows)
    blk0 = row0 // tm
    assert rows % tm == 0 and row0 % tm == 0
    return pl.pallas_call(
        _final_kernel,
        out_shape=jax.ShapeDtypeStruct((rows, d), f32),
        grid=(rows // tm,),
        in_specs=[
            pl.BlockSpec((tm, d), lambda i: (blk0 + i, 0)),
            pl.BlockSpec((tm * tiles, LANES), lambda i: (blk0 + i, 0)),
            pl.BlockSpec((1, d), lambda i: (0, 0)),
        ],
        out_specs=pl.BlockSpec((tm, d), lambda i: (i, 0)),
        compiler_params=_cparams(("arbitrary",), 32),
        name="final_norm",
    )(h, f2, g.reshape(1, d))


def moe(h, g, w_router, b_router, w_up, b_up, w_down, b_down):
    n, d = h.shape
    n_exp = w_router.shape[1]
    tc = n // MOE_CHUNKS
    xp, top_e, gates, rank, counts = router(h, g, w_router, b_router, tc)
    n_sub = (counts.reshape(MOE_CHUNKS, n_exp) + MOE_SUB - 1) // MOE_SUB
    sb_start = jnp.cumsum(n_sub, axis=1) - n_sub
    hot = top_e.reshape(TOP_K, MOE_CHUNKS, tc, 1) == jnp.arange(n_exp, dtype=i32)
    row_base = (sb_start * MOE_SUB).reshape(1, MOE_CHUNKS, 1, n_exp)
    dest = jnp.sum(jnp.where(hot, row_base, 0), axis=-1) + rank.reshape(TOP_K, MOE_CHUNKS, tc)
    padded_len = -(-(TOP_K * tc + 1) // 1024) * 1024
    per_chunk = lambda a: jnp.pad(a.transpose(1, 2, 0).reshape(MOE_CHUNKS, TOP_K * tc),
                                  ((0, 0), (0, padded_len - TOP_K * tc)))
    tiles = d // LANES
    b_up_p = b_up.reshape(n_exp, tiles, LANES, 2).transpose(0, 1, 3, 2).reshape(n_exp, 2 * d)
    f = experts(sb_start.reshape(-1).astype(i32), n_sub.reshape(-1).astype(i32), xp.reshape(n, tiles, LANES),
                per_chunk(dest), per_chunk(gates.reshape(TOP_K, MOE_CHUNKS, tc)), split_pairs(w_up), b_up_p,
                w_down.astype(bf16), b_down, tc)
    return f.reshape(n * tiles, LANES)


def kernel(x_prompt, x_sample, mem_prompt, state_ret, state_conv, cache_mem_k, cache_mem_v, norm_mix, w_in, conv_w,
           ret_gn, w_out, norm_cross, norm_mem, w_mq, w_mk, w_mv, w_mo, norm_ffn, w_router, b_router, w_up, b_up,
           w_down, b_down, norm_final):
    batch, seq, d = x_prompt.shape
    n_seq, ts, _ = x_sample.shape
    depth = w_in.shape[0]
    n_mem = mem_prompt.shape[1]
    r = ret_gn.shape[1]
    cd_width = conv_w.shape[2]
    dh = r // N_RET_HEADS
    n_p = batch * seq
    n_s = n_seq * ts
    n = n_p + n_s
    nblk = n_seq // SAMPLE_BLOCK
    assert seq % ROW_TILE == 0 and n % ROW_TILE == 0 and n % MOE_CHUNKS == 0

    xs = x_sample.reshape(nblk, SAMPLE_BLOCK, ts, d).transpose(0, 2, 1, 3).reshape(n_s, d)
    h = jnp.concatenate([x_prompt.reshape(n_p, d), xs], axis=0)
    mem2d = mem_prompt.reshape(batch * n_mem, d)
    tab_p = prompt_tables(seq, dh)
    tab_s = sample_tables(ts, float(PAST_LEN), dh)

    ret_p, conv_p, mk_p, mv_p, ret_s, conv_s = [], [], [], [], [], []
    for l in range(depth):
        proj = norm_matmul(h, norm_mix[l], w_in[l].astype(bf16), "in_proj")
        mix_p, s_p, c_p = prompt_mixer(proj, tab_p, conv_w[l], ret_gn[l], batch, seq, r, cd_width)
        mix_s, s_s, c_s = sample_mixer(proj, tab_s, conv_w[l], ret_gn[l], state_ret[l],
                                       state_conv[l].transpose(1, 0, 2), n_p, ts, r, cd_width)
        h = matmul_res(mix_p, mix_s, w_out[l].astype(bf16), h, "out_proj")

        mkv = norm_matmul(mem2d, norm_mem[l], jnp.concatenate([w_mk[l], w_mv[l]], axis=1).astype(bf16), "mem_kv")
        q = norm_matmul(h, norm_cross[l], w_mq[l].astype(bf16), "q_proj")
        attn_p = cross_prompt(q, mkv, batch, seq, n_mem, d)
        attn_s = cross_sample(q, cache_mem_k[l], cache_mem_v[l], n_p, ts)
        h = matmul_res(attn_p, attn_s, w_mo[l].astype(bf16), h, "o_proj")

        f2 = moe(h, norm_ffn[l], w_router[l], b_router[l], w_up[l], b_up[l], w_down[l], b_down[l])
        if l + 1 < depth:
            tiles = d // LANES
            h = h + f2.reshape(n, tiles, LANES).reshape(n, d)

        ret_p.append(s_p)
        conv_p.append(c_p)
        mk_p.append(mkv[:, :d].reshape(batch, n_mem, N_MEM_HEADS, d // N_MEM_HEADS))
        mv_p.append(mkv[:, d:].reshape(batch, n_mem, N_MEM_HEADS, d // N_MEM_HEADS))
        ret_s.append(s_s)
        conv_s.append(c_s.transpose(1, 0, 2))

    y_p = final_norm(h, f2, norm_final, 0, n_p).reshape(batch, seq, d)
    y_s = final_norm(h, f2, norm_final, n_p, n_s)
    y_s = y_s.reshape(nblk, ts, SAMPLE_BLOCK, d).transpose(0, 2, 1, 3).reshape(n_seq, ts, d)
    return (y_p, y_s, jnp.stack(ret_p), jnp.stack(conv_p), jnp.stack(mk_p), jnp.stack(mv_p), jnp.stack(ret_s),
            jnp.stack(conv_s))
```

```python
import functools

import jax
import jax.numpy as jnp
from jax import lax
from jax.experimental import pallas as pl
from jax.experimental.pallas import tpu as pltpu

f32 = jnp.float32
bf16 = jnp.bfloat16
i32 = jnp.int32

EPS = 1e-6
ROPE_THETA = 10000.0
RET_CHUNK = 128
PAST_LEN = 16384
N_RET_HEADS = 4
N_MEM_HEADS = 4
TOP_K = 4
SWIGLU_ALPHA = 1.702
SWIGLU_LIMIT = 7.0

LANES = 128
SUBLANES = 8
MIB = 1024 * 1024

ROW_TILE = 512
SAMPLE_BLOCK = 8
MOE_CHUNKS = 4
MOE_SUB = 256
SCATTER_UNROLL = 8


def _cparams(sem, vmem_mib):
    return pltpu.CompilerParams(dimension_semantics=sem, vmem_limit_bytes=vmem_mib * MIB)


def _rms(x, g):
    ms = jnp.mean(x * x, axis=-1, keepdims=True)
    return (x * lax.rsqrt(ms + EPS)) * g


def _norm_matmul_kernel(x_ref, g_ref, w_ref, o_ref):
    xn = _rms(x_ref[...], g_ref[...])
    o_ref[...] = jnp.dot(xn.astype(bf16), w_ref[...], preferred_element_type=f32)


def norm_matmul(x, g, w, name):
    m, d = x.shape
    f = w.shape[1]
    tm = min(ROW_TILE, m)
    assert m % tm == 0
    return pl.pallas_call(
        _norm_matmul_kernel,
        out_shape=jax.ShapeDtypeStruct((m, f), f32),
        grid=(m // tm,),
        in_specs=[
            pl.BlockSpec((tm, d), lambda i: (i, 0)),
            pl.BlockSpec((1, d), lambda i: (0, 0)),
            pl.BlockSpec((d, f), lambda i: (0, 0)),
        ],
        out_specs=pl.BlockSpec((tm, f), lambda i: (i, 0)),
        compiler_params=_cparams(("arbitrary",), 48),
        name=name,
    )(x, g.reshape(1, d), w)


def _matmul_res_kernel(ap_ref, as_ref, w_ref, r_ref, o_ref, *, prompt_tiles):
    a = jnp.where(pl.program_id(0) < prompt_tiles, ap_ref[...], as_ref[...])
    o_ref[...] = r_ref[...] + jnp.dot(a, w_ref[...], preferred_element_type=f32)


def matmul_res(a_prompt, a_sample, w, res, name):
    m_p, d = a_prompt.shape
    m_s = a_sample.shape[0]
    f = w.shape[1]
    tm = ROW_TILE
    assert m_p % tm == 0 and m_s == tm
    prompt_tiles = m_p // tm
    return pl.pallas_call(
        functools.partial(_matmul_res_kernel, prompt_tiles=prompt_tiles),
        out_shape=jax.ShapeDtypeStruct((m_p + m_s, f), f32),
        grid=(prompt_tiles + 1,),
        in_specs=[
            pl.BlockSpec((tm, d), lambda i: (jnp.minimum(i, prompt_tiles - 1), 0)),
            pl.BlockSpec((tm, d), lambda i: (0, 0)),
            pl.BlockSpec((d, f), lambda i: (0, 0)),
            pl.BlockSpec((tm, f), lambda i: (i, 0)),
        ],
        out_specs=pl.BlockSpec((tm, f), lambda i: (i, 0)),
        compiler_params=_cparams(("arbitrary",), 32),
        name=name,
    )(a_prompt, a_sample, w, res)


def _rope(x, cos, sin_signed, half):
    return x * cos + pltpu.roll(x, half, 1) * sin_signed


def _group_norm_gate(o, gate, gn):
    mu = jnp.mean(o, axis=-1, keepdims=True)
    var = jnp.mean(jnp.square(o - mu), axis=-1, keepdims=True)
    on = ((o - mu) * lax.rsqrt(var + EPS)) * gn
    return (gate * jax.nn.sigmoid(gate)) * on


def _prompt_mixer_kernel(cd_ref, proj_ref, cos_ref, sin_ref, decay_ref, xi_ref, zeta_ref, convw_ref, gn_ref,
                         mix_ref, sfin_ref, cfin_ref, s_scr, u_scr, *, tt, r, cd_width, dh):
    j = pl.program_id(1)
    heads = r // dh
    pad = SUBLANES

    @pl.when(j == 0)
    def _():
        s_scr[...] = jnp.zeros_like(s_scr)
        u_scr[0:pad, :] = jnp.zeros((pad, cd_width), f32)

    k_scale = dh ** -0.5
    for c in range(tt // RET_CHUNK):
        rows = pl.ds(c * RET_CHUNK, RET_CHUNK)
        cosb = cos_ref[rows, :]
        sinb = sin_ref[rows, :]
        for h in range(heads):
            lo, hi = h * dh, (h + 1) * dh
            qh = _rope(proj_ref[rows, lo:hi], cosb, sinb, dh // 2)
            kh = _rope(proj_ref[rows, r + lo:r + hi], cosb, sinb, dh // 2) * k_scale
            vb = proj_ref[rows, 2 * r + lo:2 * r + hi].astype(bf16)
            gate = proj_ref[rows, 3 * r + lo:3 * r + hi]
            qb = qh.astype(bf16)
            kb = kh.astype(bf16)
            scores = lax.dot_general(qb, kb, (((1,), (1,)), ((), ())), preferred_element_type=f32) * decay_ref[h]
            inner = jnp.dot(scores.astype(bf16), vb, preferred_element_type=f32)
            s_prev = s_scr[h]
            cross = jnp.dot((qh * xi_ref[h]).astype(bf16), s_prev.astype(bf16), preferred_element_type=f32)
            kv = lax.dot_general((kh * zeta_ref[h]).astype(bf16), vb, (((0,), (0,)), ((), ())),
                                 preferred_element_type=f32)
            s_scr[h] = cd_ref[h] * s_prev + kv
            mix_ref[rows, lo:hi] = _group_norm_gate(inner + cross, gate, gn_ref[:, lo:hi]).astype(bf16)

    base = 4 * r
    u = proj_ref[:, base + cd_width:base + 2 * cd_width] * proj_ref[:, base:base + cd_width]
    u_scr[pad:pad + tt, :] = u
    conv = (u_scr[pad - 2:pad - 2 + tt, :] * convw_ref[0:1, :] + u_scr[pad - 1:pad - 1 + tt, :] * convw_ref[1:2, :]
            + u * convw_ref[2:3, :])
    mix_ref[:, r:r + cd_width] = (proj_ref[:, base + 2 * cd_width:base + 3 * cd_width] * conv).astype(bf16)
    u_scr[0:pad, :] = u_scr[tt:tt + pad, :]
    cfin_ref[0] = u_scr[pad - 2:pad, :]

    @pl.when(j == pl.num_programs(1) - 1)
    def _():
        sfin_ref[0] = s_scr[...]


def prompt_mixer(proj, tables, conv_w, ret_gn, batch, seq, r, cd_width):
    tt = ROW_TILE
    dh = r // N_RET_HEADS
    steps = seq // tt
    in_cols = proj.shape[1]
    kern = functools.partial(_prompt_mixer_kernel, tt=tt, r=r, cd_width=cd_width, dh=dh)
    tab = lambda shape: pl.BlockSpec(shape, lambda b, j, cd: (0,) * len(shape))
    return pl.pallas_call(
        kern,
        out_shape=(
            jax.ShapeDtypeStruct((batch * seq, r + cd_width), bf16),
            jax.ShapeDtypeStruct((batch, N_RET_HEADS, dh, dh), f32),
            jax.ShapeDtypeStruct((batch, 2, cd_width), f32),
        ),
        grid_spec=pltpu.PrefetchScalarGridSpec(
            num_scalar_prefetch=1,
            grid=(batch, steps),
            in_specs=[
                pl.BlockSpec((tt, in_cols), lambda b, j, cd: (b * steps + j, 0)),
                pl.BlockSpec((tt, dh), lambda b, j, cd: (j, 0)),
                pl.BlockSpec((tt, dh), lambda b, j, cd: (j, 0)),
                tab((N_RET_HEADS, RET_CHUNK, RET_CHUNK)),
                tab((N_RET_HEADS, RET_CHUNK, dh)),
                tab((N_RET_HEADS, RET_CHUNK, dh)),
                tab((3, cd_width)),
                tab((1, r)),
            ],
            out_specs=(
                pl.BlockSpec((tt, r + cd_width), lambda b, j, cd: (b * steps + j, 0)),
                pl.BlockSpec((1, N_RET_HEADS, dh, dh), lambda b, j, cd: (b, 0, 0, 0)),
                pl.BlockSpec((1, 2, cd_width), lambda b, j, cd: (b, 0, 0)),
            ),
            scratch_shapes=[
                pltpu.VMEM((N_RET_HEADS, dh, dh), f32),
                pltpu.VMEM((tt + 2 * SUBLANES, cd_width), f32),
            ],
        ),
        compiler_params=_cparams(("arbitrary", "arbitrary"), 48),
        name="prompt_mixer",
    )(tables["chunk_decay"], proj, tables["cos"], tables["sin"], tables["decay"], tables["xi"], tables["zeta"],
      conv_w, ret_gn.reshape(1, r))


def _sample_mixer_kernel(dec_ref, proj_ref, cos_ref, sin_ref, xi_ref, zeta_ref, convw_ref, gn_ref, s0_ref, c0_ref,
                         mix_ref, snew_ref, cnew_ref, *, ts, r, cd_width, dh):
    heads = r // dh
    nb = SAMPLE_BLOCK
    rows = ts * nb
    k_scale = dh ** -0.5
    cosb = cos_ref[...]
    sinb = sin_ref[...]
    seq_of_row = lax.broadcasted_iota(i32, (rows, dh), 0) % nb
    for h in range(heads):
        lo, hi = h * dh, (h + 1) * dh
        qh = _rope(proj_ref[:, lo:hi], cosb, sinb, dh // 2)
        kh = _rope(proj_ref[:, r + lo:r + hi], cosb, sinb, dh // 2) * k_scale
        vh = proj_ref[:, 2 * r + lo:2 * r + hi]
        gate = proj_ref[:, 3 * r + lo:3 * r + hi]
        inner = []
        for t in range(ts):
            qt = qh[t * nb:(t + 1) * nb]
            acc = jnp.zeros((nb, dh), f32)
            for s in range(t + 1):
                sc = jnp.sum(qt * kh[s * nb:(s + 1) * nb], axis=-1, keepdims=True) * dec_ref[h * (ts + 1) + t - s]
                acc = acc + sc * vh[s * nb:(s + 1) * nb]
            inner.append(acc)
        inner = jnp.concatenate(inner, axis=0)
        qx = (qh * xi_ref[h]).astype(bf16)
        kz = kh * zeta_ref[h]
        vb = vh.astype(bf16)
        cross = jnp.zeros((rows, dh), f32)
        for b in range(nb):
            mine = seq_of_row == b
            s_prev = s0_ref[b, h]
            res = jnp.dot(qx, s_prev.astype(bf16), preferred_element_type=f32)
            cross = cross + jnp.where(mine, res, 0.0)
            kv = lax.dot_general(jnp.where(mine, kz, 0.0).astype(bf16), vb, (((0,), (0,)), ((), ())),
                                 preferred_element_type=f32)
            snew_ref[b, h] = dec_ref[h * (ts + 1) + ts] * s_prev + kv
        mix_ref[:, lo:hi] = _group_norm_gate(inner + cross, gate, gn_ref[:, lo:hi]).astype(bf16)

    base = 4 * r
    u = proj_ref[:, base + cd_width:base + 2 * cd_width] * proj_ref[:, base:base + cd_width]
    full = [c0_ref[0], c0_ref[1]] + [u[t * nb:(t + 1) * nb] for t in range(ts)]
    conv = jnp.concatenate(
        [full[t] * convw_ref[0:1, :] + full[t + 1] * convw_ref[1:2, :] + full[t + 2] * convw_ref[2:3, :]
         for t in range(ts)], axis=0)
    mix_ref[:, r:r + cd_width] = (proj_ref[:, base + 2 * cd_width:base + 3 * cd_width] * conv).astype(bf16)
    cnew_ref[0] = full[ts]
    cnew_ref[1] = full[ts + 1]


def sample_mixer(proj, tables, conv_w, ret_gn, state_ret, state_conv_t, row0, ts, r, cd_width):
    n_seq = state_ret.shape[0]
    dh = r // N_RET_HEADS
    rows = ts * SAMPLE_BLOCK
    blk0 = row0 // rows
    assert row0 % rows == 0 and n_seq % SAMPLE_BLOCK == 0
    in_cols = proj.shape[1]
    kern = functools.partial(_sample_mixer_kernel, ts=ts, r=r, cd_width=cd_width, dh=dh)
    tab = lambda shape: pl.BlockSpec(shape, lambda i, d: (0,) * len(shape))
    return pl.pallas_call(
        kern,
        out_shape=(
            jax.ShapeDtypeStruct((n_seq * ts, r + cd_width), bf16),
            jax.ShapeDtypeStruct(state_ret.shape, f32),
            jax.ShapeDtypeStruct(state_conv_t.shape, f32),
        ),
        grid_spec=pltpu.PrefetchScalarGridSpec(
            num_scalar_prefetch=1,
            grid=(n_seq // SAMPLE_BLOCK,),
            in_specs=[
                pl.BlockSpec((rows, in_cols), lambda i, d: (blk0 + i, 0)),
                tab((rows, dh)),
                tab((rows, dh)),
                tab((N_RET_HEADS, rows, dh)),
                tab((N_RET_HEADS, rows, dh)),
                tab((3, cd_width)),
                tab((1, r)),
                pl.BlockSpec((SAMPLE_BLOCK, N_RET_HEADS, dh, dh), lambda i, d: (i, 0, 0, 0)),
                pl.BlockSpec((2, SAMPLE_BLOCK, cd_width), lambda i, d: (0, i, 0)),
            ],
            out_specs=(
                pl.BlockSpec((rows, r + cd_width), lambda i, d: (i, 0)),
                pl.BlockSpec((SAMPLE_BLOCK, N_RET_HEADS, dh, dh), lambda i, d: (i, 0, 0, 0)),
                pl.BlockSpec((2, SAMPLE_BLOCK, cd_width), lambda i, d: (0, i, 0)),
            ),
        ),
        compiler_params=_cparams(("arbitrary",), 32),
        name="sample_mixer",
    )(tables["dec"], proj, tables["cos"], tables["sin"], tables["xi"], tables["zeta"], conv_w, ret_gn.reshape(1, r),
      state_ret, state_conv_t)


def _log_gamma():
    return jnp.log1p(-jnp.exp2(-5.0 - jnp.arange(N_RET_HEADS, dtype=f32)))


def _rope_tables(pos, dh):
    half = dh // 2
    inv = ROPE_THETA ** (-jnp.arange(half, dtype=f32) / half)
    ang = pos[:, None] * inv[None, :]
    cos, sin = jnp.cos(ang), jnp.sin(ang)
    return jnp.concatenate([cos, cos], axis=-1), jnp.concatenate([-sin, sin], axis=-1)


def prompt_tables(seq, dh):
    c = RET_CHUNK
    lg = _log_gamma()
    i = jnp.arange(c, dtype=f32)
    diff = i[:, None] - i[None, :]
    decay = jnp.where(diff >= 0, jnp.exp(jnp.maximum(diff, 0.0)[None] * lg[:, None, None]), 0.0)
    xi = jnp.exp((i + 1.0)[None, :] * lg[:, None])
    zeta = jnp.exp((c - 1.0 - i)[None, :] * lg[:, None])
    cos, sin = _rope_tables(jnp.arange(seq, dtype=f32), dh)
    bc = lambda t: jnp.broadcast_to(t[:, :, None], (N_RET_HEADS, c, dh))
    return dict(cos=cos, sin=sin, decay=decay, xi=bc(xi), zeta=bc(zeta), chunk_decay=jnp.exp(c * lg))


def sample_tables(ts, pos0, dh):
    lg = _log_gamma()
    i = jnp.arange(ts, dtype=f32)
    dec = jnp.exp(jnp.arange(ts + 1, dtype=f32)[None, :] * lg[:, None])
    xi = jnp.exp((i + 1.0)[None, :] * lg[:, None])
    zeta = jnp.exp((ts - 1.0 - i)[None, :] * lg[:, None])
    cos, sin = _rope_tables(pos0 + i, dh)
    rep = lambda t: jnp.repeat(t, SAMPLE_BLOCK, axis=0)
    bc = lambda t: jnp.broadcast_to(jnp.repeat(t, SAMPLE_BLOCK, axis=1)[:, :, None],
                                    (N_RET_HEADS, ts * SAMPLE_BLOCK, dh))
    return dict(cos=rep(cos), sin=rep(sin), xi=bc(xi), zeta=bc(zeta), dec=dec.reshape(-1))


def _softmax_rows(s):
    m = jnp.max(s, axis=-1, keepdims=True)
    p = jnp.exp(s - m)
    return p / jnp.sum(p, axis=-1, keepdims=True)


def _cross_prompt_kernel(q_ref, k_ref, v_ref, o_ref, *, dh):
    scale = dh ** -0.5
    for h in range(N_MEM_HEADS):
        cols = slice(h * dh, (h + 1) * dh)
        s = lax.dot_general(q_ref[:, cols].astype(bf16), k_ref[:, cols].astype(bf16), (((1,), (1,)), ((), ())),
                            preferred_element_type=f32) * scale
        p = _softmax_rows(s)
        o_ref[:, cols] = jnp.dot(p.astype(bf16), v_ref[:, cols].astype(bf16), preferred_element_type=f32).astype(bf16)


def cross_prompt(q, mkv, batch, seq, n_mem, d):
    tq = ROW_TILE
    steps = seq // tq
    kern = functools.partial(_cross_prompt_kernel, dh=d // N_MEM_HEADS)
    return pl.pallas_call(
        kern,
        out_shape=jax.ShapeDtypeStruct((batch * seq, d), bf16),
        grid=(batch, steps),
        in_specs=[
            pl.BlockSpec((tq, d), lambda b, j: (b * steps + j, 0)),
            pl.BlockSpec((n_mem, d), lambda b, j: (b, 0)),
            pl.BlockSpec((n_mem, d), lambda b, j: (b, 1)),
        ],
        out_specs=pl.BlockSpec((tq, d), lambda b, j: (b * steps + j, 0)),
        compiler_params=_cparams(("arbitrary", "arbitrary"), 32),
        name="cross_prompt",
    )(q, mkv, mkv)


def _cross_sample_kernel(q_ref, k_ref, v_ref, o_ref, acc_ref, *, dh, seqs):
    part = pl.program_id(1)
    rows = q_ref.shape[0]
    scale = dh ** -0.5

    @pl.when(part == 0)
    def _():
        acc_ref[...] = jnp.zeros_like(acc_ref)

    heads = N_MEM_HEADS
    n_mem = k_ref.shape[1]
    qx = jnp.concatenate([q_ref[:, h * dh:(h + 1) * dh] for h in range(heads)], axis=0).astype(bf16)
    row = lax.broadcasted_iota(i32, (heads * rows, n_mem * heads), 0)
    col = lax.broadcasted_iota(i32, (heads * rows, n_mem * heads), 1)
    same_head = col % heads == row // rows
    seq_of_row = lax.broadcasted_iota(i32, (heads * rows, dh), 0) % SAMPLE_BLOCK
    out = jnp.zeros((heads * rows, dh), f32)
    for b in range(seqs):
        k2 = k_ref[b].reshape(n_mem * heads, dh).astype(bf16)
        v2 = v_ref[b].reshape(n_mem * heads, dh).astype(bf16)
        s = lax.dot_general(qx, k2, (((1,), (1,)), ((), ())), preferred_element_type=f32) * scale
        p = _softmax_rows(jnp.where(same_head, s, -jnp.inf))
        o = jnp.dot(p.astype(bf16), v2, preferred_element_type=f32)
        out = out + jnp.where(seq_of_row == part * seqs + b, o, 0.0)
    acc_ref[...] += out

    @pl.when(part == pl.num_programs(1) - 1)
    def _():
        for h in range(heads):
            o_ref[:, h * dh:(h + 1) * dh] = acc_ref[h * rows:(h + 1) * rows, :].astype(bf16)


def cross_sample(q, cache_k, cache_v, row0, ts):
    n_seq, n_mem, heads, dh = cache_k.shape
    d = heads * dh
    rows = ts * SAMPLE_BLOCK
    blk0 = row0 // rows
    parts = 4
    seqs = SAMPLE_BLOCK // parts
    kern = functools.partial(_cross_sample_kernel, dh=dh, seqs=seqs)
    return pl.pallas_call(
        kern,
        out_shape=jax.ShapeDtypeStruct((n_seq * ts, d), bf16),
        grid=(n_seq // SAMPLE_BLOCK, parts),
        in_specs=[
            pl.BlockSpec((rows, d), lambda i, p: (blk0 + i, 0)),
            pl.BlockSpec((seqs, n_mem, heads, dh), lambda i, p: (i * parts + p, 0, 0, 0)),
            pl.BlockSpec((seqs, n_mem, heads, dh), lambda i, p: (i * parts + p, 0, 0, 0)),
        ],
        out_specs=pl.BlockSpec((rows, d), lambda i, p: (i, 0)),
        scratch_shapes=[pltpu.VMEM((heads * rows, dh), f32)],
        compiler_params=_cparams(("arbitrary", "arbitrary"), 40),
        name="cross_sample",
    )(q, cache_k, cache_v)


def _router_kernel(h_ref, g_ref, wr_ref, br_ref, xp_ref, e_ref, gate_ref, rank_ref, cnt_ref, carry_ref,
                   *, tiles_per_chunk, n_exp):
    i = pl.program_id(0)

    @pl.when(i % tiles_per_chunk == 0)
    def _():
        carry_ref[...] = jnp.zeros_like(carry_ref)

    xn = _rms(h_ref[...], g_ref[...])
    tm, d = xn.shape
    xp_ref[...] = xn

    w = wr_ref[...]
    w_hi = w.astype(bf16)
    w_lo = (w - w_hi.astype(f32)).astype(bf16)
    x_hi = xn.astype(bf16)
    x_lo = (xn - x_hi.astype(f32)).astype(bf16)
    nt = (((1,), (1,)), ((), ()))
    both = lax.dot_general(jnp.concatenate([w_hi, w_lo], axis=0), x_hi, nt, preferred_element_type=f32)
    logits = (both[:n_exp] + both[n_exp:] + lax.dot_general(w_hi, x_lo, nt, preferred_element_type=f32)
              + br_ref[...])
    sub = lax.broadcasted_iota(i32, (n_exp, tm), 0).astype(f32)
    chosen, vals, hots = [], [], []
    work = logits
    for _ in range(TOP_K):
        m = jnp.max(work, axis=0, keepdims=True)
        idx = jnp.min(jnp.where(work == m, sub, float(n_exp)), axis=0, keepdims=True)
        hot = sub == idx
        chosen.append(idx)
        vals.append(m)
        hots.append(hot)
        work = jnp.where(hot, -jnp.inf, work)
    ex = [jnp.exp(v - vals[0]) for v in vals]
    denom = ex[0] + ex[1] + ex[2] + ex[3]
    e_ref[...] = jnp.concatenate(chosen, axis=0).astype(i32)
    gate_ref[...] = jnp.concatenate([x / denom for x in ex], axis=0)

    member = jnp.zeros((n_exp, tm), f32)
    for hot in hots:
        member = member + hot.astype(f32)
    earlier = (lax.broadcasted_iota(i32, (tm, tm), 0) < lax.broadcasted_iota(i32, (tm, tm), 1)).astype(bf16)
    before = jnp.dot(member.astype(bf16), earlier, preferred_element_type=f32) + carry_ref[...]
    rank_ref[...] = jnp.concatenate(
        [jnp.sum(jnp.where(hot, before, 0.0), axis=0, keepdims=True) for hot in hots], axis=0).astype(i32)
    carry_ref[...] += jnp.sum(member, axis=1, keepdims=True)
    cnt_ref[0] = carry_ref[...].astype(i32)


def router(h, g, w_router, b_router, tc):
    n, d = h.shape
    n_exp = w_router.shape[1]
    tm = ROW_TILE
    while tc % tm:
        tm -= LANES
    tiles_per_chunk = tc // tm
    kern = functools.partial(_router_kernel, tiles_per_chunk=tiles_per_chunk, n_exp=n_exp)
    return pl.pallas_call(
        kern,
        out_shape=(
            jax.ShapeDtypeStruct((n, d), f32),
            jax.ShapeDtypeStruct((TOP_K, n), i32),
            jax.ShapeDtypeStruct((TOP_K, n), f32),
            jax.ShapeDtypeStruct((TOP_K, n), i32),
            jax.ShapeDtypeStruct((n // tc, n_exp, 1), i32),
        ),
        grid=(n // tm,),
        in_specs=[
            pl.BlockSpec((tm, d), lambda i: (i, 0)),
            pl.BlockSpec((1, d), lambda i: (0, 0)),
            pl.BlockSpec((n_exp, d), lambda i: (0, 0)),
            pl.BlockSpec((n_exp, 1), lambda i: (0, 0)),
        ],
        out_specs=(
            pl.BlockSpec((tm, d), lambda i: (i, 0)),
            pl.BlockSpec((TOP_K, tm), lambda i: (0, i)),
            pl.BlockSpec((TOP_K, tm), lambda i: (0, i)),
            pl.BlockSpec((TOP_K, tm), lambda i: (0, i)),
            pl.BlockSpec((1, n_exp, 1), lambda i: (i // tiles_per_chunk, 0, 0)),
        ),
        scratch_shapes=[pltpu.VMEM((n_exp, 1), f32)],
        compiler_params=_cparams(("arbitrary",), 32),
        name="router",
    )(h, g.reshape(1, d), w_router.T, b_router.reshape(n_exp, 1))


def _split_pairs_kernel(w_ref, p_ref, o_ref):
    width = p_ref.shape[0]
    for b in range(w_ref.shape[2] // width):
        cols = slice(b * width, (b + 1) * width)
        o_ref[0, :, cols] = jnp.dot(w_ref[0, :, cols].astype(bf16), p_ref[...],
                                    preferred_element_type=f32).astype(bf16)


def split_pairs(w):
    n_exp, d, f2 = w.shape
    width = 2 * LANES
    j = jnp.arange(width)
    src = jnp.where(j < LANES, 2 * j, 2 * (j - LANES) + 1)
    perm = (jnp.arange(width)[:, None] == src[None, :]).astype(bf16)
    return pl.pallas_call(
        _split_pairs_kernel,
        out_shape=jax.ShapeDtypeStruct((n_exp, d, f2), bf16),
        grid=(n_exp,),
        in_specs=[
            pl.BlockSpec((1, d, f2), lambda e: (e, 0, 0)),
            pl.BlockSpec((width, width), lambda e: (0, 0)),
        ],
        out_specs=pl.BlockSpec((1, d, f2), lambda e: (e, 0, 0)),
        compiler_params=_cparams(("arbitrary",), 40),
        name="split_pairs",
    )(w, perm)


def _dense_row_index(r, tiles):
    return (r // SUBLANES) * tiles * SUBLANES + r % SUBLANES


def _experts_kernel(start_ref, nsub_ref, xp_hbm, dest_hbm, gates_hbm, fill_hbm, wup_ref, bup_ref, wdn_ref, bdn_ref,
                    f_hbm, xs_ref, acc_ref, xt_a, xt_b, y_a, y_b, dest_s, gate_s, rmap_s, sem_ref,
                    *, tc, n_exp, d, null_row0):
    c = pl.program_id(0)
    e = pl.program_id(1)
    ms = MOE_SUB
    tiles = d // LANES

    def gather(row0, xt_ref):
        for r in range(ms):
            t = rmap_s[row0 + r] >> 2
            xt_ref[pl.ds(_dense_row_index(r, tiles), tiles, stride=SUBLANES), :] = xs_ref[t]

    def scatter(row0, y_ref):
        for r0 in range(0, ms, SCATTER_UNROLL):
            toks, sums = [], []
            for r in range(r0, r0 + SCATTER_UNROLL):
                a = rmap_s[row0 + r]
                t = a >> 2
                yrow = y_ref[pl.ds(_dense_row_index(r, tiles), tiles, stride=SUBLANES), :]
                toks.append(t)
                sums.append(acc_ref[t] + gate_s[a] * yrow)
            for t, s in zip(toks, sums):
                acc_ref[t] = s

    @pl.when(e == 0)
    def _():
        copies = [
            pltpu.make_async_copy(xp_hbm.at[pl.ds(c * tc, tc)], xs_ref.at[pl.ds(0, tc)], sem_ref.at[0]),
            pltpu.make_async_copy(dest_hbm.at[c], dest_s, sem_ref.at[1]),
            pltpu.make_async_copy(gates_hbm.at[c], gate_s, sem_ref.at[2]),
            pltpu.make_async_copy(fill_hbm, rmap_s, sem_ref.at[3]),
        ]
        for cp in copies:
            cp.start()
        xs_ref[pl.ds(tc, SUBLANES)] = jnp.zeros((SUBLANES, tiles, LANES), f32)
        acc_ref[...] = jnp.zeros_like(acc_ref)

        @pl.when(c == 0)
        def _():
            y_a[...] = jnp.zeros_like(y_a)
            y_b[...] = jnp.zeros_like(y_b)

        for cp in copies:
            cp.wait()

        def place(i, carry):
            for u in range(SUBLANES):
                a = i * SUBLANES + u
                rmap_s[dest_s[a]] = a
            return carry

        lax.fori_loop(0, TOP_K * tc // SUBLANES, place, 0)
        gather(0, xt_a)

    g = c * n_exp + e
    j0 = start_ref[g]

    def step(j, xt_cur, xt_nxt, y_cur, y_prv):
        gather((j + 1) * ms, xt_nxt)
        scatter(jnp.where(j == 0, null_row0, (j - 1) * ms), y_prv)
        x = jnp.concatenate(
            [jnp.concatenate([xt_cur[pl.ds((rg * tiles + jt) * SUBLANES, SUBLANES), :] for jt in range(tiles)], axis=1)
             for rg in range(ms // SUBLANES)], axis=0).astype(bf16)
        hmid = jnp.dot(x, wup_ref[0], preferred_element_type=f32) + bup_ref[0]
        glu = jnp.concatenate([hmid[:, 2 * jt * LANES:(2 * jt + 1) * LANES] for jt in range(tiles)], axis=1)
        lin = jnp.concatenate([hmid[:, (2 * jt + 1) * LANES:(2 * jt + 2) * LANES] for jt in range(tiles)], axis=1)
        glu = jnp.minimum(glu, SWIGLU_LIMIT)
        lin = jnp.clip(lin, -SWIGLU_LIMIT, SWIGLU_LIMIT)
        act = glu * jax.nn.sigmoid(SWIGLU_ALPHA * glu) * (lin + 1.0)
        y = jnp.dot(act.astype(bf16), wdn_ref[0], preferred_element_type=f32) + bdn_ref[0]
        for rg in range(ms // SUBLANES):
            for jt in range(tiles):
                y_cur[pl.ds((rg * tiles + jt) * SUBLANES, SUBLANES), :] = (
                    y[rg * SUBLANES:(rg + 1) * SUBLANES, jt * LANES:(jt + 1) * LANES])

    def sub_block(i, carry):
        j = j0 + i

        @pl.when(j % 2 == 0)
        def _():
            step(j, xt_a, xt_b, y_a, y_b)

        @pl.when(j % 2 == 1)
        def _():
            step(j, xt_b, xt_a, y_b, y_a)

        return carry

    lax.fori_loop(0, nsub_ref[g], sub_block, 0)

    @pl.when(e == n_exp - 1)
    def _():
        last = j0 + nsub_ref[g] - 1

        @pl.when(last % 2 == 0)
        def _():
            scatter(last * ms, y_a)

        @pl.when(last % 2 == 1)
        def _():
            scatter(last * ms, y_b)

        cp = pltpu.make_async_copy(acc_ref.at[pl.ds(0, tc)], f_hbm.at[pl.ds(c * tc, tc)], sem_ref.at[0])
        cp.start()
        cp.wait()


def experts(sb_start, n_sub, xp, dest, gates, w_up, b_up, w_down, b_down, tc):
    n, tiles, _ = xp.shape
    d = tiles * LANES
    n_exp = w_up.shape[0]
    chunks, padded_len = dest.shape
    max_sub = (TOP_K * tc + n_exp * (MOE_SUB - 1)) // MOE_SUB
    null_row0 = (max_sub + 1) * MOE_SUB
    map_len = null_row0 + MOE_SUB
    fill = jnp.full((map_len,), TOP_K * tc, i32)
    kern = functools.partial(_experts_kernel, tc=tc, n_exp=n_exp, d=d, null_row0=null_row0)
    block = pltpu.VMEM((MOE_SUB * tiles, LANES), f32)
    return pl.pallas_call(
        kern,
        out_shape=jax.ShapeDtypeStruct((n, tiles, LANES), f32),
        grid_spec=pltpu.PrefetchScalarGridSpec(
            num_scalar_prefetch=2,
            grid=(chunks, n_exp),
            in_specs=[
                pl.BlockSpec(memory_space=pl.ANY),
                pl.BlockSpec(memory_space=pl.ANY),
                pl.BlockSpec(memory_space=pl.ANY),
                pl.BlockSpec(memory_space=pl.ANY),
                pl.BlockSpec((1, d, 2 * d), lambda c, e, *_: (e, 0, 0)),
                pl.BlockSpec((1, 1, 2 * d), lambda c, e, *_: (e, 0, 0)),
                pl.BlockSpec((1, d, d), lambda c, e, *_: (e, 0, 0)),
                pl.BlockSpec((1, 1, d), lambda c, e, *_: (e, 0, 0)),
            ],
            out_specs=pl.BlockSpec(memory_space=pl.ANY),
            scratch_shapes=[
                pltpu.VMEM((tc + SUBLANES, tiles, LANES), f32),
                pltpu.VMEM((tc + SUBLANES, tiles, LANES), f32),
                block, block, block, block,
                pltpu.SMEM((padded_len,), i32),
                pltpu.SMEM((padded_len,), f32),
                pltpu.SMEM((map_len,), i32),
                pltpu.SemaphoreType.DMA((4,)),
            ],
        ),
        compiler_params=_cparams(("arbitrary", "arbitrary"), 58),
        name="experts",
    )(sb_start, n_sub, xp, dest, gates, fill, w_up, b_up.reshape(n_exp, 1, 2 * d), w_down,
      b_down.reshape(n_exp, 1, d))


def _final_kernel(h_ref, f_ref, g_ref, o_ref):
    tm, d = h_ref.shape
    tiles = d // LANES
    parts = [h_ref[:, jt * LANES:(jt + 1) * LANES] + f_ref[pl.ds(jt, tm, stride=tiles), :] for jt in range(tiles)]
    o_ref[...] = _rms(jnp.concatenate(parts, axis=1), g_ref[...])


def final_norm(h, f2, g, row0, rows):
    d = h.shape[1]
    tiles = d // LANES
    tm = min(ROW_TILE, rows)
    blk0 = row0 // tm
    assert rows % tm == 0 and row0 % tm == 0
    return pl.pallas_call(
        _final_kernel,
        out_shape=jax.ShapeDtypeStruct((rows, d), f32),
        grid=(rows // tm,),
        in_specs=[
            pl.BlockSpec((tm, d), lambda i: (blk0 + i, 0)),
            pl.BlockSpec((tm * tiles, LANES), lambda i: (blk0 + i, 0)),
            pl.BlockSpec((1, d), lambda i: (0, 0)),
        ],
        out_specs=pl.BlockSpec((tm, d), lambda i: (i, 0)),
        compiler_params=_cparams(("arbitrary",), 32),
        name="final_norm",
    )(h, f2, g.reshape(1, d))


def moe(h, g, w_router, b_router, w_up, b_up, w_down, b_down):
    n, d = h.shape
    n_exp = w_router.shape[1]
    tc = n // MOE_CHUNKS
    xp, top_e, gates, rank, counts = router(h, g, w_router, b_router, tc)
    n_sub = (counts.reshape(MOE_CHUNKS, n_exp) + MOE_SUB - 1) // MOE_SUB
    sb_start = jnp.cumsum(n_sub, axis=1) - n_sub
    hot = top_e.reshape(TOP_K, MOE_CHUNKS, tc, 1) == jnp.arange(n_exp, dtype=i32)
    row_base = (sb_start * MOE_SUB).reshape(1, MOE_CHUNKS, 1, n_exp)
    dest = jnp.sum(jnp.where(hot, row_base, 0), axis=-1) + rank.reshape(TOP_K, MOE_CHUNKS, tc)
    padded_len = -(-(TOP_K * tc + 1) // 1024) * 1024
    per_chunk = lambda a: jnp.pad(a.transpose(1, 2, 0).reshape(MOE_CHUNKS, TOP_K * tc),
                                  ((0, 0), (0, padded_len - TOP_K * tc)))
    tiles = d // LANES
    b_up_p = b_up.reshape(n_exp, tiles, LANES, 2).transpose(0, 1, 3, 2).reshape(n_exp, 2 * d)
    f = experts(sb_start.reshape(-1).astype(i32), n_sub.reshape(-1).astype(i32), xp.reshape(n, tiles, LANES),
                per_chunk(dest), per_chunk(gates.reshape(TOP_K, MOE_CHUNKS, tc)), split_pairs(w_up), b_up_p,
                w_down.astype(bf16), b_down, tc)
    return f.reshape(n * tiles, LANES)


def kernel(x_prompt, x_sample, mem_prompt, state_ret, state_conv, cache_mem_k, cache_mem_v, norm_mix, w_in, conv_w,
           ret_gn, w_out, norm_cross, norm_mem, w_mq, w_mk, w_mv, w_mo, norm_ffn, w_router, b_router, w_up, b_up,
           w_down, b_down, norm_final):
    batch, seq, d = x_prompt.shape
    n_seq, ts, _ = x_sample.shape
    depth = w_in.shape[0]
    n_mem = mem_prompt.shape[1]
    r = ret_gn.shape[1]
    cd_width = conv_w.shape[2]
    dh = r // N_RET_HEADS
    n_p = batch * seq
    n_s = n_seq * ts
    n = n_p + n_s
    nblk = n_seq // SAMPLE_BLOCK
    assert seq % ROW_TILE == 0 and n % ROW_TILE == 0 and n % MOE_CHUNKS == 0

    xs = x_sample.reshape(nblk, SAMPLE_BLOCK, ts, d).transpose(0, 2, 1, 3).reshape(n_s, d)
    h = jnp.concatenate([x_prompt.reshape(n_p, d), xs], axis=0)
    mem2d = mem_prompt.reshape(batch * n_mem, d)
    tab_p = prompt_tables(seq, dh)
    tab_s = sample_tables(ts, float(PAST_LEN), dh)

    ret_p, conv_p, mk_p, mv_p, ret_s, conv_s = [], [], [], [], [], []
    for l in range(depth):
        proj = norm_matmul(h, norm_mix[l], w_in[l].astype(bf16), "in_proj")
        mix_p, s_p, c_p = prompt_mixer(proj, tab_p, conv_w[l], ret_gn[l], batch, seq, r, cd_width)
        mix_s, s_s, c_s = sample_mixer(proj, tab_s, conv_w[l], ret_gn[l], state_ret[l],
                                       state_conv[l].transpose(1, 0, 2), n_p, ts, r, cd_width)
        h = matmul_res(mix_p, mix_s, w_out[l].astype(bf16), h, "out_proj")

        mkv = norm_matmul(mem2d, norm_mem[l], jnp.concatenate([w_mk[l], w_mv[l]], axis=1).astype(bf16), "mem_kv")
        q = norm_matmul(h, norm_cross[l], w_mq[l].astype(bf16), "q_proj")
        attn_p = cross_prompt(q, mkv, batch, seq, n_mem, d)
        attn_s = cross_sample(q, cache_mem_k[l], cache_mem_v[l], n_p, ts)
        h = matmul_res(attn_p, attn_s, w_mo[l].astype(bf16), h, "o_proj")

        f2 = moe(h, norm_ffn[l], w_router[l], b_router[l], w_up[l], b_up[l], w_down[l], b_down[l])
        if l + 1 < depth:
            tiles = d // LANES
            h = h + f2.reshape(n, tiles, LANES).reshape(n, d)

        ret_p.append(s_p)
        conv_p.append(c_p)
        mk_p.append(mkv[:, :d].reshape(batch, n_mem, N_MEM_HEADS, d // N_MEM_HEADS))
        mv_p.append(mkv[:, d:].reshape(batch, n_mem, N_MEM_HEADS, d // N_MEM_HEADS))
        ret_s.append(s_s)
        conv_s.append(c_s.transpose(1, 0, 2))

    y_p = final_norm(h, f2, norm_final, 0, n_p).reshape(batch, seq, d)
    y_s = final_norm(h, f2, norm_final, n_p, n_s)
    y_s = y_s.reshape(nblk, ts, SAMPLE_BLOCK, d).transpose(0, 2, 1, 3).reshape(n_seq, ts, d)
    return (y_p, y_s, jnp.stack(ret_p), jnp.stack(conv_p), jnp.stack(mk_p), jnp.stack(mv_p), jnp.stack(ret_s),
            jnp.stack(conv_s))
```

```python
import functools

import jax
import jax.numpy as jnp
from jax import lax
from jax.experimental import pallas as pl
from jax.experimental.pallas import tpu as pltpu

f32 = jnp.float32
bf16 = jnp.bfloat16
i32 = jnp.int32

EPS = 1e-6
ROPE_THETA = 10000.0
RET_CHUNK = 128
PAST_LEN = 16384
N_RET_HEADS = 4
N_MEM_HEADS = 4
TOP_K = 4
SWIGLU_ALPHA = 1.702
SWIGLU_LIMIT = 7.0

LANES = 128
SUBLANES = 8
MIB = 1024 * 1024

ROW_TILE = 512
SAMPLE_BLOCK = 8
MOE_CHUNKS = 4
MOE_SUB = 256
SCATTER_UNROLL = 8


def _cparams(sem, vmem_mib):
    return pltpu.CompilerParams(dimension_semantics=sem, vmem_limit_bytes=vmem_mib * MIB)


def _rms(x, g):
    ms = jnp.mean(x * x, axis=-1, keepdims=True)
    return (x * lax.rsqrt(ms + EPS)) * g


def _norm_matmul_kernel(x_ref, g_ref, w_ref, o_ref):
    xn = _rms(x_ref[...], g_ref[...])
    o_ref[...] = jnp.dot(xn.astype(bf16), w_ref[...], preferred_element_type=f32)


def norm_matmul(x, g, w, name):
    m, d = x.shape
    f = w.shape[1]
    tm = min(ROW_TILE, m)
    assert m % tm == 0
    return pl.pallas_call(
        _norm_matmul_kernel,
        out_shape=jax.ShapeDtypeStruct((m, f), f32),
        grid=(m // tm,),
        in_specs=[
            pl.BlockSpec((tm, d), lambda i: (i, 0)),
            pl.BlockSpec((1, d), lambda i: (0, 0)),
            pl.BlockSpec((d, f), lambda i: (0, 0)),
        ],
        out_specs=pl.BlockSpec((tm, f), lambda i: (i, 0)),
        compiler_params=_cparams(("arbitrary",), 48),
        name=name,
    )(x, g.reshape(1, d), w)


def _stacked_specs(m_p, m_s, width, stacked=False):
    tm = ROW_TILE
    assert m_p % tm == 0 and m_s == tm
    prompt_tiles = m_p // tm
    sample_block = prompt_tiles if stacked else 0
    return prompt_tiles, [
        pl.BlockSpec((tm, width), lambda i: (jnp.minimum(i, prompt_tiles - 1), 0)),
        pl.BlockSpec((tm, width), lambda i: (sample_block, 0)),
    ]


def _stacked_tile(p_ref, s_ref, prompt_tiles):
    return jnp.where(pl.program_id(0) < prompt_tiles, p_ref[...], s_ref[...])


def _norm_matmul2_kernel(xp_ref, xs_ref, g_ref, w_ref, o_ref, *, prompt_tiles):
    xn = _rms(_stacked_tile(xp_ref, xs_ref, prompt_tiles), g_ref[...])
    o_ref[...] = jnp.dot(xn.astype(bf16), w_ref[...], preferred_element_type=f32)


def norm_matmul2(x_prompt, x_sample, g, w, name):
    m_p, d = x_prompt.shape
    m_s = x_sample.shape[0]
    f = w.shape[1]
    prompt_tiles, x_specs = _stacked_specs(m_p, m_s, d)
    return pl.pallas_call(
        functools.partial(_norm_matmul2_kernel, prompt_tiles=prompt_tiles),
        out_shape=jax.ShapeDtypeStruct((m_p + m_s, f), f32),
        grid=(prompt_tiles + 1,),
        in_specs=x_specs + [
            pl.BlockSpec((1, d), lambda i: (0, 0)),
            pl.BlockSpec((d, f), lambda i: (0, 0)),
        ],
        out_specs=pl.BlockSpec((ROW_TILE, f), lambda i: (i, 0)),
        compiler_params=_cparams(("arbitrary",), 48),
        name=name,
    )(x_prompt, x_sample, g.reshape(1, d), w)


def _matmul_res_kernel(ap_ref, as_ref, w_ref, rp_ref, rs_ref, o_ref, *, prompt_tiles):
    a = _stacked_tile(ap_ref, as_ref, prompt_tiles)
    res = _stacked_tile(rp_ref, rs_ref, prompt_tiles)
    o_ref[...] = res + jnp.dot(a, w_ref[...], preferred_element_type=f32)


def matmul_res(a_prompt, a_sample, w, res_prompt, res_sample, name):
    m_p, d = a_prompt.shape
    m_s = a_sample.shape[0]
    f = w.shape[1]
    prompt_tiles, a_specs = _stacked_specs(m_p, m_s, d)
    _, r_specs = _stacked_specs(m_p, m_s, f, stacked=res_prompt is res_sample)
    return pl.pallas_call(
        functools.partial(_matmul_res_kernel, prompt_tiles=prompt_tiles),
        out_shape=jax.ShapeDtypeStruct((m_p + m_s, f), f32),
        grid=(prompt_tiles + 1,),
        in_specs=a_specs + [pl.BlockSpec((d, f), lambda i: (0, 0))] + r_specs,
        out_specs=pl.BlockSpec((ROW_TILE, f), lambda i: (i, 0)),
        compiler_params=_cparams(("arbitrary",), 32),
        name=name,
    )(a_prompt, a_sample, w, res_prompt, res_sample)


def _rope(x, cos, sin_signed, half):
    return x * cos + pltpu.roll(x, half, 1) * sin_signed


def _group_norm_gate(o, gate, gn):
    mu = jnp.mean(o, axis=-1, keepdims=True)
    var = jnp.mean(jnp.square(o - mu), axis=-1, keepdims=True)
    on = ((o - mu) * lax.rsqrt(var + EPS)) * gn
    return (gate * jax.nn.sigmoid(gate)) * on


def _prompt_mixer_kernel(cd_ref, proj_ref, cos_ref, sin_ref, decay_ref, xi_ref, zeta_ref, convw_ref, gn_ref,
                         mix_ref, sfin_ref, cfin_ref, s_scr, u_scr, *, tt, r, cd_width, dh):
    j = pl.program_id(1)
    heads = r // dh
    pad = SUBLANES

    @pl.when(j == 0)
    def _():
        s_scr[...] = jnp.zeros_like(s_scr)
        u_scr[0:pad, :] = jnp.zeros((pad, cd_width), f32)

    k_scale = dh ** -0.5
    for c in range(tt // RET_CHUNK):
        rows = pl.ds(c * RET_CHUNK, RET_CHUNK)
        cosb = cos_ref[rows, :]
        sinb = sin_ref[rows, :]
        for h in range(heads):
            lo, hi = h * dh, (h + 1) * dh
            qh = _rope(proj_ref[rows, lo:hi], cosb, sinb, dh // 2)
            kh = _rope(proj_ref[rows, r + lo:r + hi], cosb, sinb, dh // 2) * k_scale
            vb = proj_ref[rows, 2 * r + lo:2 * r + hi].astype(bf16)
            gate = proj_ref[rows, 3 * r + lo:3 * r + hi]
            qb = qh.astype(bf16)
            kb = kh.astype(bf16)
            scores = lax.dot_general(qb, kb, (((1,), (1,)), ((), ())), preferred_element_type=f32) * decay_ref[h]
            inner = jnp.dot(scores.astype(bf16), vb, preferred_element_type=f32)
            s_prev = s_scr[h]
            cross = jnp.dot((qh * xi_ref[h]).astype(bf16), s_prev.astype(bf16), preferred_element_type=f32)
            kv = lax.dot_general((kh * zeta_ref[h]).astype(bf16), vb, (((0,), (0,)), ((), ())),
                                 preferred_element_type=f32)
            s_scr[h] = cd_ref[h] * s_prev + kv
            mix_ref[rows, lo:hi] = _group_norm_gate(inner + cross, gate, gn_ref[:, lo:hi]).astype(bf16)

    base = 4 * r
    u = proj_ref[:, base + cd_width:base + 2 * cd_width] * proj_ref[:, base:base + cd_width]
    u_scr[pad:pad + tt, :] = u
    conv = (u_scr[pad - 2:pad - 2 + tt, :] * convw_ref[0:1, :] + u_scr[pad - 1:pad - 1 + tt, :] * convw_ref[1:2, :]
            + u * convw_ref[2:3, :])
    mix_ref[:, r:r + cd_width] = (proj_ref[:, base + 2 * cd_width:base + 3 * cd_width] * conv).astype(bf16)
    u_scr[0:pad, :] = u_scr[tt:tt + pad, :]
    cfin_ref[0] = u_scr[pad - 2:pad, :]

    @pl.when(j == pl.num_programs(1) - 1)
    def _():
        sfin_ref[0] = s_scr[...]


def prompt_mixer(proj, tables, conv_w, ret_gn, batch, seq, r, cd_width):
    tt = ROW_TILE
    dh = r // N_RET_HEADS
    steps = seq // tt
    in_cols = proj.shape[1]
    kern = functools.partial(_prompt_mixer_kernel, tt=tt, r=r, cd_width=cd_width, dh=dh)
    tab = lambda shape: pl.BlockSpec(shape, lambda b, j, cd: (0,) * len(shape))
    return pl.pallas_call(
        kern,
        out_shape=(
            jax.ShapeDtypeStruct((batch * seq, r + cd_width), bf16),
            jax.ShapeDtypeStruct((batch, N_RET_HEADS, dh, dh), f32),
            jax.ShapeDtypeStruct((batch, 2, cd_width), f32),
        ),
        grid_spec=pltpu.PrefetchScalarGridSpec(
            num_scalar_prefetch=1,
            grid=(batch, steps),
            in_specs=[
                pl.BlockSpec((tt, in_cols), lambda b, j, cd: (b * steps + j, 0)),
                pl.BlockSpec((tt, dh), lambda b, j, cd: (j, 0)),
                pl.BlockSpec((tt, dh), lambda b, j, cd: (j, 0)),
                tab((N_RET_HEADS, RET_CHUNK, RET_CHUNK)),
                tab((N_RET_HEADS, RET_CHUNK, dh)),
                tab((N_RET_HEADS, RET_CHUNK, dh)),
                tab((3, cd_width)),
                tab((1, r)),
            ],
            out_specs=(
                pl.BlockSpec((tt, r + cd_width), lambda b, j, cd: (b * steps + j, 0)),
                pl.BlockSpec((1, N_RET_HEADS, dh, dh), lambda b, j, cd: (b, 0, 0, 0)),
                pl.BlockSpec((1, 2, cd_width), lambda b, j, cd: (b, 0, 0)),
            ),
            scratch_shapes=[
                pltpu.VMEM((N_RET_HEADS, dh, dh), f32),
                pltpu.VMEM((tt + 2 * SUBLANES, cd_width), f32),
            ],
        ),
        compiler_params=_cparams(("arbitrary", "arbitrary"), 48),
        name="prompt_mixer",
    )(tables["chunk_decay"], proj, tables["cos"], tables["sin"], tables["decay"], tables["xi"], tables["zeta"],
      conv_w, ret_gn.reshape(1, r))


def _sample_mixer_kernel(dec_ref, proj_ref, cos_ref, sin_ref, xi_ref, zeta_ref, convw_ref, gn_ref, s0_ref, c0_ref,
                         mix_ref, snew_ref, cnew_ref, *, ts, r, cd_width, dh):
    heads = r // dh
    nb = SAMPLE_BLOCK
    rows = ts * nb
    k_scale = dh ** -0.5
    cosb = cos_ref[...]
    sinb = sin_ref[...]
    seq_of_row = lax.broadcasted_iota(i32, (rows, dh), 0) % nb
    for h in range(heads):
        lo, hi = h * dh, (h + 1) * dh
        qh = _rope(proj_ref[:, lo:hi], cosb, sinb, dh // 2)
        kh = _rope(proj_ref[:, r + lo:r + hi], cosb, sinb, dh // 2) * k_scale
        vh = proj_ref[:, 2 * r + lo:2 * r + hi]
        gate = proj_ref[:, 3 * r + lo:3 * r + hi]
        inner = []
        for t in range(ts):
            qt = qh[t * nb:(t + 1) * nb]
            acc = jnp.zeros((nb, dh), f32)
            for s in range(t + 1):
                sc = jnp.sum(qt * kh[s * nb:(s + 1) * nb], axis=-1, keepdims=True) * dec_ref[h * (ts + 1) + t - s]
                acc = acc + sc * vh[s * nb:(s + 1) * nb]
            inner.append(acc)
        inner = jnp.concatenate(inner, axis=0)
        qx = (qh * xi_ref[h]).astype(bf16)
        kz = kh * zeta_ref[h]
        vb = vh.astype(bf16)
        cross = jnp.zeros((rows, dh), f32)
        for b in range(nb):
            mine = seq_of_row == b
            s_prev = s0_ref[b, h]
            res = jnp.dot(qx, s_prev.astype(bf16), preferred_element_type=f32)
            cross = cross + jnp.where(mine, res, 0.0)
            kv = lax.dot_general(jnp.where(mine, kz, 0.0).astype(bf16), vb, (((0,), (0,)), ((), ())),
                                 preferred_element_type=f32)
            snew_ref[b, h] = dec_ref[h * (ts + 1) + ts] * s_prev + kv
        mix_ref[:, lo:hi] = _group_norm_gate(inner + cross, gate, gn_ref[:, lo:hi]).astype(bf16)

    base = 4 * r
    u = proj_ref[:, base + cd_width:base + 2 * cd_width] * proj_ref[:, base:base + cd_width]
    full = [c0_ref[0], c0_ref[1]] + [u[t * nb:(t + 1) * nb] for t in range(ts)]
    conv = jnp.concatenate(
        [full[t] * convw_ref[0:1, :] + full[t + 1] * convw_ref[1:2, :] + full[t + 2] * convw_ref[2:3, :]
         for t in range(ts)], axis=0)
    mix_ref[:, r:r + cd_width] = (proj_ref[:, base + 2 * cd_width:base + 3 * cd_width] * conv).astype(bf16)
    cnew_ref[0] = full[ts]
    cnew_ref[1] = full[ts + 1]


def sample_mixer(proj, tables, conv_w, ret_gn, state_ret, state_conv_t, row0, ts, r, cd_width):
    n_seq = state_ret.shape[0]
    dh = r // N_RET_HEADS
    rows = ts * SAMPLE_BLOCK
    blk0 = row0 // rows
    assert row0 % rows == 0 and n_seq % SAMPLE_BLOCK == 0
    in_cols = proj.shape[1]
    kern = functools.partial(_sample_mixer_kernel, ts=ts, r=r, cd_width=cd_width, dh=dh)
    tab = lambda shape: pl.BlockSpec(shape, lambda i, d: (0,) * len(shape))
    return pl.pallas_call(
        kern,
        out_shape=(
            jax.ShapeDtypeStruct((n_seq * ts, r + cd_width), bf16),
            jax.ShapeDtypeStruct(state_ret.shape, f32),
            jax.ShapeDtypeStruct(state_conv_t.shape, f32),
        ),
        grid_spec=pltpu.PrefetchScalarGridSpec(
            num_scalar_prefetch=1,
            grid=(n_seq // SAMPLE_BLOCK,),
            in_specs=[
                pl.BlockSpec((rows, in_cols), lambda i, d: (blk0 + i, 0)),
                tab((rows, dh)),
                tab((rows, dh)),
                tab((N_RET_HEADS, rows, dh)),
                tab((N_RET_HEADS, rows, dh)),
                tab((3, cd_width)),
                tab((1, r)),
                pl.BlockSpec((SAMPLE_BLOCK, N_RET_HEADS, dh, dh), lambda i, d: (i, 0, 0, 0)),
                pl.BlockSpec((2, SAMPLE_BLOCK, cd_width), lambda i, d: (0, i, 0)),
            ],
            out_specs=(
                pl.BlockSpec((rows, r + cd_width), lambda i, d: (i, 0)),
                pl.BlockSpec((SAMPLE_BLOCK, N_RET_HEADS, dh, dh), lambda i, d: (i, 0, 0, 0)),
                pl.BlockSpec((2, SAMPLE_BLOCK, cd_width), lambda i, d: (0, i, 0)),
            ),
        ),
        compiler_params=_cparams(("arbitrary",), 32),
        name="sample_mixer",
    )(tables["dec"], proj, tables["cos"], tables["sin"], tables["xi"], tables["zeta"], conv_w, ret_gn.reshape(1, r),
      state_ret, state_conv_t)


def _log_gamma():
    return jnp.log1p(-jnp.exp2(-5.0 - jnp.arange(N_RET_HEADS, dtype=f32)))


def _rope_tables(pos, dh):
    half = dh // 2
    inv = ROPE_THETA ** (-jnp.arange(half, dtype=f32) / half)
    ang = pos[:, None] * inv[None, :]
    cos, sin = jnp.cos(ang), jnp.sin(ang)
    return jnp.concatenate([cos, cos], axis=-1), jnp.concatenate([-sin, sin], axis=-1)


def prompt_tables(seq, dh):
    c = RET_CHUNK
    lg = _log_gamma()
    i = jnp.arange(c, dtype=f32)
    diff = i[:, None] - i[None, :]
    decay = jnp.where(diff >= 0, jnp.exp(jnp.maximum(diff, 0.0)[None] * lg[:, None, None]), 0.0)
    xi = jnp.exp((i + 1.0)[None, :] * lg[:, None])
    zeta = jnp.exp((c - 1.0 - i)[None, :] * lg[:, None])
    cos, sin = _rope_tables(jnp.arange(seq, dtype=f32), dh)
    bc = lambda t: jnp.broadcast_to(t[:, :, None], (N_RET_HEADS, c, dh))
    return dict(cos=cos, sin=sin, decay=decay, xi=bc(xi), zeta=bc(zeta), chunk_decay=jnp.exp(c * lg))


def sample_tables(ts, pos0, dh):
    lg = _log_gamma()
    i = jnp.arange(ts, dtype=f32)
    dec = jnp.exp(jnp.arange(ts + 1, dtype=f32)[None, :] * lg[:, None])
    xi = jnp.exp((i + 1.0)[None, :] * lg[:, None])
    zeta = jnp.exp((ts - 1.0 - i)[None, :] * lg[:, None])
    cos, sin = _rope_tables(pos0 + i, dh)
    rep = lambda t: jnp.repeat(t, SAMPLE_BLOCK, axis=0)
    bc = lambda t: jnp.broadcast_to(jnp.repeat(t, SAMPLE_BLOCK, axis=1)[:, :, None],
                                    (N_RET_HEADS, ts * SAMPLE_BLOCK, dh))
    return dict(cos=rep(cos), sin=rep(sin), xi=bc(xi), zeta=bc(zeta), dec=dec.reshape(-1))


def _softmax_rows(s):
    m = jnp.max(s, axis=-1, keepdims=True)
    p = jnp.exp(s - m)
    return p / jnp.sum(p, axis=-1, keepdims=True)


def _cross_prompt_kernel(q_ref, k_ref, v_ref, o_ref, *, dh):
    scale = dh ** -0.5
    for h in range(N_MEM_HEADS):
        cols = slice(h * dh, (h + 1) * dh)
        s = lax.dot_general(q_ref[:, cols].astype(bf16), k_ref[:, cols].astype(bf16), (((1,), (1,)), ((), ())),
                            preferred_element_type=f32) * scale
        p = _softmax_rows(s)
        o_ref[:, cols] = jnp.dot(p.astype(bf16), v_ref[:, cols].astype(bf16), preferred_element_type=f32).astype(bf16)


def cross_prompt(q, mkv, batch, seq, n_mem, d):
    tq = ROW_TILE
    steps = seq // tq
    kern = functools.partial(_cross_prompt_kernel, dh=d // N_MEM_HEADS)
    return pl.pallas_call(
        kern,
        out_shape=jax.ShapeDtypeStruct((batch * seq, d), bf16),
        grid=(batch, steps),
        in_specs=[
            pl.BlockSpec((tq, d), lambda b, j: (b * steps + j, 0)),
            pl.BlockSpec((n_mem, d), lambda b, j: (b, 0)),
            pl.BlockSpec((n_mem, d), lambda b, j: (b, 1)),
        ],
        out_specs=pl.BlockSpec((tq, d), lambda b, j: (b * steps + j, 0)),
        compiler_params=_cparams(("arbitrary", "arbitrary"), 32),
        name="cross_prompt",
    )(q, mkv, mkv)


def _cross_sample_kernel(q_ref, k_ref, v_ref, o_ref, acc_ref, *, dh, seqs):
    part = pl.program_id(1)
    rows = q_ref.shape[0]
    scale = dh ** -0.5

    @pl.when(part == 0)
    def _():
        acc_ref[...] = jnp.zeros_like(acc_ref)

    heads = N_MEM_HEADS
    n_mem = k_ref.shape[1]
    qx = jnp.concatenate([q_ref[:, h * dh:(h + 1) * dh] for h in range(heads)], axis=0).astype(bf16)
    row = lax.broadcasted_iota(i32, (heads * rows, n_mem * heads), 0)
    col = lax.broadcasted_iota(i32, (heads * rows, n_mem * heads), 1)
    same_head = col % heads == row // rows
    seq_of_row = lax.broadcasted_iota(i32, (heads * rows, dh), 0) % SAMPLE_BLOCK
    out = jnp.zeros((heads * rows, dh), f32)
    for b in range(seqs):
        k2 = k_ref[b].reshape(n_mem * heads, dh).astype(bf16)
        v2 = v_ref[b].reshape(n_mem * heads, dh).astype(bf16)
        s = lax.dot_general(qx, k2, (((1,), (1,)), ((), ())), preferred_element_type=f32) * scale
        p = _softmax_rows(jnp.where(same_head, s, -jnp.inf))
        o = jnp.dot(p.astype(bf16), v2, preferred_element_type=f32)
        out = out + jnp.where(seq_of_row == part * seqs + b, o, 0.0)
    acc_ref[...] += out

    @pl.when(part == pl.num_programs(1) - 1)
    def _():
        for h in range(heads):
            o_ref[:, h * dh:(h + 1) * dh] = acc_ref[h * rows:(h + 1) * rows, :].astype(bf16)


def cross_sample(q, cache_k, cache_v, row0, ts):
    n_seq, n_mem, heads, dh = cache_k.shape
    d = heads * dh
    rows = ts * SAMPLE_BLOCK
    blk0 = row0 // rows
    parts = 4
    seqs = SAMPLE_BLOCK // parts
    kern = functools.partial(_cross_sample_kernel, dh=dh, seqs=seqs)
    return pl.pallas_call(
        kern,
        out_shape=jax.ShapeDtypeStruct((n_seq * ts, d), bf16),
        grid=(n_seq // SAMPLE_BLOCK, parts),
        in_specs=[
            pl.BlockSpec((rows, d), lambda i, p: (blk0 + i, 0)),
            pl.BlockSpec((seqs, n_mem, heads, dh), lambda i, p: (i * parts + p, 0, 0, 0)),
            pl.BlockSpec((seqs, n_mem, heads, dh), lambda i, p: (i * parts + p, 0, 0, 0)),
        ],
        out_specs=pl.BlockSpec((rows, d), lambda i, p: (i, 0)),
        scratch_shapes=[pltpu.VMEM((heads * rows, dh), f32)],
        compiler_params=_cparams(("arbitrary", "arbitrary"), 40),
        name="cross_sample",
    )(q, cache_k, cache_v)


def _router_kernel(h_ref, g_ref, wr_ref, br_ref, xp_ref, e_ref, gate_ref, rank_ref, cnt_ref, carry_ref,
                   *, tiles_per_chunk, n_exp):
    i = pl.program_id(0)

    @pl.when(i % tiles_per_chunk == 0)
    def _():
        carry_ref[...] = jnp.zeros_like(carry_ref)

    xn = _rms(h_ref[...], g_ref[...])
    tm, d = xn.shape
    xp_ref[...] = xn

    w = wr_ref[...]
    w_hi = w.astype(bf16)
    w_lo = (w - w_hi.astype(f32)).astype(bf16)
    x_hi = xn.astype(bf16)
    x_lo = (xn - x_hi.astype(f32)).astype(bf16)
    nt = (((1,), (1,)), ((), ()))
    both = lax.dot_general(jnp.concatenate([w_hi, w_lo], axis=0), x_hi, nt, preferred_element_type=f32)
    logits = (both[:n_exp] + both[n_exp:] + lax.dot_general(w_hi, x_lo, nt, preferred_element_type=f32)
              + br_ref[...])
    sub = lax.broadcasted_iota(i32, (n_exp, tm), 0).astype(f32)
    chosen, vals, hots = [], [], []
    work = logits
    for _ in range(TOP_K):
        m = jnp.max(work, axis=0, keepdims=True)
        idx = jnp.min(jnp.where(work == m, sub, float(n_exp)), axis=0, keepdims=True)
        hot = sub == idx
        chosen.append(idx)
        vals.append(m)
        hots.append(hot)
        work = jnp.where(hot, -jnp.inf, work)
    ex = [jnp.exp(v - vals[0]) for v in vals]
    denom = ex[0] + ex[1] + ex[2] + ex[3]
    e_ref[...] = jnp.concatenate(chosen, axis=0).astype(i32)
    gate_ref[...] = jnp.concatenate([x / denom for x in ex], axis=0)

    member = jnp.zeros((n_exp, tm), f32)
    for hot in hots:
        member = member + hot.astype(f32)
    earlier = (lax.broadcasted_iota(i32, (tm, tm), 0) < lax.broadcasted_iota(i32, (tm, tm), 1)).astype(bf16)
    before = jnp.dot(member.astype(bf16), earlier, preferred_element_type=f32) + carry_ref[...]
    rank_ref[...] = jnp.concatenate(
        [jnp.sum(jnp.where(hot, before, 0.0), axis=0, keepdims=True) for hot in hots], axis=0).astype(i32)
    carry_ref[...] += jnp.sum(member, axis=1, keepdims=True)
    cnt_ref[0] = carry_ref[...].astype(i32)


def router(h, g, w_router, b_router, tc):
    n, d = h.shape
    n_exp = w_router.shape[1]
    tm = ROW_TILE
    while tc % tm:
        tm -= LANES
    tiles_per_chunk = tc // tm
    kern = functools.partial(_router_kernel, tiles_per_chunk=tiles_per_chunk, n_exp=n_exp)
    return pl.pallas_call(
        kern,
        out_shape=(
            jax.ShapeDtypeStruct((n, d), f32),
            jax.ShapeDtypeStruct((TOP_K, n), i32),
            jax.ShapeDtypeStruct((TOP_K, n), f32),
            jax.ShapeDtypeStruct((TOP_K, n), i32),
            jax.ShapeDtypeStruct((n // tc, n_exp, 1), i32),
        ),
        grid=(n // tm,),
        in_specs=[
            pl.BlockSpec((tm, d), lambda i: (i, 0)),
            pl.BlockSpec((1, d), lambda i: (0, 0)),
            pl.BlockSpec((n_exp, d), lambda i: (0, 0)),
            pl.BlockSpec((n_exp, 1), lambda i: (0, 0)),
        ],
        out_specs=(
            pl.BlockSpec((tm, d), lambda i: (i, 0)),
            pl.BlockSpec((TOP_K, tm), lambda i: (0, i)),
            pl.BlockSpec((TOP_K, tm), lambda i: (0, i)),
            pl.BlockSpec((TOP_K, tm), lambda i: (0, i)),
            pl.BlockSpec((1, n_exp, 1), lambda i: (i // tiles_per_chunk, 0, 0)),
        ),
        scratch_shapes=[pltpu.VMEM((n_exp, 1), f32)],
        compiler_params=_cparams(("arbitrary",), 32),
        name="router",
    )(h, g.reshape(1, d), w_router.T, b_router.reshape(n_exp, 1))


def _split_pairs_kernel(w_ref, p_ref, o_ref):
    width = p_ref.shape[0]
    for b in range(w_ref.shape[2] // width):
        cols = slice(b * width, (b + 1) * width)
        o_ref[0, :, cols] = jnp.dot(w_ref[0, :, cols].astype(bf16), p_ref[...],
                                    preferred_element_type=f32).astype(bf16)


def split_pairs(w):
    n_exp, d, f2 = w.shape
    width = 2 * LANES
    j = jnp.arange(width)
    src = jnp.where(j < LANES, 2 * j, 2 * (j - LANES) + 1)
    perm = (jnp.arange(width)[:, None] == src[None, :]).astype(bf16)
    return pl.pallas_call(
        _split_pairs_kernel,
        out_shape=jax.ShapeDtypeStruct((n_exp, d, f2), bf16),
        grid=(n_exp,),
        in_specs=[
            pl.BlockSpec((1, d, f2), lambda e: (e, 0, 0)),
            pl.BlockSpec((width, width), lambda e: (0, 0)),
        ],
        out_specs=pl.BlockSpec((1, d, f2), lambda e: (e, 0, 0)),
        compiler_params=_cparams(("arbitrary",), 40),
        name="split_pairs",
    )(w, perm)


def _dense_row_index(r, tiles):
    return (r // SUBLANES) * tiles * SUBLANES + r % SUBLANES


def _experts_kernel(start_ref, nsub_ref, xp_hbm, dest_hbm, gates_hbm, fill_hbm, wup_ref, bup_ref, wdn_ref, bdn_ref,
                    f_hbm, xs_ref, acc_ref, xt_a, xt_b, y_a, y_b, dest_s, gate_s, rmap_s, sem_ref,
                    *, tc, n_exp, d, null_row0, seg):
    c = pl.program_id(0)
    e = pl.program_id(1)
    ms = MOE_SUB
    tiles = d // LANES

    def gather(row0, xt_ref):
        for r in range(ms):
            t = rmap_s[row0 + r] & (seg - 1)
            xt_ref[pl.ds(_dense_row_index(r, tiles), tiles, stride=SUBLANES), :] = xs_ref[t]

    def scatter(row0, y_ref):
        for r0 in range(0, ms, SCATTER_UNROLL):
            toks, sums = [], []
            for r in range(r0, r0 + SCATTER_UNROLL):
                a = rmap_s[row0 + r]
                t = a & (seg - 1)
                yrow = y_ref[pl.ds(_dense_row_index(r, tiles), tiles, stride=SUBLANES), :]
                toks.append(t)
                sums.append(acc_ref[t] + gate_s[a] * yrow)
            for t, s in zip(toks, sums):
                acc_ref[t] = s

    @pl.when(e == 0)
    def _():
        copies = [
            pltpu.make_async_copy(xp_hbm.at[pl.ds(c * tc, tc)], xs_ref.at[pl.ds(0, tc)], sem_ref.at[0]),
            pltpu.make_async_copy(dest_hbm.at[c], dest_s, sem_ref.at[1]),
            pltpu.make_async_copy(gates_hbm.at[c], gate_s, sem_ref.at[2]),
            pltpu.make_async_copy(fill_hbm, rmap_s, sem_ref.at[3]),
        ]
        for cp in copies:
            cp.start()
        xs_ref[pl.ds(tc, SUBLANES)] = jnp.zeros((SUBLANES, tiles, LANES), f32)
        acc_ref[...] = jnp.zeros_like(acc_ref)

        @pl.when(c == 0)
        def _():
            y_a[...] = jnp.zeros_like(y_a)
            y_b[...] = jnp.zeros_like(y_b)

        for cp in copies:
            cp.wait()

        def place(i, carry):
            for k in range(TOP_K):
                for u in range(SUBLANES):
                    a = k * seg + i * SUBLANES + u
                    rmap_s[dest_s[a]] = a
            return carry

        lax.fori_loop(0, tc // SUBLANES, place, 0)
        gather(0, xt_a)

    g = c * n_exp + e
    j0 = start_ref[g]

    def step(j, xt_cur, xt_nxt, y_cur, y_prv):
        gather((j + 1) * ms, xt_nxt)
        scatter(jnp.where(j == 0, null_row0, (j - 1) * ms), y_prv)
        x = jnp.concatenate(
            [jnp.concatenate([xt_cur[pl.ds((rg * tiles + jt) * SUBLANES, SUBLANES), :] for jt in range(tiles)], axis=1)
             for rg in range(ms // SUBLANES)], axis=0).astype(bf16)
        hmid = jnp.dot(x, wup_ref[0], preferred_element_type=f32) + bup_ref[0]
        glu = jnp.concatenate([hmid[:, 2 * jt * LANES:(2 * jt + 1) * LANES] for jt in range(tiles)], axis=1)
        lin = jnp.concatenate([hmid[:, (2 * jt + 1) * LANES:(2 * jt + 2) * LANES] for jt in range(tiles)], axis=1)
        glu = jnp.minimum(glu, SWIGLU_LIMIT)
        lin = jnp.clip(lin, -SWIGLU_LIMIT, SWIGLU_LIMIT)
        act = glu * jax.nn.sigmoid(SWIGLU_ALPHA * glu) * (lin + 1.0)
        y = jnp.dot(act.astype(bf16), wdn_ref[0], preferred_element_type=f32) + bdn_ref[0]
        for rg in range(ms // SUBLANES):
            for jt in range(tiles):
                y_cur[pl.ds((rg * tiles + jt) * SUBLANES, SUBLANES), :] = (
                    y[rg * SUBLANES:(rg + 1) * SUBLANES, jt * LANES:(jt + 1) * LANES])

    def sub_block(i, carry):
        j = j0 + i

        @pl.when(j % 2 == 0)
        def _():
            step(j, xt_a, xt_b, y_a, y_b)

        @pl.when(j % 2 == 1)
        def _():
            step(j, xt_b, xt_a, y_b, y_a)

        return carry

    lax.fori_loop(0, nsub_ref[g], sub_block, 0)

    @pl.when(e == n_exp - 1)
    def _():
        last = j0 + nsub_ref[g] - 1

        @pl.when(last % 2 == 0)
        def _():
            scatter(last * ms, y_a)

        @pl.when(last % 2 == 1)
        def _():
            scatter(last * ms, y_b)

        cp = pltpu.make_async_copy(acc_ref.at[pl.ds(0, tc)], f_hbm.at[pl.ds(c * tc, tc)], sem_ref.at[0])
        cp.start()
        cp.wait()


def experts(sb_start, n_sub, xp, dest, gates, w_up, b_up, w_down, b_down, tc):
    n, tiles, _ = xp.shape
    d = tiles * LANES
    n_exp = w_up.shape[0]
    chunks, padded_len = dest.shape
    seg = padded_len // TOP_K
    assert seg > tc and seg & (seg - 1) == 0
    max_sub = (TOP_K * tc + n_exp * (MOE_SUB - 1)) // MOE_SUB
    null_row0 = (max_sub + 1) * MOE_SUB
    map_len = null_row0 + MOE_SUB
    fill = jnp.full((map_len,), tc, i32)
    kern = functools.partial(_experts_kernel, tc=tc, n_exp=n_exp, d=d, null_row0=null_row0, seg=seg)
    block = pltpu.VMEM((MOE_SUB * tiles, LANES), f32)
    return pl.pallas_call(
        kern,
        out_shape=jax.ShapeDtypeStruct((n, tiles, LANES), f32),
        grid_spec=pltpu.PrefetchScalarGridSpec(
            num_scalar_prefetch=2,
            grid=(chunks, n_exp),
            in_specs=[
                pl.BlockSpec(memory_space=pl.ANY),
                pl.BlockSpec(memory_space=pl.ANY),
                pl.BlockSpec(memory_space=pl.ANY),
                pl.BlockSpec(memory_space=pl.ANY),
                pl.BlockSpec((1, d, 2 * d), lambda c, e, *_: (e, 0, 0)),
                pl.BlockSpec((1, 1, 2 * d), lambda c, e, *_: (e, 0, 0)),
                pl.BlockSpec((1, d, d), lambda c, e, *_: (e, 0, 0)),
                pl.BlockSpec((1, 1, d), lambda c, e, *_: (e, 0, 0)),
            ],
            out_specs=pl.BlockSpec(memory_space=pl.ANY),
            scratch_shapes=[
                pltpu.VMEM((tc + SUBLANES, tiles, LANES), f32),
                pltpu.VMEM((tc + SUBLANES, tiles, LANES), f32),
                block, block, block, block,
                pltpu.SMEM((padded_len,), i32),
                pltpu.SMEM((padded_len,), f32),
                pltpu.SMEM((map_len,), i32),
                pltpu.SemaphoreType.DMA((4,)),
            ],
        ),
        compiler_params=_cparams(("arbitrary", "arbitrary"), 58),
        name="experts",
    )(sb_start, n_sub, xp, dest, gates, fill, w_up, b_up.reshape(n_exp, 1, 2 * d), w_down,
      b_down.reshape(n_exp, 1, d))


def _final_kernel(h_ref, f_ref, g_ref, o_ref):
    tm, d = h_ref.shape
    tiles = d // LANES
    parts = [h_ref[:, jt * LANES:(jt + 1) * LANES] + f_ref[pl.ds(jt, tm, stride=tiles), :] for jt in range(tiles)]
    o_ref[...] = _rms(jnp.concatenate(parts, axis=1), g_ref[...])


def final_norm(h, f2, g, row0, rows):
    d = h.shape[1]
    tiles = d // LANES
    tm = min(ROW_TILE, rows)
    blk0 = row0 // tm
    assert rows % tm == 0 and row0 % tm == 0
    return pl.pallas_call(
        _final_kernel,
        out_shape=jax.ShapeDtypeStruct((rows, d), f32),
        grid=(rows // tm,),
        in_specs=[
            pl.BlockSpec((tm, d), lambda i: (blk0 + i, 0)),
            pl.BlockSpec((tm * tiles, LANES), lambda i: (blk0 + i, 0)),
            pl.BlockSpec((1, d), lambda i: (0, 0)),
        ],
        out_specs=pl.BlockSpec((tm, d), lambda i: (i, 0)),
        compiler_params=_cparams(("arbitrary",), 32),
        name="final_norm",
    )(h, f2, g.reshape(1, d))


def moe(h, g, w_router, b_router, w_up, b_up, w_down, b_down):
    n, d = h.shape
    n_exp = w_router.shape[1]
    tc = n // MOE_CHUNKS
    xp, top_e, gates, rank, counts = router(h, g, w_router, b_router, tc)
    n_sub = (counts.reshape(MOE_CHUNKS, n_exp) + MOE_SUB - 1) // MOE_SUB
    sb_start = jnp.cumsum(n_sub, axis=1) - n_sub
    hot = top_e.reshape(TOP_K, MOE_CHUNKS, tc, 1) == jnp.arange(n_exp, dtype=i32)
    row_base = (sb_start * MOE_SUB).reshape(1, MOE_CHUNKS, 1, n_exp)
    dest = jnp.sum(jnp.where(hot, row_base, 0), axis=-1) + rank.reshape(TOP_K, MOE_CHUNKS, tc)
    seg = 1 << tc.bit_length()
    per_chunk = lambda a: jnp.pad(a.transpose(1, 0, 2), ((0, 0), (0, 0), (0, seg - tc))).reshape(MOE_CHUNKS,
                                                                                                  TOP_K * seg)
    tiles = d // LANES
    b_up_p = b_up.reshape(n_exp, tiles, LANES, 2).transpose(0, 1, 3, 2).reshape(n_exp, 2 * d)
    f = experts(sb_start.reshape(-1).astype(i32), n_sub.reshape(-1).astype(i32), xp.reshape(n, tiles, LANES),
                per_chunk(dest), per_chunk(gates.reshape(TOP_K, MOE_CHUNKS, tc)), split_pairs(w_up), b_up_p,
                w_down.astype(bf16), b_down, tc)
    return f.reshape(n * tiles, LANES)


def kernel(x_prompt, x_sample, mem_prompt, state_ret, state_conv, cache_mem_k, cache_mem_v, norm_mix, w_in, conv_w,
           ret_gn, w_out, norm_cross, norm_mem, w_mq, w_mk, w_mv, w_mo, norm_ffn, w_router, b_router, w_up, b_up,
           w_down, b_down, norm_final):
    batch, seq, d = x_prompt.shape
    n_seq, ts, _ = x_sample.shape
    depth = w_in.shape[0]
    n_mem = mem_prompt.shape[1]
    r = ret_gn.shape[1]
    cd_width = conv_w.shape[2]
    dh = r // N_RET_HEADS
    n_p = batch * seq
    n_s = n_seq * ts
    n = n_p + n_s
    nblk = n_seq // SAMPLE_BLOCK
    assert seq % ROW_TILE == 0 and n % ROW_TILE == 0 and n % MOE_CHUNKS == 0

    xs = x_sample.reshape(nblk, SAMPLE_BLOCK, ts, d).transpose(0, 2, 1, 3).reshape(n_s, d)
    h_rows = (x_prompt.reshape(n_p, d), xs)
    mem2d = mem_prompt.reshape(batch * n_mem, d)
    tab_p = prompt_tables(seq, dh)
    tab_s = sample_tables(ts, float(PAST_LEN), dh)

    ret_p, conv_p, mk_p, mv_p, ret_s, conv_s = [], [], [], [], [], []
    for l in range(depth):
        if l == 0:
            proj = norm_matmul2(*h_rows, norm_mix[l], w_in[l].astype(bf16), "in_proj")
        else:
            proj = norm_matmul(h, norm_mix[l], w_in[l].astype(bf16), "in_proj")
        mix_p, s_p, c_p = prompt_mixer(proj, tab_p, conv_w[l], ret_gn[l], batch, seq, r, cd_width)
        mix_s, s_s, c_s = sample_mixer(proj, tab_s, conv_w[l], ret_gn[l], state_ret[l],
                                       state_conv[l].transpose(1, 0, 2), n_p, ts, r, cd_width)
        h = matmul_res(mix_p, mix_s, w_out[l].astype(bf16), *h_rows, "out_proj")

        mkv = norm_matmul(mem2d, norm_mem[l], jnp.concatenate([w_mk[l], w_mv[l]], axis=1).astype(bf16), "mem_kv")
        q = norm_matmul(h, norm_cross[l], w_mq[l].astype(bf16), "q_proj")
        attn_p = cross_prompt(q, mkv, batch, seq, n_mem, d)
        attn_s = cross_sample(q, cache_mem_k[l], cache_mem_v[l], n_p, ts)
        h = matmul_res(attn_p, attn_s, w_mo[l].astype(bf16), h, h, "o_proj")

        f2 = moe(h, norm_ffn[l], w_router[l], b_router[l], w_up[l], b_up[l], w_down[l], b_down[l])
        if l + 1 < depth:
            h = h + f2.reshape(n, d // LANES, LANES).reshape(n, d)
            h_rows = (h, h)

        ret_p.append(s_p)
        conv_p.append(c_p)
        mk_p.append(mkv[:, :d].reshape(batch, n_mem, N_MEM_HEADS, d // N_MEM_HEADS))
        mv_p.append(mkv[:, d:].reshape(batch, n_mem, N_MEM_HEADS, d // N_MEM_HEADS))
        ret_s.append(s_s)
        conv_s.append(c_s.transpose(1, 0, 2))

    y_p = final_norm(h, f2, norm_final, 0, n_p).reshape(batch, seq, d)
    y_s = final_norm(h, f2, norm_final, n_p, n_s)
    y_s = y_s.reshape(nblk, ts, SAMPLE_BLOCK, d).transpose(0, 2, 1, 3).reshape(n_seq, ts, d)
    return (y_p, y_s, jnp.stack(ret_p), jnp.stack(conv_p), jnp.stack(mk_p), jnp.stack(mv_p), jnp.stack(ret_s),
            jnp.stack(conv_s))
```

```python
import functools

import jax
import jax.numpy as jnp
from jax import lax
from jax.experimental import pallas as pl
from jax.experimental.pallas import tpu as pltpu

f32 = jnp.float32
bf16 = jnp.bfloat16
i32 = jnp.int32

EPS = 1e-6
ROPE_THETA = 10000.0
RET_CHUNK = 128
PAST_LEN = 16384
N_RET_HEADS = 4
N_MEM_HEADS = 4
TOP_K = 4
SWIGLU_ALPHA = 1.702
SWIGLU_LIMIT = 7.0

LANES = 128
SUBLANES = 8
MIB = 1024 * 1024

ROW_TILE = 512
SAMPLE_BLOCK = 8
MOE_CHUNKS = 4
MOE_SUB = 256
SCATTER_UNROLL = 8


def _cparams(sem, vmem_mib):
    return pltpu.CompilerParams(dimension_semantics=sem, vmem_limit_bytes=vmem_mib * MIB)


def _rms(x, g):
    ms = jnp.mean(x * x, axis=-1, keepdims=True)
    return (x * lax.rsqrt(ms + EPS)) * g


def _norm_matmul_kernel(x_ref, g_ref, w_ref, o_ref):
    xn = _rms(x_ref[...], g_ref[...])
    o_ref[...] = jnp.dot(xn.astype(bf16), w_ref[...], preferred_element_type=f32)


def norm_matmul(x, g, w, name):
    m, d = x.shape
    f = w.shape[1]
    tm = min(ROW_TILE, m)
    assert m % tm == 0
    return pl.pallas_call(
        _norm_matmul_kernel,
        out_shape=jax.ShapeDtypeStruct((m, f), f32),
        grid=(m // tm,),
        in_specs=[
            pl.BlockSpec((tm, d), lambda i: (i, 0)),
            pl.BlockSpec((1, d), lambda i: (0, 0)),
            pl.BlockSpec((d, f), lambda i: (0, 0)),
        ],
        out_specs=pl.BlockSpec((tm, f), lambda i: (i, 0)),
        compiler_params=_cparams(("arbitrary",), 48),
        name=name,
    )(x, g.reshape(1, d), w)


def _stacked_specs(m_p, m_s, width, stacked=False):
    tm = ROW_TILE
    assert m_p % tm == 0 and m_s == tm
    prompt_tiles = m_p // tm
    sample_block = prompt_tiles if stacked else 0
    return prompt_tiles, [
        pl.BlockSpec((tm, width), lambda i: (jnp.minimum(i, prompt_tiles - 1), 0)),
        pl.BlockSpec((tm, width), lambda i: (sample_block, 0)),
    ]


def _stacked_tile(p_ref, s_ref, prompt_tiles):
    return jnp.where(pl.program_id(0) < prompt_tiles, p_ref[...], s_ref[...])


def _norm_matmul2_kernel(xp_ref, xs_ref, g_ref, w_ref, o_ref, *, prompt_tiles):
    xn = _rms(_stacked_tile(xp_ref, xs_ref, prompt_tiles), g_ref[...])
    o_ref[...] = jnp.dot(xn.astype(bf16), w_ref[...], preferred_element_type=f32)


def norm_matmul2(x_prompt, x_sample, g, w, name):
    m_p, d = x_prompt.shape
    m_s = x_sample.shape[0]
    f = w.shape[1]
    prompt_tiles, x_specs = _stacked_specs(m_p, m_s, d)
    return pl.pallas_call(
        functools.partial(_norm_matmul2_kernel, prompt_tiles=prompt_tiles),
        out_shape=jax.ShapeDtypeStruct((m_p + m_s, f), f32),
        grid=(prompt_tiles + 1,),
        in_specs=x_specs + [
            pl.BlockSpec((1, d), lambda i: (0, 0)),
            pl.BlockSpec((d, f), lambda i: (0, 0)),
        ],
        out_specs=pl.BlockSpec((ROW_TILE, f), lambda i: (i, 0)),
        compiler_params=_cparams(("arbitrary",), 48),
        name=name,
    )(x_prompt, x_sample, g.reshape(1, d), w)


def _matmul_res_kernel(ap_ref, as_ref, w_ref, rp_ref, rs_ref, o_ref, *, prompt_tiles):
    a = _stacked_tile(ap_ref, as_ref, prompt_tiles)
    res = _stacked_tile(rp_ref, rs_ref, prompt_tiles)
    o_ref[...] = res + jnp.dot(a, w_ref[...], preferred_element_type=f32)


def matmul_res(a_prompt, a_sample, w, res_prompt, res_sample, name):
    m_p, d = a_prompt.shape
    m_s = a_sample.shape[0]
    f = w.shape[1]
    prompt_tiles, a_specs = _stacked_specs(m_p, m_s, d)
    _, r_specs = _stacked_specs(m_p, m_s, f, stacked=res_prompt is res_sample)
    return pl.pallas_call(
        functools.partial(_matmul_res_kernel, prompt_tiles=prompt_tiles),
        out_shape=jax.ShapeDtypeStruct((m_p + m_s, f), f32),
        grid=(prompt_tiles + 1,),
        in_specs=a_specs + [pl.BlockSpec((d, f), lambda i: (0, 0))] + r_specs,
        out_specs=pl.BlockSpec((ROW_TILE, f), lambda i: (i, 0)),
        compiler_params=_cparams(("arbitrary",), 32),
        name=name,
    )(a_prompt, a_sample, w, res_prompt, res_sample)


def _rope(x, cos, sin_signed, half):
    return x * cos + pltpu.roll(x, half, 1) * sin_signed


def _group_norm_gate(o, gate, gn):
    mu = jnp.mean(o, axis=-1, keepdims=True)
    var = jnp.mean(jnp.square(o - mu), axis=-1, keepdims=True)
    on = ((o - mu) * lax.rsqrt(var + EPS)) * gn
    return (gate * jax.nn.sigmoid(gate)) * on


def _prompt_mixer_kernel(cd_ref, proj_ref, cos_ref, sin_ref, decay_ref, xi_ref, zeta_ref, convw_ref, gn_ref,
                         mix_ref, sfin_ref, cfin_ref, s_scr, u_scr, *, tt, r, cd_width, dh):
    j = pl.program_id(1)
    heads = r // dh
    pad = SUBLANES

    @pl.when(j == 0)
    def _():
        s_scr[...] = jnp.zeros_like(s_scr)
        u_scr[0:pad, :] = jnp.zeros((pad, cd_width), f32)

    k_scale = dh ** -0.5
    for c in range(tt // RET_CHUNK):
        rows = pl.ds(c * RET_CHUNK, RET_CHUNK)
        cosb = cos_ref[rows, :]
        sinb = sin_ref[rows, :]
        for h in range(heads):
            lo, hi = h * dh, (h + 1) * dh
            qh = _rope(proj_ref[rows, lo:hi], cosb, sinb, dh // 2)
            kh = _rope(proj_ref[rows, r + lo:r + hi], cosb, sinb, dh // 2) * k_scale
            vb = proj_ref[rows, 2 * r + lo:2 * r + hi].astype(bf16)
            gate = proj_ref[rows, 3 * r + lo:3 * r + hi]
            qb = qh.astype(bf16)
            kb = kh.astype(bf16)
            scores = lax.dot_general(qb, kb, (((1,), (1,)), ((), ())), preferred_element_type=f32) * decay_ref[h]
            inner = jnp.dot(scores.astype(bf16), vb, preferred_element_type=f32)
            s_prev = s_scr[h]
            cross = jnp.dot((qh * xi_ref[h]).astype(bf16), s_prev.astype(bf16), preferred_element_type=f32)
            kv = lax.dot_general((kh * zeta_ref[h]).astype(bf16), vb, (((0,), (0,)), ((), ())),
                                 preferred_element_type=f32)
            s_scr[h] = cd_ref[h] * s_prev + kv
            mix_ref[rows, lo:hi] = _group_norm_gate(inner + cross, gate, gn_ref[:, lo:hi]).astype(bf16)

    base = 4 * r
    u = proj_ref[:, base + cd_width:base + 2 * cd_width] * proj_ref[:, base:base + cd_width]
    u_scr[pad:pad + tt, :] = u
    conv = (u_scr[pad - 2:pad - 2 + tt, :] * convw_ref[0:1, :] + u_scr[pad - 1:pad - 1 + tt, :] * convw_ref[1:2, :]
            + u * convw_ref[2:3, :])
    mix_ref[:, r:r + cd_width] = (proj_ref[:, base + 2 * cd_width:base + 3 * cd_width] * conv).astype(bf16)
    u_scr[0:pad, :] = u_scr[tt:tt + pad, :]
    cfin_ref[0] = u_scr[pad - 2:pad, :]

    @pl.when(j == pl.num_programs(1) - 1)
    def _():
        sfin_ref[0] = s_scr[...]


def prompt_mixer(proj, tables, conv_w, ret_gn, batch, seq, r, cd_width):
    tt = ROW_TILE
    dh = r // N_RET_HEADS
    steps = seq // tt
    in_cols = proj.shape[1]
    kern = functools.partial(_prompt_mixer_kernel, tt=tt, r=r, cd_width=cd_width, dh=dh)
    tab = lambda shape: pl.BlockSpec(shape, lambda b, j, cd: (0,) * len(shape))
    return pl.pallas_call(
        kern,
        out_shape=(
            jax.ShapeDtypeStruct((batch * seq, r + cd_width), bf16),
            jax.ShapeDtypeStruct((batch, N_RET_HEADS, dh, dh), f32),
            jax.ShapeDtypeStruct((batch, 2, cd_width), f32),
        ),
        grid_spec=pltpu.PrefetchScalarGridSpec(
            num_scalar_prefetch=1,
            grid=(batch, steps),
            in_specs=[
                pl.BlockSpec((tt, in_cols), lambda b, j, cd: (b * steps + j, 0)),
                pl.BlockSpec((tt, dh), lambda b, j, cd: (j, 0)),
                pl.BlockSpec((tt, dh), lambda b, j, cd: (j, 0)),
                tab((N_RET_HEADS, RET_CHUNK, RET_CHUNK)),
                tab((N_RET_HEADS, RET_CHUNK, dh)),
                tab((N_RET_HEADS, RET_CHUNK, dh)),
                tab((3, cd_width)),
                tab((1, r)),
            ],
            out_specs=(
                pl.BlockSpec((tt, r + cd_width), lambda b, j, cd: (b * steps + j, 0)),
                pl.BlockSpec((1, N_RET_HEADS, dh, dh), lambda b, j, cd: (b, 0, 0, 0)),
                pl.BlockSpec((1, 2, cd_width), lambda b, j, cd: (b, 0, 0)),
            ),
            scratch_shapes=[
                pltpu.VMEM((N_RET_HEADS, dh, dh), f32),
                pltpu.VMEM((tt + 2 * SUBLANES, cd_width), f32),
            ],
        ),
        compiler_params=_cparams(("arbitrary", "arbitrary"), 48),
        name="prompt_mixer",
    )(tables["chunk_decay"], proj, tables["cos"], tables["sin"], tables["decay"], tables["xi"], tables["zeta"],
      conv_w, ret_gn.reshape(1, r))


def _sample_mixer_kernel(dec_ref, proj_ref, cos_ref, sin_ref, xi_ref, zeta_ref, convw_ref, gn_ref, s0_ref, c0_ref,
                         mix_ref, snew_ref, cnew_ref, *, ts, r, cd_width, dh):
    heads = r // dh
    nb = SAMPLE_BLOCK
    rows = ts * nb
    k_scale = dh ** -0.5
    cosb = cos_ref[...]
    sinb = sin_ref[...]
    seq_of_row = lax.broadcasted_iota(i32, (rows, dh), 0) % nb
    for h in range(heads):
        lo, hi = h * dh, (h + 1) * dh
        qh = _rope(proj_ref[:, lo:hi], cosb, sinb, dh // 2)
        kh = _rope(proj_ref[:, r + lo:r + hi], cosb, sinb, dh // 2) * k_scale
        vh = proj_ref[:, 2 * r + lo:2 * r + hi]
        gate = proj_ref[:, 3 * r + lo:3 * r + hi]
        inner = []
        for t in range(ts):
            qt = qh[t * nb:(t + 1) * nb]
            acc = jnp.zeros((nb, dh), f32)
            for s in range(t + 1):
                sc = jnp.sum(qt * kh[s * nb:(s + 1) * nb], axis=-1, keepdims=True) * dec_ref[h * (ts + 1) + t - s]
                acc = acc + sc * vh[s * nb:(s + 1) * nb]
            inner.append(acc)
        inner = jnp.concatenate(inner, axis=0)
        qx = (qh * xi_ref[h]).astype(bf16)
        kz = kh * zeta_ref[h]
        vb = vh.astype(bf16)
        cross = jnp.zeros((rows, dh), f32)
        for b in range(nb):
            mine = seq_of_row == b
            s_prev = s0_ref[b, h]
            res = jnp.dot(qx, s_prev.astype(bf16), preferred_element_type=f32)
            cross = cross + jnp.where(mine, res, 0.0)
            kv = lax.dot_general(jnp.where(mine, kz, 0.0).astype(bf16), vb, (((0,), (0,)), ((), ())),
                                 preferred_element_type=f32)
            snew_ref[b, h] = dec_ref[h * (ts + 1) + ts] * s_prev + kv
        mix_ref[:, lo:hi] = _group_norm_gate(inner + cross, gate, gn_ref[:, lo:hi]).astype(bf16)

    base = 4 * r
    u = proj_ref[:, base + cd_width:base + 2 * cd_width] * proj_ref[:, base:base + cd_width]
    full = [c0_ref[0], c0_ref[1]] + [u[t * nb:(t + 1) * nb] for t in range(ts)]
    conv = jnp.concatenate(
        [full[t] * convw_ref[0:1, :] + full[t + 1] * convw_ref[1:2, :] + full[t + 2] * convw_ref[2:3, :]
         for t in range(ts)], axis=0)
    mix_ref[:, r:r + cd_width] = (proj_ref[:, base + 2 * cd_width:base + 3 * cd_width] * conv).astype(bf16)
    cnew_ref[0] = full[ts]
    cnew_ref[1] = full[ts + 1]


def sample_mixer(proj, tables, conv_w, ret_gn, state_ret, state_conv_t, row0, ts, r, cd_width):
    n_seq = state_ret.shape[0]
    dh = r // N_RET_HEADS
    rows = ts * SAMPLE_BLOCK
    blk0 = row0 // rows
    assert row0 % rows == 0 and n_seq % SAMPLE_BLOCK == 0
    in_cols = proj.shape[1]
    kern = functools.partial(_sample_mixer_kernel, ts=ts, r=r, cd_width=cd_width, dh=dh)
    tab = lambda shape: pl.BlockSpec(shape, lambda i, d: (0,) * len(shape))
    return pl.pallas_call(
        kern,
        out_shape=(
            jax.ShapeDtypeStruct((n_seq * ts, r + cd_width), bf16),
            jax.ShapeDtypeStruct(state_ret.shape, f32),
            jax.ShapeDtypeStruct(state_conv_t.shape, f32),
        ),
        grid_spec=pltpu.PrefetchScalarGridSpec(
            num_scalar_prefetch=1,
            grid=(n_seq // SAMPLE_BLOCK,),
            in_specs=[
                pl.BlockSpec((rows, in_cols), lambda i, d: (blk0 + i, 0)),
                tab((rows, dh)),
                tab((rows, dh)),
                tab((N_RET_HEADS, rows, dh)),
                tab((N_RET_HEADS, rows, dh)),
                tab((3, cd_width)),
                tab((1, r)),
                pl.BlockSpec((SAMPLE_BLOCK, N_RET_HEADS, dh, dh), lambda i, d: (i, 0, 0, 0)),
                pl.BlockSpec((2, SAMPLE_BLOCK, cd_width), lambda i, d: (0, i, 0)),
            ],
            out_specs=(
                pl.BlockSpec((rows, r + cd_width), lambda i, d: (i, 0)),
                pl.BlockSpec((SAMPLE_BLOCK, N_RET_HEADS, dh, dh), lambda i, d: (i, 0, 0, 0)),
                pl.BlockSpec((2, SAMPLE_BLOCK, cd_width), lambda i, d: (0, i, 0)),
            ),
        ),
        compiler_params=_cparams(("arbitrary",), 32),
        name="sample_mixer",
    )(tables["dec"], proj, tables["cos"], tables["sin"], tables["xi"], tables["zeta"], conv_w, ret_gn.reshape(1, r),
      state_ret, state_conv_t)


def _log_gamma():
    return jnp.log1p(-jnp.exp2(-5.0 - jnp.arange(N_RET_HEADS, dtype=f32)))


def _rope_tables(pos, dh):
    half = dh // 2
    inv = ROPE_THETA ** (-jnp.arange(half, dtype=f32) / half)
    ang = pos[:, None] * inv[None, :]
    cos, sin = jnp.cos(ang), jnp.sin(ang)
    return jnp.concatenate([cos, cos], axis=-1), jnp.concatenate([-sin, sin], axis=-1)


def prompt_tables(seq, dh):
    c = RET_CHUNK
    lg = _log_gamma()
    i = jnp.arange(c, dtype=f32)
    diff = i[:, None] - i[None, :]
    decay = jnp.where(diff >= 0, jnp.exp(jnp.maximum(diff, 0.0)[None] * lg[:, None, None]), 0.0)
    xi = jnp.exp((i + 1.0)[None, :] * lg[:, None])
    zeta = jnp.exp((c - 1.0 - i)[None, :] * lg[:, None])
    cos, sin = _rope_tables(jnp.arange(seq, dtype=f32), dh)
    bc = lambda t: jnp.broadcast_to(t[:, :, None], (N_RET_HEADS, c, dh))
    return dict(cos=cos, sin=sin, decay=decay, xi=bc(xi), zeta=bc(zeta), chunk_decay=jnp.exp(c * lg))


def sample_tables(ts, pos0, dh):
    lg = _log_gamma()
    i = jnp.arange(ts, dtype=f32)
    dec = jnp.exp(jnp.arange(ts + 1, dtype=f32)[None, :] * lg[:, None])
    xi = jnp.exp((i + 1.0)[None, :] * lg[:, None])
    zeta = jnp.exp((ts - 1.0 - i)[None, :] * lg[:, None])
    cos, sin = _rope_tables(pos0 + i, dh)
    rep = lambda t: jnp.repeat(t, SAMPLE_BLOCK, axis=0)
    bc = lambda t: jnp.broadcast_to(jnp.repeat(t, SAMPLE_BLOCK, axis=1)[:, :, None],
                                    (N_RET_HEADS, ts * SAMPLE_BLOCK, dh))
    return dict(cos=rep(cos), sin=rep(sin), xi=bc(xi), zeta=bc(zeta), dec=dec.reshape(-1))


def _softmax_rows(s):
    m = jnp.max(s, axis=-1, keepdims=True)
    p = jnp.exp(s - m)
    return p / jnp.sum(p, axis=-1, keepdims=True)


def _from_slabs(ref, rows=None):
    rows = slice(None) if rows is None else rows
    return jnp.concatenate([ref[lt, rows, :] for lt in range(ref.shape[0])], axis=1)


def _to_slabs(ref, value):
    for lt in range(ref.shape[0]):
        ref[lt] = value[:, lt * LANES:(lt + 1) * LANES]


def _mem_kv_kernel(m_ref, g_ref, w_ref, k_ref, v_ref, rows_ref, *, heads):
    n_mem, d = m_ref.shape
    dh = d // heads
    kv = jnp.dot(_rms(m_ref[...], g_ref[...]).astype(bf16), w_ref[...], preferred_element_type=f32)
    for part, o_ref in enumerate((k_ref, v_ref)):
        for h in range(heads):
            for lt in range(dh // LANES):
                lo = part * d + h * dh + lt * LANES
                rows_ref[lt, pl.ds(h, n_mem, stride=heads), :] = kv[:, lo:lo + LANES]
        o_ref[0] = _from_slabs(rows_ref).reshape(n_mem, heads, dh)


def mem_kv(mem, g, w_kv, heads):
    batch, n_mem, d = mem.shape
    dh = d // heads
    out = jax.ShapeDtypeStruct((batch, n_mem, heads, dh), f32)
    out_spec = pl.BlockSpec((1, n_mem, heads, dh), lambda b: (b, 0, 0, 0))
    return pl.pallas_call(
        functools.partial(_mem_kv_kernel, heads=heads),
        out_shape=(out, out),
        grid=(batch,),
        in_specs=[
            pl.BlockSpec((n_mem, d), lambda b: (b, 0)),
            pl.BlockSpec((1, d), lambda b: (0, 0)),
            pl.BlockSpec((d, 2 * d), lambda b: (0, 0)),
        ],
        out_specs=(out_spec, out_spec),
        scratch_shapes=[pltpu.VMEM((dh // LANES, n_mem * heads, LANES), f32)],
        compiler_params=_cparams(("arbitrary",), 32),
        name="mem_kv",
    )(mem.reshape(batch * n_mem, d), g.reshape(1, d), w_kv)


def _attn_block_kernel(h_ref, g_ref, wq_ref, k_ref, v_ref, as_ref, wo_ref, o_ref, k_rows, v_rows,
                       *, prompt_tiles, heads):
    i = pl.program_id(0)

    @pl.when(i < prompt_tiles)
    def _():
        n_mem, _, dh = k_ref.shape[1:]
        scale = dh ** -0.5
        q = jnp.dot(_rms(h_ref[...], g_ref[...]).astype(bf16), wq_ref[...], preferred_element_type=f32)
        _to_slabs(k_rows, k_ref[0].reshape(n_mem * heads, dh))
        _to_slabs(v_rows, v_ref[0].reshape(n_mem * heads, dh))
        outs = []
        for h in range(heads):
            kh = _from_slabs(k_rows, pl.ds(h, n_mem, stride=heads)).astype(bf16)
            vh = _from_slabs(v_rows, pl.ds(h, n_mem, stride=heads)).astype(bf16)
            s = lax.dot_general(q[:, h * dh:(h + 1) * dh].astype(bf16), kh, (((1,), (1,)), ((), ())),
                                preferred_element_type=f32) * scale
            outs.append(jnp.dot(_softmax_rows(s).astype(bf16), vh, preferred_element_type=f32).astype(bf16))
        attn = jnp.concatenate(outs, axis=1)
        o_ref[...] = h_ref[...] + jnp.dot(attn, wo_ref[...], preferred_element_type=f32)

    @pl.when(i == prompt_tiles)
    def _():
        o_ref[...] = h_ref[...] + jnp.dot(as_ref[...], wo_ref[...], preferred_element_type=f32)


def attn_block(h, g, w_mq, mk, mv, attn_sample, w_mo, seq):
    n, d = h.shape
    batch, n_mem, heads, dh = mk.shape
    tm = ROW_TILE
    prompt_tiles = batch * seq // tm
    per_batch = seq // tm
    assert n == (prompt_tiles + 1) * tm and attn_sample.shape == (tm, d)
    kv_spec = pl.BlockSpec((1, n_mem, heads, dh),
                           lambda i: (jnp.minimum(i, prompt_tiles - 1) // per_batch, 0, 0, 0))
    const = lambda shape: pl.BlockSpec(shape, lambda i: (0,) * len(shape))
    return pl.pallas_call(
        functools.partial(_attn_block_kernel, prompt_tiles=prompt_tiles, heads=heads),
        out_shape=jax.ShapeDtypeStruct((n, d), f32),
        grid=(prompt_tiles + 1,),
        in_specs=[
            pl.BlockSpec((tm, d), lambda i: (i, 0)),
            const((1, d)),
            const((d, d)),
            kv_spec,
            kv_spec,
            const((tm, d)),
            const((d, d)),
        ],
        out_specs=pl.BlockSpec((tm, d), lambda i: (i, 0)),
        scratch_shapes=[pltpu.VMEM((dh // LANES, n_mem * heads, LANES), f32)] * 2,
        compiler_params=_cparams(("arbitrary",), 48),
        name="attn_block",
    )(h, g.reshape(1, d), w_mq, mk, mv, attn_sample, w_mo)


def _cross_sample_kernel(q_ref, k_ref, v_ref, o_ref, acc_ref, *, dh, seqs):
    part = pl.program_id(1)
    rows = q_ref.shape[0]
    scale = dh ** -0.5

    @pl.when(part == 0)
    def _():
        acc_ref[...] = jnp.zeros_like(acc_ref)

    heads = N_MEM_HEADS
    n_mem = k_ref.shape[1]
    qx = jnp.concatenate([q_ref[:, h * dh:(h + 1) * dh] for h in range(heads)], axis=0).astype(bf16)
    row = lax.broadcasted_iota(i32, (heads * rows, n_mem * heads), 0)
    col = lax.broadcasted_iota(i32, (heads * rows, n_mem * heads), 1)
    same_head = col % heads == row // rows
    seq_of_row = lax.broadcasted_iota(i32, (heads * rows, dh), 0) % SAMPLE_BLOCK
    out = jnp.zeros((heads * rows, dh), f32)
    for b in range(seqs):
        k2 = k_ref[b].reshape(n_mem * heads, dh).astype(bf16)
        v2 = v_ref[b].reshape(n_mem * heads, dh).astype(bf16)
        s = lax.dot_general(qx, k2, (((1,), (1,)), ((), ())), preferred_element_type=f32) * scale
        p = _softmax_rows(jnp.where(same_head, s, -jnp.inf))
        o = jnp.dot(p.astype(bf16), v2, preferred_element_type=f32)
        out = out + jnp.where(seq_of_row == part * seqs + b, o, 0.0)
    acc_ref[...] += out

    @pl.when(part == pl.num_programs(1) - 1)
    def _():
        for h in range(heads):
            o_ref[:, h * dh:(h + 1) * dh] = acc_ref[h * rows:(h + 1) * rows, :].astype(bf16)


def cross_sample(q, cache_k, cache_v, row0, ts):
    n_seq, n_mem, heads, dh = cache_k.shape
    d = heads * dh
    rows = ts * SAMPLE_BLOCK
    blk0 = row0 // rows
    parts = 4
    seqs = SAMPLE_BLOCK // parts
    kern = functools.partial(_cross_sample_kernel, dh=dh, seqs=seqs)
    return pl.pallas_call(
        kern,
        out_shape=jax.ShapeDtypeStruct((n_seq * ts, d), bf16),
        grid=(n_seq // SAMPLE_BLOCK, parts),
        in_specs=[
            pl.BlockSpec((rows, d), lambda i, p: (blk0 + i, 0)),
            pl.BlockSpec((seqs, n_mem, heads, dh), lambda i, p: (i * parts + p, 0, 0, 0)),
            pl.BlockSpec((seqs, n_mem, heads, dh), lambda i, p: (i * parts + p, 0, 0, 0)),
        ],
        out_specs=pl.BlockSpec((rows, d), lambda i, p: (i, 0)),
        scratch_shapes=[pltpu.VMEM((heads * rows, dh), f32)],
        compiler_params=_cparams(("arbitrary", "arbitrary"), 40),
        name="cross_sample",
    )(q, cache_k, cache_v)


def _router_kernel(h_ref, g_ref, wr_ref, br_ref, xp_ref, e_ref, gate_ref, rank_ref, cnt_ref, carry_ref,
                   *, tiles_per_chunk, n_exp):
    i = pl.program_id(0)

    @pl.when(i % tiles_per_chunk == 0)
    def _():
        carry_ref[...] = jnp.zeros_like(carry_ref)

    xn = _rms(h_ref[...], g_ref[...])
    tm, d = xn.shape
    xp_ref[...] = xn

    w = wr_ref[...]
    w_hi = w.astype(bf16)
    w_lo = (w - w_hi.astype(f32)).astype(bf16)
    x_hi = xn.astype(bf16)
    x_lo = (xn - x_hi.astype(f32)).astype(bf16)
    nt = (((1,), (1,)), ((), ()))
    both = lax.dot_general(jnp.concatenate([w_hi, w_lo], axis=0), x_hi, nt, preferred_element_type=f32)
    logits = (both[:n_exp] + both[n_exp:] + lax.dot_general(w_hi, x_lo, nt, preferred_element_type=f32)
              + br_ref[...])
    sub = lax.broadcasted_iota(i32, (n_exp, tm), 0).astype(f32)
    chosen, vals, hots = [], [], []
    work = logits
    for _ in range(TOP_K):
        m = jnp.max(work, axis=0, keepdims=True)
        idx = jnp.min(jnp.where(work == m, sub, float(n_exp)), axis=0, keepdims=True)
        hot = sub == idx
        chosen.append(idx)
        vals.append(m)
        hots.append(hot)
        work = jnp.where(hot, -jnp.inf, work)
    ex = [jnp.exp(v - vals[0]) for v in vals]
    denom = ex[0] + ex[1] + ex[2] + ex[3]
    e_ref[...] = jnp.concatenate(chosen, axis=0).astype(i32)
    gate_ref[...] = jnp.concatenate([x / denom for x in ex], axis=0)

    member = jnp.zeros((n_exp, tm), f32)
    for hot in hots:
        member = member + hot.astype(f32)
    earlier = (lax.broadcasted_iota(i32, (tm, tm), 0) < lax.broadcasted_iota(i32, (tm, tm), 1)).astype(bf16)
    before = jnp.dot(member.astype(bf16), earlier, preferred_element_type=f32) + carry_ref[...]
    rank_ref[...] = jnp.concatenate(
        [jnp.sum(jnp.where(hot, before, 0.0), axis=0, keepdims=True) for hot in hots], axis=0).astype(i32)
    carry_ref[...] += jnp.sum(member, axis=1, keepdims=True)
    cnt_ref[0] = carry_ref[...].astype(i32)


def router(h, g, w_router, b_router, tc):
    n, d = h.shape
    n_exp = w_router.shape[1]
    tm = ROW_TILE
    while tc % tm:
        tm -= LANES
    tiles_per_chunk = tc // tm
    kern = functools.partial(_router_kernel, tiles_per_chunk=tiles_per_chunk, n_exp=n_exp)
    return pl.pallas_call(
        kern,
        out_shape=(
            jax.ShapeDtypeStruct((n, d), f32),
            jax.ShapeDtypeStruct((TOP_K, n), i32),
            jax.ShapeDtypeStruct((TOP_K, n), f32),
            jax.ShapeDtypeStruct((TOP_K, n), i32),
            jax.ShapeDtypeStruct((n // tc, n_exp, 1), i32),
        ),
        grid=(n // tm,),
        in_specs=[
            pl.BlockSpec((tm, d), lambda i: (i, 0)),
            pl.BlockSpec((1, d), lambda i: (0, 0)),
            pl.BlockSpec((n_exp, d), lambda i: (0, 0)),
            pl.BlockSpec((n_exp, 1), lambda i: (0, 0)),
        ],
        out_specs=(
            pl.BlockSpec((tm, d), lambda i: (i, 0)),
            pl.BlockSpec((TOP_K, tm), lambda i: (0, i)),
            pl.BlockSpec((TOP_K, tm), lambda i: (0, i)),
            pl.BlockSpec((TOP_K, tm), lambda i: (0, i)),
            pl.BlockSpec((1, n_exp, 1), lambda i: (i // tiles_per_chunk, 0, 0)),
        ),
        scratch_shapes=[pltpu.VMEM((n_exp, 1), f32)],
        compiler_params=_cparams(("arbitrary",), 32),
        name="router",
    )(h, g.reshape(1, d), w_router.T, b_router.reshape(n_exp, 1))


def _split_pairs_kernel(w_ref, p_ref, o_ref):
    width = p_ref.shape[0]
    for b in range(w_ref.shape[2] // width):
        cols = slice(b * width, (b + 1) * width)
        o_ref[0, :, cols] = jnp.dot(w_ref[0, :, cols].astype(bf16), p_ref[...],
                                    preferred_element_type=f32).astype(bf16)


def split_pairs(w):
    n_exp, d, f2 = w.shape
    width = 2 * LANES
    j = jnp.arange(width)
    src = jnp.where(j < LANES, 2 * j, 2 * (j - LANES) + 1)
    perm = (jnp.arange(width)[:, None] == src[None, :]).astype(bf16)
    return pl.pallas_call(
        _split_pairs_kernel,
        out_shape=jax.ShapeDtypeStruct((n_exp, d, f2), bf16),
        grid=(n_exp,),
        in_specs=[
            pl.BlockSpec((1, d, f2), lambda e: (e, 0, 0)),
            pl.BlockSpec((width, width), lambda e: (0, 0)),
        ],
        out_specs=pl.BlockSpec((1, d, f2), lambda e: (e, 0, 0)),
        compiler_params=_cparams(("arbitrary",), 40),
        name="split_pairs",
    )(w, perm)


def _dense_row_index(r, tiles):
    return (r // SUBLANES) * tiles * SUBLANES + r % SUBLANES


def _experts_kernel(start_ref, nsub_ref, xp_hbm, dest_hbm, gates_hbm, fill_hbm, wup_ref, bup_ref, wdn_ref, bdn_ref,
                    f_hbm, xs_ref, acc_ref, xt_a, xt_b, y_a, y_b, dest_s, gate_s, rmap_s, sem_ref,
                    *, tc, n_exp, d, null_row0, seg, half_len, place_per_step):
    c = pl.program_id(0)
    e = pl.program_id(1)
    ms = MOE_SUB
    tiles = d // LANES
    cur = pl.multiple_of((c % 2) * half_len, LANES)
    nxt = pl.multiple_of(half_len - cur, LANES)

    def gather(row0, xt_ref):
        for r in range(ms):
            t = rmap_s[cur + row0 + r] & (seg - 1)
            xt_ref[pl.ds(_dense_row_index(r, tiles), tiles, stride=SUBLANES), :] = xs_ref[t]

    def place(base, first, count):
        for k in range(TOP_K):
            for u in range(count):
                a = k * seg + first + u
                rmap_s[base + dest_s[a]] = a

    def scatter(row0, y_ref):
        for r0 in range(0, ms, SCATTER_UNROLL):
            toks, sums = [], []
            for r in range(r0, r0 + SCATTER_UNROLL):
                a = rmap_s[cur + row0 + r]
                t = a & (seg - 1)
                yrow = y_ref[pl.ds(_dense_row_index(r, tiles), tiles, stride=SUBLANES), :]
                toks.append(t)
                sums.append(acc_ref[t] + gate_s[a] * yrow)
            for t, s in zip(toks, sums):
                acc_ref[t] = s

    @pl.when(e == 0)
    def _():
        copies = [
            pltpu.make_async_copy(xp_hbm.at[pl.ds(c * tc, tc)], xs_ref.at[pl.ds(0, tc)], sem_ref.at[0]),
            pltpu.make_async_copy(gates_hbm.at[c], gate_s, sem_ref.at[1]),
        ]
        for cp in copies:
            cp.start()
        xs_ref[pl.ds(tc, SUBLANES)] = jnp.zeros((SUBLANES, tiles, LANES), f32)
        acc_ref[...] = jnp.zeros_like(acc_ref)

        def load_map_inputs(chunk, base):
            loads = [
                pltpu.make_async_copy(dest_hbm.at[chunk], dest_s, sem_ref.at[2]),
                pltpu.make_async_copy(fill_hbm, rmap_s.at[pl.ds(base, half_len)], sem_ref.at[3]),
            ]
            for cp in loads:
                cp.start()
            for cp in loads:
                cp.wait()

        @pl.when(c == 0)
        def _():
            y_a[...] = jnp.zeros_like(y_a)
            y_b[...] = jnp.zeros_like(y_b)
            load_map_inputs(0, 0)

            def place_all(i, carry):
                place(0, i * SUBLANES, SUBLANES)
                return carry

            lax.fori_loop(0, tc // SUBLANES, place_all, 0)

        load_map_inputs(jnp.minimum(c + 1, pl.num_programs(0) - 1), nxt)
        for cp in copies:
            cp.wait()
        gather(0, xt_a)

    g = c * n_exp + e
    j0 = start_ref[g]

    def step(j, xt_cur, xt_nxt, y_cur, y_prv):
        gather((j + 1) * ms, xt_nxt)
        scatter(jnp.where(j == 0, null_row0, (j - 1) * ms), y_prv)
        place(nxt, j * place_per_step, place_per_step)
        x = jnp.concatenate(
            [jnp.concatenate([xt_cur[pl.ds((rg * tiles + jt) * SUBLANES, SUBLANES), :] for jt in range(tiles)], axis=1)
             for rg in range(ms // SUBLANES)], axis=0).astype(bf16)
        hmid = jnp.dot(x, wup_ref[0], preferred_element_type=f32) + bup_ref[0]
        glu = jnp.concatenate([hmid[:, 2 * jt * LANES:(2 * jt + 1) * LANES] for jt in range(tiles)], axis=1)
        lin = jnp.concatenate([hmid[:, (2 * jt + 1) * LANES:(2 * jt + 2) * LANES] for jt in range(tiles)], axis=1)
        glu = jnp.minimum(glu, SWIGLU_LIMIT)
        lin = jnp.clip(lin, -SWIGLU_LIMIT, SWIGLU_LIMIT)
        act = glu * jax.nn.sigmoid(SWIGLU_ALPHA * glu) * (lin + 1.0)
        y = jnp.dot(act.astype(bf16), wdn_ref[0], preferred_element_type=f32) + bdn_ref[0]
        for rg in range(ms // SUBLANES):
            for jt in range(tiles):
                y_cur[pl.ds((rg * tiles + jt) * SUBLANES, SUBLANES), :] = (
                    y[rg * SUBLANES:(rg + 1) * SUBLANES, jt * LANES:(jt + 1) * LANES])

    def sub_block(i, carry):
        j = j0 + i

        @pl.when(j % 2 == 0)
        def _():
            step(j, xt_a, xt_b, y_a, y_b)

        @pl.when(j % 2 == 1)
        def _():
            step(j, xt_b, xt_a, y_b, y_a)

        return carry

    lax.fori_loop(0, nsub_ref[g], sub_block, 0)

    @pl.when(e == n_exp - 1)
    def _():
        last = j0 + nsub_ref[g] - 1

        @pl.when(last % 2 == 0)
        def _():
            scatter(last * ms, y_a)

        @pl.when(last % 2 == 1)
        def _():
            scatter(last * ms, y_b)

        cp = pltpu.make_async_copy(acc_ref.at[pl.ds(0, tc)], f_hbm.at[pl.ds(c * tc, tc)], sem_ref.at[0])
        cp.start()
        cp.wait()


def _row_map_geometry(tc, n_exp, seg):
    assert seg > tc and seg & (seg - 1) == 0
    max_sub = (TOP_K * tc + n_exp * (MOE_SUB - 1)) // MOE_SUB
    min_sub = -(-TOP_K * tc // MOE_SUB)
    null_row0 = (max_sub + 1) * MOE_SUB
    half_len = -(-(null_row0 + MOE_SUB + 1) // LANES) * LANES
    place_per_step = -(-tc // min_sub)
    assert max_sub * place_per_step <= seg
    return null_row0, half_len, place_per_step


def experts(sb_start, n_sub, xp, dest, gates, w_up, b_up, w_down, b_down, tc):
    n, tiles, _ = xp.shape
    d = tiles * LANES
    n_exp = w_up.shape[0]
    chunks, padded_len = dest.shape
    seg = padded_len // TOP_K
    null_row0, half_len, place_per_step = _row_map_geometry(tc, n_exp, seg)
    fill = jnp.full((half_len,), tc, i32)
    kern = functools.partial(_experts_kernel, tc=tc, n_exp=n_exp, d=d, null_row0=null_row0, seg=seg,
                             half_len=half_len, place_per_step=place_per_step)
    block = pltpu.VMEM((MOE_SUB * tiles, LANES), f32)
    return pl.pallas_call(
        kern,
        out_shape=jax.ShapeDtypeStruct((n, tiles, LANES), f32),
        grid_spec=pltpu.PrefetchScalarGridSpec(
            num_scalar_prefetch=2,
            grid=(chunks, n_exp),
            in_specs=[
                pl.BlockSpec(memory_space=pl.ANY),
                pl.BlockSpec(memory_space=pl.ANY),
                pl.BlockSpec(memory_space=pl.ANY),
                pl.BlockSpec(memory_space=pl.ANY),
                pl.BlockSpec((1, d, 2 * d), lambda c, e, *_: (e, 0, 0)),
                pl.BlockSpec((1, 1, 2 * d), lambda c, e, *_: (e, 0, 0)),
                pl.BlockSpec((1, d, d), lambda c, e, *_: (e, 0, 0)),
                pl.BlockSpec((1, 1, d), lambda c, e, *_: (e, 0, 0)),
            ],
            out_specs=pl.BlockSpec(memory_space=pl.ANY),
            scratch_shapes=[
                pltpu.VMEM((tc + SUBLANES, tiles, LANES), f32),
                pltpu.VMEM((tc + SUBLANES, tiles, LANES), f32),
                block, block, block, block,
                pltpu.SMEM((padded_len,), i32),
                pltpu.SMEM((padded_len,), f32),
                pltpu.SMEM((2 * half_len,), i32),
                pltpu.SemaphoreType.DMA((4,)),
            ],
        ),
        compiler_params=_cparams(("arbitrary", "arbitrary"), 58),
        name="experts",
    )(sb_start, n_sub, xp, dest, gates, fill, w_up, b_up.reshape(n_exp, 1, 2 * d), w_down,
      b_down.reshape(n_exp, 1, d))


def _final_kernel(h_ref, f_ref, g_ref, o_ref):
    tm, d = h_ref.shape
    tiles = d // LANES
    parts = [h_ref[:, jt * LANES:(jt + 1) * LANES] + f_ref[pl.ds(jt, tm, stride=tiles), :] for jt in range(tiles)]
    o_ref[...] = _rms(jnp.concatenate(parts, axis=1), g_ref[...])


def final_norm(h, f2, g, row0, rows):
    d = h.shape[1]
    tiles = d // LANES
    tm = min(ROW_TILE, rows)
    blk0 = row0 // tm
    assert rows % tm == 0 and row0 % tm == 0
    return pl.pallas_call(
        _final_kernel,
        out_shape=jax.ShapeDtypeStruct((rows, d), f32),
        grid=(rows // tm,),
        in_specs=[
            pl.BlockSpec((tm, d), lambda i: (blk0 + i, 0)),
            pl.BlockSpec((tm * tiles, LANES), lambda i: (blk0 + i, 0)),
            pl.BlockSpec((1, d), lambda i: (0, 0)),
        ],
        out_specs=pl.BlockSpec((tm, d), lambda i: (i, 0)),
        compiler_params=_cparams(("arbitrary",), 32),
        name="final_norm",
    )(h, f2, g.reshape(1, d))


def moe(h, g, w_router, b_router, w_up, b_up, w_down, b_down):
    n, d = h.shape
    n_exp = w_router.shape[1]
    tc = n // MOE_CHUNKS
    xp, top_e, gates, rank, counts = router(h, g, w_router, b_router, tc)
    n_sub = (counts.reshape(MOE_CHUNKS, n_exp) + MOE_SUB - 1) // MOE_SUB
    sb_start = jnp.cumsum(n_sub, axis=1) - n_sub
    hot = top_e.reshape(TOP_K, MOE_CHUNKS, tc, 1) == jnp.arange(n_exp, dtype=i32)
    row_base = (sb_start * MOE_SUB).reshape(1, MOE_CHUNKS, 1, n_exp)
    dest = jnp.sum(jnp.where(hot, row_base, 0), axis=-1) + rank.reshape(TOP_K, MOE_CHUNKS, tc)
    seg = 1 << tc.bit_length()
    spare_row = _row_map_geometry(tc, n_exp, seg)[1] - 1
    per_chunk = lambda a, tail: jnp.pad(a.transpose(1, 0, 2), ((0, 0), (0, 0), (0, seg - tc)),
                                        constant_values=tail).reshape(MOE_CHUNKS, TOP_K * seg)
    tiles = d // LANES
    b_up_p = b_up.reshape(n_exp, tiles, LANES, 2).transpose(0, 1, 3, 2).reshape(n_exp, 2 * d)
    f = experts(sb_start.reshape(-1).astype(i32), n_sub.reshape(-1).astype(i32), xp.reshape(n, tiles, LANES),
                per_chunk(dest, spare_row), per_chunk(gates.reshape(TOP_K, MOE_CHUNKS, tc), 0.0), split_pairs(w_up),
                b_up_p,
                w_down.astype(bf16), b_down, tc)
    return f.reshape(n * tiles, LANES)


def kernel(x_prompt, x_sample, mem_prompt, state_ret, state_conv, cache_mem_k, cache_mem_v, norm_mix, w_in, conv_w,
           ret_gn, w_out, norm_cross, norm_mem, w_mq, w_mk, w_mv, w_mo, norm_ffn, w_router, b_router, w_up, b_up,
           w_down, b_down, norm_final):
    batch, seq, d = x_prompt.shape
    n_seq, ts, _ = x_sample.shape
    depth = w_in.shape[0]
    n_mem = mem_prompt.shape[1]
    r = ret_gn.shape[1]
    cd_width = conv_w.shape[2]
    dh = r // N_RET_HEADS
    n_p = batch * seq
    n_s = n_seq * ts
    n = n_p + n_s
    nblk = n_seq // SAMPLE_BLOCK
    assert seq % ROW_TILE == 0 and n % ROW_TILE == 0 and n % MOE_CHUNKS == 0

    xs = x_sample.reshape(nblk, SAMPLE_BLOCK, ts, d).transpose(0, 2, 1, 3).reshape(n_s, d)
    h_rows = (x_prompt.reshape(n_p, d), xs)
    mem2d = mem_prompt.reshape(batch * n_mem, d)
    tab_p = prompt_tables(seq, dh)
    tab_s = sample_tables(ts, float(PAST_LEN), dh)

    ret_p, conv_p, mk_p, mv_p, ret_s, conv_s = [], [], [], [], [], []
    for l in range(depth):
        if l == 0:
            proj = norm_matmul2(*h_rows, norm_mix[l], w_in[l].astype(bf16), "in_proj")
        else:
            proj = norm_matmul(h, norm_mix[l], w_in[l].astype(bf16), "in_proj")
        mix_p, s_p, c_p = prompt_mixer(proj, tab_p, conv_w[l], ret_gn[l], batch, seq, r, cd_width)
        mix_s, s_s, c_s = sample_mixer(proj, tab_s, conv_w[l], ret_gn[l], state_ret[l],
                                       state_conv[l].transpose(1, 0, 2), n_p, ts, r, cd_width)
        h = matmul_res(mix_p, mix_s, w_out[l].astype(bf16), *h_rows, "out_proj")

        mk, mv = mem_kv(mem_prompt, norm_mem[l], jnp.concatenate([w_mk[l], w_mv[l]], axis=1).astype(bf16),
                        N_MEM_HEADS)
        w_mq_b = w_mq[l].astype(bf16)
        q_s = norm_matmul(h[n_p:], norm_cross[l], w_mq_b, "q_sample")
        attn_s = cross_sample(q_s, cache_mem_k[l], cache_mem_v[l], 0, ts)
        h = attn_block(h, norm_cross[l], w_mq_b, mk, mv, attn_s, w_mo[l].astype(bf16), seq)

        f2 = moe(h, norm_ffn[l], w_router[l], b_router[l], w_up[l], b_up[l], w_down[l], b_down[l])
        if l + 1 < depth:
            h = h + f2.reshape(n, d // LANES, LANES).reshape(n, d)
            h_rows = (h, h)

        ret_p.append(s_p)
        conv_p.append(c_p)
        mk_p.append(mk)
        mv_p.append(mv)
        ret_s.append(s_s)
        conv_s.append(c_s.transpose(1, 0, 2))

    y_p = final_norm(h, f2, norm_final, 0, n_p).reshape(batch, seq, d)
    y_s = final_norm(h, f2, norm_final, n_p, n_s)
    y_s = y_s.reshape(nblk, ts, SAMPLE_BLOCK, d).transpose(0, 2, 1, 3).reshape(n_seq, ts, d)
    return (y_p, y_s, jnp.stack(ret_p), jnp.stack(conv_p), jnp.stack(mk_p), jnp.stack(mv_p), jnp.stack(ret_s),
            jnp.stack(conv_s))
```

```python
import functools

import jax
import jax.numpy as jnp
from jax import lax
from jax.experimental import pallas as pl
from jax.experimental.pallas import tpu as pltpu

f32 = jnp.float32
bf16 = jnp.bfloat16
i32 = jnp.int32

EPS = 1e-6
ROPE_THETA = 10000.0
RET_CHUNK = 128
PAST_LEN = 16384
N_RET_HEADS = 4
N_MEM_HEADS = 4
TOP_K = 4
SWIGLU_ALPHA = 1.702
SWIGLU_LIMIT = 7.0

LANES = 128
SUBLANES = 8
MIB = 1024 * 1024

ROW_TILE = 512
SAMPLE_BLOCK = 8
MOE_CHUNKS = 4
MOE_SUB = 256
SCATTER_UNROLL = 8


def _cparams(sem, vmem_mib):
    return pltpu.CompilerParams(dimension_semantics=sem, vmem_limit_bytes=vmem_mib * MIB)


def _rms(x, g):
    ms = jnp.mean(x * x, axis=-1, keepdims=True)
    return (x * lax.rsqrt(ms + EPS)) * g


def _norm_matmul_kernel(x_ref, g_ref, w_ref, o_ref):
    xn = _rms(x_ref[...], g_ref[...])
    o_ref[...] = jnp.dot(xn.astype(bf16), w_ref[...], preferred_element_type=f32)


def norm_matmul(x, g, w, name):
    m, d = x.shape
    f = w.shape[1]
    tm = min(ROW_TILE, m)
    assert m % tm == 0
    return pl.pallas_call(
        _norm_matmul_kernel,
        out_shape=jax.ShapeDtypeStruct((m, f), f32),
        grid=(m // tm,),
        in_specs=[
            pl.BlockSpec((tm, d), lambda i: (i, 0)),
            pl.BlockSpec((1, d), lambda i: (0, 0)),
            pl.BlockSpec((d, f), lambda i: (0, 0)),
        ],
        out_specs=pl.BlockSpec((tm, f), lambda i: (i, 0)),
        compiler_params=_cparams(("arbitrary",), 48),
        name=name,
    )(x, g.reshape(1, d), w)


def _stacked_specs(m_p, m_s, width):
    tm = ROW_TILE
    assert m_p % tm == 0 and m_s == tm
    prompt_tiles = m_p // tm
    return prompt_tiles, [
        pl.BlockSpec((tm, width), lambda i: (jnp.minimum(i, prompt_tiles - 1), 0)),
        pl.BlockSpec((tm, width), lambda i: (0, 0)),
    ]


def _matmul_res_kernel(a_ref, w_ref, r_ref, o_ref):
    o_ref[...] = r_ref[...] + jnp.dot(a_ref[...], w_ref[...], preferred_element_type=f32)


def matmul_res(a, w, res, name):
    m, d = a.shape
    f = w.shape[1]
    tm = min(ROW_TILE, m)
    assert m % tm == 0
    return pl.pallas_call(
        _matmul_res_kernel,
        out_shape=jax.ShapeDtypeStruct((m, f), f32),
        grid=(m // tm,),
        in_specs=[
            pl.BlockSpec((tm, d), lambda i: (i, 0)),
            pl.BlockSpec((d, f), lambda i: (0, 0)),
            pl.BlockSpec((tm, f), lambda i: (i, 0)),
        ],
        out_specs=pl.BlockSpec((tm, f), lambda i: (i, 0)),
        compiler_params=_cparams(("arbitrary",), 32),
        name=name,
    )(a, w, res)


def _rope(x, cos, sin_signed, half):
    return x * cos + pltpu.roll(x, half, 1) * sin_signed


def _group_norm_gate(o, gate, gn):
    mu = jnp.mean(o, axis=-1, keepdims=True)
    var = jnp.mean(jnp.square(o - mu), axis=-1, keepdims=True)
    on = ((o - mu) * lax.rsqrt(var + EPS)) * gn
    return (gate * jax.nn.sigmoid(gate)) * on


def _prompt_mixer_kernel(cd_ref, x_ref, g_ref, win_ref, cos_ref, sin_ref, decay_ref, xi_ref, zeta_ref, convw_ref,
                         gn_ref, wout_ref, h_ref, sfin_ref, cfin_ref, s_scr, u_scr, proj_ref, mix_ref,
                         *, tt, r, cd_width, dh):
    j = pl.program_id(1)
    heads = r // dh
    pad = SUBLANES

    @pl.when(j == 0)
    def _():
        s_scr[...] = jnp.zeros_like(s_scr)
        u_scr[0:pad, :] = jnp.zeros((pad, cd_width), f32)

    proj_ref[...] = jnp.dot(_rms(x_ref[...], g_ref[...]).astype(bf16), win_ref[...], preferred_element_type=f32)

    k_scale = dh ** -0.5
    for c in range(tt // RET_CHUNK):
        rows = pl.ds(c * RET_CHUNK, RET_CHUNK)
        cosb = cos_ref[rows, :]
        sinb = sin_ref[rows, :]
        for h in range(heads):
            lo, hi = h * dh, (h + 1) * dh
            qh = _rope(proj_ref[rows, lo:hi], cosb, sinb, dh // 2)
            kh = _rope(proj_ref[rows, r + lo:r + hi], cosb, sinb, dh // 2) * k_scale
            vb = proj_ref[rows, 2 * r + lo:2 * r + hi].astype(bf16)
            gate = proj_ref[rows, 3 * r + lo:3 * r + hi]
            qb = qh.astype(bf16)
            kb = kh.astype(bf16)
            scores = lax.dot_general(qb, kb, (((1,), (1,)), ((), ())), preferred_element_type=f32) * decay_ref[h]
            inner = jnp.dot(scores.astype(bf16), vb, preferred_element_type=f32)
            s_prev = s_scr[h]
            cross = jnp.dot((qh * xi_ref[h]).astype(bf16), s_prev.astype(bf16), preferred_element_type=f32)
            kv = lax.dot_general((kh * zeta_ref[h]).astype(bf16), vb, (((0,), (0,)), ((), ())),
                                 preferred_element_type=f32)
            s_scr[h] = cd_ref[h] * s_prev + kv
            mix_ref[rows, lo:hi] = _group_norm_gate(inner + cross, gate, gn_ref[:, lo:hi]).astype(bf16)

    base = 4 * r
    u = proj_ref[:, base + cd_width:base + 2 * cd_width] * proj_ref[:, base:base + cd_width]
    u_scr[pad:pad + tt, :] = u
    conv = (u_scr[pad - 2:pad - 2 + tt, :] * convw_ref[0:1, :] + u_scr[pad - 1:pad - 1 + tt, :] * convw_ref[1:2, :]
            + u * convw_ref[2:3, :])
    mix_ref[:, r:r + cd_width] = (proj_ref[:, base + 2 * cd_width:base + 3 * cd_width] * conv).astype(bf16)
    u_scr[0:pad, :] = u_scr[tt:tt + pad, :]
    cfin_ref[0] = u_scr[pad - 2:pad, :]
    h_ref[...] = x_ref[...] + jnp.dot(mix_ref[...], wout_ref[...], preferred_element_type=f32)

    @pl.when(j == pl.num_programs(1) - 1)
    def _():
        sfin_ref[0] = s_scr[...]


def prompt_mixer(x, g, w_in, tables, conv_w, ret_gn, w_out, batch, seq, r, cd_width):
    tt = ROW_TILE
    d = x.shape[1]
    dh = r // N_RET_HEADS
    steps = seq // tt
    in_cols = w_in.shape[1]
    kern = functools.partial(_prompt_mixer_kernel, tt=tt, r=r, cd_width=cd_width, dh=dh)
    tab = lambda shape: pl.BlockSpec(shape, lambda b, j, cd: (0,) * len(shape))
    row_tile = pl.BlockSpec((tt, d), lambda b, j, cd: (b * steps + j, 0))
    return pl.pallas_call(
        kern,
        out_shape=(
            jax.ShapeDtypeStruct((batch * seq, d), f32),
            jax.ShapeDtypeStruct((batch, N_RET_HEADS, dh, dh), f32),
            jax.ShapeDtypeStruct((batch, 2, cd_width), f32),
        ),
        grid_spec=pltpu.PrefetchScalarGridSpec(
            num_scalar_prefetch=1,
            grid=(batch, steps),
            in_specs=[
                row_tile,
                tab((1, d)),
                tab((d, in_cols)),
                pl.BlockSpec((tt, dh), lambda b, j, cd: (j, 0)),
                pl.BlockSpec((tt, dh), lambda b, j, cd: (j, 0)),
                tab((N_RET_HEADS, RET_CHUNK, RET_CHUNK)),
                tab((N_RET_HEADS, RET_CHUNK, dh)),
                tab((N_RET_HEADS, RET_CHUNK, dh)),
                tab((3, cd_width)),
                tab((1, r)),
                tab((r + cd_width, d)),
            ],
            out_specs=(
                row_tile,
                pl.BlockSpec((1, N_RET_HEADS, dh, dh), lambda b, j, cd: (b, 0, 0, 0)),
                pl.BlockSpec((1, 2, cd_width), lambda b, j, cd: (b, 0, 0)),
            ),
            scratch_shapes=[
                pltpu.VMEM((N_RET_HEADS, dh, dh), f32),
                pltpu.VMEM((tt + 2 * SUBLANES, cd_width), f32),
                pltpu.VMEM((tt, in_cols), f32),
                pltpu.VMEM((tt, r + cd_width), bf16),
            ],
        ),
        compiler_params=_cparams(("arbitrary", "arbitrary"), 56),
        name="prompt_mixer",
    )(tables["chunk_decay"], x, g.reshape(1, d), w_in, tables["cos"], tables["sin"], tables["decay"], tables["xi"],
      tables["zeta"], conv_w, ret_gn.reshape(1, r), w_out)


def _sample_mixer_kernel(dec_ref, proj_ref, cos_ref, sin_ref, xi_ref, zeta_ref, convw_ref, gn_ref, s0_ref, c0_ref,
                         mix_ref, snew_ref, cnew_ref, *, ts, r, cd_width, dh):
    heads = r // dh
    nb = SAMPLE_BLOCK
    rows = ts * nb
    k_scale = dh ** -0.5
    cosb = cos_ref[...]
    sinb = sin_ref[...]
    seq_of_row = lax.broadcasted_iota(i32, (rows, dh), 0) % nb
    for h in range(heads):
        lo, hi = h * dh, (h + 1) * dh
        qh = _rope(proj_ref[:, lo:hi], cosb, sinb, dh // 2)
        kh = _rope(proj_ref[:, r + lo:r + hi], cosb, sinb, dh // 2) * k_scale
        vh = proj_ref[:, 2 * r + lo:2 * r + hi]
        gate = proj_ref[:, 3 * r + lo:3 * r + hi]
        inner = []
        for t in range(ts):
            qt = qh[t * nb:(t + 1) * nb]
            acc = jnp.zeros((nb, dh), f32)
            for s in range(t + 1):
                sc = jnp.sum(qt * kh[s * nb:(s + 1) * nb], axis=-1, keepdims=True) * dec_ref[h * (ts + 1) + t - s]
                acc = acc + sc * vh[s * nb:(s + 1) * nb]
            inner.append(acc)
        inner = jnp.concatenate(inner, axis=0)
        qx = (qh * xi_ref[h]).astype(bf16)
        kz = kh * zeta_ref[h]
        vb = vh.astype(bf16)
        cross = jnp.zeros((rows, dh), f32)
        for b in range(nb):
            mine = seq_of_row == b
            s_prev = s0_ref[b, h]
            res = jnp.dot(qx, s_prev.astype(bf16), preferred_element_type=f32)
            cross = cross + jnp.where(mine, res, 0.0)
            kv = lax.dot_general(jnp.where(mine, kz, 0.0).astype(bf16), vb, (((0,), (0,)), ((), ())),
                                 preferred_element_type=f32)
            snew_ref[b, h] = dec_ref[h * (ts + 1) + ts] * s_prev + kv
        mix_ref[:, lo:hi] = _group_norm_gate(inner + cross, gate, gn_ref[:, lo:hi]).astype(bf16)

    base = 4 * r
    u = proj_ref[:, base + cd_width:base + 2 * cd_width] * proj_ref[:, base:base + cd_width]
    full = [c0_ref[0], c0_ref[1]] + [u[t * nb:(t + 1) * nb] for t in range(ts)]
    conv = jnp.concatenate(
        [full[t] * convw_ref[0:1, :] + full[t + 1] * convw_ref[1:2, :] + full[t + 2] * convw_ref[2:3, :]
         for t in range(ts)], axis=0)
    mix_ref[:, r:r + cd_width] = (proj_ref[:, base + 2 * cd_width:base + 3 * cd_width] * conv).astype(bf16)
    cnew_ref[0] = full[ts]
    cnew_ref[1] = full[ts + 1]


def sample_mixer(proj, tables, conv_w, ret_gn, state_ret, state_conv_t, row0, ts, r, cd_width):
    n_seq = state_ret.shape[0]
    dh = r // N_RET_HEADS
    rows = ts * SAMPLE_BLOCK
    blk0 = row0 // rows
    assert row0 % rows == 0 and n_seq % SAMPLE_BLOCK == 0
    in_cols = proj.shape[1]
    kern = functools.partial(_sample_mixer_kernel, ts=ts, r=r, cd_width=cd_width, dh=dh)
    tab = lambda shape: pl.BlockSpec(shape, lambda i, d: (0,) * len(shape))
    return pl.pallas_call(
        kern,
        out_shape=(
            jax.ShapeDtypeStruct((n_seq * ts, r + cd_width), bf16),
            jax.ShapeDtypeStruct(state_ret.shape, f32),
            jax.ShapeDtypeStruct(state_conv_t.shape, f32),
        ),
        grid_spec=pltpu.PrefetchScalarGridSpec(
            num_scalar_prefetch=1,
            grid=(n_seq // SAMPLE_BLOCK,),
            in_specs=[
                pl.BlockSpec((rows, in_cols), lambda i, d: (blk0 + i, 0)),
                tab((rows, dh)),
                tab((rows, dh)),
                tab((N_RET_HEADS, rows, dh)),
                tab((N_RET_HEADS, rows, dh)),
                tab((3, cd_width)),
                tab((1, r)),
                pl.BlockSpec((SAMPLE_BLOCK, N_RET_HEADS, dh, dh), lambda i, d: (i, 0, 0, 0)),
                pl.BlockSpec((2, SAMPLE_BLOCK, cd_width), lambda i, d: (0, i, 0)),
            ],
            out_specs=(
                pl.BlockSpec((rows, r + cd_width), lambda i, d: (i, 0)),
                pl.BlockSpec((SAMPLE_BLOCK, N_RET_HEADS, dh, dh), lambda i, d: (i, 0, 0, 0)),
                pl.BlockSpec((2, SAMPLE_BLOCK, cd_width), lambda i, d: (0, i, 0)),
            ),
        ),
        compiler_params=_cparams(("arbitrary",), 32),
        name="sample_mixer",
    )(tables["dec"], proj, tables["cos"], tables["sin"], tables["xi"], tables["zeta"], conv_w, ret_gn.reshape(1, r),
      state_ret, state_conv_t)


def _log_gamma():
    return jnp.log1p(-jnp.exp2(-5.0 - jnp.arange(N_RET_HEADS, dtype=f32)))


def _rope_tables(pos, dh):
    half = dh // 2
    inv = ROPE_THETA ** (-jnp.arange(half, dtype=f32) / half)
    ang = pos[:, None] * inv[None, :]
    cos, sin = jnp.cos(ang), jnp.sin(ang)
    return jnp.concatenate([cos, cos], axis=-1), jnp.concatenate([-sin, sin], axis=-1)


def prompt_tables(seq, dh):
    c = RET_CHUNK
    lg = _log_gamma()
    i = jnp.arange(c, dtype=f32)
    diff = i[:, None] - i[None, :]
    decay = jnp.where(diff >= 0, jnp.exp(jnp.maximum(diff, 0.0)[None] * lg[:, None, None]), 0.0)
    xi = jnp.exp((i + 1.0)[None, :] * lg[:, None])
    zeta = jnp.exp((c - 1.0 - i)[None, :] * lg[:, None])
    cos, sin = _rope_tables(jnp.arange(seq, dtype=f32), dh)
    bc = lambda t: jnp.broadcast_to(t[:, :, None], (N_RET_HEADS, c, dh))
    return dict(cos=cos, sin=sin, decay=decay, xi=bc(xi), zeta=bc(zeta), chunk_decay=jnp.exp(c * lg))


def sample_tables(ts, pos0, dh):
    lg = _log_gamma()
    i = jnp.arange(ts, dtype=f32)
    dec = jnp.exp(jnp.arange(ts + 1, dtype=f32)[None, :] * lg[:, None])
    xi = jnp.exp((i + 1.0)[None, :] * lg[:, None])
    zeta = jnp.exp((ts - 1.0 - i)[None, :] * lg[:, None])
    cos, sin = _rope_tables(pos0 + i, dh)
    rep = lambda t: jnp.repeat(t, SAMPLE_BLOCK, axis=0)
    bc = lambda t: jnp.broadcast_to(jnp.repeat(t, SAMPLE_BLOCK, axis=1)[:, :, None],
                                    (N_RET_HEADS, ts * SAMPLE_BLOCK, dh))
    return dict(cos=rep(cos), sin=rep(sin), xi=bc(xi), zeta=bc(zeta), dec=dec.reshape(-1))


def _softmax_rows(s):
    m = jnp.max(s, axis=-1, keepdims=True)
    p = jnp.exp(s - m)
    return p / jnp.sum(p, axis=-1, keepdims=True)


def _from_slabs(ref, rows=None):
    rows = slice(None) if rows is None else rows
    return jnp.concatenate([ref[lt, rows, :] for lt in range(ref.shape[0])], axis=1)


def _to_slabs(ref, value):
    for lt in range(ref.shape[0]):
        ref[lt] = value[:, lt * LANES:(lt + 1) * LANES]


def _mem_kv_kernel(m_ref, g_ref, w_ref, k_ref, v_ref, rows_ref, *, heads):
    n_mem, d = m_ref.shape
    dh = d // heads
    kv = jnp.dot(_rms(m_ref[...], g_ref[...]).astype(bf16), w_ref[...], preferred_element_type=f32)
    for part, o_ref in enumerate((k_ref, v_ref)):
        for h in range(heads):
            for lt in range(dh // LANES):
                lo = part * d + h * dh + lt * LANES
                rows_ref[lt, pl.ds(h, n_mem, stride=heads), :] = kv[:, lo:lo + LANES]
        o_ref[0] = _from_slabs(rows_ref).reshape(n_mem, heads, dh)


def mem_kv(mem, g, w_kv, heads):
    batch, n_mem, d = mem.shape
    dh = d // heads
    out = jax.ShapeDtypeStruct((batch, n_mem, heads, dh), f32)
    out_spec = pl.BlockSpec((1, n_mem, heads, dh), lambda b: (b, 0, 0, 0))
    return pl.pallas_call(
        functools.partial(_mem_kv_kernel, heads=heads),
        out_shape=(out, out),
        grid=(batch,),
        in_specs=[
            pl.BlockSpec((n_mem, d), lambda b: (b, 0)),
            pl.BlockSpec((1, d), lambda b: (0, 0)),
            pl.BlockSpec((d, 2 * d), lambda b: (0, 0)),
        ],
        out_specs=(out_spec, out_spec),
        scratch_shapes=[pltpu.VMEM((dh // LANES, n_mem * heads, LANES), f32)],
        compiler_params=_cparams(("arbitrary",), 32),
        name="mem_kv",
    )(mem.reshape(batch * n_mem, d), g.reshape(1, d), w_kv)


def _attn_block_kernel(hp_ref, hs_ref, g_ref, wq_ref, k_ref, v_ref, as_ref, wo_ref, o_ref, k_rows, v_rows,
                       *, prompt_tiles, heads):
    i = pl.program_id(0)

    @pl.when(i < prompt_tiles)
    def _():
        h_ref = hp_ref
        n_mem, _, dh = k_ref.shape[1:]
        scale = dh ** -0.5
        q = jnp.dot(_rms(h_ref[...], g_ref[...]).astype(bf16), wq_ref[...], preferred_element_type=f32)
        _to_slabs(k_rows, k_ref[0].reshape(n_mem * heads, dh))
        _to_slabs(v_rows, v_ref[0].reshape(n_mem * heads, dh))
        outs = []
        for h in range(heads):
            kh = _from_slabs(k_rows, pl.ds(h, n_mem, stride=heads)).astype(bf16)
            vh = _from_slabs(v_rows, pl.ds(h, n_mem, stride=heads)).astype(bf16)
            s = lax.dot_general(q[:, h * dh:(h + 1) * dh].astype(bf16), kh, (((1,), (1,)), ((), ())),
                                preferred_element_type=f32) * scale
            outs.append(jnp.dot(_softmax_rows(s).astype(bf16), vh, preferred_element_type=f32).astype(bf16))
        attn = jnp.concatenate(outs, axis=1)
        o_ref[...] = h_ref[...] + jnp.dot(attn, wo_ref[...], preferred_element_type=f32)

    @pl.when(i == prompt_tiles)
    def _():
        o_ref[...] = hs_ref[...] + jnp.dot(as_ref[...], wo_ref[...], preferred_element_type=f32)


def attn_block(h_prompt, h_sample, g, w_mq, mk, mv, attn_sample, w_mo, seq):
    d = h_prompt.shape[1]
    batch, n_mem, heads, dh = mk.shape
    tm = ROW_TILE
    per_batch = seq // tm
    prompt_tiles, h_specs = _stacked_specs(batch * seq, h_sample.shape[0], d)
    assert attn_sample.shape == (tm, d)
    kv_spec = pl.BlockSpec((1, n_mem, heads, dh),
                           lambda i: (jnp.minimum(i, prompt_tiles - 1) // per_batch, 0, 0, 0))
    const = lambda shape: pl.BlockSpec(shape, lambda i: (0,) * len(shape))
    return pl.pallas_call(
        functools.partial(_attn_block_kernel, prompt_tiles=prompt_tiles, heads=heads),
        out_shape=jax.ShapeDtypeStruct(((prompt_tiles + 1) * tm, d), f32),
        grid=(prompt_tiles + 1,),
        in_specs=h_specs + [
            const((1, d)),
            const((d, d)),
            kv_spec,
            kv_spec,
            const((tm, d)),
            const((d, d)),
        ],
        out_specs=pl.BlockSpec((tm, d), lambda i: (i, 0)),
        scratch_shapes=[pltpu.VMEM((dh // LANES, n_mem * heads, LANES), f32)] * 2,
        compiler_params=_cparams(("arbitrary",), 48),
        name="attn_block",
    )(h_prompt, h_sample, g.reshape(1, d), w_mq, mk, mv, attn_sample, w_mo)


def _cross_sample_kernel(q_ref, k_ref, v_ref, o_ref, acc_ref, *, dh, seqs):
    part = pl.program_id(1)
    rows = q_ref.shape[0]
    scale = dh ** -0.5

    @pl.when(part == 0)
    def _():
        acc_ref[...] = jnp.zeros_like(acc_ref)

    heads = N_MEM_HEADS
    n_mem = k_ref.shape[1]
    qx = jnp.concatenate([q_ref[:, h * dh:(h + 1) * dh] for h in range(heads)], axis=0).astype(bf16)
    row = lax.broadcasted_iota(i32, (heads * rows, n_mem * heads), 0)
    col = lax.broadcasted_iota(i32, (heads * rows, n_mem * heads), 1)
    same_head = col % heads == row // rows
    seq_of_row = lax.broadcasted_iota(i32, (heads * rows, dh), 0) % SAMPLE_BLOCK
    out = jnp.zeros((heads * rows, dh), f32)
    for b in range(seqs):
        k2 = k_ref[b].reshape(n_mem * heads, dh).astype(bf16)
        v2 = v_ref[b].reshape(n_mem * heads, dh).astype(bf16)
        s = lax.dot_general(qx, k2, (((1,), (1,)), ((), ())), preferred_element_type=f32) * scale
        p = _softmax_rows(jnp.where(same_head, s, -jnp.inf))
        o = jnp.dot(p.astype(bf16), v2, preferred_element_type=f32)
        out = out + jnp.where(seq_of_row == part * seqs + b, o, 0.0)
    acc_ref[...] += out

    @pl.when(part == pl.num_programs(1) - 1)
    def _():
        for h in range(heads):
            o_ref[:, h * dh:(h + 1) * dh] = acc_ref[h * rows:(h + 1) * rows, :].astype(bf16)


def cross_sample(q, cache_k, cache_v, row0, ts):
    n_seq, n_mem, heads, dh = cache_k.shape
    d = heads * dh
    rows = ts * SAMPLE_BLOCK
    blk0 = row0 // rows
    parts = 4
    seqs = SAMPLE_BLOCK // parts
    kern = functools.partial(_cross_sample_kernel, dh=dh, seqs=seqs)
    return pl.pallas_call(
        kern,
        out_shape=jax.ShapeDtypeStruct((n_seq * ts, d), bf16),
        grid=(n_seq // SAMPLE_BLOCK, parts),
        in_specs=[
            pl.BlockSpec((rows, d), lambda i, p: (blk0 + i, 0)),
            pl.BlockSpec((seqs, n_mem, heads, dh), lambda i, p: (i * parts + p, 0, 0, 0)),
            pl.BlockSpec((seqs, n_mem, heads, dh), lambda i, p: (i * parts + p, 0, 0, 0)),
        ],
        out_specs=pl.BlockSpec((rows, d), lambda i, p: (i, 0)),
        scratch_shapes=[pltpu.VMEM((heads * rows, dh), f32)],
        compiler_params=_cparams(("arbitrary", "arbitrary"), 40),
        name="cross_sample",
    )(q, cache_k, cache_v)


def _router_kernel(h_ref, g_ref, wr_ref, br_ref, xp_ref, e_ref, gate_ref, rank_ref, cnt_ref, carry_ref,
                   *, tiles_per_chunk, n_exp):
    i = pl.program_id(0)

    @pl.when(i % tiles_per_chunk == 0)
    def _():
        carry_ref[...] = jnp.zeros_like(carry_ref)

    xn = _rms(h_ref[...], g_ref[...])
    tm, d = xn.shape
    xp_ref[...] = xn

    w = wr_ref[...]
    w_hi = w.astype(bf16)
    w_lo = (w - w_hi.astype(f32)).astype(bf16)
    x_hi = xn.astype(bf16)
    x_lo = (xn - x_hi.astype(f32)).astype(bf16)
    nt = (((1,), (1,)), ((), ()))
    both = lax.dot_general(jnp.concatenate([w_hi, w_lo], axis=0), x_hi, nt, preferred_element_type=f32)
    logits = (both[:n_exp] + both[n_exp:] + lax.dot_general(w_hi, x_lo, nt, preferred_element_type=f32)
              + br_ref[...])
    sub = lax.broadcasted_iota(i32, (n_exp, tm), 0).astype(f32)
    chosen, vals, hots = [], [], []
    work = logits
    for _ in range(TOP_K):
        m = jnp.max(work, axis=0, keepdims=True)
        idx = jnp.min(jnp.where(work == m, sub, float(n_exp)), axis=0, keepdims=True)
        hot = sub == idx
        chosen.append(idx)
        vals.append(m)
        hots.append(hot)
        work = jnp.where(hot, -jnp.inf, work)
    ex = [jnp.exp(v - vals[0]) for v in vals]
    denom = ex[0] + ex[1] + ex[2] + ex[3]
    e_ref[...] = jnp.concatenate(chosen, axis=0).astype(i32)
    gate_ref[...] = jnp.concatenate([x / denom for x in ex], axis=0)

    member = jnp.zeros((n_exp, tm), f32)
    for hot in hots:
        member = member + hot.astype(f32)
    earlier = (lax.broadcasted_iota(i32, (tm, tm), 0) < lax.broadcasted_iota(i32, (tm, tm), 1)).astype(bf16)
    before = jnp.dot(member.astype(bf16), earlier, preferred_element_type=f32) + carry_ref[...]
    rank_ref[...] = jnp.concatenate(
        [jnp.sum(jnp.where(hot, before, 0.0), axis=0, keepdims=True) for hot in hots], axis=0).astype(i32)
    carry_ref[...] += jnp.sum(member, axis=1, keepdims=True)
    cnt_ref[0] = carry_ref[...].astype(i32)


def router(h, g, w_router, b_router, tc):
    n, d = h.shape
    n_exp = w_router.shape[1]
    tm = ROW_TILE
    while tc % tm:
        tm -= LANES
    tiles_per_chunk = tc // tm
    kern = functools.partial(_router_kernel, tiles_per_chunk=tiles_per_chunk, n_exp=n_exp)
    return pl.pallas_call(
        kern,
        out_shape=(
            jax.ShapeDtypeStruct((n, d), f32),
            jax.ShapeDtypeStruct((TOP_K, n), i32),
            jax.ShapeDtypeStruct((TOP_K, n), f32),
            jax.ShapeDtypeStruct((TOP_K, n), i32),
            jax.ShapeDtypeStruct((n // tc, n_exp, 1), i32),
        ),
        grid=(n // tm,),
        in_specs=[
            pl.BlockSpec((tm, d), lambda i: (i, 0)),
            pl.BlockSpec((1, d), lambda i: (0, 0)),
            pl.BlockSpec((n_exp, d), lambda i: (0, 0)),
            pl.BlockSpec((n_exp, 1), lambda i: (0, 0)),
        ],
        out_specs=(
            pl.BlockSpec((tm, d), lambda i: (i, 0)),
            pl.BlockSpec((TOP_K, tm), lambda i: (0, i)),
            pl.BlockSpec((TOP_K, tm), lambda i: (0, i)),
            pl.BlockSpec((TOP_K, tm), lambda i: (0, i)),
            pl.BlockSpec((1, n_exp, 1), lambda i: (i // tiles_per_chunk, 0, 0)),
        ),
        scratch_shapes=[pltpu.VMEM((n_exp, 1), f32)],
        compiler_params=_cparams(("arbitrary",), 32),
        name="router",
    )(h, g.reshape(1, d), w_router.T, b_router.reshape(n_exp, 1))


def _split_pairs_kernel(w_ref, p_ref, o_ref):
    width = p_ref.shape[0]
    for b in range(w_ref.shape[2] // width):
        cols = slice(b * width, (b + 1) * width)
        o_ref[0, :, cols] = jnp.dot(w_ref[0, :, cols].astype(bf16), p_ref[...],
                                    preferred_element_type=f32).astype(bf16)


def split_pairs(w):
    n_exp, d, f2 = w.shape
    width = 2 * LANES
    j = jnp.arange(width)
    src = jnp.where(j < LANES, 2 * j, 2 * (j - LANES) + 1)
    perm = (jnp.arange(width)[:, None] == src[None, :]).astype(bf16)
    return pl.pallas_call(
        _split_pairs_kernel,
        out_shape=jax.ShapeDtypeStruct((n_exp, d, f2), bf16),
        grid=(n_exp,),
        in_specs=[
            pl.BlockSpec((1, d, f2), lambda e: (e, 0, 0)),
            pl.BlockSpec((width, width), lambda e: (0, 0)),
        ],
        out_specs=pl.BlockSpec((1, d, f2), lambda e: (e, 0, 0)),
        compiler_params=_cparams(("arbitrary",), 40),
        name="split_pairs",
    )(w, perm)


def _dense_row_index(r, tiles):
    return (r // SUBLANES) * tiles * SUBLANES + r % SUBLANES


def _experts_kernel(start_ref, nsub_ref, xp_hbm, dest_hbm, gates_hbm, fill_hbm, wup_ref, bup_ref, wdn_ref, bdn_ref,
                    f_hbm, xs_ref, acc_ref, xt_a, xt_b, y_a, y_b, dest_s, gate_s, rmap_s, sem_ref,
                    *, tc, n_exp, d, null_row0, seg, half_len, place_per_step):
    c = pl.program_id(0)
    e = pl.program_id(1)
    ms = MOE_SUB
    tiles = d // LANES
    cur = pl.multiple_of((c % 2) * half_len, LANES)
    nxt = pl.multiple_of(half_len - cur, LANES)

    def gather(row0, xt_ref):
        for r in range(ms):
            t = rmap_s[cur + row0 + r] & (seg - 1)
            xt_ref[pl.ds(_dense_row_index(r, tiles), tiles, stride=SUBLANES), :] = xs_ref[t]

    def place(base, first, count):
        for k in range(TOP_K):
            for u in range(count):
                a = k * seg + first + u
                rmap_s[base + dest_s[a]] = a

    def scatter(row0, y_ref):
        for r0 in range(0, ms, SCATTER_UNROLL):
            toks, sums = [], []
            for r in range(r0, r0 + SCATTER_UNROLL):
                a = rmap_s[cur + row0 + r]
                t = a & (seg - 1)
                yrow = y_ref[pl.ds(_dense_row_index(r, tiles), tiles, stride=SUBLANES), :]
                toks.append(t)
                sums.append(acc_ref[t] + gate_s[a] * yrow)
            for t, s in zip(toks, sums):
                acc_ref[t] = s

    @pl.when(e == 0)
    def _():
        copies = [
            pltpu.make_async_copy(xp_hbm.at[pl.ds(c * tc, tc)], xs_ref.at[pl.ds(0, tc)], sem_ref.at[0]),
            pltpu.make_async_copy(gates_hbm.at[c], gate_s, sem_ref.at[1]),
        ]
        for cp in copies:
            cp.start()
        xs_ref[pl.ds(tc, SUBLANES)] = jnp.zeros((SUBLANES, tiles, LANES), f32)
        acc_ref[...] = jnp.zeros_like(acc_ref)

        def load_map_inputs(chunk, base):
            loads = [
                pltpu.make_async_copy(dest_hbm.at[chunk], dest_s, sem_ref.at[2]),
                pltpu.make_async_copy(fill_hbm, rmap_s.at[pl.ds(base, half_len)], sem_ref.at[3]),
            ]
            for cp in loads:
                cp.start()
            for cp in loads:
                cp.wait()

        @pl.when(c == 0)
        def _():
            y_a[...] = jnp.zeros_like(y_a)
            y_b[...] = jnp.zeros_like(y_b)
            load_map_inputs(0, 0)

            def place_all(i, carry):
                place(0, i * SUBLANES, SUBLANES)
                return carry

            lax.fori_loop(0, tc // SUBLANES, place_all, 0)

        load_map_inputs(jnp.minimum(c + 1, pl.num_programs(0) - 1), nxt)
        for cp in copies:
            cp.wait()
        gather(0, xt_a)

    g = c * n_exp + e
    j0 = start_ref[g]

    def step(j, xt_cur, xt_nxt, y_cur, y_prv):
        gather((j + 1) * ms, xt_nxt)
        scatter(jnp.where(j == 0, null_row0, (j - 1) * ms), y_prv)
        place(nxt, j * place_per_step, place_per_step)
        x = jnp.concatenate(
            [jnp.concatenate([xt_cur[pl.ds((rg * tiles + jt) * SUBLANES, SUBLANES), :] for jt in range(tiles)], axis=1)
             for rg in range(ms // SUBLANES)], axis=0).astype(bf16)
        hmid = jnp.dot(x, wup_ref[0], preferred_element_type=f32) + bup_ref[0]
        glu = jnp.concatenate([hmid[:, 2 * jt * LANES:(2 * jt + 1) * LANES] for jt in range(tiles)], axis=1)
        lin = jnp.concatenate([hmid[:, (2 * jt + 1) * LANES:(2 * jt + 2) * LANES] for jt in range(tiles)], axis=1)
        glu = jnp.minimum(glu, SWIGLU_LIMIT)
        lin = jnp.clip(lin, -SWIGLU_LIMIT, SWIGLU_LIMIT)
        act = glu * jax.nn.sigmoid(SWIGLU_ALPHA * glu) * (lin + 1.0)
        y = jnp.dot(act.astype(bf16), wdn_ref[0], preferred_element_type=f32) + bdn_ref[0]
        for rg in range(ms // SUBLANES):
            for jt in range(tiles):
                y_cur[pl.ds((rg * tiles + jt) * SUBLANES, SUBLANES), :] = (
                    y[rg * SUBLANES:(rg + 1) * SUBLANES, jt * LANES:(jt + 1) * LANES])

    def sub_block(i, carry):
        j = j0 + i

        @pl.when(j % 2 == 0)
        def _():
            step(j, xt_a, xt_b, y_a, y_b)

        @pl.when(j % 2 == 1)
        def _():
            step(j, xt_b, xt_a, y_b, y_a)

        return carry

    lax.fori_loop(0, nsub_ref[g], sub_block, 0)

    @pl.when(e == n_exp - 1)
    def _():
        last = j0 + nsub_ref[g] - 1

        @pl.when(last % 2 == 0)
        def _():
            scatter(last * ms, y_a)

        @pl.when(last % 2 == 1)
        def _():
            scatter(last * ms, y_b)

        cp = pltpu.make_async_copy(acc_ref.at[pl.ds(0, tc)], f_hbm.at[pl.ds(c * tc, tc)], sem_ref.at[0])
        cp.start()
        cp.wait()


def _row_map_geometry(tc, n_exp, seg):
    assert seg > tc and seg & (seg - 1) == 0
    max_sub = (TOP_K * tc + n_exp * (MOE_SUB - 1)) // MOE_SUB
    min_sub = -(-TOP_K * tc // MOE_SUB)
    null_row0 = (max_sub + 1) * MOE_SUB
    half_len = -(-(null_row0 + MOE_SUB + 1) // LANES) * LANES
    place_per_step = -(-tc // min_sub)
    assert max_sub * place_per_step <= seg
    return null_row0, half_len, place_per_step


def experts(sb_start, n_sub, xp, dest, gates, w_up, b_up, w_down, b_down, tc):
    n, tiles, _ = xp.shape
    d = tiles * LANES
    n_exp = w_up.shape[0]
    chunks, padded_len = dest.shape
    seg = padded_len // TOP_K
    null_row0, half_len, place_per_step = _row_map_geometry(tc, n_exp, seg)
    fill = jnp.full((half_len,), tc, i32)
    kern = functools.partial(_experts_kernel, tc=tc, n_exp=n_exp, d=d, null_row0=null_row0, seg=seg,
                             half_len=half_len, place_per_step=place_per_step)
    block = pltpu.VMEM((MOE_SUB * tiles, LANES), f32)
    return pl.pallas_call(
        kern,
        out_shape=jax.ShapeDtypeStruct((n, tiles, LANES), f32),
        grid_spec=pltpu.PrefetchScalarGridSpec(
            num_scalar_prefetch=2,
            grid=(chunks, n_exp),
            in_specs=[
                pl.BlockSpec(memory_space=pl.ANY),
                pl.BlockSpec(memory_space=pl.ANY),
                pl.BlockSpec(memory_space=pl.ANY),
                pl.BlockSpec(memory_space=pl.ANY),
                pl.BlockSpec((1, d, 2 * d), lambda c, e, *_: (e, 0, 0)),
                pl.BlockSpec((1, 1, 2 * d), lambda c, e, *_: (e, 0, 0)),
                pl.BlockSpec((1, d, d), lambda c, e, *_: (e, 0, 0)),
                pl.BlockSpec((1, 1, d), lambda c, e, *_: (e, 0, 0)),
            ],
            out_specs=pl.BlockSpec(memory_space=pl.ANY),
            scratch_shapes=[
                pltpu.VMEM((tc + SUBLANES, tiles, LANES), f32),
                pltpu.VMEM((tc + SUBLANES, tiles, LANES), f32),
                block, block, block, block,
                pltpu.SMEM((padded_len,), i32),
                pltpu.SMEM((padded_len,), f32),
                pltpu.SMEM((2 * half_len,), i32),
                pltpu.SemaphoreType.DMA((4,)),
            ],
        ),
        compiler_params=_cparams(("arbitrary", "arbitrary"), 58),
        name="experts",
    )(sb_start, n_sub, xp, dest, gates, fill, w_up, b_up.reshape(n_exp, 1, 2 * d), w_down,
      b_down.reshape(n_exp, 1, d))


def _final_kernel(h_ref, f_ref, g_ref, o_ref):
    tm, d = h_ref.shape
    tiles = d // LANES
    parts = [h_ref[:, jt * LANES:(jt + 1) * LANES] + f_ref[pl.ds(jt, tm, stride=tiles), :] for jt in range(tiles)]
    o_ref[...] = _rms(jnp.concatenate(parts, axis=1), g_ref[...])


def final_norm(h, f2, g, row0, rows):
    d = h.shape[1]
    tiles = d // LANES
    tm = min(ROW_TILE, rows)
    blk0 = row0 // tm
    assert rows % tm == 0 and row0 % tm == 0
    return pl.pallas_call(
        _final_kernel,
        out_shape=jax.ShapeDtypeStruct((rows, d), f32),
        grid=(rows // tm,),
        in_specs=[
            pl.BlockSpec((tm, d), lambda i: (blk0 + i, 0)),
            pl.BlockSpec((tm * tiles, LANES), lambda i: (blk0 + i, 0)),
            pl.BlockSpec((1, d), lambda i: (0, 0)),
        ],
        out_specs=pl.BlockSpec((tm, d), lambda i: (i, 0)),
        compiler_params=_cparams(("arbitrary",), 32),
        name="final_norm",
    )(h, f2, g.reshape(1, d))


def moe(h, g, w_router, b_router, w_up, b_up, w_down, b_down):
    n, d = h.shape
    n_exp = w_router.shape[1]
    tc = n // MOE_CHUNKS
    xp, top_e, gates, rank, counts = router(h, g, w_router, b_router, tc)
    n_sub = (counts.reshape(MOE_CHUNKS, n_exp) + MOE_SUB - 1) // MOE_SUB
    sb_start = jnp.cumsum(n_sub, axis=1) - n_sub
    hot = top_e.reshape(TOP_K, MOE_CHUNKS, tc, 1) == jnp.arange(n_exp, dtype=i32)
    row_base = (sb_start * MOE_SUB).reshape(1, MOE_CHUNKS, 1, n_exp)
    dest = jnp.sum(jnp.where(hot, row_base, 0), axis=-1) + rank.reshape(TOP_K, MOE_CHUNKS, tc)
    seg = 1 << tc.bit_length()
    spare_row = _row_map_geometry(tc, n_exp, seg)[1] - 1
    per_chunk = lambda a, tail: jnp.pad(a.transpose(1, 0, 2), ((0, 0), (0, 0), (0, seg - tc)),
                                        constant_values=tail).reshape(MOE_CHUNKS, TOP_K * seg)
    tiles = d // LANES
    b_up_p = b_up.reshape(n_exp, tiles, LANES, 2).transpose(0, 1, 3, 2).reshape(n_exp, 2 * d)
    f = experts(sb_start.reshape(-1).astype(i32), n_sub.reshape(-1).astype(i32), xp.reshape(n, tiles, LANES),
                per_chunk(dest, spare_row), per_chunk(gates.reshape(TOP_K, MOE_CHUNKS, tc), 0.0), split_pairs(w_up),
                b_up_p,
                w_down.astype(bf16), b_down, tc)
    return f.reshape(n * tiles, LANES)


def kernel(x_prompt, x_sample, mem_prompt, state_ret, state_conv, cache_mem_k, cache_mem_v, norm_mix, w_in, conv_w,
           ret_gn, w_out, norm_cross, norm_mem, w_mq, w_mk, w_mv, w_mo, norm_ffn, w_router, b_router, w_up, b_up,
           w_down, b_down, norm_final):
    batch, seq, d = x_prompt.shape
    n_seq, ts, _ = x_sample.shape
    depth = w_in.shape[0]
    n_mem = mem_prompt.shape[1]
    r = ret_gn.shape[1]
    cd_width = conv_w.shape[2]
    dh = r // N_RET_HEADS
    n_p = batch * seq
    n_s = n_seq * ts
    n = n_p + n_s
    nblk = n_seq // SAMPLE_BLOCK
    assert seq % ROW_TILE == 0 and n % ROW_TILE == 0 and n % MOE_CHUNKS == 0

    xs = x_sample.reshape(nblk, SAMPLE_BLOCK, ts, d).transpose(0, 2, 1, 3).reshape(n_s, d)
    h_p, h_s = x_prompt.reshape(n_p, d), xs
    tab_p = prompt_tables(seq, dh)
    tab_s = sample_tables(ts, float(PAST_LEN), dh)

    ret_p, conv_p, mk_p, mv_p, ret_s, conv_s = [], [], [], [], [], []
    for l in range(depth):
        w_in_b, w_out_b, w_mq_b = w_in[l].astype(bf16), w_out[l].astype(bf16), w_mq[l].astype(bf16)
        h_p, s_p, c_p = prompt_mixer(h_p, norm_mix[l], w_in_b, tab_p, conv_w[l], ret_gn[l], w_out_b, batch, seq,
                                     r, cd_width)
        proj_s = norm_matmul(h_s, norm_mix[l], w_in_b, "in_proj_sample")
        mix_s, s_s, c_s = sample_mixer(proj_s, tab_s, conv_w[l], ret_gn[l], state_ret[l],
                                       state_conv[l].transpose(1, 0, 2), 0, ts, r, cd_width)
        h_s = matmul_res(mix_s, w_out_b, h_s, "out_proj_sample")

        mk, mv = mem_kv(mem_prompt, norm_mem[l], jnp.concatenate([w_mk[l], w_mv[l]], axis=1).astype(bf16),
                        N_MEM_HEADS)
        q_s = norm_matmul(h_s, norm_cross[l], w_mq_b, "q_sample")
        attn_s = cross_sample(q_s, cache_mem_k[l], cache_mem_v[l], 0, ts)
        h = attn_block(h_p, h_s, norm_cross[l], w_mq_b, mk, mv, attn_s, w_mo[l].astype(bf16), seq)

        f2 = moe(h, norm_ffn[l], w_router[l], b_router[l], w_up[l], b_up[l], w_down[l], b_down[l])
        if l + 1 < depth:
            h = h + f2.reshape(n, d // LANES, LANES).reshape(n, d)
            h_p, h_s = h, h[n_p:]

        ret_p.append(s_p)
        conv_p.append(c_p)
        mk_p.append(mk)
        mv_p.append(mv)
        ret_s.append(s_s)
        conv_s.append(c_s.transpose(1, 0, 2))

    y_p = final_norm(h, f2, norm_final, 0, n_p).reshape(batch, seq, d)
    y_s = final_norm(h, f2, norm_final, n_p, n_s)
    y_s = y_s.reshape(nblk, ts, SAMPLE_BLOCK, d).transpose(0, 2, 1, 3).reshape(n_seq, ts, d)
    return (y_p, y_s, jnp.stack(ret_p), jnp.stack(conv_p), jnp.stack(mk_p), jnp.stack(mv_p), jnp.stack(ret_s),
            jnp.stack(conv_s))
```

```python
import functools

import jax
import jax.numpy as jnp
from jax import lax
from jax.experimental import pallas as pl
from jax.experimental.pallas import tpu as pltpu

f32 = jnp.float32
bf16 = jnp.bfloat16
i32 = jnp.int32

EPS = 1e-6
ROPE_THETA = 10000.0
RET_CHUNK = 128
PAST_LEN = 16384
N_RET_HEADS = 4
N_MEM_HEADS = 4
TOP_K = 4
SWIGLU_ALPHA = 1.702
SWIGLU_LIMIT = 7.0

LANES = 128
SUBLANES = 8
MIB = 1024 * 1024

ROW_TILE = 512
SAMPLE_BLOCK = 8
MOE_CHUNKS = 4
MOE_SUB = 256
SCATTER_UNROLL = 8


def _cparams(sem, vmem_mib):
    return pltpu.CompilerParams(dimension_semantics=sem, vmem_limit_bytes=vmem_mib * MIB)


def _rms(x, g):
    ms = jnp.mean(x * x, axis=-1, keepdims=True)
    return (x * lax.rsqrt(ms + EPS)) * g


def _norm_matmul_kernel(x_ref, g_ref, w_ref, o_ref):
    xn = _rms(x_ref[...], g_ref[...])
    o_ref[...] = jnp.dot(xn.astype(bf16), w_ref[...], preferred_element_type=f32)


def norm_matmul(x, g, w, name):
    m, d = x.shape
    f = w.shape[1]
    tm = min(ROW_TILE, m)
    assert m % tm == 0
    return pl.pallas_call(
        _norm_matmul_kernel,
        out_shape=jax.ShapeDtypeStruct((m, f), f32),
        grid=(m // tm,),
        in_specs=[
            pl.BlockSpec((tm, d), lambda i: (i, 0)),
            pl.BlockSpec((1, d), lambda i: (0, 0)),
            pl.BlockSpec((d, f), lambda i: (0, 0)),
        ],
        out_specs=pl.BlockSpec((tm, f), lambda i: (i, 0)),
        compiler_params=_cparams(("arbitrary",), 48),
        name=name,
    )(x, g.reshape(1, d), w)


def _stacked_specs(m_p, m_s, width):
    tm = ROW_TILE
    assert m_p % tm == 0 and m_s == tm
    prompt_tiles = m_p // tm
    return prompt_tiles, [
        pl.BlockSpec((tm, width), lambda i: (jnp.minimum(i, prompt_tiles - 1), 0)),
        pl.BlockSpec((tm, width), lambda i: (0, 0)),
    ]


def _matmul_res_kernel(a_ref, w_ref, r_ref, o_ref):
    o_ref[...] = r_ref[...] + jnp.dot(a_ref[...], w_ref[...], preferred_element_type=f32)


def matmul_res(a, w, res, name):
    m, d = a.shape
    f = w.shape[1]
    tm = min(ROW_TILE, m)
    assert m % tm == 0
    return pl.pallas_call(
        _matmul_res_kernel,
        out_shape=jax.ShapeDtypeStruct((m, f), f32),
        grid=(m // tm,),
        in_specs=[
            pl.BlockSpec((tm, d), lambda i: (i, 0)),
            pl.BlockSpec((d, f), lambda i: (0, 0)),
            pl.BlockSpec((tm, f), lambda i: (i, 0)),
        ],
        out_specs=pl.BlockSpec((tm, f), lambda i: (i, 0)),
        compiler_params=_cparams(("arbitrary",), 32),
        name=name,
    )(a, w, res)


def _rope(x, cos, sin_signed, half):
    return x * cos + pltpu.roll(x, half, 1) * sin_signed


def _group_norm_gate(o, gate, gn):
    mu = jnp.mean(o, axis=-1, keepdims=True)
    var = jnp.mean(jnp.square(o - mu), axis=-1, keepdims=True)
    on = ((o - mu) * lax.rsqrt(var + EPS)) * gn
    return (gate * jax.nn.sigmoid(gate)) * on


def _prompt_mixer_kernel(cd_ref, x_ref, g_ref, win_ref, cos_ref, sin_ref, decay_ref, xi_ref, zeta_ref, convw_ref,
                         gn_ref, wout_ref, h_ref, sfin_ref, cfin_ref, s_scr, u_scr, proj_ref, mix_ref,
                         *, tt, r, cd_width, dh):
    j = pl.program_id(1)
    heads = r // dh
    pad = SUBLANES

    @pl.when(j == 0)
    def _():
        s_scr[...] = jnp.zeros_like(s_scr)
        u_scr[0:pad, :] = jnp.zeros((pad, cd_width), f32)

    proj_ref[...] = jnp.dot(_rms(x_ref[...], g_ref[...]).astype(bf16), win_ref[...], preferred_element_type=f32)

    k_scale = dh ** -0.5
    for c in range(tt // RET_CHUNK):
        rows = pl.ds(c * RET_CHUNK, RET_CHUNK)
        cosb = cos_ref[rows, :]
        sinb = sin_ref[rows, :]
        for h in range(heads):
            lo, hi = h * dh, (h + 1) * dh
            qh = _rope(proj_ref[rows, lo:hi], cosb, sinb, dh // 2)
            kh = _rope(proj_ref[rows, r + lo:r + hi], cosb, sinb, dh // 2) * k_scale
            vb = proj_ref[rows, 2 * r + lo:2 * r + hi].astype(bf16)
            gate = proj_ref[rows, 3 * r + lo:3 * r + hi]
            qb = qh.astype(bf16)
            kb = kh.astype(bf16)
            scores = lax.dot_general(qb, kb, (((1,), (1,)), ((), ())), preferred_element_type=f32) * decay_ref[h]
            inner = jnp.dot(scores.astype(bf16), vb, preferred_element_type=f32)
            s_prev = s_scr[h]
            cross = jnp.dot((qh * xi_ref[h]).astype(bf16), s_prev.astype(bf16), preferred_element_type=f32)
            kv = lax.dot_general((kh * zeta_ref[h]).astype(bf16), vb, (((0,), (0,)), ((), ())),
                                 preferred_element_type=f32)
            s_scr[h] = cd_ref[h] * s_prev + kv
            mix_ref[rows, lo:hi] = _group_norm_gate(inner + cross, gate, gn_ref[:, lo:hi]).astype(bf16)

    base = 4 * r
    u = proj_ref[:, base + cd_width:base + 2 * cd_width] * proj_ref[:, base:base + cd_width]
    u_scr[pad:pad + tt, :] = u
    conv = (u_scr[pad - 2:pad - 2 + tt, :] * convw_ref[0:1, :] + u_scr[pad - 1:pad - 1 + tt, :] * convw_ref[1:2, :]
            + u * convw_ref[2:3, :])
    mix_ref[:, r:r + cd_width] = (proj_ref[:, base + 2 * cd_width:base + 3 * cd_width] * conv).astype(bf16)
    u_scr[0:pad, :] = u_scr[tt:tt + pad, :]
    cfin_ref[0] = u_scr[pad - 2:pad, :]
    h_ref[...] = x_ref[...] + jnp.dot(mix_ref[...], wout_ref[...], preferred_element_type=f32)

    @pl.when(j == pl.num_programs(1) - 1)
    def _():
        sfin_ref[0] = s_scr[...]


def prompt_mixer(x, g, w_in, tables, conv_w, ret_gn, w_out, batch, seq, r, cd_width):
    tt = ROW_TILE
    d = x.shape[1]
    dh = r // N_RET_HEADS
    steps = seq // tt
    in_cols = w_in.shape[1]
    kern = functools.partial(_prompt_mixer_kernel, tt=tt, r=r, cd_width=cd_width, dh=dh)
    tab = lambda shape: pl.BlockSpec(shape, lambda b, j, cd: (0,) * len(shape))
    row_tile = pl.BlockSpec((tt, d), lambda b, j, cd: (b * steps + j, 0))
    return pl.pallas_call(
        kern,
        out_shape=(
            jax.ShapeDtypeStruct((batch * seq, d), f32),
            jax.ShapeDtypeStruct((batch, N_RET_HEADS, dh, dh), f32),
            jax.ShapeDtypeStruct((batch, 2, cd_width), f32),
        ),
        grid_spec=pltpu.PrefetchScalarGridSpec(
            num_scalar_prefetch=1,
            grid=(batch, steps),
            in_specs=[
                row_tile,
                tab((1, d)),
                tab((d, in_cols)),
                pl.BlockSpec((tt, dh), lambda b, j, cd: (j, 0)),
                pl.BlockSpec((tt, dh), lambda b, j, cd: (j, 0)),
                tab((N_RET_HEADS, RET_CHUNK, RET_CHUNK)),
                tab((N_RET_HEADS, RET_CHUNK, dh)),
                tab((N_RET_HEADS, RET_CHUNK, dh)),
                tab((3, cd_width)),
                tab((1, r)),
                tab((r + cd_width, d)),
            ],
            out_specs=(
                row_tile,
                pl.BlockSpec((1, N_RET_HEADS, dh, dh), lambda b, j, cd: (b, 0, 0, 0)),
                pl.BlockSpec((1, 2, cd_width), lambda b, j, cd: (b, 0, 0)),
            ),
            scratch_shapes=[
                pltpu.VMEM((N_RET_HEADS, dh, dh), f32),
                pltpu.VMEM((tt + 2 * SUBLANES, cd_width), f32),
                pltpu.VMEM((tt, in_cols), f32),
                pltpu.VMEM((tt, r + cd_width), bf16),
            ],
        ),
        compiler_params=_cparams(("arbitrary", "arbitrary"), 56),
        name="prompt_mixer",
    )(tables["chunk_decay"], x, g.reshape(1, d), w_in, tables["cos"], tables["sin"], tables["decay"], tables["xi"],
      tables["zeta"], conv_w, ret_gn.reshape(1, r), w_out)


def _sample_mixer_kernel(dec_ref, proj_ref, cos_ref, sin_ref, xi_ref, zeta_ref, convw_ref, gn_ref, s0_ref, c0_ref,
                         mix_ref, snew_ref, cnew_ref, *, ts, r, cd_width, dh):
    heads = r // dh
    nb = SAMPLE_BLOCK
    rows = ts * nb
    k_scale = dh ** -0.5
    cosb = cos_ref[...]
    sinb = sin_ref[...]
    seq_of_row = lax.broadcasted_iota(i32, (rows, dh), 0) % nb
    for h in range(heads):
        lo, hi = h * dh, (h + 1) * dh
        qh = _rope(proj_ref[:, lo:hi], cosb, sinb, dh // 2)
        kh = _rope(proj_ref[:, r + lo:r + hi], cosb, sinb, dh // 2) * k_scale
        vh = proj_ref[:, 2 * r + lo:2 * r + hi]
        gate = proj_ref[:, 3 * r + lo:3 * r + hi]
        inner = []
        for t in range(ts):
            qt = qh[t * nb:(t + 1) * nb]
            acc = jnp.zeros((nb, dh), f32)
            for s in range(t + 1):
                sc = jnp.sum(qt * kh[s * nb:(s + 1) * nb], axis=-1, keepdims=True) * dec_ref[h * (ts + 1) + t - s]
                acc = acc + sc * vh[s * nb:(s + 1) * nb]
            inner.append(acc)
        inner = jnp.concatenate(inner, axis=0)
        qx = (qh * xi_ref[h]).astype(bf16)
        kz = kh * zeta_ref[h]
        vb = vh.astype(bf16)
        cross = jnp.zeros((rows, dh), f32)
        for b in range(nb):
            mine = seq_of_row == b
            s_prev = s0_ref[b, h]
            res = jnp.dot(qx, s_prev.astype(bf16), preferred_element_type=f32)
            cross = cross + jnp.where(mine, res, 0.0)
            kv = lax.dot_general(jnp.where(mine, kz, 0.0).astype(bf16), vb, (((0,), (0,)), ((), ())),
                                 preferred_element_type=f32)
            snew_ref[b, h] = dec_ref[h * (ts + 1) + ts] * s_prev + kv
        mix_ref[:, lo:hi] = _group_norm_gate(inner + cross, gate, gn_ref[:, lo:hi]).astype(bf16)

    base = 4 * r
    u = proj_ref[:, base + cd_width:base + 2 * cd_width] * proj_ref[:, base:base + cd_width]
    full = [c0_ref[0], c0_ref[1]] + [u[t * nb:(t + 1) * nb] for t in range(ts)]
    conv = jnp.concatenate(
        [full[t] * convw_ref[0:1, :] + full[t + 1] * convw_ref[1:2, :] + full[t + 2] * convw_ref[2:3, :]
         for t in range(ts)], axis=0)
    mix_ref[:, r:r + cd_width] = (proj_ref[:, base + 2 * cd_width:base + 3 * cd_width] * conv).astype(bf16)
    cnew_ref[0] = full[ts]
    cnew_ref[1] = full[ts + 1]


def sample_mixer(proj, tables, conv_w, ret_gn, state_ret, state_conv_t, row0, ts, r, cd_width):
    n_seq = state_ret.shape[0]
    dh = r // N_RET_HEADS
    rows = ts * SAMPLE_BLOCK
    blk0 = row0 // rows
    assert row0 % rows == 0 and n_seq % SAMPLE_BLOCK == 0
    in_cols = proj.shape[1]
    kern = functools.partial(_sample_mixer_kernel, ts=ts, r=r, cd_width=cd_width, dh=dh)
    tab = lambda shape: pl.BlockSpec(shape, lambda i, d: (0,) * len(shape))
    return pl.pallas_call(
        kern,
        out_shape=(
            jax.ShapeDtypeStruct((n_seq * ts, r + cd_width), bf16),
            jax.ShapeDtypeStruct(state_ret.shape, f32),
            jax.ShapeDtypeStruct(state_conv_t.shape, f32),
        ),
        grid_spec=pltpu.PrefetchScalarGridSpec(
            num_scalar_prefetch=1,
            grid=(n_seq // SAMPLE_BLOCK,),
            in_specs=[
                pl.BlockSpec((rows, in_cols), lambda i, d: (blk0 + i, 0)),
                tab((rows, dh)),
                tab((rows, dh)),
                tab((N_RET_HEADS, rows, dh)),
                tab((N_RET_HEADS, rows, dh)),
                tab((3, cd_width)),
                tab((1, r)),
                pl.BlockSpec((SAMPLE_BLOCK, N_RET_HEADS, dh, dh), lambda i, d: (i, 0, 0, 0)),
                pl.BlockSpec((2, SAMPLE_BLOCK, cd_width), lambda i, d: (0, i, 0)),
            ],
            out_specs=(
                pl.BlockSpec((rows, r + cd_width), lambda i, d: (i, 0)),
                pl.BlockSpec((SAMPLE_BLOCK, N_RET_HEADS, dh, dh), lambda i, d: (i, 0, 0, 0)),
                pl.BlockSpec((2, SAMPLE_BLOCK, cd_width), lambda i, d: (0, i, 0)),
            ),
        ),
        compiler_params=_cparams(("arbitrary",), 32),
        name="sample_mixer",
    )(tables["dec"], proj, tables["cos"], tables["sin"], tables["xi"], tables["zeta"], conv_w, ret_gn.reshape(1, r),
      state_ret, state_conv_t)


def _log_gamma():
    return jnp.log1p(-jnp.exp2(-5.0 - jnp.arange(N_RET_HEADS, dtype=f32)))


def _rope_tables(pos, dh):
    half = dh // 2
    inv = ROPE_THETA ** (-jnp.arange(half, dtype=f32) / half)
    ang = pos[:, None] * inv[None, :]
    cos, sin = jnp.cos(ang), jnp.sin(ang)
    return jnp.concatenate([cos, cos], axis=-1), jnp.concatenate([-sin, sin], axis=-1)


def prompt_tables(seq, dh):
    c = RET_CHUNK
    lg = _log_gamma()
    i = jnp.arange(c, dtype=f32)
    diff = i[:, None] - i[None, :]
    decay = jnp.where(diff >= 0, jnp.exp(jnp.maximum(diff, 0.0)[None] * lg[:, None, None]), 0.0)
    xi = jnp.exp((i + 1.0)[None, :] * lg[:, None])
    zeta = jnp.exp((c - 1.0 - i)[None, :] * lg[:, None])
    cos, sin = _rope_tables(jnp.arange(seq, dtype=f32), dh)
    bc = lambda t: jnp.broadcast_to(t[:, :, None], (N_RET_HEADS, c, dh))
    return dict(cos=cos, sin=sin, decay=decay, xi=bc(xi), zeta=bc(zeta), chunk_decay=jnp.exp(c * lg))


def sample_tables(ts, pos0, dh):
    lg = _log_gamma()
    i = jnp.arange(ts, dtype=f32)
    dec = jnp.exp(jnp.arange(ts + 1, dtype=f32)[None, :] * lg[:, None])
    xi = jnp.exp((i + 1.0)[None, :] * lg[:, None])
    zeta = jnp.exp((ts - 1.0 - i)[None, :] * lg[:, None])
    cos, sin = _rope_tables(pos0 + i, dh)
    rep = lambda t: jnp.repeat(t, SAMPLE_BLOCK, axis=0)
    bc = lambda t: jnp.broadcast_to(jnp.repeat(t, SAMPLE_BLOCK, axis=1)[:, :, None],
                                    (N_RET_HEADS, ts * SAMPLE_BLOCK, dh))
    return dict(cos=rep(cos), sin=rep(sin), xi=bc(xi), zeta=bc(zeta), dec=dec.reshape(-1))


def _softmax_rows(s):
    m = jnp.max(s, axis=-1, keepdims=True)
    p = jnp.exp(s - m)
    return p / jnp.sum(p, axis=-1, keepdims=True)


def _from_slabs(ref, rows=None):
    rows = slice(None) if rows is None else rows
    return jnp.concatenate([ref[lt, rows, :] for lt in range(ref.shape[0])], axis=1)


def _to_slabs(ref, value):
    for lt in range(ref.shape[0]):
        ref[lt] = value[:, lt * LANES:(lt + 1) * LANES]


def _mem_kv_kernel(m_ref, g_ref, w_ref, k_ref, v_ref, rows_ref, *, heads):
    n_mem, d = m_ref.shape
    dh = d // heads
    kv = jnp.dot(_rms(m_ref[...], g_ref[...]).astype(bf16), w_ref[...], preferred_element_type=f32)
    for part, o_ref in enumerate((k_ref, v_ref)):
        for h in range(heads):
            for lt in range(dh // LANES):
                lo = part * d + h * dh + lt * LANES
                rows_ref[lt, pl.ds(h, n_mem, stride=heads), :] = kv[:, lo:lo + LANES]
        o_ref[0] = _from_slabs(rows_ref).reshape(n_mem, heads, dh)


def mem_kv(mem, g, w_kv, heads):
    batch, n_mem, d = mem.shape
    dh = d // heads
    out = jax.ShapeDtypeStruct((batch, n_mem, heads, dh), f32)
    out_spec = pl.BlockSpec((1, n_mem, heads, dh), lambda b: (b, 0, 0, 0))
    return pl.pallas_call(
        functools.partial(_mem_kv_kernel, heads=heads),
        out_shape=(out, out),
        grid=(batch,),
        in_specs=[
            pl.BlockSpec((n_mem, d), lambda b: (b, 0)),
            pl.BlockSpec((1, d), lambda b: (0, 0)),
            pl.BlockSpec((d, 2 * d), lambda b: (0, 0)),
        ],
        out_specs=(out_spec, out_spec),
        scratch_shapes=[pltpu.VMEM((dh // LANES, n_mem * heads, LANES), f32)],
        compiler_params=_cparams(("arbitrary",), 32),
        name="mem_kv",
    )(mem.reshape(batch * n_mem, d), g.reshape(1, d), w_kv)


def _attn_block_kernel(hp_ref, hs_ref, g_ref, wq_ref, k_ref, v_ref, as_ref, wo_ref, o_ref, k_rows, v_rows,
                       *, prompt_tiles, heads):
    i = pl.program_id(0)

    @pl.when(i < prompt_tiles)
    def _():
        h_ref = hp_ref
        n_mem, _, dh = k_ref.shape[1:]
        scale = dh ** -0.5
        q = jnp.dot(_rms(h_ref[...], g_ref[...]).astype(bf16), wq_ref[...], preferred_element_type=f32)
        _to_slabs(k_rows, k_ref[0].reshape(n_mem * heads, dh))
        _to_slabs(v_rows, v_ref[0].reshape(n_mem * heads, dh))
        outs = []
        for h in range(heads):
            kh = _from_slabs(k_rows, pl.ds(h, n_mem, stride=heads)).astype(bf16)
            vh = _from_slabs(v_rows, pl.ds(h, n_mem, stride=heads)).astype(bf16)
            s = lax.dot_general(q[:, h * dh:(h + 1) * dh].astype(bf16), kh, (((1,), (1,)), ((), ())),
                                preferred_element_type=f32) * scale
            outs.append(jnp.dot(_softmax_rows(s).astype(bf16), vh, preferred_element_type=f32).astype(bf16))
        attn = jnp.concatenate(outs, axis=1)
        o_ref[...] = h_ref[...] + jnp.dot(attn, wo_ref[...], preferred_element_type=f32)

    @pl.when(i == prompt_tiles)
    def _():
        o_ref[...] = hs_ref[...] + jnp.dot(as_ref[...], wo_ref[...], preferred_element_type=f32)


def attn_block(h_prompt, h_sample, g, w_mq, mk, mv, attn_sample, w_mo, seq):
    d = h_prompt.shape[1]
    batch, n_mem, heads, dh = mk.shape
    tm = ROW_TILE
    per_batch = seq // tm
    prompt_tiles, h_specs = _stacked_specs(batch * seq, h_sample.shape[0], d)
    assert attn_sample.shape == (tm, d)
    kv_spec = pl.BlockSpec((1, n_mem, heads, dh),
                           lambda i: (jnp.minimum(i, prompt_tiles - 1) // per_batch, 0, 0, 0))
    const = lambda shape: pl.BlockSpec(shape, lambda i: (0,) * len(shape))
    return pl.pallas_call(
        functools.partial(_attn_block_kernel, prompt_tiles=prompt_tiles, heads=heads),
        out_shape=jax.ShapeDtypeStruct(((prompt_tiles + 1) * tm, d), f32),
        grid=(prompt_tiles + 1,),
        in_specs=h_specs + [
            const((1, d)),
            const((d, d)),
            kv_spec,
            kv_spec,
            const((tm, d)),
            const((d, d)),
        ],
        out_specs=pl.BlockSpec((tm, d), lambda i: (i, 0)),
        scratch_shapes=[pltpu.VMEM((dh // LANES, n_mem * heads, LANES), f32)] * 2,
        compiler_params=_cparams(("arbitrary",), 48),
        name="attn_block",
    )(h_prompt, h_sample, g.reshape(1, d), w_mq, mk, mv, attn_sample, w_mo)


def _cross_sample_kernel(q_ref, k_ref, v_ref, o_ref, acc_ref, *, dh, seqs):
    part = pl.program_id(1)
    rows = q_ref.shape[0]
    scale = dh ** -0.5

    @pl.when(part == 0)
    def _():
        acc_ref[...] = jnp.zeros_like(acc_ref)

    heads = N_MEM_HEADS
    n_mem = k_ref.shape[1]
    qx = jnp.concatenate([q_ref[:, h * dh:(h + 1) * dh] for h in range(heads)], axis=0).astype(bf16)
    row = lax.broadcasted_iota(i32, (heads * rows, n_mem * heads), 0)
    col = lax.broadcasted_iota(i32, (heads * rows, n_mem * heads), 1)
    same_head = col % heads == row // rows
    seq_of_row = lax.broadcasted_iota(i32, (heads * rows, dh), 0) % SAMPLE_BLOCK
    out = jnp.zeros((heads * rows, dh), f32)
    for b in range(seqs):
        k2 = k_ref[b].reshape(n_mem * heads, dh).astype(bf16)
        v2 = v_ref[b].reshape(n_mem * heads, dh).astype(bf16)
        s = lax.dot_general(qx, k2, (((1,), (1,)), ((), ())), preferred_element_type=f32) * scale
        p = _softmax_rows(jnp.where(same_head, s, -jnp.inf))
        o = jnp.dot(p.astype(bf16), v2, preferred_element_type=f32)
        out = out + jnp.where(seq_of_row == part * seqs + b, o, 0.0)
    acc_ref[...] += out

    @pl.when(part == pl.num_programs(1) - 1)
    def _():
        for h in range(heads):
            o_ref[:, h * dh:(h + 1) * dh] = acc_ref[h * rows:(h + 1) * rows, :].astype(bf16)


def cross_sample(q, cache_k, cache_v, row0, ts):
    n_seq, n_mem, heads, dh = cache_k.shape
    d = heads * dh
    rows = ts * SAMPLE_BLOCK
    blk0 = row0 // rows
    parts = 4
    seqs = SAMPLE_BLOCK // parts
    kern = functools.partial(_cross_sample_kernel, dh=dh, seqs=seqs)
    return pl.pallas_call(
        kern,
        out_shape=jax.ShapeDtypeStruct((n_seq * ts, d), bf16),
        grid=(n_seq // SAMPLE_BLOCK, parts),
        in_specs=[
            pl.BlockSpec((rows, d), lambda i, p: (blk0 + i, 0)),
            pl.BlockSpec((seqs, n_mem, heads, dh), lambda i, p: (i * parts + p, 0, 0, 0)),
            pl.BlockSpec((seqs, n_mem, heads, dh), lambda i, p: (i * parts + p, 0, 0, 0)),
        ],
        out_specs=pl.BlockSpec((rows, d), lambda i, p: (i, 0)),
        scratch_shapes=[pltpu.VMEM((heads * rows, dh), f32)],
        compiler_params=_cparams(("arbitrary", "arbitrary"), 40),
        name="cross_sample",
    )(q, cache_k, cache_v)


def _router_kernel(h_ref, g_ref, wr_ref, br_ref, xp_ref, e_ref, gate_ref, rank_ref, cnt_ref, carry_ref,
                   *, tiles_per_chunk, n_exp):
    i = pl.program_id(0)

    @pl.when(i % tiles_per_chunk == 0)
    def _():
        carry_ref[...] = jnp.zeros_like(carry_ref)

    xn = _rms(h_ref[...], g_ref[...])
    tm, d = xn.shape
    xp_ref[...] = xn

    w = wr_ref[...]
    w_hi = w.astype(bf16)
    w_lo = (w - w_hi.astype(f32)).astype(bf16)
    x_hi = xn.astype(bf16)
    x_lo = (xn - x_hi.astype(f32)).astype(bf16)
    nt = (((1,), (1,)), ((), ()))
    both = lax.dot_general(jnp.concatenate([w_hi, w_lo], axis=0), x_hi, nt, preferred_element_type=f32)
    logits = (both[:n_exp] + both[n_exp:] + lax.dot_general(w_hi, x_lo, nt, preferred_element_type=f32)
              + br_ref[...])
    sub = lax.broadcasted_iota(i32, (n_exp, tm), 0).astype(f32)
    chosen, vals, hots = [], [], []
    work = logits
    for _ in range(TOP_K):
        m = jnp.max(work, axis=0, keepdims=True)
        idx = jnp.min(jnp.where(work == m, sub, float(n_exp)), axis=0, keepdims=True)
        hot = sub == idx
        chosen.append(idx)
        vals.append(m)
        hots.append(hot)
        work = jnp.where(hot, -jnp.inf, work)
    ex = [jnp.exp(v - vals[0]) for v in vals]
    denom = ex[0] + ex[1] + ex[2] + ex[3]
    e_ref[...] = jnp.concatenate(chosen, axis=0).astype(i32)
    gate_ref[...] = jnp.concatenate([x / denom for x in ex], axis=0)

    member = jnp.zeros((n_exp, tm), f32)
    for hot in hots:
        member = member + hot.astype(f32)
    earlier = (lax.broadcasted_iota(i32, (tm, tm), 0) < lax.broadcasted_iota(i32, (tm, tm), 1)).astype(bf16)
    before = jnp.dot(member.astype(bf16), earlier, preferred_element_type=f32) + carry_ref[...]
    rank_ref[...] = jnp.concatenate(
        [jnp.sum(jnp.where(hot, before, 0.0), axis=0, keepdims=True) for hot in hots], axis=0).astype(i32)
    carry_ref[...] += jnp.sum(member, axis=1, keepdims=True)
    cnt_ref[0] = carry_ref[...].astype(i32)


def router(h, g, w_router, b_router, tc):
    n, d = h.shape
    n_exp = w_router.shape[1]
    tm = ROW_TILE
    while tc % tm:
        tm -= LANES
    tiles_per_chunk = tc // tm
    kern = functools.partial(_router_kernel, tiles_per_chunk=tiles_per_chunk, n_exp=n_exp)
    return pl.pallas_call(
        kern,
        out_shape=(
            jax.ShapeDtypeStruct((n, d), f32),
            jax.ShapeDtypeStruct((TOP_K, n), i32),
            jax.ShapeDtypeStruct((TOP_K, n), f32),
            jax.ShapeDtypeStruct((TOP_K, n), i32),
            jax.ShapeDtypeStruct((n // tc, n_exp, 1), i32),
        ),
        grid=(n // tm,),
        in_specs=[
            pl.BlockSpec((tm, d), lambda i: (i, 0)),
            pl.BlockSpec((1, d), lambda i: (0, 0)),
            pl.BlockSpec((n_exp, d), lambda i: (0, 0)),
            pl.BlockSpec((n_exp, 1), lambda i: (0, 0)),
        ],
        out_specs=(
            pl.BlockSpec((tm, d), lambda i: (i, 0)),
            pl.BlockSpec((TOP_K, tm), lambda i: (0, i)),
            pl.BlockSpec((TOP_K, tm), lambda i: (0, i)),
            pl.BlockSpec((TOP_K, tm), lambda i: (0, i)),
            pl.BlockSpec((1, n_exp, 1), lambda i: (i // tiles_per_chunk, 0, 0)),
        ),
        scratch_shapes=[pltpu.VMEM((n_exp, 1), f32)],
        compiler_params=_cparams(("arbitrary",), 32),
        name="router",
    )(h, g.reshape(1, d), w_router.T, b_router.reshape(n_exp, 1))


def _split_pairs_kernel(w_ref, p_ref, o_ref):
    width = p_ref.shape[0]
    for b in range(w_ref.shape[2] // width):
        cols = slice(b * width, (b + 1) * width)
        o_ref[0, :, cols] = jnp.dot(w_ref[0, :, cols].astype(bf16), p_ref[...],
                                    preferred_element_type=f32).astype(bf16)


def split_pairs(w):
    n_exp, d, f2 = w.shape
    width = 2 * LANES
    j = jnp.arange(width)
    src = jnp.where(j < LANES, 2 * j, 2 * (j - LANES) + 1)
    perm = (jnp.arange(width)[:, None] == src[None, :]).astype(bf16)
    return pl.pallas_call(
        _split_pairs_kernel,
        out_shape=jax.ShapeDtypeStruct((n_exp, d, f2), bf16),
        grid=(n_exp,),
        in_specs=[
            pl.BlockSpec((1, d, f2), lambda e: (e, 0, 0)),
            pl.BlockSpec((width, width), lambda e: (0, 0)),
        ],
        out_specs=pl.BlockSpec((1, d, f2), lambda e: (e, 0, 0)),
        compiler_params=_cparams(("arbitrary",), 40),
        name="split_pairs",
    )(w, perm)


def _dense_row_index(r, tiles):
    return (r // SUBLANES) * tiles * SUBLANES + r % SUBLANES


def _experts_kernel(start_ref, nsub_ref, tail_ref, xp_hbm, dest_hbm, gates_hbm, fill_hbm, wup_ref, bup_ref, wdn_ref, bdn_ref,
                    f_hbm, xs_ref, acc_ref, xt_a, xt_b, y_a, y_b, dest_s, gate_s, rmap_s, sem_ref,
                    *, tc, n_exp, d, null_row0, seg, half_len, place_per_step):
    c = pl.program_id(0)
    e = pl.program_id(1)
    ms = MOE_SUB
    tiles = d // LANES
    cur = pl.multiple_of((c % 2) * half_len, LANES)
    nxt = pl.multiple_of(half_len - cur, LANES)

    def gather(row0, xt_ref):
        for r in range(ms):
            t = rmap_s[cur + row0 + r] & (seg - 1)
            xt_ref[pl.ds(_dense_row_index(r, tiles), tiles, stride=SUBLANES), :] = xs_ref[t]

    def place(base, first, count):
        for k in range(TOP_K):
            for u in range(count):
                a = k * seg + first + u
                rmap_s[base + dest_s[a]] = a

    def scatter(row0, y_ref):
        for r0 in range(0, ms, SCATTER_UNROLL):
            toks, sums = [], []
            for r in range(r0, r0 + SCATTER_UNROLL):
                a = rmap_s[cur + row0 + r]
                t = a & (seg - 1)
                yrow = y_ref[pl.ds(_dense_row_index(r, tiles), tiles, stride=SUBLANES), :]
                toks.append(t)
                sums.append(acc_ref[t] + gate_s[a] * yrow)
            for t, s in zip(toks, sums):
                acc_ref[t] = s

    @pl.when(e == 0)
    def _():
        copies = [
            pltpu.make_async_copy(xp_hbm.at[pl.ds(c * tc, tc)], xs_ref.at[pl.ds(0, tc)], sem_ref.at[0]),
            pltpu.make_async_copy(gates_hbm.at[c], gate_s, sem_ref.at[1]),
        ]
        for cp in copies:
            cp.start()
        xs_ref[pl.ds(tc, SUBLANES)] = jnp.zeros((SUBLANES, tiles, LANES), f32)
        acc_ref[...] = jnp.zeros_like(acc_ref)

        def load_map_inputs(chunk, base):
            loads = [
                pltpu.make_async_copy(dest_hbm.at[chunk], dest_s, sem_ref.at[2]),
                pltpu.make_async_copy(fill_hbm, rmap_s.at[pl.ds(base, half_len)], sem_ref.at[3]),
            ]
            for cp in loads:
                cp.start()
            for cp in loads:
                cp.wait()

        @pl.when(c == 0)
        def _():
            y_a[...] = jnp.zeros_like(y_a)
            y_b[...] = jnp.zeros_like(y_b)
            load_map_inputs(0, 0)

            def place_all(i, carry):
                place(0, i * SUBLANES, SUBLANES)
                return carry

            lax.fori_loop(0, tc // SUBLANES, place_all, 0)

        load_map_inputs(jnp.minimum(c + 1, pl.num_programs(0) - 1), nxt)
        for cp in copies:
            cp.wait()
        gather(0, xt_a)

    g = c * n_exp + e
    j0 = start_ref[g]

    def mlp(xt_cur, y_cur, first_group, groups):
        x = jnp.concatenate(
            [jnp.concatenate([xt_cur[pl.ds((rg * tiles + jt) * SUBLANES, SUBLANES), :] for jt in range(tiles)], axis=1)
             for rg in range(first_group, first_group + groups)], axis=0).astype(bf16)
        hmid = jnp.dot(x, wup_ref[0], preferred_element_type=f32) + bup_ref[0]
        glu = jnp.concatenate([hmid[:, 2 * jt * LANES:(2 * jt + 1) * LANES] for jt in range(tiles)], axis=1)
        lin = jnp.concatenate([hmid[:, (2 * jt + 1) * LANES:(2 * jt + 2) * LANES] for jt in range(tiles)], axis=1)
        glu = jnp.minimum(glu, SWIGLU_LIMIT)
        lin = jnp.clip(lin, -SWIGLU_LIMIT, SWIGLU_LIMIT)
        act = glu * jax.nn.sigmoid(SWIGLU_ALPHA * glu) * (lin + 1.0)
        y = jnp.dot(act.astype(bf16), wdn_ref[0], preferred_element_type=f32) + bdn_ref[0]
        for rg in range(groups):
            for jt in range(tiles):
                y_cur[pl.ds(((first_group + rg) * tiles + jt) * SUBLANES, SUBLANES), :] = (
                    y[rg * SUBLANES:(rg + 1) * SUBLANES, jt * LANES:(jt + 1) * LANES])

    def step(j, groups, xt_cur, xt_nxt, y_cur, y_prv):
        gather((j + 1) * ms, xt_nxt)
        scatter(jnp.where(j == 0, null_row0, (j - 1) * ms), y_prv)
        place(nxt, j * place_per_step, place_per_step)
        mlp(xt_cur, y_cur, 0, groups)

    def sub_block(i, carry):
        j = j0 + i
        full = jnp.logical_or(i < nsub_ref[g] - 1, tail_ref[g] > ms // 2)
        buffers = ((xt_a, xt_b, y_a, y_b), (xt_b, xt_a, y_b, y_a))
        for parity in range(2):
            for groups, wanted in ((ms // SUBLANES, full), (ms // SUBLANES // 2, jnp.logical_not(full))):
                @pl.when(jnp.logical_and(j % 2 == parity, wanted))
                def _():
                    step(j, groups, *buffers[parity])

        return carry

    lax.fori_loop(0, nsub_ref[g], sub_block, 0)

    @pl.when(e == n_exp - 1)
    def _():
        last = j0 + nsub_ref[g] - 1

        @pl.when(last % 2 == 0)
        def _():
            scatter(last * ms, y_a)

        @pl.when(last % 2 == 1)
        def _():
            scatter(last * ms, y_b)

        cp = pltpu.make_async_copy(acc_ref.at[pl.ds(0, tc)], f_hbm.at[pl.ds(c * tc, tc)], sem_ref.at[0])
        cp.start()
        cp.wait()


def _row_map_geometry(tc, n_exp, seg):
    assert seg > tc and seg & (seg - 1) == 0
    max_sub = (TOP_K * tc + n_exp * (MOE_SUB - 1)) // MOE_SUB
    min_sub = -(-TOP_K * tc // MOE_SUB)
    null_row0 = (max_sub + 1) * MOE_SUB
    half_len = -(-(null_row0 + MOE_SUB + 1) // LANES) * LANES
    place_per_step = -(-tc // min_sub)
    assert max_sub * place_per_step <= seg
    return null_row0, half_len, place_per_step


def experts(sb_start, n_sub, tail_rows, xp, dest, gates, w_up, b_up, w_down, b_down, tc):
    n, tiles, _ = xp.shape
    d = tiles * LANES
    n_exp = w_up.shape[0]
    chunks, padded_len = dest.shape
    seg = padded_len // TOP_K
    null_row0, half_len, place_per_step = _row_map_geometry(tc, n_exp, seg)
    fill = jnp.full((half_len,), tc, i32)
    kern = functools.partial(_experts_kernel, tc=tc, n_exp=n_exp, d=d, null_row0=null_row0, seg=seg,
                             half_len=half_len, place_per_step=place_per_step)
    block = pltpu.VMEM((MOE_SUB * tiles, LANES), f32)
    return pl.pallas_call(
        kern,
        out_shape=jax.ShapeDtypeStruct((n, tiles, LANES), f32),
        grid_spec=pltpu.PrefetchScalarGridSpec(
            num_scalar_prefetch=3,
            grid=(chunks, n_exp),
            in_specs=[
                pl.BlockSpec(memory_space=pl.ANY),
                pl.BlockSpec(memory_space=pl.ANY),
                pl.BlockSpec(memory_space=pl.ANY),
                pl.BlockSpec(memory_space=pl.ANY),
                pl.BlockSpec((1, d, 2 * d), lambda c, e, *_: (e, 0, 0)),
                pl.BlockSpec((1, 1, 2 * d), lambda c, e, *_: (e, 0, 0)),
                pl.BlockSpec((1, d, d), lambda c, e, *_: (e, 0, 0)),
                pl.BlockSpec((1, 1, d), lambda c, e, *_: (e, 0, 0)),
            ],
            out_specs=pl.BlockSpec(memory_space=pl.ANY),
            scratch_shapes=[
                pltpu.VMEM((tc + SUBLANES, tiles, LANES), f32),
                pltpu.VMEM((tc + SUBLANES, tiles, LANES), f32),
                block, block, block, block,
                pltpu.SMEM((padded_len,), i32),
                pltpu.SMEM((padded_len,), f32),
                pltpu.SMEM((2 * half_len,), i32),
                pltpu.SemaphoreType.DMA((4,)),
            ],
        ),
        compiler_params=_cparams(("arbitrary", "arbitrary"), 58),
        name="experts",
    )(sb_start, n_sub, tail_rows, xp, dest, gates, fill, w_up, b_up.reshape(n_exp, 1, 2 * d), w_down,
      b_down.reshape(n_exp, 1, d))


def _final_kernel(h_ref, f_ref, g_ref, o_ref):
    tm, d = h_ref.shape
    tiles = d // LANES
    parts = [h_ref[:, jt * LANES:(jt + 1) * LANES] + f_ref[pl.ds(jt, tm, stride=tiles), :] for jt in range(tiles)]
    o_ref[...] = _rms(jnp.concatenate(parts, axis=1), g_ref[...])


def final_norm(h, f2, g, row0, rows):
    d = h.shape[1]
    tiles = d // LANES
    tm = min(ROW_TILE, rows)
    blk0 = row0 // tm
    assert rows % tm == 0 and row0 % tm == 0
    return pl.pallas_call(
        _final_kernel,
        out_shape=jax.ShapeDtypeStruct((rows, d), f32),
        grid=(rows // tm,),
        in_specs=[
            pl.BlockSpec((tm, d), lambda i: (blk0 + i, 0)),
            pl.BlockSpec((tm * tiles, LANES), lambda i: (blk0 + i, 0)),
            pl.BlockSpec((1, d), lambda i: (0, 0)),
        ],
        out_specs=pl.BlockSpec((tm, d), lambda i: (i, 0)),
        compiler_params=_cparams(("arbitrary",), 32),
        name="final_norm",
    )(h, f2, g.reshape(1, d))


def moe(h, g, w_router, b_router, w_up, b_up, w_down, b_down):
    n, d = h.shape
    n_exp = w_router.shape[1]
    tc = n // MOE_CHUNKS
    xp, top_e, gates, rank, counts = router(h, g, w_router, b_router, tc)
    n_sub = (counts.reshape(MOE_CHUNKS, n_exp) + MOE_SUB - 1) // MOE_SUB
    sb_start = jnp.cumsum(n_sub, axis=1) - n_sub
    hot = top_e.reshape(TOP_K, MOE_CHUNKS, tc, 1) == jnp.arange(n_exp, dtype=i32)
    row_base = (sb_start * MOE_SUB).reshape(1, MOE_CHUNKS, 1, n_exp)
    dest = jnp.sum(jnp.where(hot, row_base, 0), axis=-1) + rank.reshape(TOP_K, MOE_CHUNKS, tc)
    seg = 1 << tc.bit_length()
    spare_row = _row_map_geometry(tc, n_exp, seg)[1] - 1
    per_chunk = lambda a, tail: jnp.pad(a.transpose(1, 0, 2), ((0, 0), (0, 0), (0, seg - tc)),
                                        constant_values=tail).reshape(MOE_CHUNKS, TOP_K * seg)
    tiles = d // LANES
    b_up_p = b_up.reshape(n_exp, tiles, LANES, 2).transpose(0, 1, 3, 2).reshape(n_exp, 2 * d)
    tail_rows = counts.reshape(MOE_CHUNKS, n_exp) - (n_sub - 1) * MOE_SUB
    flat = lambda a: a.reshape(-1).astype(i32)
    f = experts(flat(sb_start), flat(n_sub), flat(tail_rows), xp.reshape(n, tiles, LANES),
                per_chunk(dest, spare_row), per_chunk(gates.reshape(TOP_K, MOE_CHUNKS, tc), 0.0), split_pairs(w_up),
                b_up_p, w_down.astype(bf16), b_down, tc)
    return f.reshape(n * tiles, LANES)


def kernel(x_prompt, x_sample, mem_prompt, state_ret, state_conv, cache_mem_k, cache_mem_v, norm_mix, w_in, conv_w,
           ret_gn, w_out, norm_cross, norm_mem, w_mq, w_mk, w_mv, w_mo, norm_ffn, w_router, b_router, w_up, b_up,
           w_down, b_down, norm_final):
    batch, seq, d = x_prompt.shape
    n_seq, ts, _ = x_sample.shape
    depth = w_in.shape[0]
    n_mem = mem_prompt.shape[1]
    r = ret_gn.shape[1]
    cd_width = conv_w.shape[2]
    dh = r // N_RET_HEADS
    n_p = batch * seq
    n_s = n_seq * ts
    n = n_p + n_s
    nblk = n_seq // SAMPLE_BLOCK
    assert seq % ROW_TILE == 0 and n % ROW_TILE == 0 and n % MOE_CHUNKS == 0

    xs = x_sample.reshape(nblk, SAMPLE_BLOCK, ts, d).transpose(0, 2, 1, 3).reshape(n_s, d)
    h_p, h_s = x_prompt.reshape(n_p, d), xs
    tab_p = prompt_tables(seq, dh)
    tab_s = sample_tables(ts, float(PAST_LEN), dh)

    ret_p, conv_p, mk_p, mv_p, ret_s, conv_s = [], [], [], [], [], []
    for l in range(depth):
        w_in_b, w_out_b, w_mq_b = w_in[l].astype(bf16), w_out[l].astype(bf16), w_mq[l].astype(bf16)
        h_p, s_p, c_p = prompt_mixer(h_p, norm_mix[l], w_in_b, tab_p, conv_w[l], ret_gn[l], w_out_b, batch, seq,
                                     r, cd_width)
        proj_s = norm_matmul(h_s, norm_mix[l], w_in_b, "in_proj_sample")
        mix_s, s_s, c_s = sample_mixer(proj_s, tab_s, conv_w[l], ret_gn[l], state_ret[l],
                                       state_conv[l].transpose(1, 0, 2), 0, ts, r, cd_width)
        h_s = matmul_res(mix_s, w_out_b, h_s, "out_proj_sample")

        mk, mv = mem_kv(mem_prompt, norm_mem[l], jnp.concatenate([w_mk[l], w_mv[l]], axis=1).astype(bf16),
                        N_MEM_HEADS)
        q_s = norm_matmul(h_s, norm_cross[l], w_mq_b, "q_sample")
        attn_s = cross_sample(q_s, cache_mem_k[l], cache_mem_v[l], 0, ts)
        h = attn_block(h_p, h_s, norm_cross[l], w_mq_b, mk, mv, attn_s, w_mo[l].astype(bf16), seq)

        f2 = moe(h, norm_ffn[l], w_router[l], b_router[l], w_up[l], b_up[l], w_down[l], b_down[l])
        if l + 1 < depth:
            h = h + f2.reshape(n, d // LANES, LANES).reshape(n, d)
            h_p, h_s = h, h[n_p:]

        ret_p.append(s_p)
        conv_p.append(c_p)
        mk_p.append(mk)
        mv_p.append(mv)
        ret_s.append(s_s)
        conv_s.append(c_s.transpose(1, 0, 2))

    y_p = final_norm(h, f2, norm_final, 0, n_p).reshape(batch, seq, d)
    y_s = final_norm(h, f2, norm_final, n_p, n_s)
    y_s = y_s.reshape(nblk, ts, SAMPLE_BLOCK, d).transpose(0, 2, 1, 3).reshape(n_seq, ts, d)
    return (y_p, y_s, jnp.stack(ret_p), jnp.stack(conv_p), jnp.stack(mk_p), jnp.stack(mv_p), jnp.stack(ret_s),
            jnp.stack(conv_s))
```

```python
import functools

import jax
import jax.numpy as jnp
from jax import lax
from jax.experimental import pallas as pl
from jax.experimental.pallas import tpu as pltpu

f32 = jnp.float32
bf16 = jnp.bfloat16
i32 = jnp.int32

EPS = 1e-6
ROPE_THETA = 10000.0
RET_CHUNK = 128
PAST_LEN = 16384
N_RET_HEADS = 4
N_MEM_HEADS = 4
TOP_K = 4
SWIGLU_ALPHA = 1.702
SWIGLU_LIMIT = 7.0

LANES = 128
SUBLANES = 8
MIB = 1024 * 1024

ROW_TILE = 512
SAMPLE_BLOCK = 8
MOE_CHUNKS = 4
MOE_SUB = 256
SCATTER_UNROLL = 8


def _cparams(sem, vmem_mib):
    return pltpu.CompilerParams(dimension_semantics=sem, vmem_limit_bytes=vmem_mib * MIB)


def _rms(x, g):
    ms = jnp.mean(x * x, axis=-1, keepdims=True)
    return (x * lax.rsqrt(ms + EPS)) * g


def _norm_matmul_kernel(x_ref, g_ref, w_ref, o_ref):
    xn = _rms(x_ref[...], g_ref[...])
    o_ref[...] = jnp.dot(xn.astype(bf16), w_ref[...], preferred_element_type=f32)


def norm_matmul(x, g, w, name):
    m, d = x.shape
    f = w.shape[1]
    tm = min(ROW_TILE, m)
    assert m % tm == 0
    return pl.pallas_call(
        _norm_matmul_kernel,
        out_shape=jax.ShapeDtypeStruct((m, f), f32),
        grid=(m // tm,),
        in_specs=[
            pl.BlockSpec((tm, d), lambda i: (i, 0)),
            pl.BlockSpec((1, d), lambda i: (0, 0)),
            pl.BlockSpec((d, f), lambda i: (0, 0)),
        ],
        out_specs=pl.BlockSpec((tm, f), lambda i: (i, 0)),
        compiler_params=_cparams(("arbitrary",), 48),
        name=name,
    )(x, g.reshape(1, d), w)


def _stacked_specs(m_p, m_s, width):
    tm = ROW_TILE
    assert m_p % tm == 0 and m_s == tm
    prompt_tiles = m_p // tm
    return prompt_tiles, [
        pl.BlockSpec((tm, width), lambda i: (jnp.minimum(i, prompt_tiles - 1), 0)),
        pl.BlockSpec((tm, width), lambda i: (0, 0)),
    ]


def _matmul_res_kernel(a_ref, w_ref, r_ref, o_ref):
    o_ref[...] = r_ref[...] + jnp.dot(a_ref[...], w_ref[...], preferred_element_type=f32)


def matmul_res(a, w, res, name):
    m, d = a.shape
    f = w.shape[1]
    tm = min(ROW_TILE, m)
    assert m % tm == 0
    return pl.pallas_call(
        _matmul_res_kernel,
        out_shape=jax.ShapeDtypeStruct((m, f), f32),
        grid=(m // tm,),
        in_specs=[
            pl.BlockSpec((tm, d), lambda i: (i, 0)),
            pl.BlockSpec((d, f), lambda i: (0, 0)),
            pl.BlockSpec((tm, f), lambda i: (i, 0)),
        ],
        out_specs=pl.BlockSpec((tm, f), lambda i: (i, 0)),
        compiler_params=_cparams(("arbitrary",), 32),
        name=name,
    )(a, w, res)


def _rope(x, cos, sin_signed, half):
    return x * cos + pltpu.roll(x, half, 1) * sin_signed


def _group_norm_gate(o, gate, gn):
    mu = jnp.mean(o, axis=-1, keepdims=True)
    var = jnp.mean(jnp.square(o - mu), axis=-1, keepdims=True)
    on = ((o - mu) * lax.rsqrt(var + EPS)) * gn
    return (gate * jax.nn.sigmoid(gate)) * on


def _prompt_mixer_kernel(cd_ref, x_ref, g_ref, win_ref, cos_ref, sin_ref, decay_ref, xi_ref, zeta_ref, convw_ref,
                         gn_ref, wout_ref, wcast_ref, h_ref, sfin_ref, cfin_ref, wcast_out_ref, s_scr, u_scr, proj_ref,
                         mix_ref, *, tt, r, cd_width, dh):
    wcast_out_ref[...] = wcast_ref[...].astype(bf16)
    j = pl.program_id(1)
    heads = r // dh
    pad = SUBLANES

    @pl.when(j == 0)
    def _():
        s_scr[...] = jnp.zeros_like(s_scr)
        u_scr[0:pad, :] = jnp.zeros((pad, cd_width), f32)

    proj_ref[...] = jnp.dot(_rms(x_ref[...], g_ref[...]).astype(bf16), win_ref[...], preferred_element_type=f32)

    k_scale = dh ** -0.5
    for c in range(tt // RET_CHUNK):
        rows = pl.ds(c * RET_CHUNK, RET_CHUNK)
        cosb = cos_ref[rows, :]
        sinb = sin_ref[rows, :]
        for h in range(heads):
            lo, hi = h * dh, (h + 1) * dh
            qh = _rope(proj_ref[rows, lo:hi], cosb, sinb, dh // 2)
            kh = _rope(proj_ref[rows, r + lo:r + hi], cosb, sinb, dh // 2) * k_scale
            vb = proj_ref[rows, 2 * r + lo:2 * r + hi].astype(bf16)
            gate = proj_ref[rows, 3 * r + lo:3 * r + hi]
            qb = qh.astype(bf16)
            kb = kh.astype(bf16)
            scores = lax.dot_general(qb, kb, (((1,), (1,)), ((), ())), preferred_element_type=f32) * decay_ref[h]
            inner = jnp.dot(scores.astype(bf16), vb, preferred_element_type=f32)
            s_prev = s_scr[h]
            cross = jnp.dot((qh * xi_ref[h]).astype(bf16), s_prev.astype(bf16), preferred_element_type=f32)
            kv = lax.dot_general((kh * zeta_ref[h]).astype(bf16), vb, (((0,), (0,)), ((), ())),
                                 preferred_element_type=f32)
            s_scr[h] = cd_ref[h] * s_prev + kv
            mix_ref[rows, lo:hi] = _group_norm_gate(inner + cross, gate, gn_ref[:, lo:hi]).astype(bf16)

    base = 4 * r
    u = proj_ref[:, base + cd_width:base + 2 * cd_width] * proj_ref[:, base:base + cd_width]
    u_scr[pad:pad + tt, :] = u
    conv = (u_scr[pad - 2:pad - 2 + tt, :] * convw_ref[0:1, :] + u_scr[pad - 1:pad - 1 + tt, :] * convw_ref[1:2, :]
            + u * convw_ref[2:3, :])
    mix_ref[:, r:r + cd_width] = (proj_ref[:, base + 2 * cd_width:base + 3 * cd_width] * conv).astype(bf16)
    u_scr[0:pad, :] = u_scr[tt:tt + pad, :]
    cfin_ref[0] = u_scr[pad - 2:pad, :]
    h_ref[...] = x_ref[...] + jnp.dot(mix_ref[...], wout_ref[...], preferred_element_type=f32)

    @pl.when(j == pl.num_programs(1) - 1)
    def _():
        sfin_ref[0] = s_scr[...]


def prompt_mixer(x, g, w_in, tables, conv_w, ret_gn, w_out, batch, seq, r, cd_width, w_cast):
    tt = ROW_TILE
    d = x.shape[1]
    dh = r // N_RET_HEADS
    steps = seq // tt
    in_cols = w_in.shape[1]
    assert w_cast.shape[0] == batch * steps
    kern = functools.partial(_prompt_mixer_kernel, tt=tt, r=r, cd_width=cd_width, dh=dh)
    tab = lambda shape: pl.BlockSpec(shape, lambda b, j, cd: (0,) * len(shape))
    row_tile = pl.BlockSpec((tt, d), lambda b, j, cd: (b * steps + j, 0))
    cast_slice = pl.BlockSpec((1,) + w_cast.shape[1:], lambda b, j, cd: (b * steps + j, 0, 0))
    return pl.pallas_call(
        kern,
        out_shape=(
            jax.ShapeDtypeStruct((batch * seq, d), f32),
            jax.ShapeDtypeStruct((batch, N_RET_HEADS, dh, dh), f32),
            jax.ShapeDtypeStruct((batch, 2, cd_width), f32),
            jax.ShapeDtypeStruct(w_cast.shape, bf16),
        ),
        grid_spec=pltpu.PrefetchScalarGridSpec(
            num_scalar_prefetch=1,
            grid=(batch, steps),
            in_specs=[
                row_tile,
                tab((1, d)),
                tab((d, in_cols)),
                pl.BlockSpec((tt, dh), lambda b, j, cd: (j, 0)),
                pl.BlockSpec((tt, dh), lambda b, j, cd: (j, 0)),
                tab((N_RET_HEADS, RET_CHUNK, RET_CHUNK)),
                tab((N_RET_HEADS, RET_CHUNK, dh)),
                tab((N_RET_HEADS, RET_CHUNK, dh)),
                tab((3, cd_width)),
                tab((1, r)),
                tab((r + cd_width, d)),
                cast_slice,
            ],
            out_specs=(
                row_tile,
                pl.BlockSpec((1, N_RET_HEADS, dh, dh), lambda b, j, cd: (b, 0, 0, 0)),
                pl.BlockSpec((1, 2, cd_width), lambda b, j, cd: (b, 0, 0)),
                cast_slice,
            ),
            scratch_shapes=[
                pltpu.VMEM((N_RET_HEADS, dh, dh), f32),
                pltpu.VMEM((tt + 2 * SUBLANES, cd_width), f32),
                pltpu.VMEM((tt, in_cols), f32),
                pltpu.VMEM((tt, r + cd_width), bf16),
            ],
        ),
        compiler_params=_cparams(("arbitrary", "arbitrary"), 56),
        name="prompt_mixer",
    )(tables["chunk_decay"], x, g.reshape(1, d), w_in, tables["cos"], tables["sin"], tables["decay"], tables["xi"],
      tables["zeta"], conv_w, ret_gn.reshape(1, r), w_out, w_cast)


def _sample_mixer_kernel(dec_ref, proj_ref, cos_ref, sin_ref, xi_ref, zeta_ref, convw_ref, gn_ref, s0_ref, c0_ref,
                         mix_ref, snew_ref, cnew_ref, *, ts, r, cd_width, dh):
    heads = r // dh
    nb = SAMPLE_BLOCK
    rows = ts * nb
    k_scale = dh ** -0.5
    cosb = cos_ref[...]
    sinb = sin_ref[...]
    seq_of_row = lax.broadcasted_iota(i32, (rows, dh), 0) % nb
    for h in range(heads):
        lo, hi = h * dh, (h + 1) * dh
        qh = _rope(proj_ref[:, lo:hi], cosb, sinb, dh // 2)
        kh = _rope(proj_ref[:, r + lo:r + hi], cosb, sinb, dh // 2) * k_scale
        vh = proj_ref[:, 2 * r + lo:2 * r + hi]
        gate = proj_ref[:, 3 * r + lo:3 * r + hi]
        inner = []
        for t in range(ts):
            qt = qh[t * nb:(t + 1) * nb]
            acc = jnp.zeros((nb, dh), f32)
            for s in range(t + 1):
                sc = jnp.sum(qt * kh[s * nb:(s + 1) * nb], axis=-1, keepdims=True) * dec_ref[h * (ts + 1) + t - s]
                acc = acc + sc * vh[s * nb:(s + 1) * nb]
            inner.append(acc)
        inner = jnp.concatenate(inner, axis=0)
        qx = (qh * xi_ref[h]).astype(bf16)
        kz = kh * zeta_ref[h]
        vb = vh.astype(bf16)
        cross = jnp.zeros((rows, dh), f32)
        for b in range(nb):
            mine = seq_of_row == b
            s_prev = s0_ref[b, h]
            res = jnp.dot(qx, s_prev.astype(bf16), preferred_element_type=f32)
            cross = cross + jnp.where(mine, res, 0.0)
            kv = lax.dot_general(jnp.where(mine, kz, 0.0).astype(bf16), vb, (((0,), (0,)), ((), ())),
                                 preferred_element_type=f32)
            snew_ref[b, h] = dec_ref[h * (ts + 1) + ts] * s_prev + kv
        mix_ref[:, lo:hi] = _group_norm_gate(inner + cross, gate, gn_ref[:, lo:hi]).astype(bf16)

    base = 4 * r
    u = proj_ref[:, base + cd_width:base + 2 * cd_width] * proj_ref[:, base:base + cd_width]
    full = [c0_ref[0], c0_ref[1]] + [u[t * nb:(t + 1) * nb] for t in range(ts)]
    conv = jnp.concatenate(
        [full[t] * convw_ref[0:1, :] + full[t + 1] * convw_ref[1:2, :] + full[t + 2] * convw_ref[2:3, :]
         for t in range(ts)], axis=0)
    mix_ref[:, r:r + cd_width] = (proj_ref[:, base + 2 * cd_width:base + 3 * cd_width] * conv).astype(bf16)
    cnew_ref[0] = full[ts]
    cnew_ref[1] = full[ts + 1]


def sample_mixer(proj, tables, conv_w, ret_gn, state_ret, state_conv_t, row0, ts, r, cd_width):
    n_seq = state_ret.shape[0]
    dh = r // N_RET_HEADS
    rows = ts * SAMPLE_BLOCK
    blk0 = row0 // rows
    assert row0 % rows == 0 and n_seq % SAMPLE_BLOCK == 0
    in_cols = proj.shape[1]
    kern = functools.partial(_sample_mixer_kernel, ts=ts, r=r, cd_width=cd_width, dh=dh)
    tab = lambda shape: pl.BlockSpec(shape, lambda i, d: (0,) * len(shape))
    return pl.pallas_call(
        kern,
        out_shape=(
            jax.ShapeDtypeStruct((n_seq * ts, r + cd_width), bf16),
            jax.ShapeDtypeStruct(state_ret.shape, f32),
            jax.ShapeDtypeStruct(state_conv_t.shape, f32),
        ),
        grid_spec=pltpu.PrefetchScalarGridSpec(
            num_scalar_prefetch=1,
            grid=(n_seq // SAMPLE_BLOCK,),
            in_specs=[
                pl.BlockSpec((rows, in_cols), lambda i, d: (blk0 + i, 0)),
                tab((rows, dh)),
                tab((rows, dh)),
                tab((N_RET_HEADS, rows, dh)),
                tab((N_RET_HEADS, rows, dh)),
                tab((3, cd_width)),
                tab((1, r)),
                pl.BlockSpec((SAMPLE_BLOCK, N_RET_HEADS, dh, dh), lambda i, d: (i, 0, 0, 0)),
                pl.BlockSpec((2, SAMPLE_BLOCK, cd_width), lambda i, d: (0, i, 0)),
            ],
            out_specs=(
                pl.BlockSpec((rows, r + cd_width), lambda i, d: (i, 0)),
                pl.BlockSpec((SAMPLE_BLOCK, N_RET_HEADS, dh, dh), lambda i, d: (i, 0, 0, 0)),
                pl.BlockSpec((2, SAMPLE_BLOCK, cd_width), lambda i, d: (0, i, 0)),
            ),
        ),
        compiler_params=_cparams(("arbitrary",), 32),
        name="sample_mixer",
    )(tables["dec"], proj, tables["cos"], tables["sin"], tables["xi"], tables["zeta"], conv_w, ret_gn.reshape(1, r),
      state_ret, state_conv_t)


def _log_gamma():
    return jnp.log1p(-jnp.exp2(-5.0 - jnp.arange(N_RET_HEADS, dtype=f32)))


def _rope_tables(pos, dh):
    half = dh // 2
    inv = ROPE_THETA ** (-jnp.arange(half, dtype=f32) / half)
    ang = pos[:, None] * inv[None, :]
    cos, sin = jnp.cos(ang), jnp.sin(ang)
    return jnp.concatenate([cos, cos], axis=-1), jnp.concatenate([-sin, sin], axis=-1)


def prompt_tables(seq, dh):
    c = RET_CHUNK
    lg = _log_gamma()
    i = jnp.arange(c, dtype=f32)
    diff = i[:, None] - i[None, :]
    decay = jnp.where(diff >= 0, jnp.exp(jnp.maximum(diff, 0.0)[None] * lg[:, None, None]), 0.0)
    xi = jnp.exp((i + 1.0)[None, :] * lg[:, None])
    zeta = jnp.exp((c - 1.0 - i)[None, :] * lg[:, None])
    cos, sin = _rope_tables(jnp.arange(seq, dtype=f32), dh)
    bc = lambda t: jnp.broadcast_to(t[:, :, None], (N_RET_HEADS, c, dh))
    return dict(cos=cos, sin=sin, decay=decay, xi=bc(xi), zeta=bc(zeta), chunk_decay=jnp.exp(c * lg))


def sample_tables(ts, pos0, dh):
    lg = _log_gamma()
    i = jnp.arange(ts, dtype=f32)
    dec = jnp.exp(jnp.arange(ts + 1, dtype=f32)[None, :] * lg[:, None])
    xi = jnp.exp((i + 1.0)[None, :] * lg[:, None])
    zeta = jnp.exp((ts - 1.0 - i)[None, :] * lg[:, None])
    cos, sin = _rope_tables(pos0 + i, dh)
    rep = lambda t: jnp.repeat(t, SAMPLE_BLOCK, axis=0)
    bc = lambda t: jnp.broadcast_to(jnp.repeat(t, SAMPLE_BLOCK, axis=1)[:, :, None],
                                    (N_RET_HEADS, ts * SAMPLE_BLOCK, dh))
    return dict(cos=rep(cos), sin=rep(sin), xi=bc(xi), zeta=bc(zeta), dec=dec.reshape(-1))


def _softmax_rows(s):
    m = jnp.max(s, axis=-1, keepdims=True)
    p = jnp.exp(s - m)
    return p / jnp.sum(p, axis=-1, keepdims=True)


def _from_slabs(ref, rows=None):
    rows = slice(None) if rows is None else rows
    return jnp.concatenate([ref[lt, rows, :] for lt in range(ref.shape[0])], axis=1)


def _to_slabs(ref, value):
    for lt in range(ref.shape[0]):
        ref[lt] = value[:, lt * LANES:(lt + 1) * LANES]


def _mem_kv_kernel(m_ref, g_ref, w_ref, k_ref, v_ref, rows_ref, *, heads):
    n_mem, d = m_ref.shape
    dh = d // heads
    kv = jnp.dot(_rms(m_ref[...], g_ref[...]).astype(bf16), w_ref[...], preferred_element_type=f32)
    for part, o_ref in enumerate((k_ref, v_ref)):
        for h in range(heads):
            for lt in range(dh // LANES):
                lo = part * d + h * dh + lt * LANES
                rows_ref[lt, pl.ds(h, n_mem, stride=heads), :] = kv[:, lo:lo + LANES]
        o_ref[0] = _from_slabs(rows_ref).reshape(n_mem, heads, dh)


def mem_kv(mem, g, w_kv, heads):
    batch, n_mem, d = mem.shape
    dh = d // heads
    out = jax.ShapeDtypeStruct((batch, n_mem, heads, dh), f32)
    out_spec = pl.BlockSpec((1, n_mem, heads, dh), lambda b: (b, 0, 0, 0))
    return pl.pallas_call(
        functools.partial(_mem_kv_kernel, heads=heads),
        out_shape=(out, out),
        grid=(batch,),
        in_specs=[
            pl.BlockSpec((n_mem, d), lambda b: (b, 0)),
            pl.BlockSpec((1, d), lambda b: (0, 0)),
            pl.BlockSpec((d, 2 * d), lambda b: (0, 0)),
        ],
        out_specs=(out_spec, out_spec),
        scratch_shapes=[pltpu.VMEM((dh // LANES, n_mem * heads, LANES), f32)],
        compiler_params=_cparams(("arbitrary",), 32),
        name="mem_kv",
    )(mem.reshape(batch * n_mem, d), g.reshape(1, d), w_kv)


def _attn_block_kernel(hp_ref, hs_ref, g_ref, wq_ref, k_ref, v_ref, as_ref, wo_ref, wup_ref, perm_ref, o_ref,
                       wup_out_ref, k_rows, v_rows, *, prompt_tiles, heads):
    i = pl.program_id(0)

    @pl.when(i < prompt_tiles)
    def _():
        _split_pairs(wup_ref, perm_ref, wup_out_ref)
        h_ref = hp_ref
        n_mem, _, dh = k_ref.shape[1:]
        scale = dh ** -0.5
        q = jnp.dot(_rms(h_ref[...], g_ref[...]).astype(bf16), wq_ref[...], preferred_element_type=f32)
        _to_slabs(k_rows, k_ref[0].reshape(n_mem * heads, dh))
        _to_slabs(v_rows, v_ref[0].reshape(n_mem * heads, dh))
        outs = []
        for h in range(heads):
            kh = _from_slabs(k_rows, pl.ds(h, n_mem, stride=heads)).astype(bf16)
            vh = _from_slabs(v_rows, pl.ds(h, n_mem, stride=heads)).astype(bf16)
            s = lax.dot_general(q[:, h * dh:(h + 1) * dh].astype(bf16), kh, (((1,), (1,)), ((), ())),
                                preferred_element_type=f32) * scale
            outs.append(jnp.dot(_softmax_rows(s).astype(bf16), vh, preferred_element_type=f32).astype(bf16))
        attn = jnp.concatenate(outs, axis=1)
        o_ref[...] = h_ref[...] + jnp.dot(attn, wo_ref[...], preferred_element_type=f32)

    @pl.when(i == prompt_tiles)
    def _():
        o_ref[...] = hs_ref[...] + jnp.dot(as_ref[...], wo_ref[...], preferred_element_type=f32)


def attn_block(h_prompt, h_sample, g, w_mq, mk, mv, attn_sample, w_mo, seq, w_up):
    d = h_prompt.shape[1]
    batch, n_mem, heads, dh = mk.shape
    tm = ROW_TILE
    per_batch = seq // tm
    prompt_tiles, h_specs = _stacked_specs(batch * seq, h_sample.shape[0], d)
    n_exp, _, f2 = w_up.shape
    assert attn_sample.shape == (tm, d) and n_exp == prompt_tiles
    prompt_tile = lambda i: jnp.minimum(i, prompt_tiles - 1)
    kv_spec = pl.BlockSpec((1, n_mem, heads, dh), lambda i: (prompt_tile(i) // per_batch, 0, 0, 0))
    wup_spec = pl.BlockSpec((1, d, f2), lambda i: (prompt_tile(i), 0, 0))
    const = lambda shape: pl.BlockSpec(shape, lambda i: (0,) * len(shape))
    perm = pair_split_matrix()
    return pl.pallas_call(
        functools.partial(_attn_block_kernel, prompt_tiles=prompt_tiles, heads=heads),
        out_shape=(
            jax.ShapeDtypeStruct(((prompt_tiles + 1) * tm, d), f32),
            jax.ShapeDtypeStruct(w_up.shape, bf16),
        ),
        grid=(prompt_tiles + 1,),
        in_specs=h_specs + [
            const((1, d)),
            const((d, d)),
            kv_spec,
            kv_spec,
            const((tm, d)),
            const((d, d)),
            wup_spec,
            const(perm.shape),
        ],
        out_specs=(pl.BlockSpec((tm, d), lambda i: (i, 0)), wup_spec),
        scratch_shapes=[pltpu.VMEM((dh // LANES, n_mem * heads, LANES), f32)] * 2,
        compiler_params=_cparams(("arbitrary",), 56),
        name="attn_block",
    )(h_prompt, h_sample, g.reshape(1, d), w_mq, mk, mv, attn_sample, w_mo, w_up, perm)


def _cross_sample_kernel(q_ref, k_ref, v_ref, o_ref, acc_ref, *, dh, seqs):
    part = pl.program_id(1)
    rows = q_ref.shape[0]
    scale = dh ** -0.5

    @pl.when(part == 0)
    def _():
        acc_ref[...] = jnp.zeros_like(acc_ref)

    heads = N_MEM_HEADS
    n_mem = k_ref.shape[1]
    qx = jnp.concatenate([q_ref[:, h * dh:(h + 1) * dh] for h in range(heads)], axis=0).astype(bf16)
    row = lax.broadcasted_iota(i32, (heads * rows, n_mem * heads), 0)
    col = lax.broadcasted_iota(i32, (heads * rows, n_mem * heads), 1)
    same_head = col % heads == row // rows
    seq_of_row = lax.broadcasted_iota(i32, (heads * rows, dh), 0) % SAMPLE_BLOCK
    out = jnp.zeros((heads * rows, dh), f32)
    for b in range(seqs):
        k2 = k_ref[b].reshape(n_mem * heads, dh).astype(bf16)
        v2 = v_ref[b].reshape(n_mem * heads, dh).astype(bf16)
        s = lax.dot_general(qx, k2, (((1,), (1,)), ((), ())), preferred_element_type=f32) * scale
        p = _softmax_rows(jnp.where(same_head, s, -jnp.inf))
        o = jnp.dot(p.astype(bf16), v2, preferred_element_type=f32)
        out = out + jnp.where(seq_of_row == part * seqs + b, o, 0.0)
    acc_ref[...] += out

    @pl.when(part == pl.num_programs(1) - 1)
    def _():
        for h in range(heads):
            o_ref[:, h * dh:(h + 1) * dh] = acc_ref[h * rows:(h + 1) * rows, :].astype(bf16)


def cross_sample(q, cache_k, cache_v, row0, ts):
    n_seq, n_mem, heads, dh = cache_k.shape
    d = heads * dh
    rows = ts * SAMPLE_BLOCK
    blk0 = row0 // rows
    parts = 4
    seqs = SAMPLE_BLOCK // parts
    kern = functools.partial(_cross_sample_kernel, dh=dh, seqs=seqs)
    return pl.pallas_call(
        kern,
        out_shape=jax.ShapeDtypeStruct((n_seq * ts, d), bf16),
        grid=(n_seq // SAMPLE_BLOCK, parts),
        in_specs=[
            pl.BlockSpec((rows, d), lambda i, p: (blk0 + i, 0)),
            pl.BlockSpec((seqs, n_mem, heads, dh), lambda i, p: (i * parts + p, 0, 0, 0)),
            pl.BlockSpec((seqs, n_mem, heads, dh), lambda i, p: (i * parts + p, 0, 0, 0)),
        ],
        out_specs=pl.BlockSpec((rows, d), lambda i, p: (i, 0)),
        scratch_shapes=[pltpu.VMEM((heads * rows, dh), f32)],
        compiler_params=_cparams(("arbitrary", "arbitrary"), 40),
        name="cross_sample",
    )(q, cache_k, cache_v)


def _router_kernel(h_ref, g_ref, wr_ref, br_ref, xp_ref, e_ref, gate_ref, rank_ref, cnt_ref, carry_ref,
                   *, tiles_per_chunk, n_exp):
    i = pl.program_id(0)

    @pl.when(i % tiles_per_chunk == 0)
    def _():
        carry_ref[...] = jnp.zeros_like(carry_ref)

    xn = _rms(h_ref[...], g_ref[...])
    tm, d = xn.shape
    xp_ref[...] = xn

    w = wr_ref[...]
    w_hi = w.astype(bf16)
    w_lo = (w - w_hi.astype(f32)).astype(bf16)
    x_hi = xn.astype(bf16)
    x_lo = (xn - x_hi.astype(f32)).astype(bf16)
    nt = (((1,), (1,)), ((), ()))
    both = lax.dot_general(jnp.concatenate([w_hi, w_lo], axis=0), x_hi, nt, preferred_element_type=f32)
    logits = (both[:n_exp] + both[n_exp:] + lax.dot_general(w_hi, x_lo, nt, preferred_element_type=f32)
              + br_ref[...])
    sub = lax.broadcasted_iota(i32, (n_exp, tm), 0).astype(f32)
    chosen, vals, hots = [], [], []
    work = logits
    for _ in range(TOP_K):
        m = jnp.max(work, axis=0, keepdims=True)
        idx = jnp.min(jnp.where(work == m, sub, float(n_exp)), axis=0, keepdims=True)
        hot = sub == idx
        chosen.append(idx)
        vals.append(m)
        hots.append(hot)
        work = jnp.where(hot, -jnp.inf, work)
    ex = [jnp.exp(v - vals[0]) for v in vals]
    denom = ex[0] + ex[1] + ex[2] + ex[3]
    e_ref[...] = jnp.concatenate(chosen, axis=0).astype(i32)
    gate_ref[...] = jnp.concatenate([x / denom for x in ex], axis=0)

    member = jnp.zeros((n_exp, tm), f32)
    for hot in hots:
        member = member + hot.astype(f32)
    earlier = (lax.broadcasted_iota(i32, (tm, tm), 0) < lax.broadcasted_iota(i32, (tm, tm), 1)).astype(bf16)
    before = jnp.dot(member.astype(bf16), earlier, preferred_element_type=f32) + carry_ref[...]
    rank_ref[...] = jnp.concatenate(
        [jnp.sum(jnp.where(hot, before, 0.0), axis=0, keepdims=True) for hot in hots], axis=0).astype(i32)
    carry_ref[...] += jnp.sum(member, axis=1, keepdims=True)
    cnt_ref[0] = carry_ref[...].astype(i32)


def router(h, g, w_router, b_router, tc):
    n, d = h.shape
    n_exp = w_router.shape[1]
    tm = ROW_TILE
    while tc % tm:
        tm -= LANES
    tiles_per_chunk = tc // tm
    kern = functools.partial(_router_kernel, tiles_per_chunk=tiles_per_chunk, n_exp=n_exp)
    return pl.pallas_call(
        kern,
        out_shape=(
            jax.ShapeDtypeStruct((n, d), f32),
            jax.ShapeDtypeStruct((TOP_K, n), i32),
            jax.ShapeDtypeStruct((TOP_K, n), f32),
            jax.ShapeDtypeStruct((TOP_K, n), i32),
            jax.ShapeDtypeStruct((n // tc, n_exp, 1), i32),
        ),
        grid=(n // tm,),
        in_specs=[
            pl.BlockSpec((tm, d), lambda i: (i, 0)),
            pl.BlockSpec((1, d), lambda i: (0, 0)),
            pl.BlockSpec((n_exp, d), lambda i: (0, 0)),
            pl.BlockSpec((n_exp, 1), lambda i: (0, 0)),
        ],
        out_specs=(
            pl.BlockSpec((tm, d), lambda i: (i, 0)),
            pl.BlockSpec((TOP_K, tm), lambda i: (0, i)),
            pl.BlockSpec((TOP_K, tm), lambda i: (0, i)),
            pl.BlockSpec((TOP_K, tm), lambda i: (0, i)),
            pl.BlockSpec((1, n_exp, 1), lambda i: (i // tiles_per_chunk, 0, 0)),
        ),
        scratch_shapes=[pltpu.VMEM((n_exp, 1), f32)],
        compiler_params=_cparams(("arbitrary",), 32),
        name="router",
    )(h, g.reshape(1, d), w_router.T, b_router.reshape(n_exp, 1))


def _split_pairs(w_ref, p_ref, o_ref):
    width = p_ref.shape[0]
    for b in range(w_ref.shape[2] // width):
        cols = slice(b * width, (b + 1) * width)
        o_ref[0, :, cols] = jnp.dot(w_ref[0, :, cols].astype(bf16), p_ref[...],
                                    preferred_element_type=f32).astype(bf16)


def pair_split_matrix():
    width = 2 * LANES
    j = jnp.arange(width)
    src = jnp.where(j < LANES, 2 * j, 2 * (j - LANES) + 1)
    return (jnp.arange(width)[:, None] == src[None, :]).astype(bf16)


def _dense_row_index(r, tiles):
    return (r // SUBLANES) * tiles * SUBLANES + r % SUBLANES


def _experts_kernel(start_ref, nsub_ref, tail_ref, xp_hbm, dest_hbm, gates_hbm, fill_hbm, wup_ref, bup_ref, wdn_ref, bdn_ref,
                    f_hbm, xs_ref, acc_ref, xt_a, xt_b, y_a, y_b, dest_s, gate_s, rmap_s, sem_ref,
                    *, tc, n_exp, d, null_row0, seg, half_len, place_per_step):
    c = pl.program_id(0)
    e = pl.program_id(1)
    ms = MOE_SUB
    tiles = d // LANES
    cur = pl.multiple_of((c % 2) * half_len, LANES)
    nxt = pl.multiple_of(half_len - cur, LANES)

    def gather(row0, xt_ref):
        for r in range(ms):
            t = rmap_s[cur + row0 + r] & (seg - 1)
            xt_ref[pl.ds(_dense_row_index(r, tiles), tiles, stride=SUBLANES), :] = xs_ref[t]

    def place(base, first, count):
        for k in range(TOP_K):
            for u in range(count):
                a = k * seg + first + u
                rmap_s[base + dest_s[a]] = a

    def scatter(row0, y_ref):
        for r0 in range(0, ms, SCATTER_UNROLL):
            toks, sums = [], []
            for r in range(r0, r0 + SCATTER_UNROLL):
                a = rmap_s[cur + row0 + r]
                t = a & (seg - 1)
                yrow = y_ref[pl.ds(_dense_row_index(r, tiles), tiles, stride=SUBLANES), :]
                toks.append(t)
                sums.append(acc_ref[t] + gate_s[a] * yrow)
            for t, s in zip(toks, sums):
                acc_ref[t] = s

    @pl.when(e == 0)
    def _():
        copies = [
            pltpu.make_async_copy(xp_hbm.at[pl.ds(c * tc, tc)], xs_ref.at[pl.ds(0, tc)], sem_ref.at[0]),
            pltpu.make_async_copy(gates_hbm.at[c], gate_s, sem_ref.at[1]),
        ]
        for cp in copies:
            cp.start()
        xs_ref[pl.ds(tc, SUBLANES)] = jnp.zeros((SUBLANES, tiles, LANES), f32)
        acc_ref[...] = jnp.zeros_like(acc_ref)

        def load_map_inputs(chunk, base):
            loads = [
                pltpu.make_async_copy(dest_hbm.at[chunk], dest_s, sem_ref.at[2]),
                pltpu.make_async_copy(fill_hbm, rmap_s.at[pl.ds(base, half_len)], sem_ref.at[3]),
            ]
            for cp in loads:
                cp.start()
            for cp in loads:
                cp.wait()

        @pl.when(c == 0)
        def _():
            y_a[...] = jnp.zeros_like(y_a)
            y_b[...] = jnp.zeros_like(y_b)
            load_map_inputs(0, 0)

            def place_all(i, carry):
                place(0, i * SUBLANES, SUBLANES)
                return carry

            lax.fori_loop(0, tc // SUBLANES, place_all, 0)

        load_map_inputs(jnp.minimum(c + 1, pl.num_programs(0) - 1), nxt)
        for cp in copies:
            cp.wait()
        gather(0, xt_a)

    g = c * n_exp + e
    j0 = start_ref[g]

    def mlp(xt_cur, y_cur, first_group, groups):
        x = jnp.concatenate(
            [jnp.concatenate([xt_cur[pl.ds((rg * tiles + jt) * SUBLANES, SUBLANES), :] for jt in range(tiles)], axis=1)
             for rg in range(first_group, first_group + groups)], axis=0).astype(bf16)
        hmid = jnp.dot(x, wup_ref[0], preferred_element_type=f32) + bup_ref[0]
        glu = jnp.concatenate([hmid[:, 2 * jt * LANES:(2 * jt + 1) * LANES] for jt in range(tiles)], axis=1)
        lin = jnp.concatenate([hmid[:, (2 * jt + 1) * LANES:(2 * jt + 2) * LANES] for jt in range(tiles)], axis=1)
        glu = jnp.minimum(glu, SWIGLU_LIMIT)
        lin = jnp.clip(lin, -SWIGLU_LIMIT, SWIGLU_LIMIT)
        act = glu * jax.nn.sigmoid(SWIGLU_ALPHA * glu) * (lin + 1.0)
        y = jnp.dot(act.astype(bf16), wdn_ref[0], preferred_element_type=f32) + bdn_ref[0]
        for rg in range(groups):
            for jt in range(tiles):
                y_cur[pl.ds(((first_group + rg) * tiles + jt) * SUBLANES, SUBLANES), :] = (
                    y[rg * SUBLANES:(rg + 1) * SUBLANES, jt * LANES:(jt + 1) * LANES])

    def step(j, groups, xt_cur, xt_nxt, y_cur, y_prv):
        gather((j + 1) * ms, xt_nxt)
        scatter(jnp.where(j == 0, null_row0, (j - 1) * ms), y_prv)
        place(nxt, j * place_per_step, place_per_step)
        mlp(xt_cur, y_cur, 0, groups)

    def sub_block(i, carry):
        j = j0 + i
        full = jnp.logical_or(i < nsub_ref[g] - 1, tail_ref[g] > ms // 2)
        buffers = ((xt_a, xt_b, y_a, y_b), (xt_b, xt_a, y_b, y_a))
        for parity in range(2):
            for groups, wanted in ((ms // SUBLANES, full), (ms // SUBLANES // 2, jnp.logical_not(full))):
                @pl.when(jnp.logical_and(j % 2 == parity, wanted))
                def _():
                    step(j, groups, *buffers[parity])

        return carry

    lax.fori_loop(0, nsub_ref[g], sub_block, 0)

    @pl.when(e == n_exp - 1)
    def _():
        last = j0 + nsub_ref[g] - 1

        @pl.when(last % 2 == 0)
        def _():
            scatter(last * ms, y_a)

        @pl.when(last % 2 == 1)
        def _():
            scatter(last * ms, y_b)

        cp = pltpu.make_async_copy(acc_ref.at[pl.ds(0, tc)], f_hbm.at[pl.ds(c * tc, tc)], sem_ref.at[0])
        cp.start()
        cp.wait()


def _row_map_geometry(tc, n_exp, seg):
    assert seg > tc and seg & (seg - 1) == 0
    max_sub = (TOP_K * tc + n_exp * (MOE_SUB - 1)) // MOE_SUB
    min_sub = -(-TOP_K * tc // MOE_SUB)
    null_row0 = (max_sub + 1) * MOE_SUB
    half_len = -(-(null_row0 + MOE_SUB + 1) // LANES) * LANES
    place_per_step = -(-tc // min_sub)
    assert max_sub * place_per_step <= seg
    return null_row0, half_len, place_per_step


def experts(sb_start, n_sub, tail_rows, xp, dest, gates, w_up, b_up, w_down, b_down, tc):
    n, tiles, _ = xp.shape
    d = tiles * LANES
    n_exp = w_up.shape[0]
    chunks, padded_len = dest.shape
    seg = padded_len // TOP_K
    null_row0, half_len, place_per_step = _row_map_geometry(tc, n_exp, seg)
    fill = jnp.full((half_len,), tc, i32)
    kern = functools.partial(_experts_kernel, tc=tc, n_exp=n_exp, d=d, null_row0=null_row0, seg=seg,
                             half_len=half_len, place_per_step=place_per_step)
    block = pltpu.VMEM((MOE_SUB * tiles, LANES), f32)
    return pl.pallas_call(
        kern,
        out_shape=jax.ShapeDtypeStruct((n, tiles, LANES), f32),
        grid_spec=pltpu.PrefetchScalarGridSpec(
            num_scalar_prefetch=3,
            grid=(chunks, n_exp),
            in_specs=[
                pl.BlockSpec(memory_space=pl.ANY),
                pl.BlockSpec(memory_space=pl.ANY),
                pl.BlockSpec(memory_space=pl.ANY),
                pl.BlockSpec(memory_space=pl.ANY),
                pl.BlockSpec((1, d, 2 * d), lambda c, e, *_: (e, 0, 0)),
                pl.BlockSpec((1, 1, 2 * d), lambda c, e, *_: (e, 0, 0)),
                pl.BlockSpec((1, d, d), lambda c, e, *_: (e, 0, 0)),
                pl.BlockSpec((1, 1, d), lambda c, e, *_: (e, 0, 0)),
            ],
            out_specs=pl.BlockSpec(memory_space=pl.ANY),
            scratch_shapes=[
                pltpu.VMEM((tc + SUBLANES, tiles, LANES), f32),
                pltpu.VMEM((tc + SUBLANES, tiles, LANES), f32),
                block, block, block, block,
                pltpu.SMEM((padded_len,), i32),
                pltpu.SMEM((padded_len,), f32),
                pltpu.SMEM((2 * half_len,), i32),
                pltpu.SemaphoreType.DMA((4,)),
            ],
        ),
        compiler_params=_cparams(("arbitrary", "arbitrary"), 58),
        name="experts",
    )(sb_start, n_sub, tail_rows, xp, dest, gates, fill, w_up, b_up.reshape(n_exp, 1, 2 * d), w_down,
      b_down.reshape(n_exp, 1, d))


def _final_kernel(h_ref, f_ref, g_ref, o_ref):
    tm, d = h_ref.shape
    tiles = d // LANES
    parts = [h_ref[:, jt * LANES:(jt + 1) * LANES] + f_ref[pl.ds(jt, tm, stride=tiles), :] for jt in range(tiles)]
    o_ref[...] = _rms(jnp.concatenate(parts, axis=1), g_ref[...])


def final_norm(h, f2, g, row0, rows):
    d = h.shape[1]
    tiles = d // LANES
    tm = min(ROW_TILE, rows)
    blk0 = row0 // tm
    assert rows % tm == 0 and row0 % tm == 0
    return pl.pallas_call(
        _final_kernel,
        out_shape=jax.ShapeDtypeStruct((rows, d), f32),
        grid=(rows // tm,),
        in_specs=[
            pl.BlockSpec((tm, d), lambda i: (blk0 + i, 0)),
            pl.BlockSpec((tm * tiles, LANES), lambda i: (blk0 + i, 0)),
            pl.BlockSpec((1, d), lambda i: (0, 0)),
        ],
        out_specs=pl.BlockSpec((tm, d), lambda i: (i, 0)),
        compiler_params=_cparams(("arbitrary",), 32),
        name="final_norm",
    )(h, f2, g.reshape(1, d))


def moe(h, g, w_router, b_router, w_up_p, b_up, w_down_b, b_down):
    n, d = h.shape
    n_exp = w_router.shape[1]
    tc = n // MOE_CHUNKS
    xp, top_e, gates, rank, counts = router(h, g, w_router, b_router, tc)
    n_sub = (counts.reshape(MOE_CHUNKS, n_exp) + MOE_SUB - 1) // MOE_SUB
    sb_start = jnp.cumsum(n_sub, axis=1) - n_sub
    hot = top_e.reshape(TOP_K, MOE_CHUNKS, tc, 1) == jnp.arange(n_exp, dtype=i32)
    row_base = (sb_start * MOE_SUB).reshape(1, MOE_CHUNKS, 1, n_exp)
    dest = jnp.sum(jnp.where(hot, row_base, 0), axis=-1) + rank.reshape(TOP_K, MOE_CHUNKS, tc)
    seg = 1 << tc.bit_length()
    spare_row = _row_map_geometry(tc, n_exp, seg)[1] - 1
    per_chunk = lambda a, tail: jnp.pad(a.transpose(1, 0, 2), ((0, 0), (0, 0), (0, seg - tc)),
                                        constant_values=tail).reshape(MOE_CHUNKS, TOP_K * seg)
    tiles = d // LANES
    b_up_p = b_up.reshape(n_exp, tiles, LANES, 2).transpose(0, 1, 3, 2).reshape(n_exp, 2 * d)
    tail_rows = counts.reshape(MOE_CHUNKS, n_exp) - (n_sub - 1) * MOE_SUB
    flat = lambda a: a.reshape(-1).astype(i32)
    f = experts(flat(sb_start), flat(n_sub), flat(tail_rows), xp.reshape(n, tiles, LANES),
                per_chunk(dest, spare_row), per_chunk(gates.reshape(TOP_K, MOE_CHUNKS, tc), 0.0), w_up_p, b_up_p,
                w_down_b, b_down, tc)
    return f.reshape(n * tiles, LANES)


def kernel(x_prompt, x_sample, mem_prompt, state_ret, state_conv, cache_mem_k, cache_mem_v, norm_mix, w_in, conv_w,
           ret_gn, w_out, norm_cross, norm_mem, w_mq, w_mk, w_mv, w_mo, norm_ffn, w_router, b_router, w_up, b_up,
           w_down, b_down, norm_final):
    batch, seq, d = x_prompt.shape
    n_seq, ts, _ = x_sample.shape
    depth = w_in.shape[0]
    n_mem = mem_prompt.shape[1]
    r = ret_gn.shape[1]
    cd_width = conv_w.shape[2]
    dh = r // N_RET_HEADS
    n_p = batch * seq
    n_s = n_seq * ts
    n = n_p + n_s
    nblk = n_seq // SAMPLE_BLOCK
    assert seq % ROW_TILE == 0 and n % ROW_TILE == 0 and n % MOE_CHUNKS == 0

    xs = x_sample.reshape(nblk, SAMPLE_BLOCK, ts, d).transpose(0, 2, 1, 3).reshape(n_s, d)
    h_p, h_s = x_prompt.reshape(n_p, d), xs
    tab_p = prompt_tables(seq, dh)
    tab_s = sample_tables(ts, float(PAST_LEN), dh)

    ret_p, conv_p, mk_p, mv_p, ret_s, conv_s = [], [], [], [], [], []
    for l in range(depth):
        w_in_b, w_out_b, w_mq_b = w_in[l].astype(bf16), w_out[l].astype(bf16), w_mq[l].astype(bf16)
        h_p, s_p, c_p, w_down_b = prompt_mixer(h_p, norm_mix[l], w_in_b, tab_p, conv_w[l], ret_gn[l], w_out_b,
                                               batch, seq, r, cd_width, w_down[l])
        proj_s = norm_matmul(h_s, norm_mix[l], w_in_b, "in_proj_sample")
        mix_s, s_s, c_s = sample_mixer(proj_s, tab_s, conv_w[l], ret_gn[l], state_ret[l],
                                       state_conv[l].transpose(1, 0, 2), 0, ts, r, cd_width)
        h_s = matmul_res(mix_s, w_out_b, h_s, "out_proj_sample")

        mk, mv = mem_kv(mem_prompt, norm_mem[l], jnp.concatenate([w_mk[l], w_mv[l]], axis=1).astype(bf16),
                        N_MEM_HEADS)
        q_s = norm_matmul(h_s, norm_cross[l], w_mq_b, "q_sample")
        attn_s = cross_sample(q_s, cache_mem_k[l], cache_mem_v[l], 0, ts)
        h, w_up_p = attn_block(h_p, h_s, norm_cross[l], w_mq_b, mk, mv, attn_s, w_mo[l].astype(bf16), seq, w_up[l])

        f2 = moe(h, norm_ffn[l], w_router[l], b_router[l], w_up_p, b_up[l], w_down_b, b_down[l])
        if l + 1 < depth:
            h = h + f2.reshape(n, d // LANES, LANES).reshape(n, d)
            h_p, h_s = h, h[n_p:]

        ret_p.append(s_p)
        conv_p.append(c_p)
        mk_p.append(mk)
        mv_p.append(mv)
        ret_s.append(s_s)
        conv_s.append(c_s.transpose(1, 0, 2))

    y_p = final_norm(h, f2, norm_final, 0, n_p).reshape(batch, seq, d)
    y_s = final_norm(h, f2, norm_final, n_p, n_s)
    y_s = y_s.reshape(nblk, ts, SAMPLE_BLOCK, d).transpose(0, 2, 1, 3).reshape(n_seq, ts, d)
    return (y_p, y_s, jnp.stack(ret_p), jnp.stack(conv_p), jnp.stack(mk_p), jnp.stack(mv_p), jnp.stack(ret_s),
            jnp.stack(conv_s))
```

```python
import functools

import jax
import jax.numpy as jnp
from jax import lax
from jax.experimental import pallas as pl
from jax.experimental.pallas import tpu as pltpu

f32 = jnp.float32
bf16 = jnp.bfloat16
i32 = jnp.int32

EPS = 1e-6
ROPE_THETA = 10000.0
RET_CHUNK = 128
PAST_LEN = 16384
N_RET_HEADS = 4
N_MEM_HEADS = 4
TOP_K = 4
SWIGLU_ALPHA = 1.702
SWIGLU_LIMIT = 7.0

LANES = 128
SUBLANES = 8
MIB = 1024 * 1024

ROW_TILE = 512
SAMPLE_BLOCK = 8
MOE_CHUNKS = 4
MOE_SUB = 256
SCATTER_UNROLL = 8


def _cparams(sem, vmem_mib):
    return pltpu.CompilerParams(dimension_semantics=sem, vmem_limit_bytes=vmem_mib * MIB)


def _rms(x, g):
    ms = jnp.mean(x * x, axis=-1, keepdims=True)
    return (x * lax.rsqrt(ms + EPS)) * g


def _norm_matmul_kernel(x_ref, g_ref, w_ref, o_ref):
    xn = _rms(x_ref[...], g_ref[...])
    o_ref[...] = jnp.dot(xn.astype(bf16), w_ref[...], preferred_element_type=f32)


def norm_matmul(x, g, w, name):
    m, d = x.shape
    f = w.shape[1]
    tm = min(ROW_TILE, m)
    assert m % tm == 0
    return pl.pallas_call(
        _norm_matmul_kernel,
        out_shape=jax.ShapeDtypeStruct((m, f), f32),
        grid=(m // tm,),
        in_specs=[
            pl.BlockSpec((tm, d), lambda i: (i, 0)),
            pl.BlockSpec((1, d), lambda i: (0, 0)),
            pl.BlockSpec((d, f), lambda i: (0, 0)),
        ],
        out_specs=pl.BlockSpec((tm, f), lambda i: (i, 0)),
        compiler_params=_cparams(("arbitrary",), 48),
        name=name,
    )(x, g.reshape(1, d), w)


def _stacked_specs(m_p, m_s, width):
    tm = ROW_TILE
    assert m_p % tm == 0 and m_s == tm
    prompt_tiles = m_p // tm
    return prompt_tiles, [
        pl.BlockSpec((tm, width), lambda i: (jnp.minimum(i, prompt_tiles - 1), 0)),
        pl.BlockSpec((tm, width), lambda i: (0, 0)),
    ]


def _matmul_res_kernel(a_ref, w_ref, r_ref, o_ref):
    o_ref[...] = r_ref[...] + jnp.dot(a_ref[...], w_ref[...], preferred_element_type=f32)


def matmul_res(a, w, res, name):
    m, d = a.shape
    f = w.shape[1]
    tm = min(ROW_TILE, m)
    assert m % tm == 0
    return pl.pallas_call(
        _matmul_res_kernel,
        out_shape=jax.ShapeDtypeStruct((m, f), f32),
        grid=(m // tm,),
        in_specs=[
            pl.BlockSpec((tm, d), lambda i: (i, 0)),
            pl.BlockSpec((d, f), lambda i: (0, 0)),
            pl.BlockSpec((tm, f), lambda i: (i, 0)),
        ],
        out_specs=pl.BlockSpec((tm, f), lambda i: (i, 0)),
        compiler_params=_cparams(("arbitrary",), 32),
        name=name,
    )(a, w, res)


def _rope(x, cos, sin_signed, half):
    return x * cos + pltpu.roll(x, half, 1) * sin_signed


def _group_norm_gate(o, gate, gn):
    mu = jnp.mean(o, axis=-1, keepdims=True)
    var = jnp.mean(jnp.square(o - mu), axis=-1, keepdims=True)
    on = ((o - mu) * lax.rsqrt(var + EPS)) * gn
    return (gate * jax.nn.sigmoid(gate)) * on


def _prompt_mixer_kernel(cd_ref, x_ref, g_ref, win_ref, cos_ref, sin_ref, decay_ref, xi_ref, zeta_ref, convw_ref,
                         gn_ref, wout_ref, wcast_ref, h_ref, sfin_ref, cfin_ref, wcast_out_ref, s_scr, u_scr, proj_ref,
                         mix_ref, *, tt, r, cd_width, dh):
    wcast_out_ref[...] = wcast_ref[...].astype(bf16)
    j = pl.program_id(1)
    heads = r // dh
    pad = SUBLANES

    @pl.when(j == 0)
    def _():
        s_scr[...] = jnp.zeros_like(s_scr)
        u_scr[0:pad, :] = jnp.zeros((pad, cd_width), f32)

    proj_ref[...] = jnp.dot(_rms(x_ref[...], g_ref[...]).astype(bf16), win_ref[...], preferred_element_type=f32)

    k_scale = dh ** -0.5
    for c in range(tt // RET_CHUNK):
        rows = pl.ds(c * RET_CHUNK, RET_CHUNK)
        cosb = cos_ref[rows, :]
        sinb = sin_ref[rows, :]
        for h in range(heads):
            lo, hi = h * dh, (h + 1) * dh
            qh = _rope(proj_ref[rows, lo:hi], cosb, sinb, dh // 2)
            kh = _rope(proj_ref[rows, r + lo:r + hi], cosb, sinb, dh // 2) * k_scale
            vb = proj_ref[rows, 2 * r + lo:2 * r + hi].astype(bf16)
            gate = proj_ref[rows, 3 * r + lo:3 * r + hi]
            qb = qh.astype(bf16)
            kb = kh.astype(bf16)
            scores = lax.dot_general(qb, kb, (((1,), (1,)), ((), ())), preferred_element_type=f32) * decay_ref[h]
            inner = jnp.dot(scores.astype(bf16), vb, preferred_element_type=f32)
            s_prev = s_scr[h]
            cross = jnp.dot((qh * xi_ref[h]).astype(bf16), s_prev.astype(bf16), preferred_element_type=f32)
            kv = lax.dot_general((kh * zeta_ref[h]).astype(bf16), vb, (((0,), (0,)), ((), ())),
                                 preferred_element_type=f32)
            s_scr[h] = cd_ref[h] * s_prev + kv
            mix_ref[rows, lo:hi] = _group_norm_gate(inner + cross, gate, gn_ref[:, lo:hi]).astype(bf16)

    base = 4 * r
    u = proj_ref[:, base + cd_width:base + 2 * cd_width] * proj_ref[:, base:base + cd_width]
    u_scr[pad:pad + tt, :] = u
    conv = (u_scr[pad - 2:pad - 2 + tt, :] * convw_ref[0:1, :] + u_scr[pad - 1:pad - 1 + tt, :] * convw_ref[1:2, :]
            + u * convw_ref[2:3, :])
    mix_ref[:, r:r + cd_width] = (proj_ref[:, base + 2 * cd_width:base + 3 * cd_width] * conv).astype(bf16)
    u_scr[0:pad, :] = u_scr[tt:tt + pad, :]
    cfin_ref[0] = u_scr[pad - 2:pad, :]
    h_ref[...] = x_ref[...] + jnp.dot(mix_ref[...], wout_ref[...], preferred_element_type=f32)

    @pl.when(j == pl.num_programs(1) - 1)
    def _():
        sfin_ref[0] = s_scr[...]


def prompt_mixer(x, g, w_in, tables, conv_w, ret_gn, w_out, batch, seq, r, cd_width, w_cast):
    tt = ROW_TILE
    d = x.shape[1]
    dh = r // N_RET_HEADS
    steps = seq // tt
    in_cols = w_in.shape[1]
    assert w_cast.shape[0] == batch * steps
    kern = functools.partial(_prompt_mixer_kernel, tt=tt, r=r, cd_width=cd_width, dh=dh)
    tab = lambda shape: pl.BlockSpec(shape, lambda b, j, cd: (0,) * len(shape))
    row_tile = pl.BlockSpec((tt, d), lambda b, j, cd: (b * steps + j, 0))
    cast_slice = pl.BlockSpec((1,) + w_cast.shape[1:], lambda b, j, cd: (b * steps + j, 0, 0))
    return pl.pallas_call(
        kern,
        out_shape=(
            jax.ShapeDtypeStruct((batch * seq, d), f32),
            jax.ShapeDtypeStruct((batch, N_RET_HEADS, dh, dh), f32),
            jax.ShapeDtypeStruct((batch, 2, cd_width), f32),
            jax.ShapeDtypeStruct(w_cast.shape, bf16),
        ),
        grid_spec=pltpu.PrefetchScalarGridSpec(
            num_scalar_prefetch=1,
            grid=(batch, steps),
            in_specs=[
                row_tile,
                tab((1, d)),
                tab((d, in_cols)),
                pl.BlockSpec((tt, dh), lambda b, j, cd: (j, 0)),
                pl.BlockSpec((tt, dh), lambda b, j, cd: (j, 0)),
                tab((N_RET_HEADS, RET_CHUNK, RET_CHUNK)),
                tab((N_RET_HEADS, RET_CHUNK, dh)),
                tab((N_RET_HEADS, RET_CHUNK, dh)),
                tab((3, cd_width)),
                tab((1, r)),
                tab((r + cd_width, d)),
                cast_slice,
            ],
            out_specs=(
                row_tile,
                pl.BlockSpec((1, N_RET_HEADS, dh, dh), lambda b, j, cd: (b, 0, 0, 0)),
                pl.BlockSpec((1, 2, cd_width), lambda b, j, cd: (b, 0, 0)),
                cast_slice,
            ),
            scratch_shapes=[
                pltpu.VMEM((N_RET_HEADS, dh, dh), f32),
                pltpu.VMEM((tt + 2 * SUBLANES, cd_width), f32),
                pltpu.VMEM((tt, in_cols), f32),
                pltpu.VMEM((tt, r + cd_width), bf16),
            ],
        ),
        compiler_params=_cparams(("arbitrary", "arbitrary"), 56),
        name="prompt_mixer",
    )(tables["chunk_decay"], x, g.reshape(1, d), w_in, tables["cos"], tables["sin"], tables["decay"], tables["xi"],
      tables["zeta"], conv_w, ret_gn.reshape(1, r), w_out, w_cast)


def _sample_mixer_kernel(dec_ref, proj_ref, cos_ref, sin_ref, xi_ref, zeta_ref, convw_ref, gn_ref, s0_ref, c0_ref,
                         mix_ref, snew_ref, cnew_ref, *, ts, r, cd_width, dh):
    heads = r // dh
    nb = SAMPLE_BLOCK
    rows = ts * nb
    k_scale = dh ** -0.5
    cosb = cos_ref[...]
    sinb = sin_ref[...]
    seq_of_row = lax.broadcasted_iota(i32, (rows, dh), 0) % nb
    for h in range(heads):
        lo, hi = h * dh, (h + 1) * dh
        qh = _rope(proj_ref[:, lo:hi], cosb, sinb, dh // 2)
        kh = _rope(proj_ref[:, r + lo:r + hi], cosb, sinb, dh // 2) * k_scale
        vh = proj_ref[:, 2 * r + lo:2 * r + hi]
        gate = proj_ref[:, 3 * r + lo:3 * r + hi]
        inner = []
        for t in range(ts):
            qt = qh[t * nb:(t + 1) * nb]
            acc = jnp.zeros((nb, dh), f32)
            for s in range(t + 1):
                sc = jnp.sum(qt * kh[s * nb:(s + 1) * nb], axis=-1, keepdims=True) * dec_ref[h * (ts + 1) + t - s]
                acc = acc + sc * vh[s * nb:(s + 1) * nb]
            inner.append(acc)
        inner = jnp.concatenate(inner, axis=0)
        qx = (qh * xi_ref[h]).astype(bf16)
        kz = kh * zeta_ref[h]
        vb = vh.astype(bf16)
        cross = jnp.zeros((rows, dh), f32)
        for b in range(nb):
            mine = seq_of_row == b
            s_prev = s0_ref[b, h]
            res = jnp.dot(qx, s_prev.astype(bf16), preferred_element_type=f32)
            cross = cross + jnp.where(mine, res, 0.0)
            kv = lax.dot_general(jnp.where(mine, kz, 0.0).astype(bf16), vb, (((0,), (0,)), ((), ())),
                                 preferred_element_type=f32)
            snew_ref[b, h] = dec_ref[h * (ts + 1) + ts] * s_prev + kv
        mix_ref[:, lo:hi] = _group_norm_gate(inner + cross, gate, gn_ref[:, lo:hi]).astype(bf16)

    base = 4 * r
    u = proj_ref[:, base + cd_width:base + 2 * cd_width] * proj_ref[:, base:base + cd_width]
    full = [c0_ref[0], c0_ref[1]] + [u[t * nb:(t + 1) * nb] for t in range(ts)]
    conv = jnp.concatenate(
        [full[t] * convw_ref[0:1, :] + full[t + 1] * convw_ref[1:2, :] + full[t + 2] * convw_ref[2:3, :]
         for t in range(ts)], axis=0)
    mix_ref[:, r:r + cd_width] = (proj_ref[:, base + 2 * cd_width:base + 3 * cd_width] * conv).astype(bf16)
    cnew_ref[0] = full[ts]
    cnew_ref[1] = full[ts + 1]


def sample_mixer(proj, tables, conv_w, ret_gn, state_ret, state_conv_t, row0, ts, r, cd_width):
    n_seq = state_ret.shape[0]
    dh = r // N_RET_HEADS
    rows = ts * SAMPLE_BLOCK
    blk0 = row0 // rows
    assert row0 % rows == 0 and n_seq % SAMPLE_BLOCK == 0
    in_cols = proj.shape[1]
    kern = functools.partial(_sample_mixer_kernel, ts=ts, r=r, cd_width=cd_width, dh=dh)
    tab = lambda shape: pl.BlockSpec(shape, lambda i, d: (0,) * len(shape))
    return pl.pallas_call(
        kern,
        out_shape=(
            jax.ShapeDtypeStruct((n_seq * ts, r + cd_width), bf16),
            jax.ShapeDtypeStruct(state_ret.shape, f32),
            jax.ShapeDtypeStruct(state_conv_t.shape, f32),
        ),
        grid_spec=pltpu.PrefetchScalarGridSpec(
            num_scalar_prefetch=1,
            grid=(n_seq // SAMPLE_BLOCK,),
            in_specs=[
                pl.BlockSpec((rows, in_cols), lambda i, d: (blk0 + i, 0)),
                tab((rows, dh)),
                tab((rows, dh)),
                tab((N_RET_HEADS, rows, dh)),
                tab((N_RET_HEADS, rows, dh)),
                tab((3, cd_width)),
                tab((1, r)),
                pl.BlockSpec((SAMPLE_BLOCK, N_RET_HEADS, dh, dh), lambda i, d: (i, 0, 0, 0)),
                pl.BlockSpec((2, SAMPLE_BLOCK, cd_width), lambda i, d: (0, i, 0)),
            ],
            out_specs=(
                pl.BlockSpec((rows, r + cd_width), lambda i, d: (i, 0)),
                pl.BlockSpec((SAMPLE_BLOCK, N_RET_HEADS, dh, dh), lambda i, d: (i, 0, 0, 0)),
                pl.BlockSpec((2, SAMPLE_BLOCK, cd_width), lambda i, d: (0, i, 0)),
            ),
        ),
        compiler_params=_cparams(("arbitrary",), 32),
        name="sample_mixer",
    )(tables["dec"], proj, tables["cos"], tables["sin"], tables["xi"], tables["zeta"], conv_w, ret_gn.reshape(1, r),
      state_ret, state_conv_t)


def _log_gamma():
    return jnp.log1p(-jnp.exp2(-5.0 - jnp.arange(N_RET_HEADS, dtype=f32)))


def _rope_tables(pos, dh):
    half = dh // 2
    inv = ROPE_THETA ** (-jnp.arange(half, dtype=f32) / half)
    ang = pos[:, None] * inv[None, :]
    cos, sin = jnp.cos(ang), jnp.sin(ang)
    return jnp.concatenate([cos, cos], axis=-1), jnp.concatenate([-sin, sin], axis=-1)


def prompt_tables(seq, dh):
    c = RET_CHUNK
    lg = _log_gamma()
    i = jnp.arange(c, dtype=f32)
    diff = i[:, None] - i[None, :]
    decay = jnp.where(diff >= 0, jnp.exp(jnp.maximum(diff, 0.0)[None] * lg[:, None, None]), 0.0)
    xi = jnp.exp((i + 1.0)[None, :] * lg[:, None])
    zeta = jnp.exp((c - 1.0 - i)[None, :] * lg[:, None])
    cos, sin = _rope_tables(jnp.arange(seq, dtype=f32), dh)
    bc = lambda t: jnp.broadcast_to(t[:, :, None], (N_RET_HEADS, c, dh))
    return dict(cos=cos, sin=sin, decay=decay, xi=bc(xi), zeta=bc(zeta), chunk_decay=jnp.exp(c * lg))


def sample_tables(ts, pos0, dh):
    lg = _log_gamma()
    i = jnp.arange(ts, dtype=f32)
    dec = jnp.exp(jnp.arange(ts + 1, dtype=f32)[None, :] * lg[:, None])
    xi = jnp.exp((i + 1.0)[None, :] * lg[:, None])
    zeta = jnp.exp((ts - 1.0 - i)[None, :] * lg[:, None])
    cos, sin = _rope_tables(pos0 + i, dh)
    rep = lambda t: jnp.repeat(t, SAMPLE_BLOCK, axis=0)
    bc = lambda t: jnp.broadcast_to(jnp.repeat(t, SAMPLE_BLOCK, axis=1)[:, :, None],
                                    (N_RET_HEADS, ts * SAMPLE_BLOCK, dh))
    return dict(cos=rep(cos), sin=rep(sin), xi=bc(xi), zeta=bc(zeta), dec=dec.reshape(-1))


def _softmax_rows(s):
    m = jnp.max(s, axis=-1, keepdims=True)
    p = jnp.exp(s - m)
    return p / jnp.sum(p, axis=-1, keepdims=True)


def _from_slabs(ref, rows=None):
    rows = slice(None) if rows is None else rows
    return jnp.concatenate([ref[lt, rows, :] for lt in range(ref.shape[0])], axis=1)


def _to_slabs(ref, value):
    for lt in range(ref.shape[0]):
        ref[lt] = value[:, lt * LANES:(lt + 1) * LANES]


def _mem_kv_kernel(m_ref, g_ref, w_ref, k_ref, v_ref, rows_ref, *, heads):
    n_mem, d = m_ref.shape
    dh = d // heads
    kv = jnp.dot(_rms(m_ref[...], g_ref[...]).astype(bf16), w_ref[...], preferred_element_type=f32)
    for part, o_ref in enumerate((k_ref, v_ref)):
        for h in range(heads):
            for lt in range(dh // LANES):
                lo = part * d + h * dh + lt * LANES
                rows_ref[lt, pl.ds(h, n_mem, stride=heads), :] = kv[:, lo:lo + LANES]
        o_ref[0] = _from_slabs(rows_ref).reshape(n_mem, heads, dh)


def mem_kv(mem, g, w_kv, heads):
    batch, n_mem, d = mem.shape
    dh = d // heads
    out = jax.ShapeDtypeStruct((batch, n_mem, heads, dh), f32)
    out_spec = pl.BlockSpec((1, n_mem, heads, dh), lambda b: (b, 0, 0, 0))
    return pl.pallas_call(
        functools.partial(_mem_kv_kernel, heads=heads),
        out_shape=(out, out),
        grid=(batch,),
        in_specs=[
            pl.BlockSpec((n_mem, d), lambda b: (b, 0)),
            pl.BlockSpec((1, d), lambda b: (0, 0)),
            pl.BlockSpec((d, 2 * d), lambda b: (0, 0)),
        ],
        out_specs=(out_spec, out_spec),
        scratch_shapes=[pltpu.VMEM((dh // LANES, n_mem * heads, LANES), f32)],
        compiler_params=_cparams(("arbitrary",), 32),
        name="mem_kv",
    )(mem.reshape(batch * n_mem, d), g.reshape(1, d), w_kv)


def _attn_block_kernel(hp_ref, hs_ref, g_ref, wq_ref, k_ref, v_ref, as_ref, wo_ref, wup_ref, perm_ref, o_ref,
                       wup_out_ref, k_rows, v_rows, *, prompt_tiles, heads):
    i = pl.program_id(0)

    @pl.when(i < prompt_tiles)
    def _():
        _split_pairs(wup_ref, perm_ref, wup_out_ref)
        h_ref = hp_ref
        n_mem, _, dh = k_ref.shape[1:]
        scale = dh ** -0.5
        q = jnp.dot(_rms(h_ref[...], g_ref[...]).astype(bf16), wq_ref[...], preferred_element_type=f32)
        _to_slabs(k_rows, k_ref[0].reshape(n_mem * heads, dh))
        _to_slabs(v_rows, v_ref[0].reshape(n_mem * heads, dh))
        outs = []
        for h in range(heads):
            kh = _from_slabs(k_rows, pl.ds(h, n_mem, stride=heads)).astype(bf16)
            vh = _from_slabs(v_rows, pl.ds(h, n_mem, stride=heads)).astype(bf16)
            s = lax.dot_general(q[:, h * dh:(h + 1) * dh].astype(bf16), kh, (((1,), (1,)), ((), ())),
                                preferred_element_type=f32) * scale
            outs.append(jnp.dot(_softmax_rows(s).astype(bf16), vh, preferred_element_type=f32).astype(bf16))
        attn = jnp.concatenate(outs, axis=1)
        o_ref[...] = h_ref[...] + jnp.dot(attn, wo_ref[...], preferred_element_type=f32)

    @pl.when(i == prompt_tiles)
    def _():
        o_ref[...] = hs_ref[...] + jnp.dot(as_ref[...], wo_ref[...], preferred_element_type=f32)


def attn_block(h_prompt, h_sample, g, w_mq, mk, mv, attn_sample, w_mo, seq, w_up):
    d = h_prompt.shape[1]
    batch, n_mem, heads, dh = mk.shape
    tm = ROW_TILE
    per_batch = seq // tm
    prompt_tiles, h_specs = _stacked_specs(batch * seq, h_sample.shape[0], d)
    n_exp, _, f2 = w_up.shape
    assert attn_sample.shape == (tm, d) and n_exp == prompt_tiles
    prompt_tile = lambda i: jnp.minimum(i, prompt_tiles - 1)
    kv_spec = pl.BlockSpec((1, n_mem, heads, dh), lambda i: (prompt_tile(i) // per_batch, 0, 0, 0))
    wup_spec = pl.BlockSpec((1, d, f2), lambda i: (prompt_tile(i), 0, 0))
    const = lambda shape: pl.BlockSpec(shape, lambda i: (0,) * len(shape))
    perm = pair_split_matrix()
    return pl.pallas_call(
        functools.partial(_attn_block_kernel, prompt_tiles=prompt_tiles, heads=heads),
        out_shape=(
            jax.ShapeDtypeStruct(((prompt_tiles + 1) * tm, d), f32),
            jax.ShapeDtypeStruct(w_up.shape, bf16),
        ),
        grid=(prompt_tiles + 1,),
        in_specs=h_specs + [
            const((1, d)),
            const((d, d)),
            kv_spec,
            kv_spec,
            const((tm, d)),
            const((d, d)),
            wup_spec,
            const(perm.shape),
        ],
        out_specs=(pl.BlockSpec((tm, d), lambda i: (i, 0)), wup_spec),
        scratch_shapes=[pltpu.VMEM((dh // LANES, n_mem * heads, LANES), f32)] * 2,
        compiler_params=_cparams(("arbitrary",), 56),
        name="attn_block",
    )(h_prompt, h_sample, g.reshape(1, d), w_mq, mk, mv, attn_sample, w_mo, w_up, perm)


def _cross_sample_kernel(q_ref, k_ref, v_ref, o_ref, acc_ref, *, dh, seqs):
    part = pl.program_id(1)
    rows = q_ref.shape[0]
    scale = dh ** -0.5

    @pl.when(part == 0)
    def _():
        acc_ref[...] = jnp.zeros_like(acc_ref)

    heads = N_MEM_HEADS
    n_mem = k_ref.shape[1]
    qx = jnp.concatenate([q_ref[:, h * dh:(h + 1) * dh] for h in range(heads)], axis=0).astype(bf16)
    row = lax.broadcasted_iota(i32, (heads * rows, n_mem * heads), 0)
    col = lax.broadcasted_iota(i32, (heads * rows, n_mem * heads), 1)
    same_head = col % heads == row // rows
    seq_of_row = lax.broadcasted_iota(i32, (heads * rows, dh), 0) % SAMPLE_BLOCK
    out = jnp.zeros((heads * rows, dh), f32)
    for b in range(seqs):
        k2 = k_ref[b].reshape(n_mem * heads, dh).astype(bf16)
        v2 = v_ref[b].reshape(n_mem * heads, dh).astype(bf16)
        s = lax.dot_general(qx, k2, (((1,), (1,)), ((), ())), preferred_element_type=f32) * scale
        p = _softmax_rows(jnp.where(same_head, s, -jnp.inf))
        o = jnp.dot(p.astype(bf16), v2, preferred_element_type=f32)
        out = out + jnp.where(seq_of_row == part * seqs + b, o, 0.0)
    acc_ref[...] += out

    @pl.when(part == pl.num_programs(1) - 1)
    def _():
        for h in range(heads):
            o_ref[:, h * dh:(h + 1) * dh] = acc_ref[h * rows:(h + 1) * rows, :].astype(bf16)


def cross_sample(q, cache_k, cache_v, row0, ts):
    n_seq, n_mem, heads, dh = cache_k.shape
    d = heads * dh
    rows = ts * SAMPLE_BLOCK
    blk0 = row0 // rows
    parts = 2
    seqs = SAMPLE_BLOCK // parts
    kern = functools.partial(_cross_sample_kernel, dh=dh, seqs=seqs)
    return pl.pallas_call(
        kern,
        out_shape=jax.ShapeDtypeStruct((n_seq * ts, d), bf16),
        grid=(n_seq // SAMPLE_BLOCK, parts),
        in_specs=[
            pl.BlockSpec((rows, d), lambda i, p: (blk0 + i, 0)),
            pl.BlockSpec((seqs, n_mem, heads, dh), lambda i, p: (i * parts + p, 0, 0, 0)),
            pl.BlockSpec((seqs, n_mem, heads, dh), lambda i, p: (i * parts + p, 0, 0, 0)),
        ],
        out_specs=pl.BlockSpec((rows, d), lambda i, p: (i, 0)),
        scratch_shapes=[pltpu.VMEM((heads * rows, dh), f32)],
        compiler_params=_cparams(("arbitrary", "arbitrary"), 40),
        name="cross_sample",
    )(q, cache_k, cache_v)


def _router_kernel(h_ref, g_ref, wr_ref, br_ref, xp_ref, e_ref, gate_ref, rank_ref, cnt_ref, carry_ref,
                   *, tiles_per_chunk, n_exp):
    i = pl.program_id(0)

    @pl.when(i % tiles_per_chunk == 0)
    def _():
        carry_ref[...] = jnp.zeros_like(carry_ref)

    xn = _rms(h_ref[...], g_ref[...])
    tm, d = xn.shape
    tiles = d // LANES
    for rg in range(tm // SUBLANES):
        for jt in range(tiles):
            xp_ref[pl.ds(rg * SUBLANES * tiles + jt, SUBLANES, stride=tiles), :] = (
                xn[rg * SUBLANES:(rg + 1) * SUBLANES, jt * LANES:(jt + 1) * LANES])

    w = wr_ref[...]
    w_hi = w.astype(bf16)
    w_lo = (w - w_hi.astype(f32)).astype(bf16)
    x_hi = xn.astype(bf16)
    x_lo = (xn - x_hi.astype(f32)).astype(bf16)
    nt = (((1,), (1,)), ((), ()))
    both = lax.dot_general(jnp.concatenate([w_hi, w_lo], axis=0), x_hi, nt, preferred_element_type=f32)
    logits = (both[:n_exp] + both[n_exp:] + lax.dot_general(w_hi, x_lo, nt, preferred_element_type=f32)
              + br_ref[...])
    sub = lax.broadcasted_iota(i32, (n_exp, tm), 0).astype(f32)
    chosen, vals, hots = [], [], []
    work = logits
    for _ in range(TOP_K):
        m = jnp.max(work, axis=0, keepdims=True)
        idx = jnp.min(jnp.where(work == m, sub, float(n_exp)), axis=0, keepdims=True)
        hot = sub == idx
        chosen.append(idx)
        vals.append(m)
        hots.append(hot)
        work = jnp.where(hot, -jnp.inf, work)
    ex = [jnp.exp(v - vals[0]) for v in vals]
    denom = ex[0] + ex[1] + ex[2] + ex[3]
    e_ref[...] = jnp.concatenate(chosen, axis=0).astype(i32)
    gate_ref[...] = jnp.concatenate([x / denom for x in ex], axis=0)

    member = jnp.zeros((n_exp, tm), f32)
    for hot in hots:
        member = member + hot.astype(f32)
    earlier = (lax.broadcasted_iota(i32, (tm, tm), 0) < lax.broadcasted_iota(i32, (tm, tm), 1)).astype(bf16)
    before = jnp.dot(member.astype(bf16), earlier, preferred_element_type=f32) + carry_ref[...]
    rank_ref[...] = jnp.concatenate(
        [jnp.sum(jnp.where(hot, before, 0.0), axis=0, keepdims=True) for hot in hots], axis=0).astype(i32)
    carry_ref[...] += jnp.sum(member, axis=1, keepdims=True)
    cnt_ref[0] = carry_ref[...].astype(i32)


def router(h, g, w_router, b_router, tc):
    n, d = h.shape
    n_exp = w_router.shape[1]
    tiles = d // LANES
    tm = ROW_TILE
    while tc % tm:
        tm -= LANES
    tiles_per_chunk = tc // tm
    kern = functools.partial(_router_kernel, tiles_per_chunk=tiles_per_chunk, n_exp=n_exp)
    return pl.pallas_call(
        kern,
        out_shape=(
            jax.ShapeDtypeStruct((n * tiles, LANES), f32),
            jax.ShapeDtypeStruct((TOP_K, n), i32),
            jax.ShapeDtypeStruct((TOP_K, n), f32),
            jax.ShapeDtypeStruct((TOP_K, n), i32),
            jax.ShapeDtypeStruct((n // tc, n_exp, 1), i32),
        ),
        grid=(n // tm,),
        in_specs=[
            pl.BlockSpec((tm, d), lambda i: (i, 0)),
            pl.BlockSpec((1, d), lambda i: (0, 0)),
            pl.BlockSpec((n_exp, d), lambda i: (0, 0)),
            pl.BlockSpec((n_exp, 1), lambda i: (0, 0)),
        ],
        out_specs=(
            pl.BlockSpec((tm * tiles, LANES), lambda i: (i, 0)),
            pl.BlockSpec((TOP_K, tm), lambda i: (0, i)),
            pl.BlockSpec((TOP_K, tm), lambda i: (0, i)),
            pl.BlockSpec((TOP_K, tm), lambda i: (0, i)),
            pl.BlockSpec((1, n_exp, 1), lambda i: (i // tiles_per_chunk, 0, 0)),
        ),
        scratch_shapes=[pltpu.VMEM((n_exp, 1), f32)],
        compiler_params=_cparams(("arbitrary",), 32),
        name="router",
    )(h, g.reshape(1, d), w_router.T, b_router.reshape(n_exp, 1))


def _split_pairs(w_ref, p_ref, o_ref):
    width = p_ref.shape[0]
    for b in range(w_ref.shape[2] // width):
        cols = slice(b * width, (b + 1) * width)
        o_ref[0, :, cols] = jnp.dot(w_ref[0, :, cols].astype(bf16), p_ref[...],
                                    preferred_element_type=f32).astype(bf16)


def pair_split_matrix():
    width = 2 * LANES
    j = jnp.arange(width)
    src = jnp.where(j < LANES, 2 * j, 2 * (j - LANES) + 1)
    return (jnp.arange(width)[:, None] == src[None, :]).astype(bf16)


def _dense_row_index(r, tiles):
    return (r // SUBLANES) * tiles * SUBLANES + r % SUBLANES


def _experts_kernel(start_ref, nsub_ref, tail_ref, xp_hbm, dest_hbm, gates_hbm, fill_hbm, wup_ref, bup_ref, wdn_ref, bdn_ref,
                    f_hbm, xs_ref, acc_ref, xt_a, xt_b, y_a, y_b, dest_s, gate_s, rmap_s, sem_ref,
                    *, tc, n_exp, d, null_row0, seg, half_len, place_per_step):
    c = pl.program_id(0)
    e = pl.program_id(1)
    ms = MOE_SUB
    tiles = d // LANES
    cur = pl.multiple_of((c % 2) * half_len, LANES)
    nxt = pl.multiple_of(half_len - cur, LANES)

    def gather(row0, xt_ref):
        for r in range(ms):
            t = rmap_s[cur + row0 + r] & (seg - 1)
            xt_ref[pl.ds(_dense_row_index(r, tiles), tiles, stride=SUBLANES), :] = xs_ref[t]

    def place(base, first, count):
        for k in range(TOP_K):
            for u in range(count):
                a = k * seg + first + u
                rmap_s[base + dest_s[a]] = a

    def scatter(row0, y_ref):
        for r0 in range(0, ms, SCATTER_UNROLL):
            toks, sums = [], []
            for r in range(r0, r0 + SCATTER_UNROLL):
                a = rmap_s[cur + row0 + r]
                t = a & (seg - 1)
                yrow = y_ref[pl.ds(_dense_row_index(r, tiles), tiles, stride=SUBLANES), :]
                toks.append(t)
                sums.append(acc_ref[t] + gate_s[a] * yrow)
            for t, s in zip(toks, sums):
                acc_ref[t] = s

    @pl.when(e == 0)
    def _():
        copies = [
            pltpu.make_async_copy(xp_hbm.at[pl.ds(c * tc, tc)], xs_ref.at[pl.ds(0, tc)], sem_ref.at[0]),
            pltpu.make_async_copy(gates_hbm.at[c], gate_s, sem_ref.at[1]),
        ]
        for cp in copies:
            cp.start()
        xs_ref[pl.ds(tc, SUBLANES)] = jnp.zeros((SUBLANES, tiles, LANES), f32)
        acc_ref[...] = jnp.zeros_like(acc_ref)

        def load_map_inputs(chunk, base):
            loads = [
                pltpu.make_async_copy(dest_hbm.at[chunk], dest_s, sem_ref.at[2]),
                pltpu.make_async_copy(fill_hbm, rmap_s.at[pl.ds(base, half_len)], sem_ref.at[3]),
            ]
            for cp in loads:
                cp.start()
            for cp in loads:
                cp.wait()

        @pl.when(c == 0)
        def _():
            y_a[...] = jnp.zeros_like(y_a)
            y_b[...] = jnp.zeros_like(y_b)
            load_map_inputs(0, 0)

            def place_all(i, carry):
                place(0, i * SUBLANES, SUBLANES)
                return carry

            lax.fori_loop(0, tc // SUBLANES, place_all, 0)

        load_map_inputs(jnp.minimum(c + 1, pl.num_programs(0) - 1), nxt)
        for cp in copies:
            cp.wait()
        gather(0, xt_a)

    g = c * n_exp + e
    j0 = start_ref[g]

    def mlp(xt_cur, y_cur, first_group, groups):
        x = jnp.concatenate(
            [jnp.concatenate([xt_cur[pl.ds((rg * tiles + jt) * SUBLANES, SUBLANES), :] for jt in range(tiles)], axis=1)
             for rg in range(first_group, first_group + groups)], axis=0).astype(bf16)
        hmid = jnp.dot(x, wup_ref[0], preferred_element_type=f32) + bup_ref[0]
        glu = jnp.concatenate([hmid[:, 2 * jt * LANES:(2 * jt + 1) * LANES] for jt in range(tiles)], axis=1)
        lin = jnp.concatenate([hmid[:, (2 * jt + 1) * LANES:(2 * jt + 2) * LANES] for jt in range(tiles)], axis=1)
        glu = jnp.minimum(glu, SWIGLU_LIMIT)
        lin = jnp.clip(lin, -SWIGLU_LIMIT, SWIGLU_LIMIT)
        act = glu * jax.nn.sigmoid(SWIGLU_ALPHA * glu) * (lin + 1.0)
        y = jnp.dot(act.astype(bf16), wdn_ref[0], preferred_element_type=f32) + bdn_ref[0]
        for rg in range(groups):
            for jt in range(tiles):
                y_cur[pl.ds(((first_group + rg) * tiles + jt) * SUBLANES, SUBLANES), :] = (
                    y[rg * SUBLANES:(rg + 1) * SUBLANES, jt * LANES:(jt + 1) * LANES])

    def step(j, groups, xt_cur, xt_nxt, y_cur, y_prv):
        gather((j + 1) * ms, xt_nxt)
        scatter(jnp.where(j == 0, null_row0, (j - 1) * ms), y_prv)
        place(nxt, j * place_per_step, place_per_step)
        mlp(xt_cur, y_cur, 0, groups)

    def sub_block(i, carry):
        j = j0 + i
        full = jnp.logical_or(i < nsub_ref[g] - 1, tail_ref[g] > ms // 2)
        buffers = ((xt_a, xt_b, y_a, y_b), (xt_b, xt_a, y_b, y_a))
        for parity in range(2):
            for groups, wanted in ((ms // SUBLANES, full), (ms // SUBLANES // 2, jnp.logical_not(full))):
                @pl.when(jnp.logical_and(j % 2 == parity, wanted))
                def _():
                    step(j, groups, *buffers[parity])

        return carry

    lax.fori_loop(0, nsub_ref[g], sub_block, 0)

    @pl.when(e == n_exp - 1)
    def _():
        last = j0 + nsub_ref[g] - 1

        @pl.when(last % 2 == 0)
        def _():
            scatter(last * ms, y_a)

        @pl.when(last % 2 == 1)
        def _():
            scatter(last * ms, y_b)

        cp = pltpu.make_async_copy(acc_ref.at[pl.ds(0, tc)], f_hbm.at[pl.ds(c * tc, tc)], sem_ref.at[0])
        cp.start()
        cp.wait()


def _row_map_geometry(tc, n_exp, seg):
    assert seg > tc and seg & (seg - 1) == 0
    max_sub = (TOP_K * tc + n_exp * (MOE_SUB - 1)) // MOE_SUB
    min_sub = -(-TOP_K * tc // MOE_SUB)
    null_row0 = (max_sub + 1) * MOE_SUB
    half_len = -(-(null_row0 + MOE_SUB + 1) // LANES) * LANES
    place_per_step = -(-tc // min_sub)
    assert max_sub * place_per_step <= seg
    return null_row0, half_len, place_per_step


def experts(sb_start, n_sub, tail_rows, xp, dest, gates, w_up, b_up, w_down, b_down, tc):
    n, tiles, _ = xp.shape
    d = tiles * LANES
    n_exp = w_up.shape[0]
    chunks, padded_len = dest.shape
    seg = padded_len // TOP_K
    null_row0, half_len, place_per_step = _row_map_geometry(tc, n_exp, seg)
    fill = jnp.full((half_len,), tc, i32)
    kern = functools.partial(_experts_kernel, tc=tc, n_exp=n_exp, d=d, null_row0=null_row0, seg=seg,
                             half_len=half_len, place_per_step=place_per_step)
    block = pltpu.VMEM((MOE_SUB * tiles, LANES), f32)
    return pl.pallas_call(
        kern,
        out_shape=jax.ShapeDtypeStruct((n, tiles, LANES), f32),
        grid_spec=pltpu.PrefetchScalarGridSpec(
            num_scalar_prefetch=3,
            grid=(chunks, n_exp),
            in_specs=[
                pl.BlockSpec(memory_space=pl.ANY),
                pl.BlockSpec(memory_space=pl.ANY),
                pl.BlockSpec(memory_space=pl.ANY),
                pl.BlockSpec(memory_space=pl.ANY),
                pl.BlockSpec((1, d, 2 * d), lambda c, e, *_: (e, 0, 0)),
                pl.BlockSpec((1, 1, 2 * d), lambda c, e, *_: (e, 0, 0)),
                pl.BlockSpec((1, d, d), lambda c, e, *_: (e, 0, 0)),
                pl.BlockSpec((1, 1, d), lambda c, e, *_: (e, 0, 0)),
            ],
            out_specs=pl.BlockSpec(memory_space=pl.ANY),
            scratch_shapes=[
                pltpu.VMEM((tc + SUBLANES, tiles, LANES), f32),
                pltpu.VMEM((tc + SUBLANES, tiles, LANES), f32),
                block, block, block, block,
                pltpu.SMEM((padded_len,), i32),
                pltpu.SMEM((padded_len,), f32),
                pltpu.SMEM((2 * half_len,), i32),
                pltpu.SemaphoreType.DMA((4,)),
            ],
        ),
        compiler_params=_cparams(("arbitrary", "arbitrary"), 58),
        name="experts",
    )(sb_start, n_sub, tail_rows, xp, dest, gates, fill, w_up, b_up.reshape(n_exp, 1, 2 * d), w_down,
      b_down.reshape(n_exp, 1, d))


def _final_kernel(h_ref, f_ref, g_ref, o_ref):
    tm, d = h_ref.shape
    tiles = d // LANES
    parts = [h_ref[:, jt * LANES:(jt + 1) * LANES] + f_ref[pl.ds(jt, tm, stride=tiles), :] for jt in range(tiles)]
    o_ref[...] = _rms(jnp.concatenate(parts, axis=1), g_ref[...])


def final_norm(h, f2, g, row0, rows):
    d = h.shape[1]
    tiles = d // LANES
    tm = min(ROW_TILE, rows)
    blk0 = row0 // tm
    assert rows % tm == 0 and row0 % tm == 0
    return pl.pallas_call(
        _final_kernel,
        out_shape=jax.ShapeDtypeStruct((rows, d), f32),
        grid=(rows // tm,),
        in_specs=[
            pl.BlockSpec((tm, d), lambda i: (blk0 + i, 0)),
            pl.BlockSpec((tm * tiles, LANES), lambda i: (blk0 + i, 0)),
            pl.BlockSpec((1, d), lambda i: (0, 0)),
        ],
        out_specs=pl.BlockSpec((tm, d), lambda i: (i, 0)),
        compiler_params=_cparams(("arbitrary",), 32),
        name="final_norm",
    )(h, f2, g.reshape(1, d))


def moe(h, g, w_router, b_router, w_up_p, b_up, w_down_b, b_down):
    n, d = h.shape
    n_exp = w_router.shape[1]
    tc = n // MOE_CHUNKS
    xp, top_e, gates, rank, counts = router(h, g, w_router, b_router, tc)
    n_sub = (counts.reshape(MOE_CHUNKS, n_exp) + MOE_SUB - 1) // MOE_SUB
    sb_start = jnp.cumsum(n_sub, axis=1) - n_sub
    hot = top_e.reshape(TOP_K, MOE_CHUNKS, tc, 1) == jnp.arange(n_exp, dtype=i32)
    row_base = (sb_start * MOE_SUB).reshape(1, MOE_CHUNKS, 1, n_exp)
    dest = jnp.sum(jnp.where(hot, row_base, 0), axis=-1) + rank.reshape(TOP_K, MOE_CHUNKS, tc)
    seg = 1 << tc.bit_length()
    spare_row = _row_map_geometry(tc, n_exp, seg)[1] - 1
    per_chunk = lambda a, tail: jnp.pad(a.transpose(1, 0, 2), ((0, 0), (0, 0), (0, seg - tc)),
                                        constant_values=tail).reshape(MOE_CHUNKS, TOP_K * seg)
    tiles = d // LANES
    b_up_p = b_up.reshape(n_exp, tiles, LANES, 2).transpose(0, 1, 3, 2).reshape(n_exp, 2 * d)
    tail_rows = counts.reshape(MOE_CHUNKS, n_exp) - (n_sub - 1) * MOE_SUB
    flat = lambda a: a.reshape(-1).astype(i32)
    f = experts(flat(sb_start), flat(n_sub), flat(tail_rows), xp.reshape(n, tiles, LANES),
                per_chunk(dest, spare_row), per_chunk(gates.reshape(TOP_K, MOE_CHUNKS, tc), 0.0), w_up_p, b_up_p,
                w_down_b, b_down, tc)
    return f.reshape(n * tiles, LANES)


def kernel(x_prompt, x_sample, mem_prompt, state_ret, state_conv, cache_mem_k, cache_mem_v, norm_mix, w_in, conv_w,
           ret_gn, w_out, norm_cross, norm_mem, w_mq, w_mk, w_mv, w_mo, norm_ffn, w_router, b_router, w_up, b_up,
           w_down, b_down, norm_final):
    batch, seq, d = x_prompt.shape
    n_seq, ts, _ = x_sample.shape
    depth = w_in.shape[0]
    n_mem = mem_prompt.shape[1]
    r = ret_gn.shape[1]
    cd_width = conv_w.shape[2]
    dh = r // N_RET_HEADS
    n_p = batch * seq
    n_s = n_seq * ts
    n = n_p + n_s
    nblk = n_seq // SAMPLE_BLOCK
    assert seq % ROW_TILE == 0 and n % ROW_TILE == 0 and n % MOE_CHUNKS == 0

    xs = x_sample.reshape(nblk, SAMPLE_BLOCK, ts, d).transpose(0, 2, 1, 3).reshape(n_s, d)
    h_p, h_s = x_prompt.reshape(n_p, d), xs
    tab_p = prompt_tables(seq, dh)
    tab_s = sample_tables(ts, float(PAST_LEN), dh)

    ret_p, conv_p, mk_p, mv_p, ret_s, conv_s = [], [], [], [], [], []
    for l in range(depth):
        w_in_b, w_out_b, w_mq_b = w_in[l].astype(bf16), w_out[l].astype(bf16), w_mq[l].astype(bf16)
        h_p, s_p, c_p, w_down_b = prompt_mixer(h_p, norm_mix[l], w_in_b, tab_p, conv_w[l], ret_gn[l], w_out_b,
                                               batch, seq, r, cd_width, w_down[l])
        proj_s = norm_matmul(h_s, norm_mix[l], w_in_b, "in_proj_sample")
        mix_s, s_s, c_s = sample_mixer(proj_s, tab_s, conv_w[l], ret_gn[l], state_ret[l],
                                       state_conv[l].transpose(1, 0, 2), 0, ts, r, cd_width)
        h_s = matmul_res(mix_s, w_out_b, h_s, "out_proj_sample")

        mk, mv = mem_kv(mem_prompt, norm_mem[l], jnp.concatenate([w_mk[l], w_mv[l]], axis=1).astype(bf16),
                        N_MEM_HEADS)
        q_s = norm_matmul(h_s, norm_cross[l], w_mq_b, "q_sample")
        attn_s = cross_sample(q_s, cache_mem_k[l], cache_mem_v[l], 0, ts)
        h, w_up_p = attn_block(h_p, h_s, norm_cross[l], w_mq_b, mk, mv, attn_s, w_mo[l].astype(bf16), seq, w_up[l])

        f2 = moe(h, norm_ffn[l], w_router[l], b_router[l], w_up_p, b_up[l], w_down_b, b_down[l])
        if l + 1 < depth:
            h = h + f2.reshape(n, d // LANES, LANES).reshape(n, d)
            h_p, h_s = h, h[n_p:]

        ret_p.append(s_p)
        conv_p.append(c_p)
        mk_p.append(mk)
        mv_p.append(mv)
        ret_s.append(s_s)
        conv_s.append(c_s.transpose(1, 0, 2))

    y_p = final_norm(h, f2, norm_final, 0, n_p).reshape(batch, seq, d)
    y_s = final_norm(h, f2, norm_final, n_p, n_s)
    y_s = y_s.reshape(nblk, ts, SAMPLE_BLOCK, d).transpose(0, 2, 1, 3).reshape(n_seq, ts, d)
    return (y_p, y_s, jnp.stack(ret_p), jnp.stack(conv_p), jnp.stack(mk_p), jnp.stack(mv_p), jnp.stack(ret_s),
            jnp.stack(conv_s))
```

```python
import functools

import jax
import jax.numpy as jnp
from jax import lax
from jax.experimental import pallas as pl
from jax.experimental.pallas import tpu as pltpu

f32 = jnp.float32
bf16 = jnp.bfloat16
i32 = jnp.int32

EPS = 1e-6
ROPE_THETA = 10000.0
RET_CHUNK = 128
PAST_LEN = 16384
N_RET_HEADS = 4
N_MEM_HEADS = 4
TOP_K = 4
SWIGLU_ALPHA = 1.702
SWIGLU_LIMIT = 7.0

LANES = 128
SUBLANES = 8
MIB = 1024 * 1024

ROW_TILE = 512
SAMPLE_BLOCK = 8
MOE_CHUNKS = 4
MOE_SUB = 256
SCATTER_UNROLL = 8


def _cparams(sem, vmem_mib):
    return pltpu.CompilerParams(dimension_semantics=sem, vmem_limit_bytes=vmem_mib * MIB)


def _rms(x, g):
    ms = jnp.mean(x * x, axis=-1, keepdims=True)
    return (x * lax.rsqrt(ms + EPS)) * g


def _norm_matmul_kernel(x_ref, g_ref, w_ref, o_ref):
    xn = _rms(x_ref[...], g_ref[...])
    o_ref[...] = jnp.dot(xn.astype(bf16), w_ref[...], preferred_element_type=f32)


def norm_matmul(x, g, w, name):
    m, d = x.shape
    f = w.shape[1]
    tm = min(ROW_TILE, m)
    assert m % tm == 0
    return pl.pallas_call(
        _norm_matmul_kernel,
        out_shape=jax.ShapeDtypeStruct((m, f), f32),
        grid=(m // tm,),
        in_specs=[
            pl.BlockSpec((tm, d), lambda i: (i, 0)),
            pl.BlockSpec((1, d), lambda i: (0, 0)),
            pl.BlockSpec((d, f), lambda i: (0, 0)),
        ],
        out_specs=pl.BlockSpec((tm, f), lambda i: (i, 0)),
        compiler_params=_cparams(("arbitrary",), 48),
        name=name,
    )(x, g.reshape(1, d), w)


def _stacked_specs(m_p, m_s, width):
    tm = ROW_TILE
    assert m_p % tm == 0 and m_s == tm
    prompt_tiles = m_p // tm
    return prompt_tiles, [
        pl.BlockSpec((tm, width), lambda i: (jnp.minimum(i, prompt_tiles - 1), 0)),
        pl.BlockSpec((tm, width), lambda i: (0, 0)),
    ]


def _matmul_res_kernel(a_ref, w_ref, r_ref, o_ref):
    o_ref[...] = r_ref[...] + jnp.dot(a_ref[...], w_ref[...], preferred_element_type=f32)


def matmul_res(a, w, res, name):
    m, d = a.shape
    f = w.shape[1]
    tm = min(ROW_TILE, m)
    assert m % tm == 0
    return pl.pallas_call(
        _matmul_res_kernel,
        out_shape=jax.ShapeDtypeStruct((m, f), f32),
        grid=(m // tm,),
        in_specs=[
            pl.BlockSpec((tm, d), lambda i: (i, 0)),
            pl.BlockSpec((d, f), lambda i: (0, 0)),
            pl.BlockSpec((tm, f), lambda i: (i, 0)),
        ],
        out_specs=pl.BlockSpec((tm, f), lambda i: (i, 0)),
        compiler_params=_cparams(("arbitrary",), 32),
        name=name,
    )(a, w, res)


def _rope(x, cos, sin_signed, half):
    return x * cos + pltpu.roll(x, half, 1) * sin_signed


def _group_norm_gate(o, gate, gn):
    mu = jnp.mean(o, axis=-1, keepdims=True)
    var = jnp.mean(jnp.square(o - mu), axis=-1, keepdims=True)
    on = ((o - mu) * lax.rsqrt(var + EPS)) * gn
    return (gate * jax.nn.sigmoid(gate)) * on


def _prompt_mixer_kernel(cd_ref, x_ref, g_ref, win_ref, cos_ref, sin_ref, decay_ref, xi_ref, zeta_ref, convw_ref,
                         gn_ref, wout_ref, wcast_ref, wsplit_ref, perm_ref, h_ref, sfin_ref, cfin_ref, wcast_out_ref,
                         wsplit_out_ref, s_scr, u_scr, proj_ref, mix_ref, *, tt, r, cd_width, dh):
    wcast_out_ref[...] = wcast_ref[...].astype(bf16)
    _split_pairs(wsplit_ref, perm_ref, wsplit_out_ref)
    j = pl.program_id(1)
    heads = r // dh
    pad = SUBLANES

    @pl.when(j == 0)
    def _():
        s_scr[...] = jnp.zeros_like(s_scr)
        u_scr[0:pad, :] = jnp.zeros((pad, cd_width), f32)

    proj_ref[...] = jnp.dot(_rms(x_ref[...], g_ref[...]).astype(bf16), win_ref[...], preferred_element_type=f32)

    k_scale = dh ** -0.5
    for c in range(tt // RET_CHUNK):
        rows = pl.ds(c * RET_CHUNK, RET_CHUNK)
        cosb = cos_ref[rows, :]
        sinb = sin_ref[rows, :]
        for h in range(heads):
            lo, hi = h * dh, (h + 1) * dh
            qh = _rope(proj_ref[rows, lo:hi], cosb, sinb, dh // 2)
            kh = _rope(proj_ref[rows, r + lo:r + hi], cosb, sinb, dh // 2) * k_scale
            vb = proj_ref[rows, 2 * r + lo:2 * r + hi].astype(bf16)
            gate = proj_ref[rows, 3 * r + lo:3 * r + hi]
            qb = qh.astype(bf16)
            kb = kh.astype(bf16)
            scores = lax.dot_general(qb, kb, (((1,), (1,)), ((), ())), preferred_element_type=f32) * decay_ref[h]
            inner = jnp.dot(scores.astype(bf16), vb, preferred_element_type=f32)
            s_prev = s_scr[h]
            cross = jnp.dot((qh * xi_ref[h]).astype(bf16), s_prev.astype(bf16), preferred_element_type=f32)
            kv = lax.dot_general((kh * zeta_ref[h]).astype(bf16), vb, (((0,), (0,)), ((), ())),
                                 preferred_element_type=f32)
            s_scr[h] = cd_ref[h] * s_prev + kv
            mix_ref[rows, lo:hi] = _group_norm_gate(inner + cross, gate, gn_ref[:, lo:hi]).astype(bf16)

    base = 4 * r
    u = proj_ref[:, base + cd_width:base + 2 * cd_width] * proj_ref[:, base:base + cd_width]
    u_scr[pad:pad + tt, :] = u
    conv = (u_scr[pad - 2:pad - 2 + tt, :] * convw_ref[0:1, :] + u_scr[pad - 1:pad - 1 + tt, :] * convw_ref[1:2, :]
            + u * convw_ref[2:3, :])
    mix_ref[:, r:r + cd_width] = (proj_ref[:, base + 2 * cd_width:base + 3 * cd_width] * conv).astype(bf16)
    u_scr[0:pad, :] = u_scr[tt:tt + pad, :]
    cfin_ref[0] = u_scr[pad - 2:pad, :]
    h_ref[...] = x_ref[...] + jnp.dot(mix_ref[...], wout_ref[...], preferred_element_type=f32)

    @pl.when(j == pl.num_programs(1) - 1)
    def _():
        sfin_ref[0] = s_scr[...]


def prompt_mixer(x, g, w_in, tables, conv_w, ret_gn, w_out, batch, seq, r, cd_width, w_cast, w_split):
    tt = ROW_TILE
    d = x.shape[1]
    dh = r // N_RET_HEADS
    steps = seq // tt
    in_cols = w_in.shape[1]
    assert w_cast.shape[0] == batch * steps and w_split.shape[0] == batch * steps
    kern = functools.partial(_prompt_mixer_kernel, tt=tt, r=r, cd_width=cd_width, dh=dh)
    tab = lambda shape: pl.BlockSpec(shape, lambda b, j, cd: (0,) * len(shape))
    row_tile = pl.BlockSpec((tt, d), lambda b, j, cd: (b * steps + j, 0))
    cast_slice = pl.BlockSpec((1,) + w_cast.shape[1:], lambda b, j, cd: (b * steps + j, 0, 0))
    split_shape = (w_split.shape[0], w_split.shape[1], w_split.shape[2] // 2)
    split_slice = pl.BlockSpec((1,) + split_shape[1:], lambda b, j, cd: (b * steps + j, 0, 0))
    perm = pair_split_matrix()
    return pl.pallas_call(
        kern,
        out_shape=(
            jax.ShapeDtypeStruct((batch * seq, d), f32),
            jax.ShapeDtypeStruct((batch, N_RET_HEADS, dh, dh), f32),
            jax.ShapeDtypeStruct((batch, 2, cd_width), f32),
            jax.ShapeDtypeStruct(w_cast.shape, bf16),
            jax.ShapeDtypeStruct(split_shape, bf16),
        ),
        grid_spec=pltpu.PrefetchScalarGridSpec(
            num_scalar_prefetch=1,
            grid=(batch, steps),
            in_specs=[
                row_tile,
                tab((1, d)),
                tab((d, in_cols)),
                pl.BlockSpec((tt, dh), lambda b, j, cd: (j, 0)),
                pl.BlockSpec((tt, dh), lambda b, j, cd: (j, 0)),
                tab((N_RET_HEADS, RET_CHUNK, RET_CHUNK)),
                tab((N_RET_HEADS, RET_CHUNK, dh)),
                tab((N_RET_HEADS, RET_CHUNK, dh)),
                tab((3, cd_width)),
                tab((1, r)),
                tab((r + cd_width, d)),
                cast_slice,
                split_slice,
                tab(perm.shape),
            ],
            out_specs=(
                row_tile,
                pl.BlockSpec((1, N_RET_HEADS, dh, dh), lambda b, j, cd: (b, 0, 0, 0)),
                pl.BlockSpec((1, 2, cd_width), lambda b, j, cd: (b, 0, 0)),
                cast_slice,
                split_slice,
            ),
            scratch_shapes=[
                pltpu.VMEM((N_RET_HEADS, dh, dh), f32),
                pltpu.VMEM((tt + 2 * SUBLANES, cd_width), f32),
                pltpu.VMEM((tt, in_cols), f32),
                pltpu.VMEM((tt, r + cd_width), bf16),
            ],
        ),
        compiler_params=_cparams(("arbitrary", "arbitrary"), 56),
        name="prompt_mixer",
    )(tables["chunk_decay"], x, g.reshape(1, d), w_in, tables["cos"], tables["sin"], tables["decay"], tables["xi"],
      tables["zeta"], conv_w, ret_gn.reshape(1, r), w_out, w_cast, w_split, perm)


def _sample_mixer_kernel(dec_ref, proj_ref, cos_ref, sin_ref, xi_ref, zeta_ref, convw_ref, gn_ref, s0_ref, c0_ref,
                         mix_ref, snew_ref, cnew_ref, *, ts, r, cd_width, dh):
    heads = r // dh
    nb = SAMPLE_BLOCK
    rows = ts * nb
    k_scale = dh ** -0.5
    cosb = cos_ref[...]
    sinb = sin_ref[...]
    seq_of_row = lax.broadcasted_iota(i32, (rows, dh), 0) % nb
    for h in range(heads):
        lo, hi = h * dh, (h + 1) * dh
        qh = _rope(proj_ref[:, lo:hi], cosb, sinb, dh // 2)
        kh = _rope(proj_ref[:, r + lo:r + hi], cosb, sinb, dh // 2) * k_scale
        vh = proj_ref[:, 2 * r + lo:2 * r + hi]
        gate = proj_ref[:, 3 * r + lo:3 * r + hi]
        inner = []
        for t in range(ts):
            qt = qh[t * nb:(t + 1) * nb]
            acc = jnp.zeros((nb, dh), f32)
            for s in range(t + 1):
                sc = jnp.sum(qt * kh[s * nb:(s + 1) * nb], axis=-1, keepdims=True) * dec_ref[h * (ts + 1) + t - s]
                acc = acc + sc * vh[s * nb:(s + 1) * nb]
            inner.append(acc)
        inner = jnp.concatenate(inner, axis=0)
        qx = (qh * xi_ref[h]).astype(bf16)
        kz = kh * zeta_ref[h]
        vb = vh.astype(bf16)
        cross = jnp.zeros((rows, dh), f32)
        for b in range(nb):
            mine = seq_of_row == b
            s_prev = s0_ref[b, h]
            res = jnp.dot(qx, s_prev.astype(bf16), preferred_element_type=f32)
            cross = cross + jnp.where(mine, res, 0.0)
            kv = lax.dot_general(jnp.where(mine, kz, 0.0).astype(bf16), vb, (((0,), (0,)), ((), ())),
                                 preferred_element_type=f32)
            snew_ref[b, h] = dec_ref[h * (ts + 1) + ts] * s_prev + kv
        mix_ref[:, lo:hi] = _group_norm_gate(inner + cross, gate, gn_ref[:, lo:hi]).astype(bf16)

    base = 4 * r
    u = proj_ref[:, base + cd_width:base + 2 * cd_width] * proj_ref[:, base:base + cd_width]
    full = [c0_ref[0], c0_ref[1]] + [u[t * nb:(t + 1) * nb] for t in range(ts)]
    conv = jnp.concatenate(
        [full[t] * convw_ref[0:1, :] + full[t + 1] * convw_ref[1:2, :] + full[t + 2] * convw_ref[2:3, :]
         for t in range(ts)], axis=0)
    mix_ref[:, r:r + cd_width] = (proj_ref[:, base + 2 * cd_width:base + 3 * cd_width] * conv).astype(bf16)
    cnew_ref[0] = full[ts]
    cnew_ref[1] = full[ts + 1]


def sample_mixer(proj, tables, conv_w, ret_gn, state_ret, state_conv_t, row0, ts, r, cd_width):
    n_seq = state_ret.shape[0]
    dh = r // N_RET_HEADS
    rows = ts * SAMPLE_BLOCK
    blk0 = row0 // rows
    assert row0 % rows == 0 and n_seq % SAMPLE_BLOCK == 0
    in_cols = proj.shape[1]
    kern = functools.partial(_sample_mixer_kernel, ts=ts, r=r, cd_width=cd_width, dh=dh)
    tab = lambda shape: pl.BlockSpec(shape, lambda i, d: (0,) * len(shape))
    return pl.pallas_call(
        kern,
        out_shape=(
            jax.ShapeDtypeStruct((n_seq * ts, r + cd_width), bf16),
            jax.ShapeDtypeStruct(state_ret.shape, f32),
            jax.ShapeDtypeStruct(state_conv_t.shape, f32),
        ),
        grid_spec=pltpu.PrefetchScalarGridSpec(
            num_scalar_prefetch=1,
            grid=(n_seq // SAMPLE_BLOCK,),
            in_specs=[
                pl.BlockSpec((rows, in_cols), lambda i, d: (blk0 + i, 0)),
                tab((rows, dh)),
                tab((rows, dh)),
                tab((N_RET_HEADS, rows, dh)),
                tab((N_RET_HEADS, rows, dh)),
                tab((3, cd_width)),
                tab((1, r)),
                pl.BlockSpec((SAMPLE_BLOCK, N_RET_HEADS, dh, dh), lambda i, d: (i, 0, 0, 0)),
                pl.BlockSpec((2, SAMPLE_BLOCK, cd_width), lambda i, d: (0, i, 0)),
            ],
            out_specs=(
                pl.BlockSpec((rows, r + cd_width), lambda i, d: (i, 0)),
                pl.BlockSpec((SAMPLE_BLOCK, N_RET_HEADS, dh, dh), lambda i, d: (i, 0, 0, 0)),
                pl.BlockSpec((2, SAMPLE_BLOCK, cd_width), lambda i, d: (0, i, 0)),
            ),
        ),
        compiler_params=_cparams(("arbitrary",), 32),
        name="sample_mixer",
    )(tables["dec"], proj, tables["cos"], tables["sin"], tables["xi"], tables["zeta"], conv_w, ret_gn.reshape(1, r),
      state_ret, state_conv_t)


def _log_gamma():
    return jnp.log1p(-jnp.exp2(-5.0 - jnp.arange(N_RET_HEADS, dtype=f32)))


def _rope_tables(pos, dh):
    half = dh // 2
    inv = ROPE_THETA ** (-jnp.arange(half, dtype=f32) / half)
    ang = pos[:, None] * inv[None, :]
    cos, sin = jnp.cos(ang), jnp.sin(ang)
    return jnp.concatenate([cos, cos], axis=-1), jnp.concatenate([-sin, sin], axis=-1)


def prompt_tables(seq, dh):
    c = RET_CHUNK
    lg = _log_gamma()
    i = jnp.arange(c, dtype=f32)
    diff = i[:, None] - i[None, :]
    decay = jnp.where(diff >= 0, jnp.exp(jnp.maximum(diff, 0.0)[None] * lg[:, None, None]), 0.0)
    xi = jnp.exp((i + 1.0)[None, :] * lg[:, None])
    zeta = jnp.exp((c - 1.0 - i)[None, :] * lg[:, None])
    cos, sin = _rope_tables(jnp.arange(seq, dtype=f32), dh)
    bc = lambda t: jnp.broadcast_to(t[:, :, None], (N_RET_HEADS, c, dh))
    return dict(cos=cos, sin=sin, decay=decay, xi=bc(xi), zeta=bc(zeta), chunk_decay=jnp.exp(c * lg))


def sample_tables(ts, pos0, dh):
    lg = _log_gamma()
    i = jnp.arange(ts, dtype=f32)
    dec = jnp.exp(jnp.arange(ts + 1, dtype=f32)[None, :] * lg[:, None])
    xi = jnp.exp((i + 1.0)[None, :] * lg[:, None])
    zeta = jnp.exp((ts - 1.0 - i)[None, :] * lg[:, None])
    cos, sin = _rope_tables(pos0 + i, dh)
    rep = lambda t: jnp.repeat(t, SAMPLE_BLOCK, axis=0)
    bc = lambda t: jnp.broadcast_to(jnp.repeat(t, SAMPLE_BLOCK, axis=1)[:, :, None],
                                    (N_RET_HEADS, ts * SAMPLE_BLOCK, dh))
    return dict(cos=rep(cos), sin=rep(sin), xi=bc(xi), zeta=bc(zeta), dec=dec.reshape(-1))


def _softmax_rows(s):
    m = jnp.max(s, axis=-1, keepdims=True)
    p = jnp.exp(s - m)
    return p / jnp.sum(p, axis=-1, keepdims=True)


def _from_slabs(ref, rows=None):
    rows = slice(None) if rows is None else rows
    return jnp.concatenate([ref[lt, rows, :] for lt in range(ref.shape[0])], axis=1)


def _to_slabs(ref, value):
    for lt in range(ref.shape[0]):
        ref[lt] = value[:, lt * LANES:(lt + 1) * LANES]


def _mem_kv_kernel(m_ref, g_ref, w_ref, k_ref, v_ref, rows_ref, *, heads):
    n_mem, d = m_ref.shape
    dh = d // heads
    kv = jnp.dot(_rms(m_ref[...], g_ref[...]).astype(bf16), w_ref[...], preferred_element_type=f32)
    for part, o_ref in enumerate((k_ref, v_ref)):
        for h in range(heads):
            for lt in range(dh // LANES):
                lo = part * d + h * dh + lt * LANES
                rows_ref[lt, pl.ds(h, n_mem, stride=heads), :] = kv[:, lo:lo + LANES]
        o_ref[0] = _from_slabs(rows_ref).reshape(n_mem, heads, dh)


def mem_kv(mem, g, w_kv, heads):
    batch, n_mem, d = mem.shape
    dh = d // heads
    out = jax.ShapeDtypeStruct((batch, n_mem, heads, dh), f32)
    out_spec = pl.BlockSpec((1, n_mem, heads, dh), lambda b: (b, 0, 0, 0))
    return pl.pallas_call(
        functools.partial(_mem_kv_kernel, heads=heads),
        out_shape=(out, out),
        grid=(batch,),
        in_specs=[
            pl.BlockSpec((n_mem, d), lambda b: (b, 0)),
            pl.BlockSpec((1, d), lambda b: (0, 0)),
            pl.BlockSpec((d, 2 * d), lambda b: (0, 0)),
        ],
        out_specs=(out_spec, out_spec),
        scratch_shapes=[pltpu.VMEM((dh // LANES, n_mem * heads, LANES), f32)],
        compiler_params=_cparams(("arbitrary",), 32),
        name="mem_kv",
    )(mem.reshape(batch * n_mem, d), g.reshape(1, d), w_kv)


def _attn_block_kernel(hp_ref, hs_ref, g_ref, wq_ref, k_ref, v_ref, as_ref, wo_ref, wup_ref, perm_ref, o_ref,
                       wup_out_ref, k_rows, v_rows, *, prompt_tiles, heads):
    i = pl.program_id(0)

    @pl.when(i < prompt_tiles)
    def _():
        _split_pairs(wup_ref, perm_ref, wup_out_ref)
        h_ref = hp_ref
        n_mem, _, dh = k_ref.shape[1:]
        scale = dh ** -0.5
        q = jnp.dot(_rms(h_ref[...], g_ref[...]).astype(bf16), wq_ref[...], preferred_element_type=f32)
        _to_slabs(k_rows, k_ref[0].reshape(n_mem * heads, dh))
        _to_slabs(v_rows, v_ref[0].reshape(n_mem * heads, dh))
        outs = []
        for h in range(heads):
            kh = _from_slabs(k_rows, pl.ds(h, n_mem, stride=heads)).astype(bf16)
            vh = _from_slabs(v_rows, pl.ds(h, n_mem, stride=heads)).astype(bf16)
            s = lax.dot_general(q[:, h * dh:(h + 1) * dh].astype(bf16), kh, (((1,), (1,)), ((), ())),
                                preferred_element_type=f32) * scale
            outs.append(jnp.dot(_softmax_rows(s).astype(bf16), vh, preferred_element_type=f32).astype(bf16))
        attn = jnp.concatenate(outs, axis=1)
        o_ref[...] = h_ref[...] + jnp.dot(attn, wo_ref[...], preferred_element_type=f32)

    @pl.when(i == prompt_tiles)
    def _():
        o_ref[...] = hs_ref[...] + jnp.dot(as_ref[...], wo_ref[...], preferred_element_type=f32)


def attn_block(h_prompt, h_sample, g, w_mq, mk, mv, attn_sample, w_mo, seq, w_up):
    d = h_prompt.shape[1]
    batch, n_mem, heads, dh = mk.shape
    tm = ROW_TILE
    per_batch = seq // tm
    prompt_tiles, h_specs = _stacked_specs(batch * seq, h_sample.shape[0], d)
    n_exp, _, f2 = w_up.shape
    assert attn_sample.shape == (tm, d) and n_exp == prompt_tiles
    prompt_tile = lambda i: jnp.minimum(i, prompt_tiles - 1)
    kv_spec = pl.BlockSpec((1, n_mem, heads, dh), lambda i: (prompt_tile(i) // per_batch, 0, 0, 0))
    const = lambda shape: pl.BlockSpec(shape, lambda i: (0,) * len(shape))
    perm = pair_split_matrix()
    return pl.pallas_call(
        functools.partial(_attn_block_kernel, prompt_tiles=prompt_tiles, heads=heads),
        out_shape=(
            jax.ShapeDtypeStruct(((prompt_tiles + 1) * tm, d), f32),
            jax.ShapeDtypeStruct((n_exp, d, f2 // 2), bf16),
        ),
        grid=(prompt_tiles + 1,),
        in_specs=h_specs + [
            const((1, d)),
            const((d, d)),
            kv_spec,
            kv_spec,
            const((tm, d)),
            const((d, d)),
            pl.BlockSpec((1, d, f2 // 2), lambda i: (prompt_tile(i), 0, 1)),
            const(perm.shape),
        ],
        out_specs=(pl.BlockSpec((tm, d), lambda i: (i, 0)),
                   pl.BlockSpec((1, d, f2 // 2), lambda i: (prompt_tile(i), 0, 0))),
        scratch_shapes=[pltpu.VMEM((dh // LANES, n_mem * heads, LANES), f32)] * 2,
        compiler_params=_cparams(("arbitrary",), 56),
        name="attn_block",
    )(h_prompt, h_sample, g.reshape(1, d), w_mq, mk, mv, attn_sample, w_mo, w_up, perm)


def _cross_sample_kernel(q_ref, k_ref, v_ref, o_ref, acc_ref, *, dh, seqs):
    part = pl.program_id(1)
    rows = q_ref.shape[0]
    scale = dh ** -0.5

    @pl.when(part == 0)
    def _():
        acc_ref[...] = jnp.zeros_like(acc_ref)

    heads = N_MEM_HEADS
    n_mem = k_ref.shape[1]
    qx = jnp.concatenate([q_ref[:, h * dh:(h + 1) * dh] for h in range(heads)], axis=0).astype(bf16)
    row = lax.broadcasted_iota(i32, (heads * rows, n_mem * heads), 0)
    col = lax.broadcasted_iota(i32, (heads * rows, n_mem * heads), 1)
    same_head = col % heads == row // rows
    seq_of_row = lax.broadcasted_iota(i32, (heads * rows, dh), 0) % SAMPLE_BLOCK
    out = jnp.zeros((heads * rows, dh), f32)
    for b in range(seqs):
        k2 = k_ref[b].reshape(n_mem * heads, dh).astype(bf16)
        v2 = v_ref[b].reshape(n_mem * heads, dh).astype(bf16)
        s = lax.dot_general(qx, k2, (((1,), (1,)), ((), ())), preferred_element_type=f32) * scale
        p = _softmax_rows(jnp.where(same_head, s, -jnp.inf))
        o = jnp.dot(p.astype(bf16), v2, preferred_element_type=f32)
        out = out + jnp.where(seq_of_row == part * seqs + b, o, 0.0)
    acc_ref[...] += out

    @pl.when(part == pl.num_programs(1) - 1)
    def _():
        for h in range(heads):
            o_ref[:, h * dh:(h + 1) * dh] = acc_ref[h * rows:(h + 1) * rows, :].astype(bf16)


def cross_sample(q, cache_k, cache_v, row0, ts):
    n_seq, n_mem, heads, dh = cache_k.shape
    d = heads * dh
    rows = ts * SAMPLE_BLOCK
    blk0 = row0 // rows
    parts = 2
    seqs = SAMPLE_BLOCK // parts
    kern = functools.partial(_cross_sample_kernel, dh=dh, seqs=seqs)
    return pl.pallas_call(
        kern,
        out_shape=jax.ShapeDtypeStruct((n_seq * ts, d), bf16),
        grid=(n_seq // SAMPLE_BLOCK, parts),
        in_specs=[
            pl.BlockSpec((rows, d), lambda i, p: (blk0 + i, 0)),
            pl.BlockSpec((seqs, n_mem, heads, dh), lambda i, p: (i * parts + p, 0, 0, 0)),
            pl.BlockSpec((seqs, n_mem, heads, dh), lambda i, p: (i * parts + p, 0, 0, 0)),
        ],
        out_specs=pl.BlockSpec((rows, d), lambda i, p: (i, 0)),
        scratch_shapes=[pltpu.VMEM((heads * rows, dh), f32)],
        compiler_params=_cparams(("arbitrary", "arbitrary"), 40),
        name="cross_sample",
    )(q, cache_k, cache_v)


def _router_kernel(h_ref, g_ref, wr_ref, br_ref, xp_ref, e_ref, gate_ref, rank_ref, cnt_ref, carry_ref,
                   *, tiles_per_chunk, n_exp):
    i = pl.program_id(0)

    @pl.when(i % tiles_per_chunk == 0)
    def _():
        carry_ref[...] = jnp.zeros_like(carry_ref)

    xn = _rms(h_ref[...], g_ref[...])
    tm, d = xn.shape
    tiles = d // LANES
    for rg in range(tm // SUBLANES):
        for jt in range(tiles):
            xp_ref[pl.ds(rg * SUBLANES * tiles + jt, SUBLANES, stride=tiles), :] = (
                xn[rg * SUBLANES:(rg + 1) * SUBLANES, jt * LANES:(jt + 1) * LANES])

    w = wr_ref[...]
    w_hi = w.astype(bf16)
    w_lo = (w - w_hi.astype(f32)).astype(bf16)
    x_hi = xn.astype(bf16)
    x_lo = (xn - x_hi.astype(f32)).astype(bf16)
    nt = (((1,), (1,)), ((), ()))
    both = lax.dot_general(jnp.concatenate([w_hi, w_lo], axis=0), x_hi, nt, preferred_element_type=f32)
    logits = (both[:n_exp] + both[n_exp:] + lax.dot_general(w_hi, x_lo, nt, preferred_element_type=f32)
              + br_ref[...])
    sub = lax.broadcasted_iota(i32, (n_exp, tm), 0).astype(f32)
    chosen, vals, hots = [], [], []
    work = logits
    for _ in range(TOP_K):
        m = jnp.max(work, axis=0, keepdims=True)
        idx = jnp.min(jnp.where(work == m, sub, float(n_exp)), axis=0, keepdims=True)
        hot = sub == idx
        chosen.append(idx)
        vals.append(m)
        hots.append(hot)
        work = jnp.where(hot, -jnp.inf, work)
    ex = [jnp.exp(v - vals[0]) for v in vals]
    denom = ex[0] + ex[1] + ex[2] + ex[3]
    e_ref[...] = jnp.concatenate(chosen, axis=0).astype(i32)
    gate_ref[...] = jnp.concatenate([x / denom for x in ex], axis=0)

    member = jnp.zeros((n_exp, tm), f32)
    for hot in hots:
        member = member + hot.astype(f32)
    earlier = (lax.broadcasted_iota(i32, (tm, tm), 0) < lax.broadcasted_iota(i32, (tm, tm), 1)).astype(bf16)
    before = jnp.dot(member.astype(bf16), earlier, preferred_element_type=f32) + carry_ref[...]
    rank_ref[...] = jnp.concatenate(
        [jnp.sum(jnp.where(hot, before, 0.0), axis=0, keepdims=True) for hot in hots], axis=0).astype(i32)
    carry_ref[...] += jnp.sum(member, axis=1, keepdims=True)
    cnt_ref[0] = carry_ref[...].astype(i32)


def router(h, g, w_router, b_router, tc):
    n, d = h.shape
    n_exp = w_router.shape[1]
    tiles = d // LANES
    tm = ROW_TILE
    while tc % tm:
        tm -= LANES
    tiles_per_chunk = tc // tm
    kern = functools.partial(_router_kernel, tiles_per_chunk=tiles_per_chunk, n_exp=n_exp)
    return pl.pallas_call(
        kern,
        out_shape=(
            jax.ShapeDtypeStruct((n * tiles, LANES), f32),
            jax.ShapeDtypeStruct((TOP_K, n), i32),
            jax.ShapeDtypeStruct((TOP_K, n), f32),
            jax.ShapeDtypeStruct((TOP_K, n), i32),
            jax.ShapeDtypeStruct((n // tc, n_exp, 1), i32),
        ),
        grid=(n // tm,),
        in_specs=[
            pl.BlockSpec((tm, d), lambda i: (i, 0)),
            pl.BlockSpec((1, d), lambda i: (0, 0)),
            pl.BlockSpec((n_exp, d), lambda i: (0, 0)),
            pl.BlockSpec((n_exp, 1), lambda i: (0, 0)),
        ],
        out_specs=(
            pl.BlockSpec((tm * tiles, LANES), lambda i: (i, 0)),
            pl.BlockSpec((TOP_K, tm), lambda i: (0, i)),
            pl.BlockSpec((TOP_K, tm), lambda i: (0, i)),
            pl.BlockSpec((TOP_K, tm), lambda i: (0, i)),
            pl.BlockSpec((1, n_exp, 1), lambda i: (i // tiles_per_chunk, 0, 0)),
        ),
        scratch_shapes=[pltpu.VMEM((n_exp, 1), f32)],
        compiler_params=_cparams(("arbitrary",), 32),
        name="router",
    )(h, g.reshape(1, d), w_router.T, b_router.reshape(n_exp, 1))


def _split_pairs(w_ref, p_ref, o_ref):
    width = p_ref.shape[0]
    for b in range(w_ref.shape[2] // width):
        cols = slice(b * width, (b + 1) * width)
        o_ref[0, :, cols] = jnp.dot(w_ref[0, :, cols].astype(bf16), p_ref[...],
                                    preferred_element_type=f32).astype(bf16)


def pair_split_matrix():
    width = 2 * LANES
    j = jnp.arange(width)
    src = jnp.where(j < LANES, 2 * j, 2 * (j - LANES) + 1)
    return (jnp.arange(width)[:, None] == src[None, :]).astype(bf16)


def _dense_row_index(r, tiles):
    return (r // SUBLANES) * tiles * SUBLANES + r % SUBLANES


def _experts_kernel(start_ref, nsub_ref, tail_ref, xp_hbm, dest_hbm, gates_hbm, fill_hbm, wup_lo_ref, wup_hi_ref,
                    bup_ref, wdn_ref, bdn_ref,
                    f_hbm, xs_ref, acc_ref, xt_a, xt_b, y_a, y_b, dest_s, gate_s, rmap_s, sem_ref,
                    *, tc, n_exp, d, null_row0, seg, half_len, place_per_step):
    c = pl.program_id(0)
    e = pl.program_id(1)
    ms = MOE_SUB
    tiles = d // LANES
    cur = pl.multiple_of((c % 2) * half_len, LANES)
    nxt = pl.multiple_of(half_len - cur, LANES)

    def gather(row0, xt_ref):
        for r in range(ms):
            t = rmap_s[cur + row0 + r] & (seg - 1)
            xt_ref[pl.ds(_dense_row_index(r, tiles), tiles, stride=SUBLANES), :] = xs_ref[t]

    def place(base, first, count):
        for k in range(TOP_K):
            for u in range(count):
                a = k * seg + first + u
                rmap_s[base + dest_s[a]] = a

    def scatter(row0, y_ref):
        for r0 in range(0, ms, SCATTER_UNROLL):
            toks, sums = [], []
            for r in range(r0, r0 + SCATTER_UNROLL):
                a = rmap_s[cur + row0 + r]
                t = a & (seg - 1)
                yrow = y_ref[pl.ds(_dense_row_index(r, tiles), tiles, stride=SUBLANES), :]
                toks.append(t)
                sums.append(acc_ref[t] + gate_s[a] * yrow)
            for t, s in zip(toks, sums):
                acc_ref[t] = s

    @pl.when(e == 0)
    def _():
        copies = [
            pltpu.make_async_copy(xp_hbm.at[pl.ds(c * tc, tc)], xs_ref.at[pl.ds(0, tc)], sem_ref.at[0]),
            pltpu.make_async_copy(gates_hbm.at[c], gate_s, sem_ref.at[1]),
        ]
        for cp in copies:
            cp.start()
        xs_ref[pl.ds(tc, SUBLANES)] = jnp.zeros((SUBLANES, tiles, LANES), f32)
        acc_ref[...] = jnp.zeros_like(acc_ref)

        def load_map_inputs(chunk, base):
            loads = [
                pltpu.make_async_copy(dest_hbm.at[chunk], dest_s, sem_ref.at[2]),
                pltpu.make_async_copy(fill_hbm, rmap_s.at[pl.ds(base, half_len)], sem_ref.at[3]),
            ]
            for cp in loads:
                cp.start()
            for cp in loads:
                cp.wait()

        @pl.when(c == 0)
        def _():
            y_a[...] = jnp.zeros_like(y_a)
            y_b[...] = jnp.zeros_like(y_b)
            load_map_inputs(0, 0)

            def place_all(i, carry):
                place(0, i * SUBLANES, SUBLANES)
                return carry

            lax.fori_loop(0, tc // SUBLANES, place_all, 0)

        load_map_inputs(jnp.minimum(c + 1, pl.num_programs(0) - 1), nxt)
        for cp in copies:
            cp.wait()
        gather(0, xt_a)

    g = c * n_exp + e
    j0 = start_ref[g]

    def mlp(xt_cur, y_cur, first_group, groups):
        x = jnp.concatenate(
            [jnp.concatenate([xt_cur[pl.ds((rg * tiles + jt) * SUBLANES, SUBLANES), :] for jt in range(tiles)], axis=1)
             for rg in range(first_group, first_group + groups)], axis=0).astype(bf16)
        hmid = jnp.concatenate([jnp.dot(x, w_ref[0], preferred_element_type=f32)
                                for w_ref in (wup_lo_ref, wup_hi_ref)], axis=1) + bup_ref[0]
        glu = jnp.concatenate([hmid[:, 2 * jt * LANES:(2 * jt + 1) * LANES] for jt in range(tiles)], axis=1)
        lin = jnp.concatenate([hmid[:, (2 * jt + 1) * LANES:(2 * jt + 2) * LANES] for jt in range(tiles)], axis=1)
        glu = jnp.minimum(glu, SWIGLU_LIMIT)
        lin = jnp.clip(lin, -SWIGLU_LIMIT, SWIGLU_LIMIT)
        act = glu * jax.nn.sigmoid(SWIGLU_ALPHA * glu) * (lin + 1.0)
        y = jnp.dot(act.astype(bf16), wdn_ref[0], preferred_element_type=f32) + bdn_ref[0]
        for rg in range(groups):
            for jt in range(tiles):
                y_cur[pl.ds(((first_group + rg) * tiles + jt) * SUBLANES, SUBLANES), :] = (
                    y[rg * SUBLANES:(rg + 1) * SUBLANES, jt * LANES:(jt + 1) * LANES])

    def step(j, groups, xt_cur, xt_nxt, y_cur, y_prv):
        gather((j + 1) * ms, xt_nxt)
        scatter(jnp.where(j == 0, null_row0, (j - 1) * ms), y_prv)
        place(nxt, j * place_per_step, place_per_step)
        mlp(xt_cur, y_cur, 0, groups)

    def sub_block(i, carry):
        j = j0 + i
        full = jnp.logical_or(i < nsub_ref[g] - 1, tail_ref[g] > ms // 2)
        buffers = ((xt_a, xt_b, y_a, y_b), (xt_b, xt_a, y_b, y_a))
        for parity in range(2):
            for groups, wanted in ((ms // SUBLANES, full), (ms // SUBLANES // 2, jnp.logical_not(full))):
                @pl.when(jnp.logical_and(j % 2 == parity, wanted))
                def _():
                    step(j, groups, *buffers[parity])

        return carry

    lax.fori_loop(0, nsub_ref[g], sub_block, 0)

    @pl.when(e == n_exp - 1)
    def _():
        last = j0 + nsub_ref[g] - 1

        @pl.when(last % 2 == 0)
        def _():
            scatter(last * ms, y_a)

        @pl.when(last % 2 == 1)
        def _():
            scatter(last * ms, y_b)

        cp = pltpu.make_async_copy(acc_ref.at[pl.ds(0, tc)], f_hbm.at[pl.ds(c * tc, tc)], sem_ref.at[0])
        cp.start()
        cp.wait()


def _row_map_geometry(tc, n_exp, seg):
    assert seg > tc and seg & (seg - 1) == 0
    max_sub = (TOP_K * tc + n_exp * (MOE_SUB - 1)) // MOE_SUB
    min_sub = -(-TOP_K * tc // MOE_SUB)
    null_row0 = (max_sub + 1) * MOE_SUB
    half_len = -(-(null_row0 + MOE_SUB + 1) // LANES) * LANES
    place_per_step = -(-tc // min_sub)
    assert max_sub * place_per_step <= seg
    return null_row0, half_len, place_per_step


def experts(sb_start, n_sub, tail_rows, xp, dest, gates, w_up_halves, b_up, w_down, b_down, tc):
    n, tiles, _ = xp.shape
    d = tiles * LANES
    n_exp = w_down.shape[0]
    chunks, padded_len = dest.shape
    seg = padded_len // TOP_K
    null_row0, half_len, place_per_step = _row_map_geometry(tc, n_exp, seg)
    fill = jnp.full((half_len,), tc, i32)
    kern = functools.partial(_experts_kernel, tc=tc, n_exp=n_exp, d=d, null_row0=null_row0, seg=seg,
                             half_len=half_len, place_per_step=place_per_step)
    block = pltpu.VMEM((MOE_SUB * tiles, LANES), f32)
    return pl.pallas_call(
        kern,
        out_shape=jax.ShapeDtypeStruct((n, tiles, LANES), f32),
        grid_spec=pltpu.PrefetchScalarGridSpec(
            num_scalar_prefetch=3,
            grid=(chunks, n_exp),
            in_specs=[
                pl.BlockSpec(memory_space=pl.ANY),
                pl.BlockSpec(memory_space=pl.ANY),
                pl.BlockSpec(memory_space=pl.ANY),
                pl.BlockSpec(memory_space=pl.ANY),
                pl.BlockSpec((1, d, d), lambda c, e, *_: (e, 0, 0)),
                pl.BlockSpec((1, d, d), lambda c, e, *_: (e, 0, 0)),
                pl.BlockSpec((1, 1, 2 * d), lambda c, e, *_: (e, 0, 0)),
                pl.BlockSpec((1, d, d), lambda c, e, *_: (e, 0, 0)),
                pl.BlockSpec((1, 1, d), lambda c, e, *_: (e, 0, 0)),
            ],
            out_specs=pl.BlockSpec(memory_space=pl.ANY),
            scratch_shapes=[
                pltpu.VMEM((tc + SUBLANES, tiles, LANES), f32),
                pltpu.VMEM((tc + SUBLANES, tiles, LANES), f32),
                block, block, block, block,
                pltpu.SMEM((padded_len,), i32),
                pltpu.SMEM((padded_len,), f32),
                pltpu.SMEM((2 * half_len,), i32),
                pltpu.SemaphoreType.DMA((4,)),
            ],
        ),
        compiler_params=_cparams(("arbitrary", "arbitrary"), 58),
        name="experts",
    )(sb_start, n_sub, tail_rows, xp, dest, gates, fill, *w_up_halves, b_up.reshape(n_exp, 1, 2 * d), w_down,
      b_down.reshape(n_exp, 1, d))


def _final_kernel(h_ref, f_ref, g_ref, o_ref):
    tm, d = h_ref.shape
    tiles = d // LANES
    parts = [h_ref[:, jt * LANES:(jt + 1) * LANES] + f_ref[pl.ds(jt, tm, stride=tiles), :] for jt in range(tiles)]
    o_ref[...] = _rms(jnp.concatenate(parts, axis=1), g_ref[...])


def final_norm(h, f2, g, row0, rows):
    d = h.shape[1]
    tiles = d // LANES
    tm = min(ROW_TILE, rows)
    blk0 = row0 // tm
    assert rows % tm == 0 and row0 % tm == 0
    return pl.pallas_call(
        _final_kernel,
        out_shape=jax.ShapeDtypeStruct((rows, d), f32),
        grid=(rows // tm,),
        in_specs=[
            pl.BlockSpec((tm, d), lambda i: (blk0 + i, 0)),
            pl.BlockSpec((tm * tiles, LANES), lambda i: (blk0 + i, 0)),
            pl.BlockSpec((1, d), lambda i: (0, 0)),
        ],
        out_specs=pl.BlockSpec((tm, d), lambda i: (i, 0)),
        compiler_params=_cparams(("arbitrary",), 32),
        name="final_norm",
    )(h, f2, g.reshape(1, d))


def moe(h, g, w_router, b_router, w_up_p, b_up, w_down_b, b_down):
    n, d = h.shape
    n_exp = w_router.shape[1]
    tc = n // MOE_CHUNKS
    xp, top_e, gates, rank, counts = router(h, g, w_router, b_router, tc)
    n_sub = (counts.reshape(MOE_CHUNKS, n_exp) + MOE_SUB - 1) // MOE_SUB
    sb_start = jnp.cumsum(n_sub, axis=1) - n_sub
    hot = top_e.reshape(TOP_K, MOE_CHUNKS, tc, 1) == jnp.arange(n_exp, dtype=i32)
    row_base = (sb_start * MOE_SUB).reshape(1, MOE_CHUNKS, 1, n_exp)
    dest = jnp.sum(jnp.where(hot, row_base, 0), axis=-1) + rank.reshape(TOP_K, MOE_CHUNKS, tc)
    seg = 1 << tc.bit_length()
    spare_row = _row_map_geometry(tc, n_exp, seg)[1] - 1
    per_chunk = lambda a, tail: jnp.pad(a.transpose(1, 0, 2), ((0, 0), (0, 0), (0, seg - tc)),
                                        constant_values=tail).reshape(MOE_CHUNKS, TOP_K * seg)
    tiles = d // LANES
    b_up_p = b_up.reshape(n_exp, tiles, LANES, 2).transpose(0, 1, 3, 2).reshape(n_exp, 2 * d)
    tail_rows = counts.reshape(MOE_CHUNKS, n_exp) - (n_sub - 1) * MOE_SUB
    flat = lambda a: a.reshape(-1).astype(i32)
    f = experts(flat(sb_start), flat(n_sub), flat(tail_rows), xp.reshape(n, tiles, LANES),
                per_chunk(dest, spare_row), per_chunk(gates.reshape(TOP_K, MOE_CHUNKS, tc), 0.0), w_up_p, b_up_p,
                w_down_b, b_down, tc)
    return f.reshape(n * tiles, LANES)


def kernel(x_prompt, x_sample, mem_prompt, state_ret, state_conv, cache_mem_k, cache_mem_v, norm_mix, w_in, conv_w,
           ret_gn, w_out, norm_cross, norm_mem, w_mq, w_mk, w_mv, w_mo, norm_ffn, w_router, b_router, w_up, b_up,
           w_down, b_down, norm_final):
    batch, seq, d = x_prompt.shape
    n_seq, ts, _ = x_sample.shape
    depth = w_in.shape[0]
    n_mem = mem_prompt.shape[1]
    r = ret_gn.shape[1]
    cd_width = conv_w.shape[2]
    dh = r // N_RET_HEADS
    n_p = batch * seq
    n_s = n_seq * ts
    n = n_p + n_s
    nblk = n_seq // SAMPLE_BLOCK
    assert seq % ROW_TILE == 0 and n % ROW_TILE == 0 and n % MOE_CHUNKS == 0

    xs = x_sample.reshape(nblk, SAMPLE_BLOCK, ts, d).transpose(0, 2, 1, 3).reshape(n_s, d)
    h_p, h_s = x_prompt.reshape(n_p, d), xs
    tab_p = prompt_tables(seq, dh)
    tab_s = sample_tables(ts, float(PAST_LEN), dh)

    ret_p, conv_p, mk_p, mv_p, ret_s, conv_s = [], [], [], [], [], []
    for l in range(depth):
        w_in_b, w_out_b, w_mq_b = w_in[l].astype(bf16), w_out[l].astype(bf16), w_mq[l].astype(bf16)
        h_p, s_p, c_p, w_down_b, w_up_lo = prompt_mixer(h_p, norm_mix[l], w_in_b, tab_p, conv_w[l], ret_gn[l],
                                                        w_out_b, batch, seq, r, cd_width, w_down[l], w_up[l])
        proj_s = norm_matmul(h_s, norm_mix[l], w_in_b, "in_proj_sample")
        mix_s, s_s, c_s = sample_mixer(proj_s, tab_s, conv_w[l], ret_gn[l], state_ret[l],
                                       state_conv[l].transpose(1, 0, 2), 0, ts, r, cd_width)
        h_s = matmul_res(mix_s, w_out_b, h_s, "out_proj_sample")

        mk, mv = mem_kv(mem_prompt, norm_mem[l], jnp.concatenate([w_mk[l], w_mv[l]], axis=1).astype(bf16),
                        N_MEM_HEADS)
        q_s = norm_matmul(h_s, norm_cross[l], w_mq_b, "q_sample")
        attn_s = cross_sample(q_s, cache_mem_k[l], cache_mem_v[l], 0, ts)
        h, w_up_hi = attn_block(h_p, h_s, norm_cross[l], w_mq_b, mk, mv, attn_s, w_mo[l].astype(bf16), seq, w_up[l])

        f2 = moe(h, norm_ffn[l], w_router[l], b_router[l], (w_up_lo, w_up_hi), b_up[l], w_down_b, b_down[l])
        if l + 1 < depth:
            h = h + f2.reshape(n, d // LANES, LANES).reshape(n, d)
            h_p, h_s = h, h[n_p:]

        ret_p.append(s_p)
        conv_p.append(c_p)
        mk_p.append(mk)
        mv_p.append(mv)
        ret_s.append(s_s)
        conv_s.append(c_s.transpose(1, 0, 2))

    y_p = final_norm(h, f2, norm_final, 0, n_p).reshape(batch, seq, d)
    y_s = final_norm(h, f2, norm_final, n_p, n_s)
    y_s = y_s.reshape(nblk, ts, SAMPLE_BLOCK, d).transpose(0, 2, 1, 3).reshape(n_seq, ts, d)
    return (y_p, y_s, jnp.stack(ret_p), jnp.stack(conv_p), jnp.stack(mk_p), jnp.stack(mv_p), jnp.stack(ret_s),
            jnp.stack(conv_s))
```

```python
import functools

import jax
import jax.numpy as jnp
from jax import lax
from jax.experimental import pallas as pl
from jax.experimental.pallas import tpu as pltpu

f32 = jnp.float32
bf16 = jnp.bfloat16
i32 = jnp.int32

EPS = 1e-6
ROPE_THETA = 10000.0
RET_CHUNK = 128
PAST_LEN = 16384
N_RET_HEADS = 4
N_MEM_HEADS = 4
TOP_K = 4
SWIGLU_ALPHA = 1.702
SWIGLU_LIMIT = 7.0

LANES = 128
SUBLANES = 8
MIB = 1024 * 1024

ROW_TILE = 512
SAMPLE_BLOCK = 8
MOE_CHUNKS = 4
MOE_SUB = 256
SCATTER_UNROLL = 8


def _cparams(sem, vmem_mib):
    return pltpu.CompilerParams(dimension_semantics=sem, vmem_limit_bytes=vmem_mib * MIB)


def _rms(x, g):
    ms = jnp.mean(x * x, axis=-1, keepdims=True)
    return (x * lax.rsqrt(ms + EPS)) * g


def _norm_matmul_kernel(x_ref, g_ref, w_ref, o_ref):
    xn = _rms(x_ref[...], g_ref[...])
    o_ref[...] = jnp.dot(xn.astype(bf16), w_ref[...], preferred_element_type=f32)


def norm_matmul(x, g, w, name):
    m, d = x.shape
    f = w.shape[1]
    tm = min(ROW_TILE, m)
    assert m % tm == 0
    return pl.pallas_call(
        _norm_matmul_kernel,
        out_shape=jax.ShapeDtypeStruct((m, f), f32),
        grid=(m // tm,),
        in_specs=[
            pl.BlockSpec((tm, d), lambda i: (i, 0)),
            pl.BlockSpec((1, d), lambda i: (0, 0)),
            pl.BlockSpec((d, f), lambda i: (0, 0)),
        ],
        out_specs=pl.BlockSpec((tm, f), lambda i: (i, 0)),
        compiler_params=_cparams(("arbitrary",), 48),
        name=name,
    )(x, g.reshape(1, d), w)


def _stacked_specs(m_p, m_s, width):
    tm = ROW_TILE
    assert m_p % tm == 0 and m_s == tm
    prompt_tiles = m_p // tm
    return prompt_tiles, [
        pl.BlockSpec((tm, width), lambda i: (jnp.minimum(i, prompt_tiles - 1), 0)),
        pl.BlockSpec((tm, width), lambda i: (0, 0)),
    ]


def _matmul_res_kernel(a_ref, w_ref, r_ref, o_ref):
    o_ref[...] = r_ref[...] + jnp.dot(a_ref[...], w_ref[...], preferred_element_type=f32)


def matmul_res(a, w, res, name):
    m, d = a.shape
    f = w.shape[1]
    tm = min(ROW_TILE, m)
    assert m % tm == 0
    return pl.pallas_call(
        _matmul_res_kernel,
        out_shape=jax.ShapeDtypeStruct((m, f), f32),
        grid=(m // tm,),
        in_specs=[
            pl.BlockSpec((tm, d), lambda i: (i, 0)),
            pl.BlockSpec((d, f), lambda i: (0, 0)),
            pl.BlockSpec((tm, f), lambda i: (i, 0)),
        ],
        out_specs=pl.BlockSpec((tm, f), lambda i: (i, 0)),
        compiler_params=_cparams(("arbitrary",), 32),
        name=name,
    )(a, w, res)


def _rope(x, cos, sin_signed, half):
    return x * cos + pltpu.roll(x, half, 1) * sin_signed


def _group_norm_gate(o, gate, gn):
    mu = jnp.mean(o, axis=-1, keepdims=True)
    var = jnp.mean(jnp.square(o - mu), axis=-1, keepdims=True)
    on = ((o - mu) * lax.rsqrt(var + EPS)) * gn
    return (gate * jax.nn.sigmoid(gate)) * on


def _prompt_mixer_kernel(cd_ref, x_ref, g_ref, win_ref, cos_ref, sin_ref, decay_ref, xi_ref, zeta_ref, convw_ref,
                         gn_ref, wout_ref, wcast_ref, wsplit_ref, perm_ref, h_ref, sfin_ref, cfin_ref, wcast_out_ref,
                         wsplit_out_ref, s_scr, u_scr, proj_ref, mix_ref, *, tt, r, cd_width, dh):
    wcast_out_ref[...] = wcast_ref[...].astype(bf16)
    _split_pairs(wsplit_ref, perm_ref, wsplit_out_ref)
    j = pl.program_id(1)
    heads = r // dh
    pad = SUBLANES

    @pl.when(j == 0)
    def _():
        s_scr[...] = jnp.zeros_like(s_scr)
        u_scr[0:pad, :] = jnp.zeros((pad, cd_width), f32)

    proj_ref[...] = jnp.dot(_rms(x_ref[...], g_ref[...]).astype(bf16), win_ref[...], preferred_element_type=f32)

    k_scale = dh ** -0.5
    for c in range(tt // RET_CHUNK):
        rows = pl.ds(c * RET_CHUNK, RET_CHUNK)
        cosb = cos_ref[rows, :]
        sinb = sin_ref[rows, :]
        for h in range(heads):
            lo, hi = h * dh, (h + 1) * dh
            qh = _rope(proj_ref[rows, lo:hi], cosb, sinb, dh // 2)
            kh = _rope(proj_ref[rows, r + lo:r + hi], cosb, sinb, dh // 2) * k_scale
            vb = proj_ref[rows, 2 * r + lo:2 * r + hi].astype(bf16)
            gate = proj_ref[rows, 3 * r + lo:3 * r + hi]
            qb = qh.astype(bf16)
            kb = kh.astype(bf16)
            scores = lax.dot_general(qb, kb, (((1,), (1,)), ((), ())), preferred_element_type=f32) * decay_ref[h]
            inner = jnp.dot(scores.astype(bf16), vb, preferred_element_type=f32)
            s_prev = s_scr[h]
            cross = jnp.dot((qh * xi_ref[h]).astype(bf16), s_prev.astype(bf16), preferred_element_type=f32)
            kv = lax.dot_general((kh * zeta_ref[h]).astype(bf16), vb, (((0,), (0,)), ((), ())),
                                 preferred_element_type=f32)
            s_scr[h] = cd_ref[h] * s_prev + kv
            mix_ref[rows, lo:hi] = _group_norm_gate(inner + cross, gate, gn_ref[:, lo:hi]).astype(bf16)

    base = 4 * r
    u = proj_ref[:, base + cd_width:base + 2 * cd_width] * proj_ref[:, base:base + cd_width]
    u_scr[pad:pad + tt, :] = u
    conv = (u_scr[pad - 2:pad - 2 + tt, :] * convw_ref[0:1, :] + u_scr[pad - 1:pad - 1 + tt, :] * convw_ref[1:2, :]
            + u * convw_ref[2:3, :])
    mix_ref[:, r:r + cd_width] = (proj_ref[:, base + 2 * cd_width:base + 3 * cd_width] * conv).astype(bf16)
    u_scr[0:pad, :] = u_scr[tt:tt + pad, :]
    cfin_ref[0] = u_scr[pad - 2:pad, :]
    h_ref[...] = x_ref[...] + jnp.dot(mix_ref[...], wout_ref[...], preferred_element_type=f32)

    @pl.when(j == pl.num_programs(1) - 1)
    def _():
        sfin_ref[0] = s_scr[...]


def prompt_mixer(x, g, w_in, tables, conv_w, ret_gn, w_out, batch, seq, r, cd_width, w_cast, w_split):
    tt = ROW_TILE
    d = x.shape[1]
    dh = r // N_RET_HEADS
    steps = seq // tt
    in_cols = w_in.shape[1]
    assert w_cast.shape[0] == batch * steps and w_split.shape[0] == batch * steps
    kern = functools.partial(_prompt_mixer_kernel, tt=tt, r=r, cd_width=cd_width, dh=dh)
    tab = lambda shape: pl.BlockSpec(shape, lambda b, j, cd: (0,) * len(shape))
    row_tile = pl.BlockSpec((tt, d), lambda b, j, cd: (b * steps + j, 0))
    cast_slice = pl.BlockSpec((1,) + w_cast.shape[1:], lambda b, j, cd: (b * steps + j, 0, 0))
    split_shape = (w_split.shape[0], w_split.shape[1], w_split.shape[2] // 2)
    split_slice = pl.BlockSpec((1,) + split_shape[1:], lambda b, j, cd: (b * steps + j, 0, 0))
    perm = pair_split_matrix()
    return pl.pallas_call(
        kern,
        out_shape=(
            jax.ShapeDtypeStruct((batch * seq, d), f32),
            jax.ShapeDtypeStruct((batch, N_RET_HEADS, dh, dh), f32),
            jax.ShapeDtypeStruct((batch, 2, cd_width), f32),
            jax.ShapeDtypeStruct(w_cast.shape, bf16),
            jax.ShapeDtypeStruct(split_shape, bf16),
        ),
        grid_spec=pltpu.PrefetchScalarGridSpec(
            num_scalar_prefetch=1,
            grid=(batch, steps),
            in_specs=[
                row_tile,
                tab((1, d)),
                tab((d, in_cols)),
                pl.BlockSpec((tt, dh), lambda b, j, cd: (j, 0)),
                pl.BlockSpec((tt, dh), lambda b, j, cd: (j, 0)),
                tab((N_RET_HEADS, RET_CHUNK, RET_CHUNK)),
                tab((N_RET_HEADS, RET_CHUNK, dh)),
                tab((N_RET_HEADS, RET_CHUNK, dh)),
                tab((3, cd_width)),
                tab((1, r)),
                tab((r + cd_width, d)),
                cast_slice,
                split_slice,
                tab(perm.shape),
            ],
            out_specs=(
                row_tile,
                pl.BlockSpec((1, N_RET_HEADS, dh, dh), lambda b, j, cd: (b, 0, 0, 0)),
                pl.BlockSpec((1, 2, cd_width), lambda b, j, cd: (b, 0, 0)),
                cast_slice,
                split_slice,
            ),
            scratch_shapes=[
                pltpu.VMEM((N_RET_HEADS, dh, dh), f32),
                pltpu.VMEM((tt + 2 * SUBLANES, cd_width), f32),
                pltpu.VMEM((tt, in_cols), f32),
                pltpu.VMEM((tt, r + cd_width), bf16),
            ],
        ),
        compiler_params=_cparams(("arbitrary", "arbitrary"), 56),
        name="prompt_mixer",
    )(tables["chunk_decay"], x, g.reshape(1, d), w_in, tables["cos"], tables["sin"], tables["decay"], tables["xi"],
      tables["zeta"], conv_w, ret_gn.reshape(1, r), w_out, w_cast, w_split, perm)


def _sample_mixer_kernel(dec_ref, proj_ref, cos_ref, sin_ref, xi_ref, zeta_ref, convw_ref, gn_ref, s0_ref, c0_ref,
                         mix_ref, snew_ref, cnew_ref, *, ts, r, cd_width, dh):
    heads = r // dh
    nb = SAMPLE_BLOCK
    rows = ts * nb
    k_scale = dh ** -0.5
    cosb = cos_ref[...]
    sinb = sin_ref[...]
    seq_of_row = lax.broadcasted_iota(i32, (rows, dh), 0) % nb
    for h in range(heads):
        lo, hi = h * dh, (h + 1) * dh
        qh = _rope(proj_ref[:, lo:hi], cosb, sinb, dh // 2)
        kh = _rope(proj_ref[:, r + lo:r + hi], cosb, sinb, dh // 2) * k_scale
        vh = proj_ref[:, 2 * r + lo:2 * r + hi]
        gate = proj_ref[:, 3 * r + lo:3 * r + hi]
        inner = []
        for t in range(ts):
            qt = qh[t * nb:(t + 1) * nb]
            acc = jnp.zeros((nb, dh), f32)
            for s in range(t + 1):
                sc = jnp.sum(qt * kh[s * nb:(s + 1) * nb], axis=-1, keepdims=True) * dec_ref[h * (ts + 1) + t - s]
                acc = acc + sc * vh[s * nb:(s + 1) * nb]
            inner.append(acc)
        inner = jnp.concatenate(inner, axis=0)
        qx = (qh * xi_ref[h]).astype(bf16)
        kz = kh * zeta_ref[h]
        vb = vh.astype(bf16)
        cross = jnp.zeros((rows, dh), f32)
        for b in range(nb):
            mine = seq_of_row == b
            s_prev = s0_ref[b, h]
            res = jnp.dot(qx, s_prev.astype(bf16), preferred_element_type=f32)
            cross = cross + jnp.where(mine, res, 0.0)
            kv = lax.dot_general(jnp.where(mine, kz, 0.0).astype(bf16), vb, (((0,), (0,)), ((), ())),
                                 preferred_element_type=f32)
            snew_ref[b, h] = dec_ref[h * (ts + 1) + ts] * s_prev + kv
        mix_ref[:, lo:hi] = _group_norm_gate(inner + cross, gate, gn_ref[:, lo:hi]).astype(bf16)

    base = 4 * r
    u = proj_ref[:, base + cd_width:base + 2 * cd_width] * proj_ref[:, base:base + cd_width]
    full = [c0_ref[0], c0_ref[1]] + [u[t * nb:(t + 1) * nb] for t in range(ts)]
    conv = jnp.concatenate(
        [full[t] * convw_ref[0:1, :] + full[t + 1] * convw_ref[1:2, :] + full[t + 2] * convw_ref[2:3, :]
         for t in range(ts)], axis=0)
    mix_ref[:, r:r + cd_width] = (proj_ref[:, base + 2 * cd_width:base + 3 * cd_width] * conv).astype(bf16)
    cnew_ref[0] = full[ts]
    cnew_ref[1] = full[ts + 1]


def sample_mixer(proj, tables, conv_w, ret_gn, state_ret, state_conv_t, row0, ts, r, cd_width):
    n_seq = state_ret.shape[0]
    dh = r // N_RET_HEADS
    rows = ts * SAMPLE_BLOCK
    blk0 = row0 // rows
    assert row0 % rows == 0 and n_seq % SAMPLE_BLOCK == 0
    in_cols = proj.shape[1]
    kern = functools.partial(_sample_mixer_kernel, ts=ts, r=r, cd_width=cd_width, dh=dh)
    tab = lambda shape: pl.BlockSpec(shape, lambda i, d: (0,) * len(shape))
    return pl.pallas_call(
        kern,
        out_shape=(
            jax.ShapeDtypeStruct((n_seq * ts, r + cd_width), bf16),
            jax.ShapeDtypeStruct(state_ret.shape, f32),
            jax.ShapeDtypeStruct(state_conv_t.shape, f32),
        ),
        grid_spec=pltpu.PrefetchScalarGridSpec(
            num_scalar_prefetch=1,
            grid=(n_seq // SAMPLE_BLOCK,),
            in_specs=[
                pl.BlockSpec((rows, in_cols), lambda i, d: (blk0 + i, 0)),
                tab((rows, dh)),
                tab((rows, dh)),
                tab((N_RET_HEADS, rows, dh)),
                tab((N_RET_HEADS, rows, dh)),
                tab((3, cd_width)),
                tab((1, r)),
                pl.BlockSpec((SAMPLE_BLOCK, N_RET_HEADS, dh, dh), lambda i, d: (i, 0, 0, 0)),
                pl.BlockSpec((2, SAMPLE_BLOCK, cd_width), lambda i, d: (0, i, 0)),
            ],
            out_specs=(
                pl.BlockSpec((rows, r + cd_width), lambda i, d: (i, 0)),
                pl.BlockSpec((SAMPLE_BLOCK, N_RET_HEADS, dh, dh), lambda i, d: (i, 0, 0, 0)),
                pl.BlockSpec((2, SAMPLE_BLOCK, cd_width), lambda i, d: (0, i, 0)),
            ),
        ),
        compiler_params=_cparams(("arbitrary",), 32),
        name="sample_mixer",
    )(tables["dec"], proj, tables["cos"], tables["sin"], tables["xi"], tables["zeta"], conv_w, ret_gn.reshape(1, r),
      state_ret, state_conv_t)


def _log_gamma():
    return jnp.log1p(-jnp.exp2(-5.0 - jnp.arange(N_RET_HEADS, dtype=f32)))


def _rope_tables(pos, dh):
    half = dh // 2
    inv = ROPE_THETA ** (-jnp.arange(half, dtype=f32) / half)
    ang = pos[:, None] * inv[None, :]
    cos, sin = jnp.cos(ang), jnp.sin(ang)
    return jnp.concatenate([cos, cos], axis=-1), jnp.concatenate([-sin, sin], axis=-1)


def prompt_tables(seq, dh):
    c = RET_CHUNK
    lg = _log_gamma()
    i = jnp.arange(c, dtype=f32)
    diff = i[:, None] - i[None, :]
    decay = jnp.where(diff >= 0, jnp.exp(jnp.maximum(diff, 0.0)[None] * lg[:, None, None]), 0.0)
    xi = jnp.exp((i + 1.0)[None, :] * lg[:, None])
    zeta = jnp.exp((c - 1.0 - i)[None, :] * lg[:, None])
    cos, sin = _rope_tables(jnp.arange(seq, dtype=f32), dh)
    bc = lambda t: jnp.broadcast_to(t[:, :, None], (N_RET_HEADS, c, dh))
    return dict(cos=cos, sin=sin, decay=decay, xi=bc(xi), zeta=bc(zeta), chunk_decay=jnp.exp(c * lg))


def sample_tables(ts, pos0, dh):
    lg = _log_gamma()
    i = jnp.arange(ts, dtype=f32)
    dec = jnp.exp(jnp.arange(ts + 1, dtype=f32)[None, :] * lg[:, None])
    xi = jnp.exp((i + 1.0)[None, :] * lg[:, None])
    zeta = jnp.exp((ts - 1.0 - i)[None, :] * lg[:, None])
    cos, sin = _rope_tables(pos0 + i, dh)
    rep = lambda t: jnp.repeat(t, SAMPLE_BLOCK, axis=0)
    bc = lambda t: jnp.broadcast_to(jnp.repeat(t, SAMPLE_BLOCK, axis=1)[:, :, None],
                                    (N_RET_HEADS, ts * SAMPLE_BLOCK, dh))
    return dict(cos=rep(cos), sin=rep(sin), xi=bc(xi), zeta=bc(zeta), dec=dec.reshape(-1))


def _softmax_rows(s):
    m = jnp.max(s, axis=-1, keepdims=True)
    p = jnp.exp(s - m)
    return p / jnp.sum(p, axis=-1, keepdims=True)


def _from_slabs(ref, rows=None):
    rows = slice(None) if rows is None else rows
    return jnp.concatenate([ref[lt, rows, :] for lt in range(ref.shape[0])], axis=1)


def _to_slabs(ref, value):
    for lt in range(ref.shape[0]):
        ref[lt] = value[:, lt * LANES:(lt + 1) * LANES]


def _mem_kv_kernel(m_ref, g_ref, w_ref, k_ref, v_ref, rows_ref, *, heads):
    n_mem, d = m_ref.shape
    dh = d // heads
    kv = jnp.dot(_rms(m_ref[...], g_ref[...]).astype(bf16), w_ref[...], preferred_element_type=f32)
    for part, o_ref in enumerate((k_ref, v_ref)):
        for h in range(heads):
            for lt in range(dh // LANES):
                lo = part * d + h * dh + lt * LANES
                rows_ref[lt, pl.ds(h, n_mem, stride=heads), :] = kv[:, lo:lo + LANES]
        o_ref[0] = _from_slabs(rows_ref).reshape(n_mem, heads, dh)


def mem_kv(mem, g, w_kv, heads):
    batch, n_mem, d = mem.shape
    dh = d // heads
    out = jax.ShapeDtypeStruct((batch, n_mem, heads, dh), f32)
    out_spec = pl.BlockSpec((1, n_mem, heads, dh), lambda b: (b, 0, 0, 0))
    return pl.pallas_call(
        functools.partial(_mem_kv_kernel, heads=heads),
        out_shape=(out, out),
        grid=(batch,),
        in_specs=[
            pl.BlockSpec((n_mem, d), lambda b: (b, 0)),
            pl.BlockSpec((1, d), lambda b: (0, 0)),
            pl.BlockSpec((d, 2 * d), lambda b: (0, 0)),
        ],
        out_specs=(out_spec, out_spec),
        scratch_shapes=[pltpu.VMEM((dh // LANES, n_mem * heads, LANES), f32)],
        compiler_params=_cparams(("arbitrary",), 32),
        name="mem_kv",
    )(mem.reshape(batch * n_mem, d), g.reshape(1, d), w_kv)


def _attn_block_kernel(hp_ref, hs_ref, g_ref, wq_ref, k_ref, v_ref, as_ref, wo_ref, wup_ref, perm_ref, o_ref,
                       wup_out_ref, k_rows, v_rows, *, prompt_tiles, heads):
    i = pl.program_id(0)

    @pl.when(i < prompt_tiles)
    def _():
        _split_pairs(wup_ref, perm_ref, wup_out_ref)
        h_ref = hp_ref
        n_mem, _, dh = k_ref.shape[1:]
        scale = dh ** -0.5
        q = jnp.dot(_rms(h_ref[...], g_ref[...]).astype(bf16), wq_ref[...], preferred_element_type=f32)
        _to_slabs(k_rows, k_ref[0].reshape(n_mem * heads, dh))
        _to_slabs(v_rows, v_ref[0].reshape(n_mem * heads, dh))
        outs = []
        for h in range(heads):
            kh = _from_slabs(k_rows, pl.ds(h, n_mem, stride=heads)).astype(bf16)
            vh = _from_slabs(v_rows, pl.ds(h, n_mem, stride=heads)).astype(bf16)
            s = lax.dot_general(q[:, h * dh:(h + 1) * dh].astype(bf16), kh, (((1,), (1,)), ((), ())),
                                preferred_element_type=f32) * scale
            outs.append(jnp.dot(_softmax_rows(s).astype(bf16), vh, preferred_element_type=f32).astype(bf16))
        attn = jnp.concatenate(outs, axis=1)
        o_ref[...] = h_ref[...] + jnp.dot(attn, wo_ref[...], preferred_element_type=f32)

    @pl.when(i == prompt_tiles)
    def _():
        o_ref[...] = hs_ref[...] + jnp.dot(as_ref[...], wo_ref[...], preferred_element_type=f32)


def attn_block(h_prompt, h_sample, g, w_mq, mk, mv, attn_sample, w_mo, seq, w_up):
    d = h_prompt.shape[1]
    batch, n_mem, heads, dh = mk.shape
    tm = ROW_TILE
    per_batch = seq // tm
    prompt_tiles, h_specs = _stacked_specs(batch * seq, h_sample.shape[0], d)
    n_exp, _, f2 = w_up.shape
    assert attn_sample.shape == (tm, d) and n_exp == prompt_tiles
    prompt_tile = lambda i: jnp.minimum(i, prompt_tiles - 1)
    kv_spec = pl.BlockSpec((1, n_mem, heads, dh), lambda i: (prompt_tile(i) // per_batch, 0, 0, 0))
    const = lambda shape: pl.BlockSpec(shape, lambda i: (0,) * len(shape))
    perm = pair_split_matrix()
    return pl.pallas_call(
        functools.partial(_attn_block_kernel, prompt_tiles=prompt_tiles, heads=heads),
        out_shape=(
            jax.ShapeDtypeStruct(((prompt_tiles + 1) * tm, d), f32),
            jax.ShapeDtypeStruct((n_exp, d, f2 // 2), bf16),
        ),
        grid=(prompt_tiles + 1,),
        in_specs=h_specs + [
            const((1, d)),
            const((d, d)),
            kv_spec,
            kv_spec,
            const((tm, d)),
            const((d, d)),
            pl.BlockSpec((1, d, f2 // 2), lambda i: (prompt_tile(i), 0, 1)),
            const(perm.shape),
        ],
        out_specs=(pl.BlockSpec((tm, d), lambda i: (i, 0)),
                   pl.BlockSpec((1, d, f2 // 2), lambda i: (prompt_tile(i), 0, 0))),
        scratch_shapes=[pltpu.VMEM((dh // LANES, n_mem * heads, LANES), f32)] * 2,
        compiler_params=_cparams(("arbitrary",), 56),
        name="attn_block",
    )(h_prompt, h_sample, g.reshape(1, d), w_mq, mk, mv, attn_sample, w_mo, w_up, perm)


def _cross_sample_kernel(q_ref, k_ref, v_ref, o_ref, acc_ref, *, dh, seqs):
    part = pl.program_id(1)
    rows = q_ref.shape[0]
    scale = dh ** -0.5

    @pl.when(part == 0)
    def _():
        acc_ref[...] = jnp.zeros_like(acc_ref)

    heads = N_MEM_HEADS
    n_mem = k_ref.shape[1]
    qx = jnp.concatenate([q_ref[:, h * dh:(h + 1) * dh] for h in range(heads)], axis=0).astype(bf16)
    row = lax.broadcasted_iota(i32, (heads * rows, n_mem * heads), 0)
    col = lax.broadcasted_iota(i32, (heads * rows, n_mem * heads), 1)
    same_head = col % heads == row // rows
    seq_of_row = lax.broadcasted_iota(i32, (heads * rows, dh), 0) % SAMPLE_BLOCK
    out = jnp.zeros((heads * rows, dh), f32)
    for b in range(seqs):
        k2 = k_ref[b].reshape(n_mem * heads, dh).astype(bf16)
        v2 = v_ref[b].reshape(n_mem * heads, dh).astype(bf16)
        s = lax.dot_general(qx, k2, (((1,), (1,)), ((), ())), preferred_element_type=f32) * scale
        p = _softmax_rows(jnp.where(same_head, s, -jnp.inf))
        o = jnp.dot(p.astype(bf16), v2, preferred_element_type=f32)
        out = out + jnp.where(seq_of_row == part * seqs + b, o, 0.0)
    acc_ref[...] += out

    @pl.when(part == pl.num_programs(1) - 1)
    def _():
        for h in range(heads):
            o_ref[:, h * dh:(h + 1) * dh] = acc_ref[h * rows:(h + 1) * rows, :].astype(bf16)


def cross_sample(q, cache_k, cache_v, row0, ts):
    n_seq, n_mem, heads, dh = cache_k.shape
    d = heads * dh
    rows = ts * SAMPLE_BLOCK
    blk0 = row0 // rows
    parts = 2
    seqs = SAMPLE_BLOCK // parts
    kern = functools.partial(_cross_sample_kernel, dh=dh, seqs=seqs)
    return pl.pallas_call(
        kern,
        out_shape=jax.ShapeDtypeStruct((n_seq * ts, d), bf16),
        grid=(n_seq // SAMPLE_BLOCK, parts),
        in_specs=[
            pl.BlockSpec((rows, d), lambda i, p: (blk0 + i, 0)),
            pl.BlockSpec((seqs, n_mem, heads, dh), lambda i, p: (i * parts + p, 0, 0, 0)),
            pl.BlockSpec((seqs, n_mem, heads, dh), lambda i, p: (i * parts + p, 0, 0, 0)),
        ],
        out_specs=pl.BlockSpec((rows, d), lambda i, p: (i, 0)),
        scratch_shapes=[pltpu.VMEM((heads * rows, dh), f32)],
        compiler_params=_cparams(("arbitrary", "arbitrary"), 40),
        name="cross_sample",
    )(q, cache_k, cache_v)


def _router_kernel(h_ref, g_ref, wr_ref, br_ref, xp_ref, e_ref, gate_ref, rank_ref, cnt_ref, carry_ref,
                   *, tiles_per_chunk, n_exp):
    i = pl.program_id(0)

    @pl.when(i % tiles_per_chunk == 0)
    def _():
        carry_ref[...] = jnp.zeros_like(carry_ref)

    xn = _rms(h_ref[...], g_ref[...])
    tm, d = xn.shape
    tiles = d // LANES
    for rg in range(tm // SUBLANES):
        for jt in range(tiles):
            xp_ref[pl.ds(rg * SUBLANES * tiles + jt, SUBLANES, stride=tiles), :] = (
                xn[rg * SUBLANES:(rg + 1) * SUBLANES, jt * LANES:(jt + 1) * LANES])

    w = wr_ref[...]
    w_hi = w.astype(bf16)
    w_lo = (w - w_hi.astype(f32)).astype(bf16)
    x_hi = xn.astype(bf16)
    x_lo = (xn - x_hi.astype(f32)).astype(bf16)
    nt = (((1,), (1,)), ((), ()))
    both = lax.dot_general(jnp.concatenate([w_hi, w_lo], axis=0), x_hi, nt, preferred_element_type=f32)
    logits = (both[:n_exp] + both[n_exp:] + lax.dot_general(w_hi, x_lo, nt, preferred_element_type=f32)
              + br_ref[...])
    sub = lax.broadcasted_iota(i32, (n_exp, tm), 0).astype(f32)
    chosen, vals, hots = [], [], []
    work = logits
    for _ in range(TOP_K):
        m = jnp.max(work, axis=0, keepdims=True)
        idx = jnp.min(jnp.where(work == m, sub, float(n_exp)), axis=0, keepdims=True)
        hot = sub == idx
        chosen.append(idx)
        vals.append(m)
        hots.append(hot)
        work = jnp.where(hot, -jnp.inf, work)
    ex = [jnp.exp(v - vals[0]) for v in vals]
    denom = ex[0] + ex[1] + ex[2] + ex[3]
    e_ref[...] = jnp.concatenate(chosen, axis=0).astype(i32)
    gate_ref[...] = jnp.concatenate([x / denom for x in ex], axis=0)

    member = jnp.zeros((n_exp, tm), f32)
    for hot in hots:
        member = member + hot.astype(f32)
    earlier = (lax.broadcasted_iota(i32, (tm, tm), 0) < lax.broadcasted_iota(i32, (tm, tm), 1)).astype(bf16)
    before = jnp.dot(member.astype(bf16), earlier, preferred_element_type=f32) + carry_ref[...]
    rank_ref[...] = jnp.concatenate(
        [jnp.sum(jnp.where(hot, before, 0.0), axis=0, keepdims=True) for hot in hots], axis=0).astype(i32)
    carry_ref[...] += jnp.sum(member, axis=1, keepdims=True)
    cnt_ref[0] = carry_ref[...].astype(i32)


def router(h, g, w_router, b_router, tc):
    n, d = h.shape
    n_exp = w_router.shape[1]
    tiles = d // LANES
    tm = ROW_TILE
    while tc % tm:
        tm -= LANES
    tiles_per_chunk = tc // tm
    kern = functools.partial(_router_kernel, tiles_per_chunk=tiles_per_chunk, n_exp=n_exp)
    return pl.pallas_call(
        kern,
        out_shape=(
            jax.ShapeDtypeStruct((n * tiles, LANES), f32),
            jax.ShapeDtypeStruct((TOP_K, n), i32),
            jax.ShapeDtypeStruct((TOP_K, n), f32),
            jax.ShapeDtypeStruct((TOP_K, n), i32),
            jax.ShapeDtypeStruct((n // tc, n_exp, 1), i32),
        ),
        grid=(n // tm,),
        in_specs=[
            pl.BlockSpec((tm, d), lambda i: (i, 0)),
            pl.BlockSpec((1, d), lambda i: (0, 0)),
            pl.BlockSpec((n_exp, d), lambda i: (0, 0)),
            pl.BlockSpec((n_exp, 1), lambda i: (0, 0)),
        ],
        out_specs=(
            pl.BlockSpec((tm * tiles, LANES), lambda i: (i, 0)),
            pl.BlockSpec((TOP_K, tm), lambda i: (0, i)),
            pl.BlockSpec((TOP_K, tm), lambda i: (0, i)),
            pl.BlockSpec((TOP_K, tm), lambda i: (0, i)),
            pl.BlockSpec((1, n_exp, 1), lambda i: (i // tiles_per_chunk, 0, 0)),
        ),
        scratch_shapes=[pltpu.VMEM((n_exp, 1), f32)],
        compiler_params=_cparams(("arbitrary",), 32),
        name="router",
    )(h, g.reshape(1, d), w_router.T, b_router.reshape(n_exp, 1))


def _split_pairs(w_ref, p_ref, o_ref):
    width = p_ref.shape[0]
    for b in range(w_ref.shape[2] // width):
        cols = slice(b * width, (b + 1) * width)
        o_ref[0, :, cols] = jnp.dot(w_ref[0, :, cols].astype(bf16), p_ref[...],
                                    preferred_element_type=f32).astype(bf16)


def pair_split_matrix():
    width = 2 * LANES
    j = jnp.arange(width)
    src = jnp.where(j < LANES, 2 * j, 2 * (j - LANES) + 1)
    return (jnp.arange(width)[:, None] == src[None, :]).astype(bf16)


def _dense_row_index(r, tiles):
    return (r // SUBLANES) * tiles * SUBLANES + r % SUBLANES


def _experts_kernel(start_ref, nsub_ref, tail_ref, xp_hbm, dest_hbm, gates_hbm, fill_hbm, wup_lo_ref, wup_hi_ref,
                    bup_ref, wdn_ref, bdn_ref,
                    f_hbm, xs_ref, acc_ref, xt_a, xt_b, y_a, y_b, dest_s, gate_s, rmap_s, sem_ref,
                    *, tc, n_exp, d, null_row0, seg, half_len, place_per_step):
    c = pl.program_id(0)
    e = pl.program_id(1)
    ms = MOE_SUB
    tiles = d // LANES
    cur = pl.multiple_of((c % 2) * half_len, LANES)
    nxt = pl.multiple_of(half_len - cur, LANES)

    def gather(row0, xt_ref):
        for r in range(ms):
            t = rmap_s[cur + row0 + r] & (seg - 1)
            xt_ref[pl.ds(_dense_row_index(r, tiles), tiles, stride=SUBLANES), :] = xs_ref[t]

    def place(base, first, count):
        for k in range(TOP_K):
            for u in range(count):
                a = k * seg + first + u
                rmap_s[base + dest_s[a]] = a

    def scatter(row0, y_ref):
        for r0 in range(0, ms, SCATTER_UNROLL):
            toks, sums = [], []
            for r in range(r0, r0 + SCATTER_UNROLL):
                a = rmap_s[cur + row0 + r]
                t = a & (seg - 1)
                yrow = y_ref[pl.ds(_dense_row_index(r, tiles), tiles, stride=SUBLANES), :]
                toks.append(t)
                sums.append(acc_ref[t] + gate_s[a] * yrow)
            for t, s in zip(toks, sums):
                acc_ref[t] = s

    @pl.when(e == 0)
    def _():
        copies = [
            pltpu.make_async_copy(xp_hbm.at[pl.ds(c * tc, tc)], xs_ref.at[pl.ds(0, tc)], sem_ref.at[0]),
            pltpu.make_async_copy(gates_hbm.at[c], gate_s, sem_ref.at[1]),
        ]
        for cp in copies:
            cp.start()
        xs_ref[pl.ds(tc, SUBLANES)] = jnp.zeros((SUBLANES, tiles, LANES), f32)
        acc_ref[...] = jnp.zeros_like(acc_ref)

        def load_map_inputs(chunk, base):
            loads = [
                pltpu.make_async_copy(dest_hbm.at[chunk], dest_s, sem_ref.at[2]),
                pltpu.make_async_copy(fill_hbm, rmap_s.at[pl.ds(base, half_len)], sem_ref.at[3]),
            ]
            for cp in loads:
                cp.start()
            for cp in loads:
                cp.wait()

        @pl.when(c == 0)
        def _():
            y_a[...] = jnp.zeros_like(y_a)
            y_b[...] = jnp.zeros_like(y_b)
            load_map_inputs(0, 0)

            def place_all(i, carry):
                place(0, i * SUBLANES, SUBLANES)
                return carry

            lax.fori_loop(0, tc // SUBLANES, place_all, 0)

        load_map_inputs(jnp.minimum(c + 1, pl.num_programs(0) - 1), nxt)
        for cp in copies:
            cp.wait()
        gather(0, xt_a)

    g = c * n_exp + e
    j0 = start_ref[g]

    def mlp(xt_cur, y_cur, first_group, groups):
        x = jnp.concatenate(
            [jnp.concatenate([xt_cur[pl.ds((rg * tiles + jt) * SUBLANES, SUBLANES), :] for jt in range(tiles)], axis=1)
             for rg in range(first_group, first_group + groups)], axis=0).astype(bf16)
        hmid = jnp.concatenate([jnp.dot(x, w_ref[0], preferred_element_type=f32)
                                for w_ref in (wup_lo_ref, wup_hi_ref)], axis=1) + bup_ref[0]
        glu = jnp.concatenate([hmid[:, 2 * jt * LANES:(2 * jt + 1) * LANES] for jt in range(tiles)], axis=1)
        lin = jnp.concatenate([hmid[:, (2 * jt + 1) * LANES:(2 * jt + 2) * LANES] for jt in range(tiles)], axis=1)
        glu = jnp.minimum(glu, SWIGLU_LIMIT)
        lin = jnp.clip(lin, -SWIGLU_LIMIT, SWIGLU_LIMIT)
        act = glu * jax.nn.sigmoid(SWIGLU_ALPHA * glu) * (lin + 1.0)
        y = jnp.dot(act.astype(bf16), wdn_ref[0], preferred_element_type=f32) + bdn_ref[0]
        for rg in range(groups):
            for jt in range(tiles):
                y_cur[pl.ds(((first_group + rg) * tiles + jt) * SUBLANES, SUBLANES), :] = (
                    y[rg * SUBLANES:(rg + 1) * SUBLANES, jt * LANES:(jt + 1) * LANES])

    def step(j, groups, xt_cur, xt_nxt, y_cur, y_prv):
        scatter(jnp.where(j == 0, null_row0, (j - 1) * ms), y_prv)
        place(nxt, j * place_per_step, place_per_step)
        gather((j + 1) * ms, xt_nxt)
        mlp(xt_cur, y_cur, 0, groups)

    def sub_block(i, carry):
        j = j0 + i
        full = jnp.logical_or(i < nsub_ref[g] - 1, tail_ref[g] > ms // 2)
        buffers = ((xt_a, xt_b, y_a, y_b), (xt_b, xt_a, y_b, y_a))
        for parity in range(2):
            for groups, wanted in ((ms // SUBLANES, full), (ms // SUBLANES // 2, jnp.logical_not(full))):
                @pl.when(jnp.logical_and(j % 2 == parity, wanted))
                def _():
                    step(j, groups, *buffers[parity])

        return carry

    lax.fori_loop(0, nsub_ref[g], sub_block, 0)

    @pl.when(e == n_exp - 1)
    def _():
        last = j0 + nsub_ref[g] - 1

        @pl.when(last % 2 == 0)
        def _():
            scatter(last * ms, y_a)

        @pl.when(last % 2 == 1)
        def _():
            scatter(last * ms, y_b)

        cp = pltpu.make_async_copy(acc_ref.at[pl.ds(0, tc)], f_hbm.at[pl.ds(c * tc, tc)], sem_ref.at[0])
        cp.start()
        cp.wait()


def _row_map_geometry(tc, n_exp, seg):
    assert seg > tc and seg & (seg - 1) == 0
    max_sub = (TOP_K * tc + n_exp * (MOE_SUB - 1)) // MOE_SUB
    min_sub = -(-TOP_K * tc // MOE_SUB)
    null_row0 = (max_sub + 1) * MOE_SUB
    half_len = -(-(null_row0 + MOE_SUB + 1) // LANES) * LANES
    place_per_step = -(-tc // min_sub)
    assert max_sub * place_per_step <= seg
    return null_row0, half_len, place_per_step


def experts(sb_start, n_sub, tail_rows, xp, dest, gates, w_up_halves, b_up, w_down, b_down, tc):
    n, tiles, _ = xp.shape
    d = tiles * LANES
    n_exp = w_down.shape[0]
    chunks, padded_len = dest.shape
    seg = padded_len // TOP_K
    null_row0, half_len, place_per_step = _row_map_geometry(tc, n_exp, seg)
    fill = jnp.full((half_len,), tc, i32)
    kern = functools.partial(_experts_kernel, tc=tc, n_exp=n_exp, d=d, null_row0=null_row0, seg=seg,
                             half_len=half_len, place_per_step=place_per_step)
    block = pltpu.VMEM((MOE_SUB * tiles, LANES), f32)
    return pl.pallas_call(
        kern,
        out_shape=jax.ShapeDtypeStruct((n, tiles, LANES), f32),
        grid_spec=pltpu.PrefetchScalarGridSpec(
            num_scalar_prefetch=3,
            grid=(chunks, n_exp),
            in_specs=[
                pl.BlockSpec(memory_space=pl.ANY),
                pl.BlockSpec(memory_space=pl.ANY),
                pl.BlockSpec(memory_space=pl.ANY),
                pl.BlockSpec(memory_space=pl.ANY),
                pl.BlockSpec((1, d, d), lambda c, e, *_: (e, 0, 0)),
                pl.BlockSpec((1, d, d), lambda c, e, *_: (e, 0, 0)),
                pl.BlockSpec((1, 1, 2 * d), lambda c, e, *_: (e, 0, 0)),
                pl.BlockSpec((1, d, d), lambda c, e, *_: (e, 0, 0)),
                pl.BlockSpec((1, 1, d), lambda c, e, *_: (e, 0, 0)),
            ],
            out_specs=pl.BlockSpec(memory_space=pl.ANY),
            scratch_shapes=[
                pltpu.VMEM((tc + SUBLANES, tiles, LANES), f32),
                pltpu.VMEM((tc + SUBLANES, tiles, LANES), f32),
                block, block, block, block,
                pltpu.SMEM((padded_len,), i32),
                pltpu.SMEM((padded_len,), f32),
                pltpu.SMEM((2 * half_len,), i32),
                pltpu.SemaphoreType.DMA((4,)),
            ],
        ),
        compiler_params=_cparams(("arbitrary", "arbitrary"), 58),
        name="experts",
    )(sb_start, n_sub, tail_rows, xp, dest, gates, fill, *w_up_halves, b_up.reshape(n_exp, 1, 2 * d), w_down,
      b_down.reshape(n_exp, 1, d))


def _final_kernel(h_ref, f_ref, g_ref, o_ref):
    tm, d = h_ref.shape
    tiles = d // LANES
    parts = [h_ref[:, jt * LANES:(jt + 1) * LANES] + f_ref[pl.ds(jt, tm, stride=tiles), :] for jt in range(tiles)]
    o_ref[...] = _rms(jnp.concatenate(parts, axis=1), g_ref[...])


def final_norm(h, f2, g, row0, rows):
    d = h.shape[1]
    tiles = d // LANES
    tm = min(ROW_TILE, rows)
    blk0 = row0 // tm
    assert rows % tm == 0 and row0 % tm == 0
    return pl.pallas_call(
        _final_kernel,
        out_shape=jax.ShapeDtypeStruct((rows, d), f32),
        grid=(rows // tm,),
        in_specs=[
            pl.BlockSpec((tm, d), lambda i: (blk0 + i, 0)),
            pl.BlockSpec((tm * tiles, LANES), lambda i: (blk0 + i, 0)),
            pl.BlockSpec((1, d), lambda i: (0, 0)),
        ],
        out_specs=pl.BlockSpec((tm, d), lambda i: (i, 0)),
        compiler_params=_cparams(("arbitrary",), 32),
        name="final_norm",
    )(h, f2, g.reshape(1, d))


def moe(h, g, w_router, b_router, w_up_p, b_up, w_down_b, b_down):
    n, d = h.shape
    n_exp = w_router.shape[1]
    tc = n // MOE_CHUNKS
    xp, top_e, gates, rank, counts = router(h, g, w_router, b_router, tc)
    n_sub = (counts.reshape(MOE_CHUNKS, n_exp) + MOE_SUB - 1) // MOE_SUB
    sb_start = jnp.cumsum(n_sub, axis=1) - n_sub
    hot = top_e.reshape(TOP_K, MOE_CHUNKS, tc, 1) == jnp.arange(n_exp, dtype=i32)
    row_base = (sb_start * MOE_SUB).reshape(1, MOE_CHUNKS, 1, n_exp)
    dest = jnp.sum(jnp.where(hot, row_base, 0), axis=-1) + rank.reshape(TOP_K, MOE_CHUNKS, tc)
    seg = 1 << tc.bit_length()
    spare_row = _row_map_geometry(tc, n_exp, seg)[1] - 1
    per_chunk = lambda a, tail: jnp.pad(a.transpose(1, 0, 2), ((0, 0), (0, 0), (0, seg - tc)),
                                        constant_values=tail).reshape(MOE_CHUNKS, TOP_K * seg)
    tiles = d // LANES
    b_up_p = b_up.reshape(n_exp, tiles, LANES, 2).transpose(0, 1, 3, 2).reshape(n_exp, 2 * d)
    tail_rows = counts.reshape(MOE_CHUNKS, n_exp) - (n_sub - 1) * MOE_SUB
    flat = lambda a: a.reshape(-1).astype(i32)
    f = experts(flat(sb_start), flat(n_sub), flat(tail_rows), xp.reshape(n, tiles, LANES),
                per_chunk(dest, spare_row), per_chunk(gates.reshape(TOP_K, MOE_CHUNKS, tc), 0.0), w_up_p, b_up_p,
                w_down_b, b_down, tc)
    return f.reshape(n * tiles, LANES)


def kernel(x_prompt, x_sample, mem_prompt, state_ret, state_conv, cache_mem_k, cache_mem_v, norm_mix, w_in, conv_w,
           ret_gn, w_out, norm_cross, norm_mem, w_mq, w_mk, w_mv, w_mo, norm_ffn, w_router, b_router, w_up, b_up,
           w_down, b_down, norm_final):
    batch, seq, d = x_prompt.shape
    n_seq, ts, _ = x_sample.shape
    depth = w_in.shape[0]
    n_mem = mem_prompt.shape[1]
    r = ret_gn.shape[1]
    cd_width = conv_w.shape[2]
    dh = r // N_RET_HEADS
    n_p = batch * seq
    n_s = n_seq * ts
    n = n_p + n_s
    nblk = n_seq // SAMPLE_BLOCK
    assert seq % ROW_TILE == 0 and n % ROW_TILE == 0 and n % MOE_CHUNKS == 0

    xs = x_sample.reshape(nblk, SAMPLE_BLOCK, ts, d).transpose(0, 2, 1, 3).reshape(n_s, d)
    h_p, h_s = x_prompt.reshape(n_p, d), xs
    tab_p = prompt_tables(seq, dh)
    tab_s = sample_tables(ts, float(PAST_LEN), dh)

    ret_p, conv_p, mk_p, mv_p, ret_s, conv_s = [], [], [], [], [], []
    for l in range(depth):
        w_in_b, w_out_b, w_mq_b = w_in[l].astype(bf16), w_out[l].astype(bf16), w_mq[l].astype(bf16)
        h_p, s_p, c_p, w_down_b, w_up_lo = prompt_mixer(h_p, norm_mix[l], w_in_b, tab_p, conv_w[l], ret_gn[l],
                                                        w_out_b, batch, seq, r, cd_width, w_down[l], w_up[l])
        proj_s = norm_matmul(h_s, norm_mix[l], w_in_b, "in_proj_sample")
        mix_s, s_s, c_s = sample_mixer(proj_s, tab_s, conv_w[l], ret_gn[l], state_ret[l],
                                       state_conv[l].transpose(1, 0, 2), 0, ts, r, cd_width)
        h_s = matmul_res(mix_s, w_out_b, h_s, "out_proj_sample")

        mk, mv = mem_kv(mem_prompt, norm_mem[l], jnp.concatenate([w_mk[l], w_mv[l]], axis=1).astype(bf16),
                        N_MEM_HEADS)
        q_s = norm_matmul(h_s, norm_cross[l], w_mq_b, "q_sample")
        attn_s = cross_sample(q_s, cache_mem_k[l], cache_mem_v[l], 0, ts)
        h, w_up_hi = attn_block(h_p, h_s, norm_cross[l], w_mq_b, mk, mv, attn_s, w_mo[l].astype(bf16), seq, w_up[l])

        f2 = moe(h, norm_ffn[l], w_router[l], b_router[l], (w_up_lo, w_up_hi), b_up[l], w_down_b, b_down[l])
        if l + 1 < depth:
            h = h + f2.reshape(n, d // LANES, LANES).reshape(n, d)
            h_p, h_s = h, h[n_p:]

        ret_p.append(s_p)
        conv_p.append(c_p)
        mk_p.append(mk)
        mv_p.append(mv)
        ret_s.append(s_s)
        conv_s.append(c_s.transpose(1, 0, 2))

    y_p = final_norm(h, f2, norm_final, 0, n_p).reshape(batch, seq, d)
    y_s = final_norm(h, f2, norm_final, n_p, n_s)
    y_s = y_s.reshape(nblk, ts, SAMPLE_BLOCK, d).transpose(0, 2, 1, 3).reshape(n_seq, ts, d)
    return (y_p, y_s, jnp.stack(ret_p), jnp.stack(conv_p), jnp.stack(mk_p), jnp.stack(mv_p), jnp.stack(ret_s),
            jnp.stack(conv_s))
```

```python
import functools

import jax
import jax.numpy as jnp
from jax import lax
from jax.experimental import pallas as pl
from jax.experimental.pallas import tpu as pltpu

f32 = jnp.float32
bf16 = jnp.bfloat16
i32 = jnp.int32

EPS = 1e-6
ROPE_THETA = 10000.0
RET_CHUNK = 128
PAST_LEN = 16384
N_RET_HEADS = 4
N_MEM_HEADS = 4
TOP_K = 4
SWIGLU_ALPHA = 1.702
SWIGLU_LIMIT = 7.0

LANES = 128
SUBLANES = 8
MIB = 1024 * 1024

ROW_TILE = 512
SAMPLE_BLOCK = 8
MOE_CHUNKS = 4
MOE_SUB = 256
SCATTER_UNROLL = 8


def _cparams(sem, vmem_mib):
    return pltpu.CompilerParams(dimension_semantics=sem, vmem_limit_bytes=vmem_mib * MIB)


def _rms(x, g):
    ms = jnp.mean(x * x, axis=-1, keepdims=True)
    return (x * lax.rsqrt(ms + EPS)) * g


def _norm_matmul_kernel(x_ref, g_ref, w_ref, o_ref):
    xn = _rms(x_ref[...], g_ref[...])
    o_ref[...] = jnp.dot(xn.astype(bf16), w_ref[...], preferred_element_type=f32)


def norm_matmul(x, g, w, name):
    m, d = x.shape
    f = w.shape[1]
    tm = min(ROW_TILE, m)
    assert m % tm == 0
    return pl.pallas_call(
        _norm_matmul_kernel,
        out_shape=jax.ShapeDtypeStruct((m, f), f32),
        grid=(m // tm,),
        in_specs=[
            pl.BlockSpec((tm, d), lambda i: (i, 0)),
            pl.BlockSpec((1, d), lambda i: (0, 0)),
            pl.BlockSpec((d, f), lambda i: (0, 0)),
        ],
        out_specs=pl.BlockSpec((tm, f), lambda i: (i, 0)),
        compiler_params=_cparams(("arbitrary",), 48),
        name=name,
    )(x, g.reshape(1, d), w)


def _stacked_specs(m_p, m_s, width):
    tm = ROW_TILE
    assert m_p % tm == 0 and m_s == tm
    prompt_tiles = m_p // tm
    return prompt_tiles, [
        pl.BlockSpec((tm, width), lambda i: (jnp.minimum(i, prompt_tiles - 1), 0)),
        pl.BlockSpec((tm, width), lambda i: (0, 0)),
    ]


def _matmul_res_kernel(a_ref, w_ref, r_ref, o_ref):
    o_ref[...] = r_ref[...] + jnp.dot(a_ref[...], w_ref[...], preferred_element_type=f32)


def matmul_res(a, w, res, name):
    m, d = a.shape
    f = w.shape[1]
    tm = min(ROW_TILE, m)
    assert m % tm == 0
    return pl.pallas_call(
        _matmul_res_kernel,
        out_shape=jax.ShapeDtypeStruct((m, f), f32),
        grid=(m // tm,),
        in_specs=[
            pl.BlockSpec((tm, d), lambda i: (i, 0)),
            pl.BlockSpec((d, f), lambda i: (0, 0)),
            pl.BlockSpec((tm, f), lambda i: (i, 0)),
        ],
        out_specs=pl.BlockSpec((tm, f), lambda i: (i, 0)),
        compiler_params=_cparams(("arbitrary",), 32),
        name=name,
    )(a, w, res)


def _rope(x, cos, sin_signed, half):
    return x * cos + pltpu.roll(x, half, 1) * sin_signed


def _group_norm_gate(o, gate, gn):
    mu = jnp.mean(o, axis=-1, keepdims=True)
    var = jnp.mean(jnp.square(o - mu), axis=-1, keepdims=True)
    on = ((o - mu) * lax.rsqrt(var + EPS)) * gn
    return (gate * jax.nn.sigmoid(gate)) * on


def _prompt_mixer_kernel(cd_ref, x_ref, g_ref, win_ref, cos_ref, sin_ref, decay_ref, xi_ref, zeta_ref, convw_ref,
                         gn_ref, wout_ref, wcast_ref, wsplit_ref, perm_ref, h_ref, sfin_ref, cfin_ref, wcast_out_ref,
                         wsplit_out_ref, s_scr, u_scr, proj_ref, mix_ref, *, tt, r, cd_width, dh):
    j = pl.program_id(1)
    heads = r // dh
    pad = SUBLANES

    @pl.when(j == 0)
    def _():
        s_scr[...] = jnp.zeros_like(s_scr)
        u_scr[0:pad, :] = jnp.zeros((pad, cd_width), f32)

    proj_ref[...] = jnp.dot(_rms(x_ref[...], g_ref[...]).astype(bf16), win_ref[...], preferred_element_type=f32)

    k_scale = dh ** -0.5
    for c in range(tt // RET_CHUNK):
        rows = pl.ds(c * RET_CHUNK, RET_CHUNK)
        cosb = cos_ref[rows, :]
        sinb = sin_ref[rows, :]
        for h in range(heads):
            lo, hi = h * dh, (h + 1) * dh
            qh = _rope(proj_ref[rows, lo:hi], cosb, sinb, dh // 2)
            kh = _rope(proj_ref[rows, r + lo:r + hi], cosb, sinb, dh // 2) * k_scale
            vb = proj_ref[rows, 2 * r + lo:2 * r + hi].astype(bf16)
            gate = proj_ref[rows, 3 * r + lo:3 * r + hi]
            qb = qh.astype(bf16)
            kb = kh.astype(bf16)
            scores = lax.dot_general(qb, kb, (((1,), (1,)), ((), ())), preferred_element_type=f32) * decay_ref[h]
            inner = jnp.dot(scores.astype(bf16), vb, preferred_element_type=f32)
            s_prev = s_scr[h]
            cross = jnp.dot((qh * xi_ref[h]).astype(bf16), s_prev.astype(bf16), preferred_element_type=f32)
            kv = lax.dot_general((kh * zeta_ref[h]).astype(bf16), vb, (((0,), (0,)), ((), ())),
                                 preferred_element_type=f32)
            s_scr[h] = cd_ref[h] * s_prev + kv
            mix_ref[rows, lo:hi] = _group_norm_gate(inner + cross, gate, gn_ref[:, lo:hi]).astype(bf16)

    wcast_out_ref[...] = wcast_ref[...].astype(bf16)
    _split_pairs(wsplit_ref, perm_ref, wsplit_out_ref)

    base = 4 * r
    u = proj_ref[:, base + cd_width:base + 2 * cd_width] * proj_ref[:, base:base + cd_width]
    u_scr[pad:pad + tt, :] = u
    conv = (u_scr[pad - 2:pad - 2 + tt, :] * convw_ref[0:1, :] + u_scr[pad - 1:pad - 1 + tt, :] * convw_ref[1:2, :]
            + u * convw_ref[2:3, :])
    mix_ref[:, r:r + cd_width] = (proj_ref[:, base + 2 * cd_width:base + 3 * cd_width] * conv).astype(bf16)
    u_scr[0:pad, :] = u_scr[tt:tt + pad, :]
    cfin_ref[0] = u_scr[pad - 2:pad, :]
    h_ref[...] = x_ref[...] + jnp.dot(mix_ref[...], wout_ref[...], preferred_element_type=f32)

    @pl.when(j == pl.num_programs(1) - 1)
    def _():
        sfin_ref[0] = s_scr[...]


def prompt_mixer(x, g, w_in, tables, conv_w, ret_gn, w_out, batch, seq, r, cd_width, w_cast, w_split):
    tt = ROW_TILE
    d = x.shape[1]
    dh = r // N_RET_HEADS
    steps = seq // tt
    in_cols = w_in.shape[1]
    assert w_cast.shape[0] == batch * steps and w_split.shape[0] == batch * steps
    kern = functools.partial(_prompt_mixer_kernel, tt=tt, r=r, cd_width=cd_width, dh=dh)
    tab = lambda shape: pl.BlockSpec(shape, lambda b, j, cd: (0,) * len(shape))
    row_tile = pl.BlockSpec((tt, d), lambda b, j, cd: (b * steps + j, 0))
    cast_slice = pl.BlockSpec((1,) + w_cast.shape[1:], lambda b, j, cd: (b * steps + j, 0, 0))
    split_shape = (w_split.shape[0], w_split.shape[1], w_split.shape[2] // 2)
    split_slice = pl.BlockSpec((1,) + split_shape[1:], lambda b, j, cd: (b * steps + j, 0, 0))
    perm = pair_split_matrix()
    return pl.pallas_call(
        kern,
        out_shape=(
            jax.ShapeDtypeStruct((batch * seq, d), f32),
            jax.ShapeDtypeStruct((batch, N_RET_HEADS, dh, dh), f32),
            jax.ShapeDtypeStruct((batch, 2, cd_width), f32),
            jax.ShapeDtypeStruct(w_cast.shape, bf16),
            jax.ShapeDtypeStruct(split_shape, bf16),
        ),
        grid_spec=pltpu.PrefetchScalarGridSpec(
            num_scalar_prefetch=1,
            grid=(batch, steps),
            in_specs=[
                row_tile,
                tab((1, d)),
                tab((d, in_cols)),
                pl.BlockSpec((tt, dh), lambda b, j, cd: (j, 0)),
                pl.BlockSpec((tt, dh), lambda b, j, cd: (j, 0)),
                tab((N_RET_HEADS, RET_CHUNK, RET_CHUNK)),
                tab((N_RET_HEADS, RET_CHUNK, dh)),
                tab((N_RET_HEADS, RET_CHUNK, dh)),
                tab((3, cd_width)),
                tab((1, r)),
                tab((r + cd_width, d)),
                cast_slice,
                split_slice,
                tab(perm.shape),
            ],
            out_specs=(
                row_tile,
                pl.BlockSpec((1, N_RET_HEADS, dh, dh), lambda b, j, cd: (b, 0, 0, 0)),
                pl.BlockSpec((1, 2, cd_width), lambda b, j, cd: (b, 0, 0)),
                cast_slice,
                split_slice,
            ),
            scratch_shapes=[
                pltpu.VMEM((N_RET_HEADS, dh, dh), f32),
                pltpu.VMEM((tt + 2 * SUBLANES, cd_width), f32),
                pltpu.VMEM((tt, in_cols), f32),
                pltpu.VMEM((tt, r + cd_width), bf16),
            ],
        ),
        compiler_params=_cparams(("arbitrary", "arbitrary"), 56),
        name="prompt_mixer",
    )(tables["chunk_decay"], x, g.reshape(1, d), w_in, tables["cos"], tables["sin"], tables["decay"], tables["xi"],
      tables["zeta"], conv_w, ret_gn.reshape(1, r), w_out, w_cast, w_split, perm)


def _sample_mixer_kernel(dec_ref, proj_ref, cos_ref, sin_ref, xi_ref, zeta_ref, convw_ref, gn_ref, s0_ref, c0_ref,
                         mix_ref, snew_ref, cnew_ref, *, ts, r, cd_width, dh):
    heads = r // dh
    nb = SAMPLE_BLOCK
    rows = ts * nb
    k_scale = dh ** -0.5
    cosb = cos_ref[...]
    sinb = sin_ref[...]
    seq_of_row = lax.broadcasted_iota(i32, (rows, dh), 0) % nb
    for h in range(heads):
        lo, hi = h * dh, (h + 1) * dh
        qh = _rope(proj_ref[:, lo:hi], cosb, sinb, dh // 2)
        kh = _rope(proj_ref[:, r + lo:r + hi], cosb, sinb, dh // 2) * k_scale
        vh = proj_ref[:, 2 * r + lo:2 * r + hi]
        gate = proj_ref[:, 3 * r + lo:3 * r + hi]
        inner = []
        for t in range(ts):
            qt = qh[t * nb:(t + 1) * nb]
            acc = jnp.zeros((nb, dh), f32)
            for s in range(t + 1):
                sc = jnp.sum(qt * kh[s * nb:(s + 1) * nb], axis=-1, keepdims=True) * dec_ref[h * (ts + 1) + t - s]
                acc = acc + sc * vh[s * nb:(s + 1) * nb]
            inner.append(acc)
        inner = jnp.concatenate(inner, axis=0)
        qx = (qh * xi_ref[h]).astype(bf16)
        kz = kh * zeta_ref[h]
        vb = vh.astype(bf16)
        cross = jnp.zeros((rows, dh), f32)
        for b in range(nb):
            mine = seq_of_row == b
            s_prev = s0_ref[b, h]
            res = jnp.dot(qx, s_prev.astype(bf16), preferred_element_type=f32)
            cross = cross + jnp.where(mine, res, 0.0)
            kv = lax.dot_general(jnp.where(mine, kz, 0.0).astype(bf16), vb, (((0,), (0,)), ((), ())),
                                 preferred_element_type=f32)
            snew_ref[b, h] = dec_ref[h * (ts + 1) + ts] * s_prev + kv
        mix_ref[:, lo:hi] = _group_norm_gate(inner + cross, gate, gn_ref[:, lo:hi]).astype(bf16)

    base = 4 * r
    u = proj_ref[:, base + cd_width:base + 2 * cd_width] * proj_ref[:, base:base + cd_width]
    full = [c0_ref[0], c0_ref[1]] + [u[t * nb:(t + 1) * nb] for t in range(ts)]
    conv = jnp.concatenate(
        [full[t] * convw_ref[0:1, :] + full[t + 1] * convw_ref[1:2, :] + full[t + 2] * convw_ref[2:3, :]
         for t in range(ts)], axis=0)
    mix_ref[:, r:r + cd_width] = (proj_ref[:, base + 2 * cd_width:base + 3 * cd_width] * conv).astype(bf16)
    cnew_ref[0] = full[ts]
    cnew_ref[1] = full[ts + 1]


def sample_mixer(proj, tables, conv_w, ret_gn, state_ret, state_conv_t, row0, ts, r, cd_width):
    n_seq = state_ret.shape[0]
    dh = r // N_RET_HEADS
    rows = ts * SAMPLE_BLOCK
    blk0 = row0 // rows
    assert row0 % rows == 0 and n_seq % SAMPLE_BLOCK == 0
    in_cols = proj.shape[1]
    kern = functools.partial(_sample_mixer_kernel, ts=ts, r=r, cd_width=cd_width, dh=dh)
    tab = lambda shape: pl.BlockSpec(shape, lambda i, d: (0,) * len(shape))
    return pl.pallas_call(
        kern,
        out_shape=(
            jax.ShapeDtypeStruct((n_seq * ts, r + cd_width), bf16),
            jax.ShapeDtypeStruct(state_ret.shape, f32),
            jax.ShapeDtypeStruct(state_conv_t.shape, f32),
        ),
        grid_spec=pltpu.PrefetchScalarGridSpec(
            num_scalar_prefetch=1,
            grid=(n_seq // SAMPLE_BLOCK,),
            in_specs=[
                pl.BlockSpec((rows, in_cols), lambda i, d: (blk0 + i, 0)),
                tab((rows, dh)),
                tab((rows, dh)),
                tab((N_RET_HEADS, rows, dh)),
                tab((N_RET_HEADS, rows, dh)),
                tab((3, cd_width)),
                tab((1, r)),
                pl.BlockSpec((SAMPLE_BLOCK, N_RET_HEADS, dh, dh), lambda i, d: (i, 0, 0, 0)),
                pl.BlockSpec((2, SAMPLE_BLOCK, cd_width), lambda i, d: (0, i, 0)),
            ],
            out_specs=(
                pl.BlockSpec((rows, r + cd_width), lambda i, d: (i, 0)),
                pl.BlockSpec((SAMPLE_BLOCK, N_RET_HEADS, dh, dh), lambda i, d: (i, 0, 0, 0)),
                pl.BlockSpec((2, SAMPLE_BLOCK, cd_width), lambda i, d: (0, i, 0)),
            ),
        ),
        compiler_params=_cparams(("arbitrary",), 32),
        name="sample_mixer",
    )(tables["dec"], proj, tables["cos"], tables["sin"], tables["xi"], tables["zeta"], conv_w, ret_gn.reshape(1, r),
      state_ret, state_conv_t)


def _log_gamma():
    return jnp.log1p(-jnp.exp2(-5.0 - jnp.arange(N_RET_HEADS, dtype=f32)))


def _rope_tables(pos, dh):
    half = dh // 2
    inv = ROPE_THETA ** (-jnp.arange(half, dtype=f32) / half)
    ang = pos[:, None] * inv[None, :]
    cos, sin = jnp.cos(ang), jnp.sin(ang)
    return jnp.concatenate([cos, cos], axis=-1), jnp.concatenate([-sin, sin], axis=-1)


def prompt_tables(seq, dh):
    c = RET_CHUNK
    lg = _log_gamma()
    i = jnp.arange(c, dtype=f32)
    diff = i[:, None] - i[None, :]
    decay = jnp.where(diff >= 0, jnp.exp(jnp.maximum(diff, 0.0)[None] * lg[:, None, None]), 0.0)
    xi = jnp.exp((i + 1.0)[None, :] * lg[:, None])
    zeta = jnp.exp((c - 1.0 - i)[None, :] * lg[:, None])
    cos, sin = _rope_tables(jnp.arange(seq, dtype=f32), dh)
    bc = lambda t: jnp.broadcast_to(t[:, :, None], (N_RET_HEADS, c, dh))
    return dict(cos=cos, sin=sin, decay=decay, xi=bc(xi), zeta=bc(zeta), chunk_decay=jnp.exp(c * lg))


def sample_tables(ts, pos0, dh):
    lg = _log_gamma()
    i = jnp.arange(ts, dtype=f32)
    dec = jnp.exp(jnp.arange(ts + 1, dtype=f32)[None, :] * lg[:, None])
    xi = jnp.exp((i + 1.0)[None, :] * lg[:, None])
    zeta = jnp.exp((ts - 1.0 - i)[None, :] * lg[:, None])
    cos, sin = _rope_tables(pos0 + i, dh)
    rep = lambda t: jnp.repeat(t, SAMPLE_BLOCK, axis=0)
    bc = lambda t: jnp.broadcast_to(jnp.repeat(t, SAMPLE_BLOCK, axis=1)[:, :, None],
                                    (N_RET_HEADS, ts * SAMPLE_BLOCK, dh))
    return dict(cos=rep(cos), sin=rep(sin), xi=bc(xi), zeta=bc(zeta), dec=dec.reshape(-1))


def _softmax_rows(s):
    m = jnp.max(s, axis=-1, keepdims=True)
    p = jnp.exp(s - m)
    return p / jnp.sum(p, axis=-1, keepdims=True)


def _from_slabs(ref, rows=None):
    rows = slice(None) if rows is None else rows
    return jnp.concatenate([ref[lt, rows, :] for lt in range(ref.shape[0])], axis=1)


def _to_slabs(ref, value):
    for lt in range(ref.shape[0]):
        ref[lt] = value[:, lt * LANES:(lt + 1) * LANES]


def _mem_kv_kernel(m_ref, g_ref, w_ref, k_ref, v_ref, rows_ref, *, heads):
    n_mem, d = m_ref.shape
    dh = d // heads
    kv = jnp.dot(_rms(m_ref[...], g_ref[...]).astype(bf16), w_ref[...], preferred_element_type=f32)
    for part, o_ref in enumerate((k_ref, v_ref)):
        for h in range(heads):
            for lt in range(dh // LANES):
                lo = part * d + h * dh + lt * LANES
                rows_ref[lt, pl.ds(h, n_mem, stride=heads), :] = kv[:, lo:lo + LANES]
        o_ref[0] = _from_slabs(rows_ref).reshape(n_mem, heads, dh)


def mem_kv(mem, g, w_kv, heads):
    batch, n_mem, d = mem.shape
    dh = d // heads
    out = jax.ShapeDtypeStruct((batch, n_mem, heads, dh), f32)
    out_spec = pl.BlockSpec((1, n_mem, heads, dh), lambda b: (b, 0, 0, 0))
    return pl.pallas_call(
        functools.partial(_mem_kv_kernel, heads=heads),
        out_shape=(out, out),
        grid=(batch,),
        in_specs=[
            pl.BlockSpec((n_mem, d), lambda b: (b, 0)),
            pl.BlockSpec((1, d), lambda b: (0, 0)),
            pl.BlockSpec((d, 2 * d), lambda b: (0, 0)),
        ],
        out_specs=(out_spec, out_spec),
        scratch_shapes=[pltpu.VMEM((dh // LANES, n_mem * heads, LANES), f32)],
        compiler_params=_cparams(("arbitrary",), 32),
        name="mem_kv",
    )(mem.reshape(batch * n_mem, d), g.reshape(1, d), w_kv)


def _attn_block_kernel(hp_ref, hs_ref, g_ref, wq_ref, k_ref, v_ref, as_ref, wo_ref, wup_ref, perm_ref, o_ref,
                       wup_out_ref, k_rows, v_rows, *, prompt_tiles, heads):
    i = pl.program_id(0)

    @pl.when(i < prompt_tiles)
    def _():
        _split_pairs(wup_ref, perm_ref, wup_out_ref)
        h_ref = hp_ref
        n_mem, _, dh = k_ref.shape[1:]
        scale = dh ** -0.5
        q = jnp.dot(_rms(h_ref[...], g_ref[...]).astype(bf16), wq_ref[...], preferred_element_type=f32)
        _to_slabs(k_rows, k_ref[0].reshape(n_mem * heads, dh))
        _to_slabs(v_rows, v_ref[0].reshape(n_mem * heads, dh))
        outs = []
        for h in range(heads):
            kh = _from_slabs(k_rows, pl.ds(h, n_mem, stride=heads)).astype(bf16)
            vh = _from_slabs(v_rows, pl.ds(h, n_mem, stride=heads)).astype(bf16)
            s = lax.dot_general(q[:, h * dh:(h + 1) * dh].astype(bf16), kh, (((1,), (1,)), ((), ())),
                                preferred_element_type=f32) * scale
            outs.append(jnp.dot(_softmax_rows(s).astype(bf16), vh, preferred_element_type=f32).astype(bf16))
        attn = jnp.concatenate(outs, axis=1)
        o_ref[...] = h_ref[...] + jnp.dot(attn, wo_ref[...], preferred_element_type=f32)

    @pl.when(i == prompt_tiles)
    def _():
        o_ref[...] = hs_ref[...] + jnp.dot(as_ref[...], wo_ref[...], preferred_element_type=f32)


def attn_block(h_prompt, h_sample, g, w_mq, mk, mv, attn_sample, w_mo, seq, w_up):
    d = h_prompt.shape[1]
    batch, n_mem, heads, dh = mk.shape
    tm = ROW_TILE
    per_batch = seq // tm
    prompt_tiles, h_specs = _stacked_specs(batch * seq, h_sample.shape[0], d)
    n_exp, _, f2 = w_up.shape
    assert attn_sample.shape == (tm, d) and n_exp == prompt_tiles
    prompt_tile = lambda i: jnp.minimum(i, prompt_tiles - 1)
    kv_spec = pl.BlockSpec((1, n_mem, heads, dh), lambda i: (prompt_tile(i) // per_batch, 0, 0, 0))
    const = lambda shape: pl.BlockSpec(shape, lambda i: (0,) * len(shape))
    perm = pair_split_matrix()
    return pl.pallas_call(
        functools.partial(_attn_block_kernel, prompt_tiles=prompt_tiles, heads=heads),
        out_shape=(
            jax.ShapeDtypeStruct(((prompt_tiles + 1) * tm, d), f32),
            jax.ShapeDtypeStruct((n_exp, d, f2 // 2), bf16),
        ),
        grid=(prompt_tiles + 1,),
        in_specs=h_specs + [
            const((1, d)),
            const((d, d)),
            kv_spec,
            kv_spec,
            const((tm, d)),
            const((d, d)),
            pl.BlockSpec((1, d, f2 // 2), lambda i: (prompt_tile(i), 0, 1)),
            const(perm.shape),
        ],
        out_specs=(pl.BlockSpec((tm, d), lambda i: (i, 0)),
                   pl.BlockSpec((1, d, f2 // 2), lambda i: (prompt_tile(i), 0, 0))),
        scratch_shapes=[pltpu.VMEM((dh // LANES, n_mem * heads, LANES), f32)] * 2,
        compiler_params=_cparams(("arbitrary",), 56),
        name="attn_block",
    )(h_prompt, h_sample, g.reshape(1, d), w_mq, mk, mv, attn_sample, w_mo, w_up, perm)


def _cross_sample_kernel(q_ref, k_ref, v_ref, o_ref, acc_ref, *, dh, seqs):
    part = pl.program_id(1)
    rows = q_ref.shape[0]
    scale = dh ** -0.5

    @pl.when(part == 0)
    def _():
        acc_ref[...] = jnp.zeros_like(acc_ref)

    heads = N_MEM_HEADS
    n_mem = k_ref.shape[1]
    qx = jnp.concatenate([q_ref[:, h * dh:(h + 1) * dh] for h in range(heads)], axis=0).astype(bf16)
    row = lax.broadcasted_iota(i32, (heads * rows, n_mem * heads), 0)
    col = lax.broadcasted_iota(i32, (heads * rows, n_mem * heads), 1)
    same_head = col % heads == row // rows
    seq_of_row = lax.broadcasted_iota(i32, (heads * rows, dh), 0) % SAMPLE_BLOCK
    out = jnp.zeros((heads * rows, dh), f32)
    for b in range(seqs):
        k2 = k_ref[b].reshape(n_mem * heads, dh).astype(bf16)
        v2 = v_ref[b].reshape(n_mem * heads, dh).astype(bf16)
        s = lax.dot_general(qx, k2, (((1,), (1,)), ((), ())), preferred_element_type=f32) * scale
        p = _softmax_rows(jnp.where(same_head, s, -jnp.inf))
        o = jnp.dot(p.astype(bf16), v2, preferred_element_type=f32)
        out = out + jnp.where(seq_of_row == part * seqs + b, o, 0.0)
    acc_ref[...] += out

    @pl.when(part == pl.num_programs(1) - 1)
    def _():
        for h in range(heads):
            o_ref[:, h * dh:(h + 1) * dh] = acc_ref[h * rows:(h + 1) * rows, :].astype(bf16)


def cross_sample(q, cache_k, cache_v, row0, ts):
    n_seq, n_mem, heads, dh = cache_k.shape
    d = heads * dh
    rows = ts * SAMPLE_BLOCK
    blk0 = row0 // rows
    parts = 1
    seqs = SAMPLE_BLOCK // parts
    kern = functools.partial(_cross_sample_kernel, dh=dh, seqs=seqs)
    return pl.pallas_call(
        kern,
        out_shape=jax.ShapeDtypeStruct((n_seq * ts, d), bf16),
        grid=(n_seq // SAMPLE_BLOCK, parts),
        in_specs=[
            pl.BlockSpec((rows, d), lambda i, p: (blk0 + i, 0)),
            pl.BlockSpec((seqs, n_mem, heads, dh), lambda i, p: (i * parts + p, 0, 0, 0)),
            pl.BlockSpec((seqs, n_mem, heads, dh), lambda i, p: (i * parts + p, 0, 0, 0)),
        ],
        out_specs=pl.BlockSpec((rows, d), lambda i, p: (i, 0)),
        scratch_shapes=[pltpu.VMEM((heads * rows, dh), f32)],
        compiler_params=_cparams(("arbitrary", "arbitrary"), 40),
        name="cross_sample",
    )(q, cache_k, cache_v)


def _router_kernel(h_ref, g_ref, wr_ref, br_ref, xp_ref, e_ref, gate_ref, rank_ref, cnt_ref, carry_ref,
                   *, tiles_per_chunk, n_exp):
    i = pl.program_id(0)

    @pl.when(i % tiles_per_chunk == 0)
    def _():
        carry_ref[...] = jnp.zeros_like(carry_ref)

    xn = _rms(h_ref[...], g_ref[...])
    tm, d = xn.shape
    tiles = d // LANES
    for rg in range(tm // SUBLANES):
        for jt in range(tiles):
            xp_ref[pl.ds(rg * SUBLANES * tiles + jt, SUBLANES, stride=tiles), :] = (
                xn[rg * SUBLANES:(rg + 1) * SUBLANES, jt * LANES:(jt + 1) * LANES])

    w = wr_ref[...]
    w_hi = w.astype(bf16)
    w_lo = (w - w_hi.astype(f32)).astype(bf16)
    x_hi = xn.astype(bf16)
    x_lo = (xn - x_hi.astype(f32)).astype(bf16)
    nt = (((1,), (1,)), ((), ()))
    both = lax.dot_general(jnp.concatenate([w_hi, w_lo], axis=0), x_hi, nt, preferred_element_type=f32)
    logits = (both[:n_exp] + both[n_exp:] + lax.dot_general(w_hi, x_lo, nt, preferred_element_type=f32)
              + br_ref[...])
    sub = lax.broadcasted_iota(i32, (n_exp, tm), 0).astype(f32)
    chosen, vals, hots = [], [], []
    work = logits
    for _ in range(TOP_K):
        m = jnp.max(work, axis=0, keepdims=True)
        idx = jnp.min(jnp.where(work == m, sub, float(n_exp)), axis=0, keepdims=True)
        hot = sub == idx
        chosen.append(idx)
        vals.append(m)
        hots.append(hot)
        work = jnp.where(hot, -jnp.inf, work)
    ex = [jnp.exp(v - vals[0]) for v in vals]
    denom = ex[0] + ex[1] + ex[2] + ex[3]
    e_ref[...] = jnp.concatenate(chosen, axis=0).astype(i32)
    gate_ref[...] = jnp.concatenate([x / denom for x in ex], axis=0)

    member = jnp.zeros((n_exp, tm), f32)
    for hot in hots:
        member = member + hot.astype(f32)
    earlier = (lax.broadcasted_iota(i32, (tm, tm), 0) < lax.broadcasted_iota(i32, (tm, tm), 1)).astype(bf16)
    before = jnp.dot(member.astype(bf16), earlier, preferred_element_type=f32) + carry_ref[...]
    rank_ref[...] = jnp.concatenate(
        [jnp.sum(jnp.where(hot, before, 0.0), axis=0, keepdims=True) for hot in hots], axis=0).astype(i32)
    carry_ref[...] += jnp.sum(member, axis=1, keepdims=True)
    cnt_ref[0] = carry_ref[...].astype(i32)


def router(h, g, w_router, b_router, tc):
    n, d = h.shape
    n_exp = w_router.shape[1]
    tiles = d // LANES
    tm = ROW_TILE
    while tc % tm:
        tm -= LANES
    tiles_per_chunk = tc // tm
    kern = functools.partial(_router_kernel, tiles_per_chunk=tiles_per_chunk, n_exp=n_exp)
    return pl.pallas_call(
        kern,
        out_shape=(
            jax.ShapeDtypeStruct((n * tiles, LANES), f32),
            jax.ShapeDtypeStruct((TOP_K, n), i32),
            jax.ShapeDtypeStruct((TOP_K, n), f32),
            jax.ShapeDtypeStruct((TOP_K, n), i32),
            jax.ShapeDtypeStruct((n // tc, n_exp, 1), i32),
        ),
        grid=(n // tm,),
        in_specs=[
            pl.BlockSpec((tm, d), lambda i: (i, 0)),
            pl.BlockSpec((1, d), lambda i: (0, 0)),
            pl.BlockSpec((n_exp, d), lambda i: (0, 0)),
            pl.BlockSpec((n_exp, 1), lambda i: (0, 0)),
        ],
        out_specs=(
            pl.BlockSpec((tm * tiles, LANES), lambda i: (i, 0)),
            pl.BlockSpec((TOP_K, tm), lambda i: (0, i)),
            pl.BlockSpec((TOP_K, tm), lambda i: (0, i)),
            pl.BlockSpec((TOP_K, tm), lambda i: (0, i)),
            pl.BlockSpec((1, n_exp, 1), lambda i: (i // tiles_per_chunk, 0, 0)),
        ),
        scratch_shapes=[pltpu.VMEM((n_exp, 1), f32)],
        compiler_params=_cparams(("arbitrary",), 32),
        name="router",
    )(h, g.reshape(1, d), w_router.T, b_router.reshape(n_exp, 1))


def _split_pairs(w_ref, p_ref, o_ref):
    width = p_ref.shape[0]
    for b in range(w_ref.shape[2] // width):
        cols = slice(b * width, (b + 1) * width)
        o_ref[0, :, cols] = jnp.dot(w_ref[0, :, cols].astype(bf16), p_ref[...],
                                    preferred_element_type=f32).astype(bf16)


def pair_split_matrix():
    width = 2 * LANES
    j = jnp.arange(width)
    src = jnp.where(j < LANES, 2 * j, 2 * (j - LANES) + 1)
    return (jnp.arange(width)[:, None] == src[None, :]).astype(bf16)


def _dense_row_index(r, tiles):
    return (r // SUBLANES) * tiles * SUBLANES + r % SUBLANES


def _experts_kernel(start_ref, nsub_ref, tail_ref, xp_hbm, dest_hbm, gates_hbm, fill_hbm, wup_lo_ref, wup_hi_ref,
                    bup_ref, wdn_ref, bdn_ref,
                    f_hbm, xs_ref, acc_ref, xt_a, xt_b, y_a, y_b, dest_s, gate_s, rmap_s, sem_ref,
                    *, tc, n_exp, d, null_row0, seg, half_len, place_per_step):
    c = pl.program_id(0)
    e = pl.program_id(1)
    ms = MOE_SUB
    tiles = d // LANES
    cur = pl.multiple_of((c % 2) * half_len, LANES)
    nxt = pl.multiple_of(half_len - cur, LANES)

    def gather(row0, xt_ref):
        for r in range(ms):
            t = rmap_s[cur + row0 + r] & (seg - 1)
            xt_ref[pl.ds(_dense_row_index(r, tiles), tiles, stride=SUBLANES), :] = xs_ref[t]

    def place(base, first, count):
        for k in range(TOP_K):
            for u in range(count):
                a = k * seg + first + u
                rmap_s[base + dest_s[a]] = a

    def scatter(row0, y_ref):
        for r0 in range(0, ms, SCATTER_UNROLL):
            toks, sums = [], []
            for r in range(r0, r0 + SCATTER_UNROLL):
                a = rmap_s[cur + row0 + r]
                t = a & (seg - 1)
                yrow = y_ref[pl.ds(_dense_row_index(r, tiles), tiles, stride=SUBLANES), :]
                toks.append(t)
                sums.append(acc_ref[t] + gate_s[a] * yrow)
            for t, s in zip(toks, sums):
                acc_ref[t] = s

    @pl.when(e == 0)
    def _():
        copies = [
            pltpu.make_async_copy(xp_hbm.at[pl.ds(c * tc, tc)], xs_ref.at[pl.ds(0, tc)], sem_ref.at[0]),
            pltpu.make_async_copy(gates_hbm.at[c], gate_s, sem_ref.at[1]),
        ]
        for cp in copies:
            cp.start()
        xs_ref[pl.ds(tc, SUBLANES)] = jnp.zeros((SUBLANES, tiles, LANES), f32)
        acc_ref[...] = jnp.zeros_like(acc_ref)

        def load_map_inputs(chunk, base):
            loads = [
                pltpu.make_async_copy(dest_hbm.at[chunk], dest_s, sem_ref.at[2]),
                pltpu.make_async_copy(fill_hbm, rmap_s.at[pl.ds(base, half_len)], sem_ref.at[3]),
            ]
            for cp in loads:
                cp.start()
            for cp in loads:
                cp.wait()

        @pl.when(c == 0)
        def _():
            y_a[...] = jnp.zeros_like(y_a)
            y_b[...] = jnp.zeros_like(y_b)
            load_map_inputs(0, 0)

            def place_all(i, carry):
                place(0, i * SUBLANES, SUBLANES)
                return carry

            lax.fori_loop(0, tc // SUBLANES, place_all, 0)

        load_map_inputs(jnp.minimum(c + 1, pl.num_programs(0) - 1), nxt)
        for cp in copies:
            cp.wait()
        gather(0, xt_a)

    g = c * n_exp + e
    j0 = start_ref[g]

    def mlp(xt_cur, y_cur, first_group, groups):
        x = jnp.concatenate(
            [jnp.concatenate([xt_cur[pl.ds((rg * tiles + jt) * SUBLANES, SUBLANES), :] for jt in range(tiles)], axis=1)
             for rg in range(first_group, first_group + groups)], axis=0).astype(bf16)
        hmid = jnp.concatenate([jnp.dot(x, w_ref[0], preferred_element_type=f32)
                                for w_ref in (wup_lo_ref, wup_hi_ref)], axis=1) + bup_ref[0]
        glu = jnp.concatenate([hmid[:, 2 * jt * LANES:(2 * jt + 1) * LANES] for jt in range(tiles)], axis=1)
        lin = jnp.concatenate([hmid[:, (2 * jt + 1) * LANES:(2 * jt + 2) * LANES] for jt in range(tiles)], axis=1)
        glu = jnp.minimum(glu, SWIGLU_LIMIT)
        lin = jnp.clip(lin, -SWIGLU_LIMIT, SWIGLU_LIMIT)
        act = glu * jax.nn.sigmoid(SWIGLU_ALPHA * glu) * (lin + 1.0)
        y = jnp.dot(act.astype(bf16), wdn_ref[0], preferred_element_type=f32) + bdn_ref[0]
        for rg in range(groups):
            for jt in range(tiles):
                y_cur[pl.ds(((first_group + rg) * tiles + jt) * SUBLANES, SUBLANES), :] = (
                    y[rg * SUBLANES:(rg + 1) * SUBLANES, jt * LANES:(jt + 1) * LANES])

    def step(j, groups, xt_cur, xt_nxt, y_cur, y_prv):
        scatter(jnp.where(j == 0, null_row0, (j - 1) * ms), y_prv)
        place(nxt, j * place_per_step, place_per_step)
        gather((j + 1) * ms, xt_nxt)
        mlp(xt_cur, y_cur, 0, groups)

    def sub_block(i, carry):
        j = j0 + i
        full = jnp.logical_or(i < nsub_ref[g] - 1, tail_ref[g] > ms // 2)
        buffers = ((xt_a, xt_b, y_a, y_b), (xt_b, xt_a, y_b, y_a))
        for parity in range(2):
            for groups, wanted in ((ms // SUBLANES, full), (ms // SUBLANES // 2, jnp.logical_not(full))):
                @pl.when(jnp.logical_and(j % 2 == parity, wanted))
                def _():
                    step(j, groups, *buffers[parity])

        return carry

    lax.fori_loop(0, nsub_ref[g], sub_block, 0)

    @pl.when(e == n_exp - 1)
    def _():
        last = j0 + nsub_ref[g] - 1

        @pl.when(last % 2 == 0)
        def _():
            scatter(last * ms, y_a)

        @pl.when(last % 2 == 1)
        def _():
            scatter(last * ms, y_b)

        cp = pltpu.make_async_copy(acc_ref.at[pl.ds(0, tc)], f_hbm.at[pl.ds(c * tc, tc)], sem_ref.at[0])
        cp.start()
        cp.wait()


def _row_map_geometry(tc, n_exp, seg):
    assert seg > tc and seg & (seg - 1) == 0
    max_sub = (TOP_K * tc + n_exp * (MOE_SUB - 1)) // MOE_SUB
    min_sub = -(-TOP_K * tc // MOE_SUB)
    null_row0 = (max_sub + 1) * MOE_SUB
    half_len = -(-(null_row0 + MOE_SUB + 1) // LANES) * LANES
    place_per_step = -(-tc // min_sub)
    assert max_sub * place_per_step <= seg
    return null_row0, half_len, place_per_step


def experts(sb_start, n_sub, tail_rows, xp, dest, gates, w_up_halves, b_up, w_down, b_down, tc):
    n, tiles, _ = xp.shape
    d = tiles * LANES
    n_exp = w_down.shape[0]
    chunks, padded_len = dest.shape
    seg = padded_len // TOP_K
    null_row0, half_len, place_per_step = _row_map_geometry(tc, n_exp, seg)
    fill = jnp.full((half_len,), tc, i32)
    kern = functools.partial(_experts_kernel, tc=tc, n_exp=n_exp, d=d, null_row0=null_row0, seg=seg,
                             half_len=half_len, place_per_step=place_per_step)
    block = pltpu.VMEM((MOE_SUB * tiles, LANES), f32)
    return pl.pallas_call(
        kern,
        out_shape=jax.ShapeDtypeStruct((n, tiles, LANES), f32),
        grid_spec=pltpu.PrefetchScalarGridSpec(
            num_scalar_prefetch=3,
            grid=(chunks, n_exp),
            in_specs=[
                pl.BlockSpec(memory_space=pl.ANY),
                pl.BlockSpec(memory_space=pl.ANY),
                pl.BlockSpec(memory_space=pl.ANY),
                pl.BlockSpec(memory_space=pl.ANY),
                pl.BlockSpec((1, d, d), lambda c, e, *_: (e, 0, 0)),
                pl.BlockSpec((1, d, d), lambda c, e, *_: (e, 0, 0)),
                pl.BlockSpec((1, 1, 2 * d), lambda c, e, *_: (e, 0, 0)),
                pl.BlockSpec((1, d, d), lambda c, e, *_: (e, 0, 0)),
                pl.BlockSpec((1, 1, d), lambda c, e, *_: (e, 0, 0)),
            ],
            out_specs=pl.BlockSpec(memory_space=pl.ANY),
            scratch_shapes=[
                pltpu.VMEM((tc + SUBLANES, tiles, LANES), f32),
                pltpu.VMEM((tc + SUBLANES, tiles, LANES), f32),
                block, block, block, block,
                pltpu.SMEM((padded_len,), i32),
                pltpu.SMEM((padded_len,), f32),
                pltpu.SMEM((2 * half_len,), i32),
                pltpu.SemaphoreType.DMA((4,)),
            ],
        ),
        compiler_params=_cparams(("arbitrary", "arbitrary"), 58),
        name="experts",
    )(sb_start, n_sub, tail_rows, xp, dest, gates, fill, *w_up_halves, b_up.reshape(n_exp, 1, 2 * d), w_down,
      b_down.reshape(n_exp, 1, d))


def _final_kernel(h_ref, f_ref, g_ref, o_ref):
    tm, d = h_ref.shape
    tiles = d // LANES
    parts = [h_ref[:, jt * LANES:(jt + 1) * LANES] + f_ref[pl.ds(jt, tm, stride=tiles), :] for jt in range(tiles)]
    o_ref[...] = _rms(jnp.concatenate(parts, axis=1), g_ref[...])


def final_norm(h, f2, g, row0, rows):
    d = h.shape[1]
    tiles = d // LANES
    tm = min(ROW_TILE, rows)
    blk0 = row0 // tm
    assert rows % tm == 0 and row0 % tm == 0
    return pl.pallas_call(
        _final_kernel,
        out_shape=jax.ShapeDtypeStruct((rows, d), f32),
        grid=(rows // tm,),
        in_specs=[
            pl.BlockSpec((tm, d), lambda i: (blk0 + i, 0)),
            pl.BlockSpec((tm * tiles, LANES), lambda i: (blk0 + i, 0)),
            pl.BlockSpec((1, d), lambda i: (0, 0)),
        ],
        out_specs=pl.BlockSpec((tm, d), lambda i: (i, 0)),
        compiler_params=_cparams(("arbitrary",), 32),
        name="final_norm",
    )(h, f2, g.reshape(1, d))


def moe(h, g, w_router, b_router, w_up_p, b_up, w_down_b, b_down):
    n, d = h.shape
    n_exp = w_router.shape[1]
    tc = n // MOE_CHUNKS
    xp, top_e, gates, rank, counts = router(h, g, w_router, b_router, tc)
    n_sub = (counts.reshape(MOE_CHUNKS, n_exp) + MOE_SUB - 1) // MOE_SUB
    sb_start = jnp.cumsum(n_sub, axis=1) - n_sub
    hot = top_e.reshape(TOP_K, MOE_CHUNKS, tc, 1) == jnp.arange(n_exp, dtype=i32)
    row_base = (sb_start * MOE_SUB).reshape(1, MOE_CHUNKS, 1, n_exp)
    dest = jnp.sum(jnp.where(hot, row_base, 0), axis=-1) + rank.reshape(TOP_K, MOE_CHUNKS, tc)
    seg = 1 << tc.bit_length()
    spare_row = _row_map_geometry(tc, n_exp, seg)[1] - 1
    per_chunk = lambda a, tail: jnp.pad(a.transpose(1, 0, 2), ((0, 0), (0, 0), (0, seg - tc)),
                                        constant_values=tail).reshape(MOE_CHUNKS, TOP_K * seg)
    tiles = d // LANES
    b_up_p = b_up.reshape(n_exp, tiles, LANES, 2).transpose(0, 1, 3, 2).reshape(n_exp, 2 * d)
    tail_rows = counts.reshape(MOE_CHUNKS, n_exp) - (n_sub - 1) * MOE_SUB
    flat = lambda a: a.reshape(-1).astype(i32)
    f = experts(flat(sb_start), flat(n_sub), flat(tail_rows), xp.reshape(n, tiles, LANES),
                per_chunk(dest, spare_row), per_chunk(gates.reshape(TOP_K, MOE_CHUNKS, tc), 0.0), w_up_p, b_up_p,
                w_down_b, b_down, tc)
    return f.reshape(n * tiles, LANES)


def kernel(x_prompt, x_sample, mem_prompt, state_ret, state_conv, cache_mem_k, cache_mem_v, norm_mix, w_in, conv_w,
           ret_gn, w_out, norm_cross, norm_mem, w_mq, w_mk, w_mv, w_mo, norm_ffn, w_router, b_router, w_up, b_up,
           w_down, b_down, norm_final):
    batch, seq, d = x_prompt.shape
    n_seq, ts, _ = x_sample.shape
    depth = w_in.shape[0]
    n_mem = mem_prompt.shape[1]
    r = ret_gn.shape[1]
    cd_width = conv_w.shape[2]
    dh = r // N_RET_HEADS
    n_p = batch * seq
    n_s = n_seq * ts
    n = n_p + n_s
    nblk = n_seq // SAMPLE_BLOCK
    assert seq % ROW_TILE == 0 and n % ROW_TILE == 0 and n % MOE_CHUNKS == 0

    xs = x_sample.reshape(nblk, SAMPLE_BLOCK, ts, d).transpose(0, 2, 1, 3).reshape(n_s, d)
    h_p, h_s = x_prompt.reshape(n_p, d), xs
    tab_p = prompt_tables(seq, dh)
    tab_s = sample_tables(ts, float(PAST_LEN), dh)

    ret_p, conv_p, mk_p, mv_p, ret_s, conv_s = [], [], [], [], [], []
    for l in range(depth):
        w_in_b, w_out_b, w_mq_b = w_in[l].astype(bf16), w_out[l].astype(bf16), w_mq[l].astype(bf16)
        h_p, s_p, c_p, w_down_b, w_up_lo = prompt_mixer(h_p, norm_mix[l], w_in_b, tab_p, conv_w[l], ret_gn[l],
                                                        w_out_b, batch, seq, r, cd_width, w_down[l], w_up[l])
        proj_s = norm_matmul(h_s, norm_mix[l], w_in_b, "in_proj_sample")
        mix_s, s_s, c_s = sample_mixer(proj_s, tab_s, conv_w[l], ret_gn[l], state_ret[l],
                                       state_conv[l].transpose(1, 0, 2), 0, ts, r, cd_width)
        h_s = matmul_res(mix_s, w_out_b, h_s, "out_proj_sample")

        mk, mv = mem_kv(mem_prompt, norm_mem[l], jnp.concatenate([w_mk[l], w_mv[l]], axis=1).astype(bf16),
                        N_MEM_HEADS)
        q_s = norm_matmul(h_s, norm_cross[l], w_mq_b, "q_sample")
        attn_s = cross_sample(q_s, cache_mem_k[l], cache_mem_v[l], 0, ts)
        h, w_up_hi = attn_block(h_p, h_s, norm_cross[l], w_mq_b, mk, mv, attn_s, w_mo[l].astype(bf16), seq, w_up[l])

        f2 = moe(h, norm_ffn[l], w_router[l], b_router[l], (w_up_lo, w_up_hi), b_up[l], w_down_b, b_down[l])
        if l + 1 < depth:
            h = h + f2.reshape(n, d // LANES, LANES).reshape(n, d)
            h_p, h_s = h, h[n_p:]

        ret_p.append(s_p)
        conv_p.append(c_p)
        mk_p.append(mk)
        mv_p.append(mv)
        ret_s.append(s_s)
        conv_s.append(c_s.transpose(1, 0, 2))

    y_p = final_norm(h, f2, norm_final, 0, n_p).reshape(batch, seq, d)
    y_s = final_norm(h, f2, norm_final, n_p, n_s)
    y_s = y_s.reshape(nblk, ts, SAMPLE_BLOCK, d).transpose(0, 2, 1, 3).reshape(n_seq, ts, d)
    return (y_p, y_s, jnp.stack(ret_p), jnp.stack(conv_p), jnp.stack(mk_p), jnp.stack(mv_p), jnp.stack(ret_s),
            jnp.stack(conv_s))
```
